```python
import math
import jax, jax.numpy as jnp
from jax import lax
import numpy as np

D_MODEL = 1024
BATCH = 32
SEQ = 256
DEPTH = 2
DEC_BATCH = 2
DEC_SEQ = 1024
PAST_LEN = 256

GRID_W = 64
N_EVEN = (DEPTH + 1) // 2
N_ODD = DEPTH // 2
A_WIDTH = D_MODEL // 2
A_HEADS = 4
A_HEAD_DIM = A_WIDTH // A_HEADS
A_HALF = A_HEAD_DIM // 2
B_WIDTH = D_MODEL - A_WIDTH
B_HEAD = 64
B_HEADS = B_WIDTH // B_HEAD
LORA_W = 64
LORA_A = 64
LORA_G = 128
IN_B = 3 * B_WIDTH + 2 * LORA_W + 2 * LORA_A + LORA_G
IN_EVEN = 3 * A_WIDTH + IN_B
C_GROUPS = 8
C_GROUP_DIM = D_MODEL // C_GROUPS
D_FF = 2816
ROPE_BASE = 10000.0
Q_BLOCK = 128
RMS_EPS = 1e-6
LNX_EPS = 64e-5

kernel_name = "hybrid_diffattn_rwkv7_fnet_prefix_step"


def rmsnorm(x, g):
    xf = x.astype(jnp.float32)
    y = xf * lax.rsqrt(jnp.mean(xf * xf, axis=-1, keepdims=True) + RMS_EPS)
    return y.astype(x.dtype) * g


def axial_rope_tables(rows):
    row = jnp.repeat(jnp.arange(rows, dtype=jnp.float32), GRID_W)
    col = jnp.tile(jnp.arange(GRID_W, dtype=jnp.float32), rows)
    n_freq = A_HALF // 4
    inv = ROPE_BASE ** (-jnp.arange(n_freq, dtype=jnp.float32) / n_freq)
    ang = jnp.concatenate([row[:, None] * inv, col[:, None] * inv], axis=-1)
    return jnp.cos(ang), jnp.sin(ang)


def apply_rope(x, cos, sin):
    xf = x.astype(jnp.float32)
    x1, x2 = xf[..., 0::2], xf[..., 1::2]
    c = cos[:, None, None, :]
    s = sin[:, None, None, :]
    out = jnp.stack([x1 * c - x2 * s, x1 * s + x2 * c], axis=-1).reshape(x.shape)
    return out.astype(x.dtype)


def diff_attention(q, k, v, lam):
    b, lq = q.shape[0], q.shape[1]
    nblk = lq // Q_BLOCK
    qb = q.reshape(b, nblk, Q_BLOCK, A_HEADS, 2, A_HALF).transpose(1, 0, 2, 3, 4, 5)
    scale = A_HALF ** -0.5

    def one_block(qblk):
        s = jnp.einsum("bqhmd,bkhmd->bmhqk", qblk, k).astype(jnp.float32) * scale
        p = jax.nn.softmax(s, axis=-1)
        w = p[:, 0] - lam * p[:, 1]
        return jnp.einsum("bhqk,bkhd->bqhd", w.astype(v.dtype), v)

    out = lax.map(one_block, qb)
    return out.transpose(1, 0, 2, 3, 4).reshape(b, lq, A_HEADS, A_HEAD_DIM)


def token_shift_centred(f, mu):
    prev = jnp.pad(f, ((0, 0), (1, 0), (0, 0)))[:, :-1]
    nxt = jnp.pad(f, ((0, 0), (0, 1), (0, 0)))[:, 1:]
    return f + mu[0] * (prev - f) + mu[1] * (nxt - f)


def wkv_scan(state0, r, w, k, v, kk, a):
    def step(S, inp):
        r_t, w_t, k_t, v_t, kk_t, a_t = inp
        sa = jnp.einsum("bhij,bhj->bhi", S, -kk_t)
        S = (S * w_t[:, :, None, :] + sa[..., None] * (kk_t * a_t)[:, :, None, :]
             + v_t[..., None] * k_t[:, :, None, :])
        return S, jnp.einsum("bhij,bhj->bhi", S, r_t)

    xs = tuple(jnp.moveaxis(t, 1, 0) for t in (r, w, k, v, kk, a))
    S, y = lax.scan(step, state0.astype(jnp.float32), xs)
    return S, jnp.moveaxis(y, 0, 1)


def rwkv7_bidir(fb, s0, e, P):
    b, L, _ = fb.shape
    fb = token_shift_centred(fb, P["rwkv_shift_mu"][e])
    r = fb[..., :B_WIDTH]
    k = fb[..., B_WIDTH:2 * B_WIDTH]
    v = fb[..., 2 * B_WIDTH:3 * B_WIDTH]
    off = 3 * B_WIDTH
    wd = fb[..., off:off + 2 * LORA_W].reshape(b, L, 2, LORA_W)
    off = off + 2 * LORA_W
    ad = fb[..., off:off + 2 * LORA_A].reshape(b, L, 2, LORA_A)
    off = off + 2 * LORA_A
    gd = fb[..., off:]
    g = jax.nn.sigmoid(gd) @ P["rwkv_g2"][e]
    kvec = P["rwkv_kvec"][e]

    def heads(t):
        return t.reshape(b, L, B_HEADS, B_HEAD).astype(jnp.float32)

    rh, vh = heads(r), heads(v)
    kk = heads(k * kvec[0])
    kk = kk * lax.rsqrt(jnp.sum(kk * kk, axis=-1, keepdims=True) + 1e-12)
    r_k = kvec[2].reshape(B_HEADS, B_HEAD).astype(jnp.float32)
    ys = 0.0
    bonus = 0.0
    finals = []
    for d in range(2):
        w_raw = (P["rwkv_w0"][e, d] + jnp.tanh(wd[:, :, d]) @ P["rwkv_w2"][e, d]).astype(jnp.float32)
        decay = jnp.exp(-jnp.exp(-jax.nn.softplus(-w_raw) - 0.5))
        a = jax.nn.sigmoid(P["rwkv_a0"][e, d] + ad[:, :, d] @ P["rwkv_a2"][e, d])
        kd = k * (1 + (a - 1) * kvec[1])
        ah, kdh, wh = heads(a), heads(kd), heads(decay)
        inputs = (rh, wh, kdh, vh, kk, ah)
        if d == 1:
            inputs = tuple(jnp.flip(t, axis=1) for t in inputs)
        S, y = wkv_scan(s0[:, d], *inputs)
        if d == 1:
            y = jnp.flip(y, axis=1)
        ys = ys + y
        bonus = bonus + jnp.sum(rh * kdh * r_k, axis=-1, keepdims=True) * vh
        finals.append(S)
    mu = jnp.mean(ys, axis=-1, keepdims=True)
    var = jnp.mean((ys - mu) ** 2, axis=-1, keepdims=True)
    yn = ((ys - mu) * lax.rsqrt(var + LNX_EPS)).reshape(b, L, B_WIDTH)
    lnx = P["rwkv_lnx"][e]
    y = (yn * lnx[0] + lnx[1] + bonus.reshape(b, L, B_WIDTH)).astype(fb.dtype) * g
    return y, jnp.stack(finals, axis=1).astype(fb.dtype)


def fourier_mix(h, w_out):
    b, L, _ = h.shape
    hg = h.reshape(b, L, C_GROUPS, C_GROUP_DIM).astype(jnp.float32)
    f = jnp.fft.fft2(hg, axes=(1, 3), norm="ortho").real
    return f.reshape(b, L, D_MODEL).astype(h.dtype) @ w_out


def conv_ffn(h, w_in, conv_w, conv_b, w_out):
    L = h.shape[1]
    ug = h @ w_in
    u, g = ug[..., :D_FF], ug[..., D_FF:]
    gp = jnp.pad(g, ((0, 0), (1, 1), (0, 0)))
    g = gp[:, :L] * conv_w[0] + gp[:, 1:L + 1] * conv_w[1] + gp[:, 2:] * conv_w[2] + conv_b
    return (jax.nn.silu(g) * u) @ w_out


def even_mixer(h, l, rope, ctx, P):
    e = l // 2
    b, L, _ = h.shape
    proj = h @ P["w_in_even"][e]
    q = proj[..., :A_WIDTH].reshape(b, L, A_HEADS, 2, A_HALF)
    k = proj[..., A_WIDTH:2 * A_WIDTH].reshape(b, L, A_HEADS, 2, A_HALF)
    v = proj[..., 2 * A_WIDTH:3 * A_WIDTH].reshape(b, L, A_HEADS, A_HEAD_DIM)
    fb = proj[..., 3 * A_WIDTH:]
    if ctx is None:
        keys, vals = k, v
        s0 = jnp.zeros((b, 2, B_HEADS, B_HEAD, B_HEAD), jnp.float32)
    else:
        ck, cv, cs = ctx
        cos, sin = rope
        q = apply_rope(q, cos, sin)
        k_lat = apply_rope(k, cos, sin)
        keys = jnp.concatenate([ck.reshape(b, -1, A_HEADS, 2, A_HALF), k_lat], axis=1)
        vals = jnp.concatenate([cv, v], axis=1)
        s0 = cs
    lp = P["diff_lambda"][e].astype(jnp.float32)
    lam_init = 0.8 - 0.6 * math.exp(-0.3 * l)
    lam = jnp.exp(jnp.sum(lp[0] * lp[1])) - jnp.exp(jnp.sum(lp[2] * lp[3])) + lam_init
    ya = diff_attention(q, keys, vals, lam)
    ya = rmsnorm(ya, P["diff_subln"][e]) * (1.0 - lam_init)
    yb, s_final = rwkv7_bidir(fb, s0, e, P)
    m = jnp.concatenate([ya.reshape(b, L, A_WIDTH), yb], axis=-1) @ P["w_out_even"][e]
    new_ctx = (k.reshape(b, L, A_HEADS, A_HEAD_DIM), v, s_final) if ctx is None else None
    return m, new_ctx


def trunk_layer(l, x, mod, rope, ctx, P):
    shift_m, scale_m, gate_m, shift_f, scale_f, gate_f = jnp.split(mod, 6, axis=-1)
    gains = P["norm_gains"][l]
    h = rmsnorm(x, gains[0]) * (1 + scale_m) + shift_m
    if l % 2 == 0:
        m, new_ctx = even_mixer(h, l, rope, ctx, P)
    else:
        m, new_ctx = fourier_mix(h, P["w_out_odd"][l // 2]), None
    x = x + gate_m * rmsnorm(m, gains[1])
    h = rmsnorm(x, gains[2]) * (1 + scale_f) + shift_f
    f = conv_ffn(h, P["w_ffn_in"][l], P["ffn_conv"][l], P["ffn_conv_b"][l], P["w_ffn_out"][l])
    x = x + gate_f * rmsnorm(f, gains[3])
    return x, new_ctx


def setup_inputs(seed: int = 0) -> dict:
    key = jax.random.key(seed)
    ks = jax.random.split(key, 32)
    D = D_MODEL

    def nrm(k, shape, s):
        return jax.random.normal(k, shape, jnp.float32) * s

    return {
        "x_prompt": nrm(ks[0], (BATCH, SEQ, D), 1.0),
        "x_sample": nrm(ks[1], (DEC_BATCH, DEC_SEQ, D), 1.0),
        "cache_k": nrm(ks[2], (DEC_BATCH, N_EVEN, PAST_LEN, A_HEADS, A_HEAD_DIM), 1.0),
        "cache_v": nrm(ks[3], (DEC_BATCH, N_EVEN, PAST_LEN, A_HEADS, A_HEAD_DIM), 1.0),
        "state_wkv": nrm(ks[4], (DEC_BATCH, N_EVEN, 2, B_HEADS, B_HEAD, B_HEAD), 0.5),
        "c": nrm(ks[5], (DEC_BATCH, D), 1.0),
        "c_ctx": nrm(ks[6], (D,), 1.0),
        "w_ada": nrm(ks[7], (DEPTH, D, 6 * D), 0.5 * D ** -0.5),
        "b_ada": nrm(ks[8], (DEPTH, 6 * D), 0.02),
        "norm_gains": 1.0 + nrm(ks[9], (DEPTH, 4, D), 0.05),
        "w_in_even": nrm(ks[10], (N_EVEN, D, IN_EVEN), D ** -0.5),
        "w_out_even": nrm(ks[11], (N_EVEN, D, D), D ** -0.5),
        "diff_lambda": nrm(ks[12], (N_EVEN, 4, A_HALF), 0.1),
        "diff_subln": 1.0 + nrm(ks[13], (N_EVEN, A_HEAD_DIM), 0.05),
        "rwkv_shift_mu": jax.random.uniform(ks[14], (N_EVEN, 2, IN_B), jnp.float32, 0.0, 0.5),
        "rwkv_w0": jax.random.uniform(ks[15], (N_EVEN, 2, B_WIDTH), jnp.float32, -6.0, -1.0),
        "rwkv_w2": nrm(ks[16], (N_EVEN, 2, LORA_W, B_WIDTH), 0.1),
        "rwkv_a0": nrm(ks[17], (N_EVEN, 2, B_WIDTH), 0.1),
        "rwkv_a2": nrm(ks[18], (N_EVEN, 2, LORA_A, B_WIDTH), 0.1),
        "rwkv_g2": nrm(ks[19], (N_EVEN, LORA_G, B_WIDTH), LORA_G ** -0.5),
        "rwkv_kvec": jnp.array([0.85, 1.0, 0.0], jnp.float32)[None, :, None]
        + nrm(ks[20], (N_EVEN, 3, B_WIDTH), 0.1),
        "rwkv_lnx": jnp.array([1.0, 0.0], jnp.float32)[None, :, None]
        + nrm(ks[21], (N_EVEN, 2, B_WIDTH), 0.05),
        "w_out_odd": nrm(ks[22], (N_ODD, D, D), D ** -0.5),
        "w_ffn_in": nrm(ks[23], (DEPTH, D, 2 * D_FF), D ** -0.5),
        "ffn_conv": nrm(ks[24], (DEPTH, 3, D_FF), 3 ** -0.5),
        "ffn_conv_b": nrm(ks[25], (DEPTH, D_FF), 0.02),
        "w_ffn_out": nrm(ks[26], (DEPTH, D_FF, D), D_FF ** -0.5),
    }


def reference(x_prompt, x_sample, cache_k, cache_v, state_wkv, c, c_ctx, w_ada, b_ada,
              norm_gains, w_in_even, w_out_even, diff_lambda, diff_subln, rwkv_shift_mu,
              rwkv_w0, rwkv_w2, rwkv_a0, rwkv_a2, rwkv_g2, rwkv_kvec, rwkv_lnx, w_out_odd,
              w_ffn_in, ffn_conv, ffn_conv_b, w_ffn_out):
    P = {
        "norm_gains": norm_gains, "w_in_even": w_in_even, "w_out_even": w_out_even,
        "diff_lambda": diff_lambda, "diff_subln": diff_subln, "rwkv_shift_mu": rwkv_shift_mu,
        "rwkv_w0": rwkv_w0, "rwkv_w2": rwkv_w2, "rwkv_a0": rwkv_a0, "rwkv_a2": rwkv_a2,
        "rwkv_g2": rwkv_g2, "rwkv_kvec": rwkv_kvec, "rwkv_lnx": rwkv_lnx,
        "w_out_odd": w_out_odd, "w_ffn_in": w_ffn_in, "ffn_conv": ffn_conv,
        "ffn_conv_b": ffn_conv_b, "w_ffn_out": w_ffn_out,
    }
    x = x_prompt
    ks, vs, ss = [], [], []
    for l in range(DEPTH):
        mod = (jax.nn.silu(c_ctx) @ w_ada[l] + b_ada[l])[None, None, :]
        x, new_ctx = trunk_layer(l, x, mod, None, None, P)
        if new_ctx is not None:
            ks.append(new_ctx[0])
            vs.append(new_ctx[1])
            ss.append(new_ctx[2])
    y_prompt = x
    new_cache_k = jnp.stack(ks, axis=1)
    new_cache_v = jnp.stack(vs, axis=1)
    new_state_wkv = jnp.stack(ss, axis=1)

    rows = x_sample.shape[1] // GRID_W
    rope = axial_rope_tables(rows)
    x = x_sample
    for l in range(DEPTH):
        mod = (jax.nn.silu(c) @ w_ada[l] + b_ada[l])[:, None, :]
        if l % 2 == 0:
            e = l // 2
            ctx = (cache_k[:, e], cache_v[:, e], state_wkv[:, e])
        else:
            ctx = None
        x, _ = trunk_layer(l, x, mod, rope, ctx, P)
    y_sample = x
    return (y_prompt, y_sample, new_cache_k, new_cache_v, new_state_wkv)
```

```python
import functools
import math

import numpy as np
import jax
import jax.numpy as jnp
from jax import lax
from jax.experimental import pallas as pl
from jax.experimental.pallas import tpu as pltpu

F32 = jnp.float32
BF16 = jnp.bfloat16

D_MODEL = 1024
DEPTH = 2
GRID_W = 64
A_WIDTH = D_MODEL // 2
A_HEADS = 4
A_HEAD_DIM = A_WIDTH // A_HEADS
A_HALF = A_HEAD_DIM // 2
B_WIDTH = D_MODEL - A_WIDTH
B_HEAD = 64
B_HEADS = B_WIDTH // B_HEAD
LORA_W = 64
LORA_A = 64
LORA_G = 128
IN_B = 3 * B_WIDTH + 2 * LORA_W + 2 * LORA_A + LORA_G
IN_EVEN = 3 * A_WIDTH + IN_B
C_GROUPS = 8
C_GROUP_DIM = D_MODEL // C_GROUPS
D_FF = 2816
ROPE_BASE = 10000.0
RMS_EPS = 1e-6
LNX_EPS = 64e-5

LANES = 128
SUBLANES = 8
PAIR = 2 * B_HEAD
N_PAIRS = B_HEADS // 2
CHUNK = 64
FF_CHUNK = 256
VMEM_LIMIT = 48 * 1024 * 1024
MOD_ROWS = 8


def _cparams(*sem):
    return pltpu.CompilerParams(dimension_semantics=sem, vmem_limit_bytes=VMEM_LIMIT)


def _sigmoid(x):
    return 1.0 / (1.0 + jnp.exp(-x))


def _dot(a, b):
    return jnp.dot(a.astype(BF16), b.astype(BF16), preferred_element_type=F32)


_NN = (((1,), (0,)), ((), ()))
_NT = (((1,), (1,)), ((), ()))
_TN = (((0,), (0,)), ((), ()))


def _split_bf16(x, n):
    parts = []
    rem = x
    for i in range(n):
        p = rem.astype(BF16)
        parts.append(p)
        if i + 1 < n:
            rem = rem - p.astype(F32)
    return parts


def _mm(a, b, dims, passes):
    if passes == 1:
        return lax.dot_general(a.astype(BF16), b.astype(BF16), dims, preferred_element_type=F32)
    a_hi, a_lo = _split_bf16(a, 2)
    b_hi, b_lo = _split_bf16(b, 2)
    dg = functools.partial(lax.dot_general, dimension_numbers=dims, preferred_element_type=F32)
    return dg(a_hi, b_hi) + (dg(a_hi, b_lo) + dg(a_lo, b_hi))


def _segsum(x, bd):
    acc = None
    for p in _split_bf16(x, 3):
        t = jnp.dot(p, bd, preferred_element_type=F32)
        acc = t if acc is None else acc + t
    return acc


def _rms(x):
    return x * lax.rsqrt(jnp.mean(x * x, axis=-1, keepdims=True) + RMS_EPS)


def _modnorm(x, gain, scale, shift):
    return _rms(x) * gain * (1.0 + scale) + shift


def _mod_kernel(c_ref, w_ref, b_ref, o_ref):
    c = c_ref[...]
    s = c * _sigmoid(c)
    o_ref[0] = _dot(s, w_ref[0]) + b_ref[0]


def _modulation(cvec, w_ada, b_ada):
    tn = 1536
    n = 6 * D_MODEL
    return pl.pallas_call(
        _mod_kernel,
        grid=(DEPTH, n // tn),
        in_specs=[
            pl.BlockSpec((MOD_ROWS, D_MODEL), lambda l, j: (0, 0)),
            pl.BlockSpec((1, D_MODEL, tn), lambda l, j: (l, 0, j)),
            pl.BlockSpec((1, 1, tn), lambda l, j: (l, 0, j)),
        ],
        out_specs=pl.BlockSpec((1, MOD_ROWS, tn), lambda l, j: (l, 0, j)),
        out_shape=jax.ShapeDtypeStruct((DEPTH, MOD_ROWS, n), F32),
        compiler_params=_cparams("parallel", "parallel"),
        name="modulation",
    )(cvec, w_ada, b_ada.reshape(DEPTH, 1, n))


def _mod_spec(tm, seq_len, mod_row0):
    if mod_row0 == 0:
        return pl.BlockSpec((1, 6, D_MODEL), lambda i, *_: (0, 0, 0))
    return pl.BlockSpec((1, 6, D_MODEL), lambda i, *_: (mod_row0 + (i * tm) // seq_len, 0, 0))


def _even_in_kernel(x_ref, mod_ref, gains_ref, w_ref, q_ref, k_ref, v_ref, fb_ref):
    mod = mod_ref[0]
    h = _modnorm(x_ref[...], gains_ref[0:1], mod[1:2], mod[0:1]).astype(BF16)
    a = A_WIDTH
    q_ref[...] = jnp.dot(h, w_ref[:, 0:a], preferred_element_type=F32)
    k_ref[...] = jnp.dot(h, w_ref[:, a:2 * a], preferred_element_type=F32)
    v_ref[...] = jnp.dot(h, w_ref[:, 2 * a:3 * a], preferred_element_type=F32)
    fb_ref[...] = jnp.dot(h, w_ref[:, 3 * a:], preferred_element_type=F32)


def _even_in(x, mod, gains, w, seq_len, mod_row0, tm):
    t = x.shape[0]
    row = lambda i: (i, 0)
    return pl.pallas_call(
        _even_in_kernel,
        grid=(t // tm,),
        in_specs=[
            pl.BlockSpec((tm, D_MODEL), row),
            _mod_spec(tm, seq_len, mod_row0),
            pl.BlockSpec((4, D_MODEL), lambda i: (0, 0)),
            pl.BlockSpec((D_MODEL, IN_EVEN), lambda i: (0, 0)),
        ],
        out_specs=[
            pl.BlockSpec((tm, A_WIDTH), row),
            pl.BlockSpec((tm, A_WIDTH), row),
            pl.BlockSpec((tm, A_WIDTH), row),
            pl.BlockSpec((tm, IN_B), row),
        ],
        out_shape=[
            jax.ShapeDtypeStruct((t, A_WIDTH), F32),
            jax.ShapeDtypeStruct((t, A_WIDTH), F32),
            jax.ShapeDtypeStruct((t, A_WIDTH), F32),
            jax.ShapeDtypeStruct((t, IN_B), F32),
        ],
        compiler_params=_cparams("parallel"),
        name="even_in",
    )(x, mod, gains, w)


def _rope(x, cos_t, sin_t):
    lane = lax.broadcasted_iota(jnp.int32, x.shape, 1)
    nxt = pltpu.roll(x, LANES - 1, 1)
    prv = pltpu.roll(x, 1, 1)
    partner = jnp.where(lane % 2 == 0, nxt, prv)
    return x * cos_t + partner * sin_t


def _attn_kernel(*refs, has_ctx, lam_init):
    if has_ctx:
        (lam_ref, sub_ref, q_ref, k_ref, v_ref, ck_ref, cv_ref,
         cq_ref, sq_ref, ckt_ref, skt_ref, o_ref) = refs
    else:
        lam_ref, sub_ref, q_ref, k_ref, v_ref, o_ref = refs
    lp = lam_ref[...]
    l1 = jnp.sum(lp[0:1] * lp[1:2], axis=-1, keepdims=True)
    l2 = jnp.sum(lp[2:3] * lp[3:4], axis=-1, keepdims=True)
    lam = jnp.exp(l1) - jnp.exp(l2) + lam_init
    q = q_ref[...]
    k = k_ref[...]
    v = v_ref[...].astype(BF16)
    if has_ctx:
        q = _rope(q, cq_ref[...], sq_ref[...])
        k = _rope(k, ckt_ref[...], skt_ref[...])
        ck = ck_ref[...].astype(BF16)
        cv = cv_ref[...].astype(BF16)
    k = k.astype(BF16)
    lane = lax.broadcasted_iota(jnp.int32, q.shape, 1)
    scale = A_HALF ** -0.5
    probs = []
    for m in range(2):
        qm = jnp.where((lane < A_HALF) == (m == 0), q, 0.0).astype(BF16)
        s = lax.dot_general(qm, k, _NT, preferred_element_type=F32) * scale
        mx = jnp.max(s, axis=-1, keepdims=True)
        if has_ctx:
            sc = lax.dot_general(qm, ck, _NT, preferred_element_type=F32) * scale
            mx = jnp.maximum(mx, jnp.max(sc, axis=-1, keepdims=True))
            ec = jnp.exp(sc - mx)
        e = jnp.exp(s - mx)
        z = jnp.sum(e, axis=-1, keepdims=True)
        if has_ctx:
            z = z + jnp.sum(ec, axis=-1, keepdims=True)
        inv = 1.0 / z
        probs.append((e * inv, ec * inv if has_ctx else None))
    w = probs[0][0] - lam * probs[1][0]
    o = jnp.dot(w.astype(BF16), v, preferred_element_type=F32)
    if has_ctx:
        wc = probs[0][1] - lam * probs[1][1]
        o = o + jnp.dot(wc.astype(BF16), cv, preferred_element_type=F32)
    o_ref[...] = _rms(o) * sub_ref[...] * (1.0 - lam_init)


def _attention(q, k, v, lam_p, subln, lam_init, n_seq, seq_len, ctx=None, tq=256):
    t = q.shape[0]
    hd = A_HEAD_DIM
    out_shape = jax.ShapeDtypeStruct((t, A_WIDTH), F32)
    small = [pl.BlockSpec((4, A_HALF), lambda *_: (0, 0)), pl.BlockSpec((1, hd), lambda *_: (0, 0))]
    kern = functools.partial(_attn_kernel, has_ctx=ctx is not None, lam_init=lam_init)
    if ctx is None:
        blk = pl.BlockSpec((seq_len, hd), lambda b, h: (b, h))
        return pl.pallas_call(
            kern, grid=(n_seq, A_HEADS),
            in_specs=small + [blk, blk, blk],
            out_specs=blk, out_shape=out_shape,
            compiler_params=_cparams("parallel", "parallel"), name="diff_attn_ctx",
        )(lam_p, subln, q, k, v)
    ck, cv, cos_t, sin_t = ctx
    past = ck.shape[0] // n_seq
    nq = seq_len // tq
    qblk = pl.BlockSpec((tq, hd), lambda b, h, i: (b * nq + i, h))
    kblk = pl.BlockSpec((seq_len, hd), lambda b, h, i: (b, h))
    cblk = pl.BlockSpec((past, hd), lambda b, h, i: (b, h))
    tq_blk = pl.BlockSpec((tq, hd), lambda b, h, i: (i, 0))
    tk_blk = pl.BlockSpec((seq_len, hd), lambda b, h, i: (0, 0))
    return pl.pallas_call(
        kern, grid=(n_seq, A_HEADS, nq),
        in_specs=small + [qblk, kblk, kblk, cblk, cblk, tq_blk, tq_blk, tk_blk, tk_blk],
        out_specs=qblk, out_shape=out_shape,
        compiler_params=_cparams("parallel", "parallel", "parallel"), name="diff_attn_lat",
    )(lam_p, subln, q, k, v, ck, cv, cos_t, sin_t, cos_t, sin_t)


def _shift_rows(f, halo_prev, halo_next, pos, seq_len):
    tm = f.shape[0]
    local = lax.broadcasted_iota(jnp.int32, (tm, 1), 0)
    prev = jnp.where(local == 0, halo_prev, pltpu.roll(f, 1, 0))
    nxt = jnp.where(local == tm - 1, halo_next, pltpu.roll(f, tm - 1, 0))
    prev = jnp.where(pos == 0, 0.0, prev)
    nxt = jnp.where(pos == seq_len - 1, 0.0, nxt)
    return prev, nxt


def _prep_kernel(fb_ref, fp_ref, fn_ref, mu_ref, w0_ref, w2_ref, a0_ref, a2_ref, g2_ref, kvec_ref,
                 bd_ref, scan_ref, post_ref, *, seq_len):
    f = fb_ref[...]
    tm = f.shape[0]
    pos = (pl.program_id(0) * tm + lax.broadcasted_iota(jnp.int32, (tm, 1), 0)) % seq_len
    prev, nxt = _shift_rows(f, fp_ref[SUBLANES - 1:SUBLANES, :], fn_ref[0:1, :], pos, seq_len)
    f = f + mu_ref[0:1] * (prev - f) + mu_ref[1:2] * (nxt - f)
    bw = B_WIDTH
    r = f[:, 0:bw]
    k = f[:, bw:2 * bw]
    v = f[:, 2 * bw:3 * bw]
    off = 3 * bw
    wd = jnp.tanh(f[:, off:off + 2 * LORA_W])
    ad = f[:, off + 2 * LORA_W:off + 2 * LORA_W + 2 * LORA_A]
    gd = f[:, off + 2 * LORA_W + 2 * LORA_A:]
    bd = bd_ref[...]
    kv = kvec_ref[...]
    g = _dot(_sigmoid(gd), g2_ref[...])
    kk = k * kv[0:1]
    kk = kk * lax.rsqrt(_segsum(kk * kk, bd) + 1e-12)
    scan_ref[0] = r
    scan_ref[1] = v
    scan_ref[2] = kk
    ksum = None
    for d in range(2):
        w_raw = w0_ref[d:d + 1] + _dot(wd, w2_ref[d])
        scan_ref[7 + d] = -math.exp(-0.5) * _sigmoid(w_raw)
        a = _sigmoid(a0_ref[d:d + 1] + _dot(ad, a2_ref[d]))
        kd = k * (1.0 + (a - 1.0) * kv[1:2])
        scan_ref[3 + d] = kd
        scan_ref[5 + d] = kk * a
        ksum = kd if ksum is None else ksum + kd
    post_ref[0] = g
    post_ref[1] = _segsum(r * ksum * kv[2:3], bd) * v


def _rwkv_prep(fb, p, seq_len, tm=256):
    t = fb.shape[0]
    nh = tm // SUBLANES
    last = t // SUBLANES - 1
    full = lambda shape: pl.BlockSpec(shape, lambda i: (0,) * len(shape))
    return pl.pallas_call(
        functools.partial(_prep_kernel, seq_len=seq_len),
        grid=(t // tm,),
        in_specs=[
            pl.BlockSpec((tm, IN_B), lambda i: (i, 0)),
            pl.BlockSpec((SUBLANES, IN_B), lambda i: (jnp.maximum(i * nh - 1, 0), 0)),
            pl.BlockSpec((SUBLANES, IN_B), lambda i: (jnp.minimum((i + 1) * nh, last), 0)),
            full((2, IN_B)), full((2, B_WIDTH)), full((2, 2 * LORA_W, B_WIDTH)),
            full((2, B_WIDTH)), full((2, 2 * LORA_A, B_WIDTH)), full((LORA_G, B_WIDTH)),
            full((3, B_WIDTH)), full((B_WIDTH, B_WIDTH)),
        ],
        out_specs=[
            pl.BlockSpec((9, tm, B_WIDTH), lambda i: (0, i, 0)),
            pl.BlockSpec((2, tm, B_WIDTH), lambda i: (0, i, 0)),
        ],
        out_shape=[
            jax.ShapeDtypeStruct((9, t, B_WIDTH), F32),
            jax.ShapeDtypeStruct((2, t, B_WIDTH), F32),
        ],
        compiler_params=_cparams("parallel"),
        name="rwkv_prep",
    )(fb, fb, fb, p["mu"], p["w0"], p["w2"], p["a0"], p["a2"], p["g2"], p["kvec"], p["bd_ones"])


def _cumsum_rows(x, reverse):
    n = x.shape[0]
    ridx = lax.broadcasted_iota(jnp.int32, x.shape, 0)
    s = 1
    while s < n:
        if reverse:
            x = x + jnp.where(ridx < n - s, pltpu.roll(x, n - s, 0), 0.0)
        else:
            x = x + jnp.where(ridx >= s, pltpu.roll(x, s, 0), 0.0)
        s *= 2
    return x


def _scan_kernel(pre_ref, s0_ref, ys_ref, sf_ref, s_scr, y_scr, *, seq_len, passes):
    nc = seq_len // CHUNK
    c2 = 2 * CHUNK
    lane = lax.broadcasted_iota(jnp.int32, (CHUNK, PAIR), 1)
    head0 = lane < B_HEAD
    row = lax.broadcasted_iota(jnp.int32, (c2, c2), 0)
    col = lax.broadcasted_iota(jnp.int32, (c2, c2), 1)
    same = (row // CHUNK) == (col // CHUNK)
    eye = jnp.where(row == col, 1.0, 0.0)
    mm = functools.partial(_mm, passes=passes)

    def stack(x):
        return jnp.concatenate([jnp.where(head0, x, 0.0), jnp.where(head0, 0.0, x)], axis=0)

    for d in range(2):
        s_scr[d] = jnp.zeros((PAIR, PAIR), F32)
        s_scr[d, 0:B_HEAD, 0:B_HEAD] = s0_ref[0, d, 0]
        s_scr[d, B_HEAD:PAIR, B_HEAD:PAIR] = s0_ref[0, d, 1]

    def body(c, carry):
        for d in range(2):
            cc = c if d == 0 else nc - 1 - c
            rows = pl.ds(pl.multiple_of(cc * CHUNK, CHUNK), CHUNK)
            r = pre_ref[0, rows, :]
            v = pre_ref[1, rows, :]
            kk = pre_ref[2, rows, :]
            k = pre_ref[3 + d, rows, :]
            b = pre_ref[5 + d, rows, :]
            lw = pre_ref[7 + d, rows, :]
            incl = same & ((col <= row) if d == 0 else (col >= row))
            strict = same & ((col < row) if d == 0 else (col > row))
            g = _cumsum_rows(lw, reverse=(d == 1))
            gp = g - lw
            gtot = g[CHUNK - 1:CHUNK, :] if d == 0 else g[0:1, :]
            gm = 0.5 * gtot
            eng = jnp.exp(gm - g)
            lhs = jnp.concatenate([stack(r * jnp.exp(g - gm)), stack(kk * jnp.exp(gp - gm))], axis=0)
            rhs = jnp.concatenate([stack(k * eng), stack(b * eng)], axis=0)
            a_all = mm(lhs, rhs, _NT)
            a_rk = jnp.where(incl, a_all[0:c2, 0:c2], 0.0)
            a_rb = jnp.where(incl, a_all[0:c2, c2:2 * c2], 0.0)
            a_kk = jnp.where(strict, a_all[c2:2 * c2, 0:c2], 0.0)
            n_mat = jnp.where(strict, a_all[c2:2 * c2, c2:2 * c2], 0.0)
            p_inv = eye - n_mat
            q_pow = mm(n_mat, n_mat, _NN)
            step = 2
            while step < CHUNK:
                p_inv = p_inv + mm(p_inv, q_pow, _NN)
                step *= 2
                if step < CHUNK:
                    q_pow = mm(q_pow, q_pow, _NN)
            s0 = s_scr[d]
            vs = stack(v)
            x = mm(stack(kk * jnp.exp(gp)), s0, _NT) + mm(a_kk, vs, _NN)
            u = mm(p_inv, x, _NN)
            y = mm(stack(r * jnp.exp(g)), s0, _NT) + mm(a_rk, vs, _NN) - mm(a_rb, u, _NN)
            y_scr[d, rows, :] = y[0:CHUNK] + y[CHUNK:c2]
            et = jnp.exp(gtot - g)
            upd = mm(jnp.concatenate([vs, -u], axis=0),
                     jnp.concatenate([stack(k * et), stack(b * et)], axis=0), _TN)
            s_scr[d] = s0 * jnp.exp(gtot) + upd
        return carry

    lax.fori_loop(0, nc, body, 0)
    ys_ref[...] = y_scr[0] + y_scr[1]
    for d in range(2):
        sf_ref[0, d, 0] = s_scr[d, 0:B_HEAD, 0:B_HEAD]
        sf_ref[0, d, 1] = s_scr[d, B_HEAD:PAIR, B_HEAD:PAIR]


def _rwkv_scan(pre, s0, n_seq, seq_len, passes):
    t = pre.shape[1]
    sblk = pl.BlockSpec((1, 2, 2, B_HEAD, B_HEAD), lambda b, p: (b, 0, p, 0, 0))
    return pl.pallas_call(
        functools.partial(_scan_kernel, seq_len=seq_len, passes=passes),
        grid=(n_seq, N_PAIRS),
        in_specs=[pl.BlockSpec((9, seq_len, PAIR), lambda b, p: (0, b, p)), sblk],
        out_specs=[pl.BlockSpec((seq_len, PAIR), lambda b, p: (b, p)), sblk],
        out_shape=[
            jax.ShapeDtypeStruct((t, B_WIDTH), F32),
            jax.ShapeDtypeStruct((n_seq, 2, B_HEADS, B_HEAD, B_HEAD), F32),
        ],
        scratch_shapes=[pltpu.VMEM((2, PAIR, PAIR), F32), pltpu.VMEM((2, seq_len, PAIR), F32)],
        compiler_params=_cparams("parallel", "parallel"),
        name="rwkv_scan",
    )(pre, s0)


def _even_out_kernel(x_ref, mod_ref, gains_ref, ya_ref, ys_ref, post_ref, lnx_ref, bdm_ref, w_ref,
                     o_ref):
    bdm = bdm_ref[...]
    ys = ys_ref[...]
    dlt = ys - _segsum(ys, bdm)
    yn = dlt * lax.rsqrt(_segsum(dlt * dlt, bdm) + LNX_EPS)
    yb = (yn * lnx_ref[0:1] + lnx_ref[1:2] + post_ref[1]) * post_ref[0]
    m = _dot(ya_ref[...], w_ref[0:A_WIDTH, :]) + _dot(yb, w_ref[A_WIDTH:, :])
    mod = mod_ref[0]
    o_ref[...] = x_ref[...] + mod[2:3] * (_rms(m) * gains_ref[1:2])


def _even_out(x, mod, gains, ya, ys, post, lnx, bd_mean, w, seq_len, mod_row0, tm):
    t = x.shape[0]
    row = lambda i: (i, 0)
    return pl.pallas_call(
        _even_out_kernel,
        grid=(t // tm,),
        in_specs=[
            pl.BlockSpec((tm, D_MODEL), row),
            _mod_spec(tm, seq_len, mod_row0),
            pl.BlockSpec((4, D_MODEL), lambda i: (0, 0)),
            pl.BlockSpec((tm, A_WIDTH), row),
            pl.BlockSpec((tm, B_WIDTH), row),
            pl.BlockSpec((2, tm, B_WIDTH), lambda i: (0, i, 0)),
            pl.BlockSpec((2, B_WIDTH), lambda i: (0, 0)),
            pl.BlockSpec((B_WIDTH, B_WIDTH), lambda i: (0, 0)),
            pl.BlockSpec((D_MODEL, D_MODEL), lambda i: (0, 0)),
        ],
        out_specs=pl.BlockSpec((tm, D_MODEL), row),
        out_shape=jax.ShapeDtypeStruct((t, D_MODEL), F32),
        compiler_params=_cparams("parallel"),
        name="even_out",
    )(x, mod, gains, ya, ys, post, lnx, bd_mean, w)


def _odd_kernel(x_ref, mod_ref, gains_ref, csc_ref, cl_ref, sl_ref, w_ref, o_ref):
    mod = mod_ref[0]
    x = x_ref[...]
    h = _modnorm(x, gains_ref[0:1], mod[1:2], mod[0:1]).astype(BF16)
    csc = csc_ref[...]
    gd = C_GROUP_DIM
    m = None
    for g in range(C_GROUPS):
        t = jnp.dot(h[:, g * gd:(g + 1) * gd], csc, preferred_element_type=F32)
        f = (jnp.dot(cl_ref[...], t[:, 0:gd].astype(BF16), preferred_element_type=F32)
             - jnp.dot(sl_ref[...], t[:, gd:].astype(BF16), preferred_element_type=F32))
        part = jnp.dot(f.astype(BF16), w_ref[g * gd:(g + 1) * gd, :], preferred_element_type=F32)
        m = part if m is None else m + part
    o_ref[...] = x + mod[2:3] * (_rms(m) * gains_ref[1:2])


def _dft_tables(n):
    idx = np.arange(n, dtype=np.int64)
    ang = 2.0 * np.pi * ((idx[:, None] * idx[None, :]) % n).astype(np.float64) / n
    s = 1.0 / math.sqrt(n)
    return np.cos(ang) * s, np.sin(ang) * s


def _odd_mixer(x, mod, gains, w, n_seq, seq_len, mod_row0):
    cc, sc = _dft_tables(C_GROUP_DIM)
    cl, sl = _dft_tables(seq_len)
    csc, cl, sl = (jnp.asarray(a, F32).astype(BF16) for a in (np.concatenate([cc, sc], axis=1), cl, sl))
    full = lambda shape: pl.BlockSpec(shape, lambda i: (0,) * len(shape))
    row = lambda i: (i, 0)
    return pl.pallas_call(
        _odd_kernel,
        grid=(n_seq,),
        in_specs=[
            pl.BlockSpec((seq_len, D_MODEL), row),
            _mod_spec(seq_len, seq_len, mod_row0),
            full((4, D_MODEL)), full((C_GROUP_DIM, 2 * C_GROUP_DIM)),
            full((seq_len, seq_len)), full((seq_len, seq_len)), full((D_MODEL, D_MODEL)),
        ],
        out_specs=pl.BlockSpec((seq_len, D_MODEL), row),
        out_shape=jax.ShapeDtypeStruct(x.shape, F32),
        compiler_params=_cparams("parallel"),
        name="odd_mixer",
    )(x, mod, gains, csc, cl, sl, w)


def _ffn_kernel(x_ref, mod_ref, gains_ref, wu_ref, wg_ref, cw_ref, cb_ref, wo_ref, o_ref,
                h_scr, acc_scr, *, seq_len):
    j = pl.program_id(1)
    mod = mod_ref[0]

    @pl.when(j == 0)
    def _():
        h_scr[...] = _modnorm(x_ref[...], gains_ref[2:3], mod[4:5], mod[3:4]).astype(BF16)
        acc_scr[...] = jnp.zeros_like(acc_scr)

    h = h_scr[...]
    u = jnp.dot(h, wu_ref[...], preferred_element_type=F32)
    g = jnp.dot(h, wg_ref[...], preferred_element_type=F32)
    tm = g.shape[0]
    pos = lax.broadcasted_iota(jnp.int32, (tm, 1), 0) % seq_len
    prev = jnp.where(pos == 0, 0.0, pltpu.roll(g, 1, 0))
    nxt = jnp.where(pos == seq_len - 1, 0.0, pltpu.roll(g, tm - 1, 0))
    gc = prev * cw_ref[0:1] + g * cw_ref[1:2] + nxt * cw_ref[2:3] + cb_ref[...]
    act = gc * _sigmoid(gc) * u
    acc_scr[...] += jnp.dot(act.astype(BF16), wo_ref[...], preferred_element_type=F32)

    @pl.when(j == pl.num_programs(1) - 1)
    def _():
        o_ref[...] = x_ref[...] + mod[5:6] * (_rms(acc_scr[...]) * gains_ref[3:4])


def _conv_ffn(x, mod, gains, w_in, conv_w, conv_b, w_out, seq_len, mod_row0, tm):
    t = x.shape[0]
    nj = D_FF // FF_CHUNK
    return pl.pallas_call(
        functools.partial(_ffn_kernel, seq_len=seq_len),
        grid=(t // tm, nj),
        in_specs=[
            pl.BlockSpec((tm, D_MODEL), lambda i, j: (i, 0)),
            _mod_spec(tm, seq_len, mod_row0),
            pl.BlockSpec((4, D_MODEL), lambda i, j: (0, 0)),
            pl.BlockSpec((D_MODEL, FF_CHUNK), lambda i, j: (0, j)),
            pl.BlockSpec((D_MODEL, FF_CHUNK), lambda i, j: (0, nj + j)),
            pl.BlockSpec((3, FF_CHUNK), lambda i, j: (0, j)),
            pl.BlockSpec((1, FF_CHUNK), lambda i, j: (0, j)),
            pl.BlockSpec((FF_CHUNK, D_MODEL), lambda i, j: (j, 0)),
        ],
        out_specs=pl.BlockSpec((tm, D_MODEL), lambda i, j: (i, 0)),
        out_shape=jax.ShapeDtypeStruct((t, D_MODEL), F32),
        scratch_shapes=[pltpu.VMEM((tm, D_MODEL), BF16), pltpu.VMEM((tm, D_MODEL), F32)],
        compiler_params=_cparams("parallel", "arbitrary"),
        name="conv_ffn",
    )(x, mod, gains, w_in, w_in, conv_w, conv_b, w_out)


def _rope_tables(seq_len):
    rows = seq_len // GRID_W
    row = jnp.repeat(jnp.arange(rows, dtype=F32), GRID_W)
    col = jnp.tile(jnp.arange(GRID_W, dtype=F32), rows)
    n_freq = A_HALF // 4
    inv = ROPE_BASE ** (-jnp.arange(n_freq, dtype=F32) / n_freq)
    ang = jnp.concatenate([row[:, None] * inv, col[:, None] * inv], axis=-1)
    cos_t = jnp.tile(jnp.repeat(jnp.cos(ang), 2, axis=1), (1, 2))
    sin_t = jnp.tile(jnp.repeat(jnp.sin(ang), 2, axis=1), (1, 2))
    sign = jnp.where(jnp.arange(A_HEAD_DIM) % 2 == 0, -1.0, 1.0).astype(F32)
    return cos_t, sin_t * sign


def _pad_lora(w):
    z = jnp.zeros_like(w[0])
    return jnp.stack([jnp.concatenate([w[0], z], axis=0), jnp.concatenate([z, w[1]], axis=0)])


def _block_diag(value):
    head = np.arange(B_WIDTH) // B_HEAD
    return jnp.asarray(np.where(head[:, None] == head[None, :], value, 0.0), BF16)


def _tile_rows(seq_len):
    return max(seq_len, 512)


def kernel(x_prompt, x_sample, cache_k, cache_v, state_wkv, c, c_ctx, w_ada, b_ada, norm_gains,
           w_in_even, w_out_even, diff_lambda, diff_subln, rwkv_shift_mu, rwkv_w0, rwkv_w2, rwkv_a0,
           rwkv_a2, rwkv_g2, rwkv_kvec, rwkv_lnx, w_out_odd, w_ffn_in, ffn_conv, ffn_conv_b,
           w_ffn_out):
    n_ctx, l_ctx, _ = x_prompt.shape
    n_lat, l_lat, _ = x_sample.shape
    assert 1 + n_lat <= MOD_ROWS
    cvec = jnp.concatenate(
        [c_ctx[None, :], c, jnp.zeros((MOD_ROWS - 1 - n_lat, D_MODEL), F32)], axis=0)
    mods = _modulation(cvec, w_ada, b_ada).reshape(DEPTH, MOD_ROWS, 6, D_MODEL)

    w_in_even_b = w_in_even.astype(BF16)
    w_out_even_b = w_out_even.astype(BF16)
    w_out_odd_b = w_out_odd.astype(BF16)
    w_ffn_in_b = w_ffn_in.astype(BF16)
    w_ffn_out_b = w_ffn_out.astype(BF16)
    bd_ones = _block_diag(1.0)
    bd_mean = _block_diag(1.0 / B_HEAD)
    cos_t, sin_t = _rope_tables(l_lat)

    def run_group(x, n_seq, seq_len, mod_row0, latent):
        tm = _tile_rows(seq_len)
        ctx_out = None
        for l in range(DEPTH):
            mod = mods[l]
            gains = norm_gains[l]
            if l % 2 == 0:
                e = l // 2
                lam_init = 0.8 - 0.6 * math.exp(-0.3 * l)
                q, k, v, fb = _even_in(x, mod, gains, w_in_even_b[e], seq_len, mod_row0, 256)
                if latent:
                    past = cache_k.shape[2]
                    ctx = (cache_k[:, e].reshape(n_seq * past, A_WIDTH),
                           cache_v[:, e].reshape(n_seq * past, A_WIDTH), cos_t, sin_t)
                    s0 = state_wkv[:, e]
                else:
                    ctx = None
                    s0 = jnp.zeros((n_seq, 2, B_HEADS, B_HEAD, B_HEAD), F32)
                ya = _attention(q, k, v, diff_lambda[e], diff_subln[e][None, :], lam_init,
                                n_seq, seq_len, ctx)
                prep = {
                    "mu": rwkv_shift_mu[e], "w0": rwkv_w0[e], "w2": _pad_lora(rwkv_w2[e]).astype(BF16),
                    "a0": rwkv_a0[e], "a2": _pad_lora(rwkv_a2[e]).astype(BF16),
                    "g2": rwkv_g2[e].astype(BF16), "kvec": rwkv_kvec[e], "bd_ones": bd_ones,
                }
                pre, post = _rwkv_prep(fb, prep, seq_len)
                ys, s_fin = _rwkv_scan(pre, s0, n_seq, seq_len, passes=3)
                x = _even_out(x, mod, gains, ya, ys, post, rwkv_lnx[e], bd_mean, w_out_even_b[e],
                              seq_len, mod_row0, 256)
                if not latent:
                    ctx_out = (k, v, s_fin)
            else:
                x = _odd_mixer(x, mod, gains, w_out_odd_b[l // 2], n_seq, seq_len, mod_row0)
            x = _conv_ffn(x, mod, gains, w_ffn_in_b[l], ffn_conv[l], ffn_conv_b[l][None, :],
                          w_ffn_out_b[l], seq_len, mod_row0, tm)
        return x, ctx_out

    y_ctx, (k_new, v_new, s_new) = run_group(
        x_prompt.reshape(n_ctx * l_ctx, D_MODEL), n_ctx, l_ctx, 0, False)
    y_lat, _ = run_group(x_sample.reshape(n_lat * l_lat, D_MODEL), n_lat, l_lat, 1, True)

    n_even = (DEPTH + 1) // 2
    assert n_even == 1
    return (
        y_ctx.reshape(n_ctx, l_ctx, D_MODEL),
        y_lat.reshape(n_lat, l_lat, D_MODEL),
        k_new.reshape(n_ctx, n_even, l_ctx, A_HEADS, A_HEAD_DIM),
        v_new.reshape(n_ctx, n_even, l_ctx, A_HEADS, A_HEAD_DIM),
        s_new.reshape(n_ctx, n_even, 2, B_HEADS, B_HEAD, B_HEAD),
    )
```

```python
import functools
import math

import numpy as np
import jax
import jax.numpy as jnp
from jax import lax
from jax.experimental import pallas as pl
from jax.experimental.pallas import tpu as pltpu

F32 = jnp.float32
BF16 = jnp.bfloat16

D_MODEL = 1024
DEPTH = 2
GRID_W = 64
A_WIDTH = D_MODEL // 2
A_HEADS = 4
A_HEAD_DIM = A_WIDTH // A_HEADS
A_HALF = A_HEAD_DIM // 2
B_WIDTH = D_MODEL - A_WIDTH
B_HEAD = 64
B_HEADS = B_WIDTH // B_HEAD
LORA_W = 64
LORA_A = 64
LORA_G = 128
IN_B = 3 * B_WIDTH + 2 * LORA_W + 2 * LORA_A + LORA_G
IN_EVEN = 3 * A_WIDTH + IN_B
C_GROUPS = 8
C_GROUP_DIM = D_MODEL // C_GROUPS
D_FF = 2816
ROPE_BASE = 10000.0
RMS_EPS = 1e-6
LNX_EPS = 64e-5

LANES = 128
SUBLANES = 8
PAIR = 2 * B_HEAD
N_PAIRS = B_HEADS // 2
CHUNK = 64
FF_CHUNK = 256
VMEM_LIMIT = 48 * 1024 * 1024
MOD_ROWS = 8


def _cparams(*sem):
    return pltpu.CompilerParams(dimension_semantics=sem, vmem_limit_bytes=VMEM_LIMIT)


def _sigmoid(x):
    return 1.0 / (1.0 + jnp.exp(-x))


def _dot(a, b):
    return jnp.dot(a.astype(BF16), b.astype(BF16), preferred_element_type=F32)


_NN = (((1,), (0,)), ((), ()))
_NT = (((1,), (1,)), ((), ()))
_TN = (((0,), (0,)), ((), ()))


def _split_bf16(x, n):
    parts = []
    rem = x
    for i in range(n):
        p = rem.astype(BF16)
        parts.append(p)
        if i + 1 < n:
            rem = rem - p.astype(F32)
    return parts


def _mm(a, b, dims, passes):
    if passes == 1:
        return lax.dot_general(a.astype(BF16), b.astype(BF16), dims, preferred_element_type=F32)
    a_hi, a_lo = _split_bf16(a, 2)
    b_hi, b_lo = _split_bf16(b, 2)
    dg = functools.partial(lax.dot_general, dimension_numbers=dims, preferred_element_type=F32)
    return dg(a_hi, b_hi) + (dg(a_hi, b_lo) + dg(a_lo, b_hi))


def _segsum(x, bd):
    acc = None
    for p in _split_bf16(x, 3):
        t = jnp.dot(p, bd, preferred_element_type=F32)
        acc = t if acc is None else acc + t
    return acc


def _rms(x):
    return x * lax.rsqrt(jnp.mean(x * x, axis=-1, keepdims=True) + RMS_EPS)


def _modnorm(x, gain, scale, shift):
    return _rms(x) * gain * (1.0 + scale) + shift


def _mod_kernel(c_ref, w_ref, b_ref, o_ref):
    c = c_ref[...]
    s = c * _sigmoid(c)
    o_ref[0] = _dot(s, w_ref[0]) + b_ref[0]


def _modulation(cvec, w_ada, b_ada):
    tn = 1536
    n = 6 * D_MODEL
    return pl.pallas_call(
        _mod_kernel,
        grid=(DEPTH, n // tn),
        in_specs=[
            pl.BlockSpec((MOD_ROWS, D_MODEL), lambda l, j: (0, 0)),
            pl.BlockSpec((1, D_MODEL, tn), lambda l, j: (l, 0, j)),
            pl.BlockSpec((1, 1, tn), lambda l, j: (l, 0, j)),
        ],
        out_specs=pl.BlockSpec((1, MOD_ROWS, tn), lambda l, j: (l, 0, j)),
        out_shape=jax.ShapeDtypeStruct((DEPTH, MOD_ROWS, n), F32),
        compiler_params=_cparams("parallel", "parallel"),
        name="modulation",
    )(cvec, w_ada, b_ada.reshape(DEPTH, 1, n))


def _mod_spec(tm, seq_len, mod_row0):
    if mod_row0 == 0:
        return pl.BlockSpec((1, 6, D_MODEL), lambda i, *_: (0, 0, 0))
    return pl.BlockSpec((1, 6, D_MODEL), lambda i, *_: (mod_row0 + (i * tm) // seq_len, 0, 0))


def _even_in_kernel(x_ref, mod_ref, gains_ref, w_ref, q_ref, k_ref, v_ref, fb_ref):
    mod = mod_ref[0]
    h = _modnorm(x_ref[...], gains_ref[0:1], mod[1:2], mod[0:1]).astype(BF16)
    a = A_WIDTH
    q_ref[...] = jnp.dot(h, w_ref[:, 0:a], preferred_element_type=F32)
    k_ref[...] = jnp.dot(h, w_ref[:, a:2 * a], preferred_element_type=F32)
    v_ref[...] = jnp.dot(h, w_ref[:, 2 * a:3 * a], preferred_element_type=F32)
    fb_ref[...] = jnp.dot(h, w_ref[:, 3 * a:], preferred_element_type=F32)


def _even_in(x, mod, gains, w, seq_len, mod_row0, tm):
    t = x.shape[0]
    row = lambda i: (i, 0)
    return pl.pallas_call(
        _even_in_kernel,
        grid=(t // tm,),
        in_specs=[
            pl.BlockSpec((tm, D_MODEL), row),
            _mod_spec(tm, seq_len, mod_row0),
            pl.BlockSpec((4, D_MODEL), lambda i: (0, 0)),
            pl.BlockSpec((D_MODEL, IN_EVEN), lambda i: (0, 0)),
        ],
        out_specs=[
            pl.BlockSpec((tm, A_WIDTH), row),
            pl.BlockSpec((tm, A_WIDTH), row),
            pl.BlockSpec((tm, A_WIDTH), row),
            pl.BlockSpec((tm, IN_B), row),
        ],
        out_shape=[
            jax.ShapeDtypeStruct((t, A_WIDTH), F32),
            jax.ShapeDtypeStruct((t, A_WIDTH), F32),
            jax.ShapeDtypeStruct((t, A_WIDTH), F32),
            jax.ShapeDtypeStruct((t, IN_B), F32),
        ],
        compiler_params=_cparams("parallel"),
        name="even_in",
    )(x, mod, gains, w)


def _rope(x, cos_t, sin_t):
    lane = lax.broadcasted_iota(jnp.int32, x.shape, 1)
    nxt = pltpu.roll(x, LANES - 1, 1)
    prv = pltpu.roll(x, 1, 1)
    partner = jnp.where(lane % 2 == 0, nxt, prv)
    return x * cos_t + partner * sin_t


def _attn_kernel(*refs, has_ctx, lam_init):
    if has_ctx:
        (lam_ref, sub_ref, q_ref, k_ref, v_ref, ck_ref, cv_ref,
         cq_ref, sq_ref, ckt_ref, skt_ref, o_ref) = refs
    else:
        lam_ref, sub_ref, q_ref, k_ref, v_ref, o_ref = refs
    lp = lam_ref[...]
    l1 = jnp.sum(lp[0:1] * lp[1:2], axis=-1, keepdims=True)
    l2 = jnp.sum(lp[2:3] * lp[3:4], axis=-1, keepdims=True)
    lam = jnp.exp(l1) - jnp.exp(l2) + lam_init
    q = q_ref[...]
    k = k_ref[...]
    v = v_ref[...].astype(BF16)
    if has_ctx:
        q = _rope(q, cq_ref[...], sq_ref[...])
        k = _rope(k, ckt_ref[...], skt_ref[...])
        ck = ck_ref[...].astype(BF16)
        cv = cv_ref[...].astype(BF16)
    k = k.astype(BF16)
    lane = lax.broadcasted_iota(jnp.int32, q.shape, 1)
    scale = A_HALF ** -0.5
    probs = []
    for m in range(2):
        qm = jnp.where((lane < A_HALF) == (m == 0), q, 0.0).astype(BF16)
        s = lax.dot_general(qm, k, _NT, preferred_element_type=F32) * scale
        mx = jnp.max(s, axis=-1, keepdims=True)
        if has_ctx:
            sc = lax.dot_general(qm, ck, _NT, preferred_element_type=F32) * scale
            mx = jnp.maximum(mx, jnp.max(sc, axis=-1, keepdims=True))
            ec = jnp.exp(sc - mx)
        e = jnp.exp(s - mx)
        z = jnp.sum(e, axis=-1, keepdims=True)
        if has_ctx:
            z = z + jnp.sum(ec, axis=-1, keepdims=True)
        inv = 1.0 / z
        probs.append((e * inv, ec * inv if has_ctx else None))
    w = probs[0][0] - lam * probs[1][0]
    o = jnp.dot(w.astype(BF16), v, preferred_element_type=F32)
    if has_ctx:
        wc = probs[0][1] - lam * probs[1][1]
        o = o + jnp.dot(wc.astype(BF16), cv, preferred_element_type=F32)
    o_ref[...] = _rms(o) * sub_ref[...] * (1.0 - lam_init)


def _attention(q, k, v, lam_p, subln, lam_init, n_seq, seq_len, ctx=None, tq=256):
    t = q.shape[0]
    hd = A_HEAD_DIM
    out_shape = jax.ShapeDtypeStruct((t, A_WIDTH), F32)
    small = [pl.BlockSpec((4, A_HALF), lambda *_: (0, 0)), pl.BlockSpec((1, hd), lambda *_: (0, 0))]
    kern = functools.partial(_attn_kernel, has_ctx=ctx is not None, lam_init=lam_init)
    if ctx is None:
        blk = pl.BlockSpec((seq_len, hd), lambda b, h: (b, h))
        return pl.pallas_call(
            kern, grid=(n_seq, A_HEADS),
            in_specs=small + [blk, blk, blk],
            out_specs=blk, out_shape=out_shape,
            compiler_params=_cparams("parallel", "parallel"), name="diff_attn_ctx",
        )(lam_p, subln, q, k, v)
    ck, cv, cos_t, sin_t = ctx
    past = ck.shape[0] // n_seq
    nq = seq_len // tq
    qblk = pl.BlockSpec((tq, hd), lambda b, h, i: (b * nq + i, h))
    kblk = pl.BlockSpec((seq_len, hd), lambda b, h, i: (b, h))
    cblk = pl.BlockSpec((past, hd), lambda b, h, i: (b, h))
    tq_blk = pl.BlockSpec((tq, hd), lambda b, h, i: (i, 0))
    tk_blk = pl.BlockSpec((seq_len, hd), lambda b, h, i: (0, 0))
    return pl.pallas_call(
        kern, grid=(n_seq, A_HEADS, nq),
        in_specs=small + [qblk, kblk, kblk, cblk, cblk, tq_blk, tq_blk, tk_blk, tk_blk],
        out_specs=qblk, out_shape=out_shape,
        compiler_params=_cparams("parallel", "parallel", "parallel"), name="diff_attn_lat",
    )(lam_p, subln, q, k, v, ck, cv, cos_t, sin_t, cos_t, sin_t)


def _shift_rows(f, halo_prev, halo_next, pos, seq_len):
    tm = f.shape[0]
    local = lax.broadcasted_iota(jnp.int32, (tm, 1), 0)
    prev = jnp.where(local == 0, halo_prev, pltpu.roll(f, 1, 0))
    nxt = jnp.where(local == tm - 1, halo_next, pltpu.roll(f, tm - 1, 0))
    prev = jnp.where(pos == 0, 0.0, prev)
    nxt = jnp.where(pos == seq_len - 1, 0.0, nxt)
    return prev, nxt


def _prep_kernel(fb_ref, fp_ref, fn_ref, mu_ref, w0_ref, w2_ref, a0_ref, a2_ref, g2_ref, kvec_ref,
                 bd_ref, scan_ref, post_ref, *, seq_len):
    f = fb_ref[...]
    tm = f.shape[0]
    pos = (pl.program_id(0) * tm + lax.broadcasted_iota(jnp.int32, (tm, 1), 0)) % seq_len
    prev, nxt = _shift_rows(f, fp_ref[SUBLANES - 1:SUBLANES, :], fn_ref[0:1, :], pos, seq_len)
    f = f + mu_ref[0:1] * (prev - f) + mu_ref[1:2] * (nxt - f)
    bw = B_WIDTH
    r = f[:, 0:bw]
    k = f[:, bw:2 * bw]
    v = f[:, 2 * bw:3 * bw]
    off = 3 * bw
    wd = jnp.tanh(f[:, off:off + 2 * LORA_W])
    ad = f[:, off + 2 * LORA_W:off + 2 * LORA_W + 2 * LORA_A]
    gd = f[:, off + 2 * LORA_W + 2 * LORA_A:]
    bd = bd_ref[...]
    kv = kvec_ref[...]
    g = _dot(_sigmoid(gd), g2_ref[...])
    kk = k * kv[0:1]
    kk = kk * lax.rsqrt(_segsum(kk * kk, bd) + 1e-12)
    scan_ref[0] = r
    scan_ref[1] = v
    scan_ref[2] = kk
    ksum = None
    for d in range(2):
        w_raw = w0_ref[d:d + 1] + _dot(wd, w2_ref[d])
        scan_ref[7 + d] = -math.exp(-0.5) * _sigmoid(w_raw)
        a = _sigmoid(a0_ref[d:d + 1] + _dot(ad, a2_ref[d]))
        kd = k * (1.0 + (a - 1.0) * kv[1:2])
        scan_ref[3 + d] = kd
        scan_ref[5 + d] = kk * a
        ksum = kd if ksum is None else ksum + kd
    post_ref[0] = g
    post_ref[1] = _segsum(r * ksum * kv[2:3], bd) * v


def _rwkv_prep(fb, p, seq_len, tm=256):
    t = fb.shape[0]
    nh = tm // SUBLANES
    last = t // SUBLANES - 1
    full = lambda shape: pl.BlockSpec(shape, lambda i: (0,) * len(shape))
    return pl.pallas_call(
        functools.partial(_prep_kernel, seq_len=seq_len),
        grid=(t // tm,),
        in_specs=[
            pl.BlockSpec((tm, IN_B), lambda i: (i, 0)),
            pl.BlockSpec((SUBLANES, IN_B), lambda i: (jnp.maximum(i * nh - 1, 0), 0)),
            pl.BlockSpec((SUBLANES, IN_B), lambda i: (jnp.minimum((i + 1) * nh, last), 0)),
            full((2, IN_B)), full((2, B_WIDTH)), full((2, 2 * LORA_W, B_WIDTH)),
            full((2, B_WIDTH)), full((2, 2 * LORA_A, B_WIDTH)), full((LORA_G, B_WIDTH)),
            full((3, B_WIDTH)), full((B_WIDTH, B_WIDTH)),
        ],
        out_specs=[
            pl.BlockSpec((9, tm, B_WIDTH), lambda i: (0, i, 0)),
            pl.BlockSpec((2, tm, B_WIDTH), lambda i: (0, i, 0)),
        ],
        out_shape=[
            jax.ShapeDtypeStruct((9, t, B_WIDTH), F32),
            jax.ShapeDtypeStruct((2, t, B_WIDTH), F32),
        ],
        compiler_params=_cparams("parallel"),
        name="rwkv_prep",
    )(fb, fb, fb, p["mu"], p["w0"], p["w2"], p["a0"], p["a2"], p["g2"], p["kvec"], p["bd_ones"])


def _cumsum_rows(x, reverse):
    n = x.shape[0]
    ridx = lax.broadcasted_iota(jnp.int32, x.shape, 0)
    s = 1
    while s < n:
        if reverse:
            x = x + jnp.where(ridx < n - s, pltpu.roll(x, n - s, 0), 0.0)
        else:
            x = x + jnp.where(ridx >= s, pltpu.roll(x, s, 0), 0.0)
        s *= 2
    return x


def _round_robin(gens):
    gens = list(gens)
    while gens:
        alive = []
        for gen in gens:
            try:
                next(gen)
                alive.append(gen)
            except StopIteration:
                pass
        gens = alive


def _scan_kernel(pre_ref, s0_ref, ys_ref, sf_ref, r1_scr, m1_scr, d0_scr, gam_scr, y_scr, s_scr, *,
                 seq_len, passes, inv_passes, carry_passes, unroll):
    nc = seq_len // CHUNK
    c2 = 2 * CHUNK
    lane = lax.broadcasted_iota(jnp.int32, (CHUNK, PAIR), 1)
    head0 = lane < B_HEAD
    row = lax.broadcasted_iota(jnp.int32, (c2, c2), 0)
    col = lax.broadcasted_iota(jnp.int32, (c2, c2), 1)
    same = (row // CHUNK) == (col // CHUNK)
    eye = jnp.where(row == col, 1.0, 0.0)
    xor_rc = jnp.bitwise_xor(row, col)
    mm = functools.partial(_mm, passes=passes)
    mm_inv = functools.partial(_mm, passes=inv_passes)
    mm_carry = functools.partial(_mm, passes=carry_passes)

    def stack(x):
        return jnp.concatenate([jnp.where(head0, x, 0.0), jnp.where(head0, 0.0, x)], axis=0)

    def fold(x):
        return x[0:CHUNK] + x[CHUNK:c2]

    def chunk_rows(cc):
        return pl.ds(pl.multiple_of(cc * CHUNK, CHUNK), CHUNK)

    def chunk_local(cc, d):
        rows = chunk_rows(cc)
        r = pre_ref[0, rows, :]
        v = pre_ref[1, rows, :]
        kk = pre_ref[2, rows, :]
        k = pre_ref[3 + d, rows, :]
        b = pre_ref[5 + d, rows, :]
        lw = pre_ref[7 + d, rows, :]
        incl = same & ((col <= row) if d == 0 else (col >= row))
        strict = same & ((col < row) if d == 0 else (col > row))
        g = _cumsum_rows(lw, reverse=(d == 1))
        gp = g - lw
        gtot = g[CHUNK - 1:CHUNK, :] if d == 0 else g[0:1, :]
        gm = 0.5 * gtot
        kx = k * jnp.exp(gm - g)
        bx = b * jnp.exp(gm - g)
        lhs = jnp.concatenate([stack(r * jnp.exp(g - gm)), stack(kk * jnp.exp(gp - gm))], axis=0)
        rhs = jnp.concatenate([kx, kx, bx, bx], axis=0)
        a_all = mm(lhs, rhs, _NT)
        yield
        a_rk = jnp.where(incl, a_all[0:c2, 0:c2], 0.0)
        a_rb = jnp.where(incl, a_all[0:c2, c2:2 * c2], 0.0)
        a_kk = jnp.where(strict, a_all[c2:2 * c2, 0:c2], 0.0)
        n_mat = jnp.where(strict, a_all[c2:2 * c2, c2:2 * c2], 0.0)
        vs = stack(v)
        akv = mm(a_kk, vs, _NN)
        y0 = mm(a_rk, vs, _NN)
        lower = (col < row) if d == 0 else (col > row)
        p_inv = eye - jnp.where(lower & (xor_rc == 1), n_mat, 0.0)
        yield
        s = 2
        while s < CHUNK:
            c_s = jnp.where(lower & (xor_rc >= s) & (xor_rc < 2 * s), n_mat, 0.0)
            t = mm_inv(c_s, p_inv, _NN)
            yield
            p_inv = p_inv - mm_inv(p_inv, t, _NN)
            yield
            s *= 2
        x = jnp.concatenate([stack(kk * jnp.exp(gp)), akv], axis=1)
        w12 = mm(p_inv, x, _NN)
        yield
        arb_w = mm(a_rb, w12, _NN)
        et = jnp.exp(gtot - g)
        bes = stack(b * et)
        d0_scr[d, cc] = mm(jnp.concatenate([vs, -w12[:, PAIR:]], axis=0),
                           jnp.concatenate([stack(k * et), bes], axis=0), _TN)
        m1_scr[d, cc] = mm(w12[:, 0:PAIR], bes, _TN)
        yield
        r1_scr[d, cc] = stack(r * jnp.exp(g)) - arb_w[:, 0:PAIR]
        y_scr[d, rows, :] = fold(y0 - arb_w[:, PAIR:])
        gam_scr[d, cc] = jnp.broadcast_to(jnp.exp(gtot), (SUBLANES, PAIR))

    def local_body(grp, carry):
        _round_robin(chunk_local(grp * unroll + u, d) for u in range(unroll) for d in range(2))
        return carry

    lax.fori_loop(0, nc // unroll, local_body, 0)

    for d in range(2):
        s_scr[d] = jnp.zeros((PAIR, PAIR), F32)
        s_scr[d, 0:B_HEAD, 0:B_HEAD] = s0_ref[0, d, 0]
        s_scr[d, B_HEAD:PAIR, B_HEAD:PAIR] = s0_ref[0, d, 1]

    def carry_body(c, carry):
        for d in range(2):
            cc = c if d == 0 else nc - 1 - c
            rows = chunk_rows(cc)
            s0 = s_scr[d]
            y_scr[d, rows, :] += fold(mm_carry(r1_scr[d, cc], s0, _NT))
            s_scr[d] = s0 * gam_scr[d, cc, 0:1, :] - mm_carry(s0, m1_scr[d, cc], _NN) + d0_scr[d, cc]
        return carry

    lax.fori_loop(0, nc, carry_body, 0)
    ys_ref[...] = y_scr[0] + y_scr[1]
    for d in range(2):
        sf_ref[0, d, 0] = s_scr[d, 0:B_HEAD, 0:B_HEAD]
        sf_ref[0, d, 1] = s_scr[d, B_HEAD:PAIR, B_HEAD:PAIR]


def _rwkv_scan(pre, s0, n_seq, seq_len, passes=1, inv_passes=1, carry_passes=1, unroll=4):
    t = pre.shape[1]
    nc = seq_len // CHUNK
    sblk = pl.BlockSpec((1, 2, 2, B_HEAD, B_HEAD), lambda b, p: (b, 0, p, 0, 0))
    mat = pltpu.VMEM((2, nc, PAIR, PAIR), F32)
    return pl.pallas_call(
        functools.partial(_scan_kernel, seq_len=seq_len, passes=passes, inv_passes=inv_passes,
                          carry_passes=carry_passes, unroll=unroll),
        grid=(n_seq, N_PAIRS),
        in_specs=[pl.BlockSpec((9, seq_len, PAIR), lambda b, p: (0, b, p)), sblk],
        out_specs=[pl.BlockSpec((seq_len, PAIR), lambda b, p: (b, p)), sblk],
        out_shape=[
            jax.ShapeDtypeStruct((t, B_WIDTH), F32),
            jax.ShapeDtypeStruct((n_seq, 2, B_HEADS, B_HEAD, B_HEAD), F32),
        ],
        scratch_shapes=[mat, mat, mat, pltpu.VMEM((2, nc, SUBLANES, PAIR), F32),
                        pltpu.VMEM((2, seq_len, PAIR), F32), pltpu.VMEM((2, PAIR, PAIR), F32)],
        compiler_params=_cparams("parallel", "parallel"),
        name="rwkv_scan",
    )(pre, s0)


def _even_out_kernel(x_ref, mod_ref, gains_ref, ya_ref, ys_ref, post_ref, lnx_ref, bdm_ref, w_ref,
                     o_ref):
    bdm = bdm_ref[...]
    ys = ys_ref[...]
    dlt = ys - _segsum(ys, bdm)
    yn = dlt * lax.rsqrt(_segsum(dlt * dlt, bdm) + LNX_EPS)
    yb = (yn * lnx_ref[0:1] + lnx_ref[1:2] + post_ref[1]) * post_ref[0]
    m = _dot(ya_ref[...], w_ref[0:A_WIDTH, :]) + _dot(yb, w_ref[A_WIDTH:, :])
    mod = mod_ref[0]
    o_ref[...] = x_ref[...] + mod[2:3] * (_rms(m) * gains_ref[1:2])


def _even_out(x, mod, gains, ya, ys, post, lnx, bd_mean, w, seq_len, mod_row0, tm):
    t = x.shape[0]
    row = lambda i: (i, 0)
    return pl.pallas_call(
        _even_out_kernel,
        grid=(t // tm,),
        in_specs=[
            pl.BlockSpec((tm, D_MODEL), row),
            _mod_spec(tm, seq_len, mod_row0),
            pl.BlockSpec((4, D_MODEL), lambda i: (0, 0)),
            pl.BlockSpec((tm, A_WIDTH), row),
            pl.BlockSpec((tm, B_WIDTH), row),
            pl.BlockSpec((2, tm, B_WIDTH), lambda i: (0, i, 0)),
            pl.BlockSpec((2, B_WIDTH), lambda i: (0, 0)),
            pl.BlockSpec((B_WIDTH, B_WIDTH), lambda i: (0, 0)),
            pl.BlockSpec((D_MODEL, D_MODEL), lambda i: (0, 0)),
        ],
        out_specs=pl.BlockSpec((tm, D_MODEL), row),
        out_shape=jax.ShapeDtypeStruct((t, D_MODEL), F32),
        compiler_params=_cparams("parallel"),
        name="even_out",
    )(x, mod, gains, ya, ys, post, lnx, bd_mean, w)


def _odd_kernel(x_ref, mod_ref, gains_ref, csc_ref, cl_ref, sl_ref, w_ref, o_ref):
    mod = mod_ref[0]
    x = x_ref[...]
    h = _modnorm(x, gains_ref[0:1], mod[1:2], mod[0:1]).astype(BF16)
    csc = csc_ref[...]
    gd = C_GROUP_DIM
    m = None
    for g in range(C_GROUPS):
        t = jnp.dot(h[:, g * gd:(g + 1) * gd], csc, preferred_element_type=F32)
        f = (jnp.dot(cl_ref[...], t[:, 0:gd].astype(BF16), preferred_element_type=F32)
             - jnp.dot(sl_ref[...], t[:, gd:].astype(BF16), preferred_element_type=F32))
        part = jnp.dot(f.astype(BF16), w_ref[g * gd:(g + 1) * gd, :], preferred_element_type=F32)
        m = part if m is None else m + part
    o_ref[...] = x + mod[2:3] * (_rms(m) * gains_ref[1:2])


def _dft_tables(n):
    idx = np.arange(n, dtype=np.int64)
    ang = 2.0 * np.pi * ((idx[:, None] * idx[None, :]) % n).astype(np.float64) / n
    s = 1.0 / math.sqrt(n)
    return np.cos(ang) * s, np.sin(ang) * s


def _odd_mixer(x, mod, gains, w, n_seq, seq_len, mod_row0):
    cc, sc = _dft_tables(C_GROUP_DIM)
    cl, sl = _dft_tables(seq_len)
    csc, cl, sl = (jnp.asarray(a, F32).astype(BF16) for a in (np.concatenate([cc, sc], axis=1), cl, sl))
    full = lambda shape: pl.BlockSpec(shape, lambda i: (0,) * len(shape))
    row = lambda i: (i, 0)
    return pl.pallas_call(
        _odd_kernel,
        grid=(n_seq,),
        in_specs=[
            pl.BlockSpec((seq_len, D_MODEL), row),
            _mod_spec(seq_len, seq_len, mod_row0),
            full((4, D_MODEL)), full((C_GROUP_DIM, 2 * C_GROUP_DIM)),
            full((seq_len, seq_len)), full((seq_len, seq_len)), full((D_MODEL, D_MODEL)),
        ],
        out_specs=pl.BlockSpec((seq_len, D_MODEL), row),
        out_shape=jax.ShapeDtypeStruct(x.shape, F32),
        compiler_params=_cparams("parallel"),
        name="odd_mixer",
    )(x, mod, gains, csc, cl, sl, w)


def _ffn_kernel(x_ref, mod_ref, gains_ref, wu_ref, wg_ref, cw_ref, cb_ref, wo_ref, o_ref,
                h_scr, acc_scr, *, seq_len):
    j = pl.program_id(1)
    mod = mod_ref[0]

    @pl.when(j == 0)
    def _():
        h_scr[...] = _modnorm(x_ref[...], gains_ref[2:3], mod[4:5], mod[3:4]).astype(BF16)
        acc_scr[...] = jnp.zeros_like(acc_scr)

    h = h_scr[...]
    u = jnp.dot(h, wu_ref[...], preferred_element_type=F32)
    g = jnp.dot(h, wg_ref[...], preferred_element_type=F32)
    tm = g.shape[0]
    pos = lax.broadcasted_iota(jnp.int32, (tm, 1), 0) % seq_len
    prev = jnp.where(pos == 0, 0.0, pltpu.roll(g, 1, 0))
    nxt = jnp.where(pos == seq_len - 1, 0.0, pltpu.roll(g, tm - 1, 0))
    gc = prev * cw_ref[0:1] + g * cw_ref[1:2] + nxt * cw_ref[2:3] + cb_ref[...]
    act = gc * _sigmoid(gc) * u
    acc_scr[...] += jnp.dot(act.astype(BF16), wo_ref[...], preferred_element_type=F32)

    @pl.when(j == pl.num_programs(1) - 1)
    def _():
        o_ref[...] = x_ref[...] + mod[5:6] * (_rms(acc_scr[...]) * gains_ref[3:4])


def _conv_ffn(x, mod, gains, w_in, conv_w, conv_b, w_out, seq_len, mod_row0, tm):
    t = x.shape[0]
    nj = D_FF // FF_CHUNK
    return pl.pallas_call(
        functools.partial(_ffn_kernel, seq_len=seq_len),
        grid=(t // tm, nj),
        in_specs=[
            pl.BlockSpec((tm, D_MODEL), lambda i, j: (i, 0)),
            _mod_spec(tm, seq_len, mod_row0),
            pl.BlockSpec((4, D_MODEL), lambda i, j: (0, 0)),
            pl.BlockSpec((D_MODEL, FF_CHUNK), lambda i, j: (0, j)),
            pl.BlockSpec((D_MODEL, FF_CHUNK), lambda i, j: (0, nj + j)),
            pl.BlockSpec((3, FF_CHUNK), lambda i, j: (0, j)),
            pl.BlockSpec((1, FF_CHUNK), lambda i, j: (0, j)),
            pl.BlockSpec((FF_CHUNK, D_MODEL), lambda i, j: (j, 0)),
        ],
        out_specs=pl.BlockSpec((tm, D_MODEL), lambda i, j: (i, 0)),
        out_shape=jax.ShapeDtypeStruct((t, D_MODEL), F32),
        scratch_shapes=[pltpu.VMEM((tm, D_MODEL), BF16), pltpu.VMEM((tm, D_MODEL), F32)],
        compiler_params=_cparams("parallel", "arbitrary"),
        name="conv_ffn",
    )(x, mod, gains, w_in, w_in, conv_w, conv_b, w_out)


def _rope_tables(seq_len):
    rows = seq_len // GRID_W
    row = jnp.repeat(jnp.arange(rows, dtype=F32), GRID_W)
    col = jnp.tile(jnp.arange(GRID_W, dtype=F32), rows)
    n_freq = A_HALF // 4
    inv = ROPE_BASE ** (-jnp.arange(n_freq, dtype=F32) / n_freq)
    ang = jnp.concatenate([row[:, None] * inv, col[:, None] * inv], axis=-1)
    cos_t = jnp.tile(jnp.repeat(jnp.cos(ang), 2, axis=1), (1, 2))
    sin_t = jnp.tile(jnp.repeat(jnp.sin(ang), 2, axis=1), (1, 2))
    sign = jnp.where(jnp.arange(A_HEAD_DIM) % 2 == 0, -1.0, 1.0).astype(F32)
    return cos_t, sin_t * sign


def _pad_lora(w):
    z = jnp.zeros_like(w[0])
    return jnp.stack([jnp.concatenate([w[0], z], axis=0), jnp.concatenate([z, w[1]], axis=0)])


def _block_diag(value):
    head = np.arange(B_WIDTH) // B_HEAD
    return jnp.asarray(np.where(head[:, None] == head[None, :], value, 0.0), BF16)


def _tile_rows(seq_len):
    return max(seq_len, 512)


def kernel(x_prompt, x_sample, cache_k, cache_v, state_wkv, c, c_ctx, w_ada, b_ada, norm_gains,
           w_in_even, w_out_even, diff_lambda, diff_subln, rwkv_shift_mu, rwkv_w0, rwkv_w2, rwkv_a0,
           rwkv_a2, rwkv_g2, rwkv_kvec, rwkv_lnx, w_out_odd, w_ffn_in, ffn_conv, ffn_conv_b,
           w_ffn_out):
    n_ctx, l_ctx, _ = x_prompt.shape
    n_lat, l_lat, _ = x_sample.shape
    assert 1 + n_lat <= MOD_ROWS
    cvec = jnp.concatenate(
        [c_ctx[None, :], c, jnp.zeros((MOD_ROWS - 1 - n_lat, D_MODEL), F32)], axis=0)
    mods = _modulation(cvec, w_ada, b_ada).reshape(DEPTH, MOD_ROWS, 6, D_MODEL)

    w_in_even_b = w_in_even.astype(BF16)
    w_out_even_b = w_out_even.astype(BF16)
    w_out_odd_b = w_out_odd.astype(BF16)
    w_ffn_in_b = w_ffn_in.astype(BF16)
    w_ffn_out_b = w_ffn_out.astype(BF16)
    bd_ones = _block_diag(1.0)
    bd_mean = _block_diag(1.0 / B_HEAD)
    cos_t, sin_t = _rope_tables(l_lat)

    def run_group(x, n_seq, seq_len, mod_row0, latent):
        tm = _tile_rows(seq_len)
        ctx_out = None
        for l in range(DEPTH):
            mod = mods[l]
            gains = norm_gains[l]
            if l % 2 == 0:
                e = l // 2
                lam_init = 0.8 - 0.6 * math.exp(-0.3 * l)
                q, k, v, fb = _even_in(x, mod, gains, w_in_even_b[e], seq_len, mod_row0, 256)
                if latent:
                    past = cache_k.shape[2]
                    ctx = (cache_k[:, e].reshape(n_seq * past, A_WIDTH),
                           cache_v[:, e].reshape(n_seq * past, A_WIDTH), cos_t, sin_t)
                    s0 = state_wkv[:, e]
                else:
                    ctx = None
                    s0 = jnp.zeros((n_seq, 2, B_HEADS, B_HEAD, B_HEAD), F32)
                ya = _attention(q, k, v, diff_lambda[e], diff_subln[e][None, :], lam_init,
                                n_seq, seq_len, ctx)
                prep = {
                    "mu": rwkv_shift_mu[e], "w0": rwkv_w0[e], "w2": _pad_lora(rwkv_w2[e]).astype(BF16),
                    "a0": rwkv_a0[e], "a2": _pad_lora(rwkv_a2[e]).astype(BF16),
                    "g2": rwkv_g2[e].astype(BF16), "kvec": rwkv_kvec[e], "bd_ones": bd_ones,
                }
                pre, post = _rwkv_prep(fb, prep, seq_len)
                ys, s_fin = _rwkv_scan(pre, s0, n_seq, seq_len)
                x = _even_out(x, mod, gains, ya, ys, post, rwkv_lnx[e], bd_mean, w_out_even_b[e],
                              seq_len, mod_row0, 256)
                if not latent:
                    ctx_out = (k, v, s_fin)
            else:
                x = _odd_mixer(x, mod, gains, w_out_odd_b[l // 2], n_seq, seq_len, mod_row0)
            x = _conv_ffn(x, mod, gains, w_ffn_in_b[l], ffn_conv[l], ffn_conv_b[l][None, :],
                          w_ffn_out_b[l], seq_len, mod_row0, tm)
        return x, ctx_out

    y_ctx, (k_new, v_new, s_new) = run_group(
        x_prompt.reshape(n_ctx * l_ctx, D_MODEL), n_ctx, l_ctx, 0, False)
    y_lat, _ = run_group(x_sample.reshape(n_lat * l_lat, D_MODEL), n_lat, l_lat, 1, True)

    n_even = (DEPTH + 1) // 2
    assert n_even == 1
    return (
        y_ctx.reshape(n_ctx, l_ctx, D_MODEL),
        y_lat.reshape(n_lat, l_lat, D_MODEL),
        k_new.reshape(n_ctx, n_even, l_ctx, A_HEADS, A_HEAD_DIM),
        v_new.reshape(n_ctx, n_even, l_ctx, A_HEADS, A_HEAD_DIM),
        s_new.reshape(n_ctx, n_even, 2, B_HEADS, B_HEAD, B_HEAD),
    )
```

```python
import functools
import math

import numpy as np
import jax
import jax.numpy as jnp
from jax import lax
from jax.experimental import pallas as pl
from jax.experimental.pallas import tpu as pltpu

F32 = jnp.float32
BF16 = jnp.bfloat16

D_MODEL = 1024
DEPTH = 2
GRID_W = 64
A_WIDTH = D_MODEL // 2
A_HEADS = 4
A_HEAD_DIM = A_WIDTH // A_HEADS
A_HALF = A_HEAD_DIM // 2
B_WIDTH = D_MODEL - A_WIDTH
B_HEAD = 64
B_HEADS = B_WIDTH // B_HEAD
LORA_W = 64
LORA_A = 64
LORA_G = 128
IN_B = 3 * B_WIDTH + 2 * LORA_W + 2 * LORA_A + LORA_G
IN_EVEN = 3 * A_WIDTH + IN_B
C_GROUPS = 8
C_GROUP_DIM = D_MODEL // C_GROUPS
D_FF = 2816
ROPE_BASE = 10000.0
RMS_EPS = 1e-6
LNX_EPS = 64e-5

LANES = 128
SUBLANES = 8
PAIR = 2 * B_HEAD
N_PAIRS = B_HEADS // 2
CHUNK = 64
FF_CHUNK = 256
FF_ROWS = 256
ATTN_Q_ROWS = 256
ODD_ROWS = 512
ODD_COLS = 256
VMEM_LIMIT = 48 * 1024 * 1024
MOD_ROWS = 8


def _cparams(*sem):
    return pltpu.CompilerParams(dimension_semantics=sem, vmem_limit_bytes=VMEM_LIMIT)


def _sigmoid(x):
    return 1.0 / (1.0 + jnp.exp(-x))


def _silu(x):
    return x * (0.5 + 0.5 * jnp.tanh(0.5 * x))


def _dot(a, b):
    return jnp.dot(a.astype(BF16), b.astype(BF16), preferred_element_type=F32)


_NN = (((1,), (0,)), ((), ()))
_NT = (((1,), (1,)), ((), ()))
_TN = (((0,), (0,)), ((), ()))


def _split_bf16(x, n):
    parts = []
    rem = x
    for i in range(n):
        p = rem.astype(BF16)
        parts.append(p)
        if i + 1 < n:
            rem = rem - p.astype(F32)
    return parts


def _mm(a, b, dims, passes):
    if passes == 1:
        return lax.dot_general(a.astype(BF16), b.astype(BF16), dims, preferred_element_type=F32)
    a_hi, a_lo = _split_bf16(a, 2)
    b_hi, b_lo = _split_bf16(b, 2)
    dg = functools.partial(lax.dot_general, dimension_numbers=dims, preferred_element_type=F32)
    return dg(a_hi, b_hi) + (dg(a_hi, b_lo) + dg(a_lo, b_hi))


def _segsum(x, bd):
    acc = None
    for p in _split_bf16(x, 2):
        t = jnp.dot(p, bd, preferred_element_type=F32)
        acc = t if acc is None else acc + t
    return acc


def _rms(x):
    return x * lax.rsqrt(jnp.mean(x * x, axis=-1, keepdims=True) + RMS_EPS)


def _modnorm(x, gain, scale, shift):
    return _rms(x) * gain * (1.0 + scale) + shift


def _mod_kernel(c_ref, w_ref, b_ref, o_ref):
    c = c_ref[...]
    s = c * _sigmoid(c)
    o_ref[0] = _dot(s, w_ref[0]) + b_ref[0]


def _modulation(cvec, w_ada, b_ada):
    tn = 1536
    n = 6 * D_MODEL
    return pl.pallas_call(
        _mod_kernel,
        grid=(DEPTH, n // tn),
        in_specs=[
            pl.BlockSpec((MOD_ROWS, D_MODEL), lambda l, j: (0, 0)),
            pl.BlockSpec((1, D_MODEL, tn), lambda l, j: (l, 0, j)),
            pl.BlockSpec((1, 1, tn), lambda l, j: (l, 0, j)),
        ],
        out_specs=pl.BlockSpec((1, MOD_ROWS, tn), lambda l, j: (l, 0, j)),
        out_shape=jax.ShapeDtypeStruct((DEPTH, MOD_ROWS, n), F32),
        compiler_params=_cparams("parallel", "parallel"),
        name="modulation",
    )(cvec, w_ada, b_ada.reshape(DEPTH, 1, n))


def _mod_spec(tm, seq_len, mod_row0):
    if mod_row0 == 0:
        return pl.BlockSpec((1, 6, D_MODEL), lambda i, *_: (0, 0, 0))
    return pl.BlockSpec((1, 6, D_MODEL), lambda i, *_: (mod_row0 + (i * tm) // seq_len, 0, 0))


def _rope(x, cos_t, sin_t):
    lane = lax.broadcasted_iota(jnp.int32, x.shape, 1)
    nxt = pltpu.roll(x, LANES - 1, 1)
    prv = pltpu.roll(x, 1, 1)
    partner = jnp.where(lane % 2 == 0, nxt, prv)
    return x * cos_t + partner * sin_t


def _even_in_kernel(*refs, rope):
    if rope:
        x_ref, mod_ref, gains_ref, w_ref, cos_ref, sin_ref, q_ref, k_ref, v_ref, fb_ref = refs
    else:
        x_ref, mod_ref, gains_ref, w_ref, q_ref, k_ref, v_ref, fb_ref = refs
    mod = mod_ref[0]
    h = _modnorm(x_ref[...], gains_ref[0:1], mod[1:2], mod[0:1]).astype(BF16)
    a = A_WIDTH
    hd = A_HEAD_DIM
    for out_ref, base in ((q_ref, 0), (k_ref, a)):
        y = jnp.dot(h, w_ref[:, base:base + a], preferred_element_type=F32)
        if rope:
            for hh in range(A_HEADS):
                cols = slice(hh * hd, (hh + 1) * hd)
                out_ref[:, cols] = _rope(y[:, cols], cos_ref[...], sin_ref[...])
        else:
            out_ref[...] = y
    v_ref[...] = jnp.dot(h, w_ref[:, 2 * a:3 * a], preferred_element_type=F32)
    fb_ref[...] = jnp.dot(h, w_ref[:, 3 * a:], preferred_element_type=F32)


def _even_in(x, mod, gains, w, seq_len, mod_row0, tm, rope=None):
    t = x.shape[0]
    row = lambda i: (i, 0)
    operands = [x, mod, gains, w]
    in_specs = [
        pl.BlockSpec((tm, D_MODEL), row),
        _mod_spec(tm, seq_len, mod_row0),
        pl.BlockSpec((4, D_MODEL), lambda i: (0, 0)),
        pl.BlockSpec((D_MODEL, IN_EVEN), lambda i: (0, 0)),
    ]
    if rope is not None:
        tab = pl.BlockSpec((tm, A_HEAD_DIM), lambda i: (i % (seq_len // tm), 0))
        operands += list(rope)
        in_specs += [tab, tab]
    return pl.pallas_call(
        functools.partial(_even_in_kernel, rope=rope is not None),
        grid=(t // tm,),
        in_specs=in_specs,
        out_specs=[
            pl.BlockSpec((tm, A_WIDTH), row),
            pl.BlockSpec((tm, A_WIDTH), row),
            pl.BlockSpec((tm, A_WIDTH), row),
            pl.BlockSpec((tm, IN_B), row),
        ],
        out_shape=[
            jax.ShapeDtypeStruct((t, A_WIDTH), F32),
            jax.ShapeDtypeStruct((t, A_WIDTH), F32),
            jax.ShapeDtypeStruct((t, A_WIDTH), F32),
            jax.ShapeDtypeStruct((t, IN_B), F32),
        ],
        compiler_params=_cparams("parallel"),
        name="even_in",
    )(*operands)


def _attn_kernel(*refs, has_ctx, lam_init):
    if has_ctx:
        lam_ref, sub_ref, q_ref, k_ref, v_ref, ck_ref, cv_ref, o_ref = refs
    else:
        lam_ref, sub_ref, q_ref, k_ref, v_ref, o_ref = refs
    lp = lam_ref[...]
    l1 = jnp.sum(lp[0:1] * lp[1:2], axis=-1, keepdims=True)
    l2 = jnp.sum(lp[2:3] * lp[3:4], axis=-1, keepdims=True)
    lam = jnp.exp(l1) - jnp.exp(l2) + lam_init
    hd = A_HEAD_DIM
    lane = lax.broadcasted_iota(jnp.int32, (q_ref.shape[0], hd), 1)
    scale = A_HALF ** -0.5

    def head(h):
        cols = pl.ds(h * hd, hd)
        q = q_ref[:, cols]
        k = k_ref[:, cols].astype(BF16)
        if has_ctx:
            ck = ck_ref[:, cols].astype(BF16)
        scores = []
        for m in range(2):
            qm = jnp.where((lane < A_HALF) == (m == 0), q, 0.0).astype(BF16)
            s = lax.dot_general(qm, k, _NT, preferred_element_type=F32) * scale
            sc = lax.dot_general(qm, ck, _NT, preferred_element_type=F32) * scale if has_ctx else None
            scores.append((s, sc))
        yield
        probs = []
        for s, sc in scores:
            mx = jnp.max(s, axis=-1, keepdims=True)
            if has_ctx:
                mx = jnp.maximum(mx, jnp.max(sc, axis=-1, keepdims=True))
                ec = jnp.exp(sc - mx)
            e = jnp.exp(s - mx)
            z = jnp.sum(e, axis=-1, keepdims=True)
            if has_ctx:
                z = z + jnp.sum(ec, axis=-1, keepdims=True)
            inv = 1.0 / z
            probs.append((e * inv, ec * inv if has_ctx else None))
        w = (probs[0][0] - lam * probs[1][0]).astype(BF16)
        if has_ctx:
            wc = (probs[0][1] - lam * probs[1][1]).astype(BF16)
        yield
        o = jnp.dot(w, v_ref[:, cols].astype(BF16), preferred_element_type=F32)
        if has_ctx:
            o = o + jnp.dot(wc, cv_ref[:, cols].astype(BF16), preferred_element_type=F32)
        yield
        o_ref[:, cols] = _rms(o) * sub_ref[...] * (1.0 - lam_init)

    _round_robin(head(h) for h in range(A_HEADS))


def _attention(q, k, v, lam_p, subln, lam_init, n_seq, seq_len, ctx=None):
    t = q.shape[0]
    tq = min(seq_len, ATTN_Q_ROWS if ctx is None else ATTN_Q_ROWS // 2)
    nq = seq_len // tq
    small = [pl.BlockSpec((4, A_HALF), lambda b, i: (0, 0)),
             pl.BlockSpec((1, A_HEAD_DIM), lambda b, i: (0, 0))]
    qblk = pl.BlockSpec((tq, A_WIDTH), lambda b, i: (b * nq + i, 0))
    kblk = pl.BlockSpec((seq_len, A_WIDTH), lambda b, i: (b, 0))
    operands = [lam_p, subln, q, k, v]
    in_specs = small + [qblk, kblk, kblk]
    if ctx is not None:
        ck, cv = ctx
        cblk = pl.BlockSpec((ck.shape[0] // n_seq, A_WIDTH), lambda b, i: (b, 0))
        operands += [ck, cv]
        in_specs += [cblk, cblk]
    return pl.pallas_call(
        functools.partial(_attn_kernel, has_ctx=ctx is not None, lam_init=lam_init),
        grid=(n_seq, nq),
        in_specs=in_specs,
        out_specs=qblk,
        out_shape=jax.ShapeDtypeStruct((t, A_WIDTH), F32),
        compiler_params=_cparams("parallel", "parallel"),
        name="diff_attn",
    )(*operands)


def _shift_rows(f, halo_prev, halo_next, start, seq_len):
    tm = f.shape[0]
    local = lax.broadcasted_iota(jnp.int32, (tm, 1), 0)
    halo_prev = jnp.where(start == 0, 0.0, halo_prev)
    halo_next = jnp.where(start + tm == seq_len, 0.0, halo_next)
    prev = jnp.where(local == 0, halo_prev, pltpu.roll(f, 1, 0))
    nxt = jnp.where(local == tm - 1, halo_next, pltpu.roll(f, tm - 1, 0))
    return prev, nxt


def _prep_kernel(fb_ref, fp_ref, fn_ref, mu_ref, w0_ref, w2_ref, a0_ref, a2_ref, g2_ref, kvec_ref,
                 bd_ref, scan_ref, post_ref, *, seq_len):
    f = fb_ref[...]
    tm = f.shape[0]
    start = (pl.program_id(0) * tm) % seq_len
    prev, nxt = _shift_rows(f, fp_ref[SUBLANES - 1:SUBLANES, :], fn_ref[0:1, :], start, seq_len)
    f = f * (1.0 - mu_ref[0:1] - mu_ref[1:2]) + mu_ref[0:1] * prev + mu_ref[1:2] * nxt
    bw = B_WIDTH
    r = f[:, 0:bw]
    k = f[:, bw:2 * bw]
    v = f[:, 2 * bw:3 * bw]
    off = 3 * bw
    wd = jnp.tanh(f[:, off:off + 2 * LORA_W])
    ad = f[:, off + 2 * LORA_W:off + 2 * LORA_W + 2 * LORA_A]
    gd = f[:, off + 2 * LORA_W + 2 * LORA_A:]
    bd = bd_ref[...]
    kv = kvec_ref[...]
    g = _dot(_sigmoid(gd), g2_ref[...])
    kk = k * kv[0:1]
    kk = kk * lax.rsqrt(_segsum(kk * kk, bd) + 1e-12)
    scan_ref[0] = r
    scan_ref[1] = v
    scan_ref[2] = kk
    ksum = None
    for d in range(2):
        w_raw = w0_ref[d:d + 1] + _dot(wd, w2_ref[d])
        scan_ref[7 + d] = -math.exp(-0.5) * _sigmoid(w_raw)
        a = _sigmoid(a0_ref[d:d + 1] + _dot(ad, a2_ref[d]))
        kd = k * (1.0 + (a - 1.0) * kv[1:2])
        scan_ref[3 + d] = kd
        scan_ref[5 + d] = kk * a
        ksum = kd if ksum is None else ksum + kd
    post_ref[0] = g
    post_ref[1] = _segsum(r * ksum * kv[2:3], bd) * v


def _rwkv_prep(fb, p, seq_len, tm=256):
    t = fb.shape[0]
    assert seq_len % tm == 0
    nh = tm // SUBLANES
    last = t // SUBLANES - 1
    full = lambda shape: pl.BlockSpec(shape, lambda i: (0,) * len(shape))
    return pl.pallas_call(
        functools.partial(_prep_kernel, seq_len=seq_len),
        grid=(t // tm,),
        in_specs=[
            pl.BlockSpec((tm, IN_B), lambda i: (i, 0)),
            pl.BlockSpec((SUBLANES, IN_B), lambda i: (jnp.maximum(i * nh - 1, 0), 0)),
            pl.BlockSpec((SUBLANES, IN_B), lambda i: (jnp.minimum((i + 1) * nh, last), 0)),
            full((2, IN_B)), full((2, B_WIDTH)), full((2, 2 * LORA_W, B_WIDTH)),
            full((2, B_WIDTH)), full((2, 2 * LORA_A, B_WIDTH)), full((LORA_G, B_WIDTH)),
            full((3, B_WIDTH)), full((B_WIDTH, B_WIDTH)),
        ],
        out_specs=[
            pl.BlockSpec((9, tm, B_WIDTH), lambda i: (0, i, 0)),
            pl.BlockSpec((2, tm, B_WIDTH), lambda i: (0, i, 0)),
        ],
        out_shape=[
            jax.ShapeDtypeStruct((9, t, B_WIDTH), F32),
            jax.ShapeDtypeStruct((2, t, B_WIDTH), F32),
        ],
        compiler_params=_cparams("parallel"),
        name="rwkv_prep",
    )(fb, fb, fb, p["mu"], p["w0"], p["w2"], p["a0"], p["a2"], p["g2"], p["kvec"], p["bd_ones"])


def _cumsum_rows(x, reverse):
    n = x.shape[0]
    ridx = lax.broadcasted_iota(jnp.int32, x.shape, 0)
    s = 1
    while s < n:
        if reverse:
            x = x + jnp.where(ridx < n - s, pltpu.roll(x, n - s, 0), 0.0)
        else:
            x = x + jnp.where(ridx >= s, pltpu.roll(x, s, 0), 0.0)
        s *= 2
    return x


def _round_robin(gens):
    gens = list(gens)
    while gens:
        alive = []
        for gen in gens:
            try:
                next(gen)
                alive.append(gen)
            except StopIteration:
                pass
        gens = alive


def _scan_kernel(*refs, seq_len, has_state, passes, inv_passes, carry_passes, unroll):
    if has_state:
        pre_ref, s0_ref, ys_ref, sf_ref, r1_scr, m1_scr, d0_scr, gam_scr, y_scr, s_scr = refs
    else:
        pre_ref, ys_ref, sf_ref, r1_scr, m1_scr, d0_scr, gam_scr, y_scr, s_scr = refs
    nc = seq_len // CHUNK
    c2 = 2 * CHUNK
    lane = lax.broadcasted_iota(jnp.int32, (CHUNK, PAIR), 1)
    head0 = lane < B_HEAD
    row = lax.broadcasted_iota(jnp.int32, (c2, c2), 0)
    col = lax.broadcasted_iota(jnp.int32, (c2, c2), 1)
    same = (row // CHUNK) == (col // CHUNK)
    eye = jnp.where(row == col, 1.0, 0.0)
    xor_rc = jnp.bitwise_xor(row, col)
    mm = functools.partial(_mm, passes=passes)
    mm_inv = functools.partial(_mm, passes=inv_passes)
    mm_carry = functools.partial(_mm, passes=carry_passes)

    def stack(x):
        return jnp.concatenate([jnp.where(head0, x, 0.0), jnp.where(head0, 0.0, x)], axis=0)

    def fold(x):
        return x[0:CHUNK] + x[CHUNK:c2]

    def chunk_rows(cc):
        return pl.ds(pl.multiple_of(cc * CHUNK, CHUNK), CHUNK)

    def chunk_local(cc, d):
        rows = chunk_rows(cc)
        r = pre_ref[0, rows, :]
        v = pre_ref[1, rows, :]
        kk = pre_ref[2, rows, :]
        k = pre_ref[3 + d, rows, :]
        b = pre_ref[5 + d, rows, :]
        lw = pre_ref[7 + d, rows, :]
        incl = same & ((col <= row) if d == 0 else (col >= row))
        strict = same & ((col < row) if d == 0 else (col > row))
        g = _cumsum_rows(lw, reverse=(d == 1))
        gp = g - lw
        gtot = g[CHUNK - 1:CHUNK, :] if d == 0 else g[0:1, :]
        gm = 0.5 * gtot
        kx = k * jnp.exp(gm - g)
        bx = b * jnp.exp(gm - g)
        lhs = jnp.concatenate([stack(r * jnp.exp(g - gm)), stack(kk * jnp.exp(gp - gm))], axis=0)
        rhs = jnp.concatenate([kx, kx, bx, bx], axis=0)
        a_all = mm(lhs, rhs, _NT)
        yield
        a_rk = jnp.where(incl, a_all[0:c2, 0:c2], 0.0)
        a_rb = jnp.where(incl, a_all[0:c2, c2:2 * c2], 0.0)
        a_kk = jnp.where(strict, a_all[c2:2 * c2, 0:c2], 0.0)
        n_mat = jnp.where(strict, a_all[c2:2 * c2, c2:2 * c2], 0.0)
        vs = stack(v)
        akv = mm(a_kk, vs, _NN)
        y0 = mm(a_rk, vs, _NN)
        lower = (col < row) if d == 0 else (col > row)
        p_inv = eye - jnp.where(lower & (xor_rc == 1), n_mat, 0.0)
        yield
        s = 2
        while s < CHUNK:
            c_s = jnp.where(lower & (xor_rc >= s) & (xor_rc < 2 * s), n_mat, 0.0)
            t = mm_inv(c_s, p_inv, _NN)
            yield
            p_inv = p_inv - mm_inv(p_inv, t, _NN)
            yield
            s *= 2
        x = jnp.concatenate([stack(kk * jnp.exp(gp)), akv], axis=1)
        w12 = mm(p_inv, x, _NN)
        yield
        arb_w = mm(a_rb, w12, _NN)
        et = jnp.exp(gtot - g)
        bes = stack(b * et)
        d0_scr[d, cc] = mm(jnp.concatenate([vs, -w12[:, PAIR:]], axis=0),
                           jnp.concatenate([stack(k * et), bes], axis=0), _TN)
        m1_scr[d, cc] = mm(w12[:, 0:PAIR], bes, _TN)
        yield
        r1_scr[d, cc] = stack(r * jnp.exp(g)) - arb_w[:, 0:PAIR]
        y_scr[d, rows, :] = fold(y0 - arb_w[:, PAIR:])
        gam_scr[d, cc] = jnp.broadcast_to(jnp.exp(gtot), (SUBLANES, PAIR))

    def local_body(grp, carry):
        _round_robin(chunk_local(grp * unroll + u, d) for u in range(unroll) for d in range(2))
        return carry

    lax.fori_loop(0, nc // unroll, local_body, 0)

    for d in range(2):
        s_scr[d] = jnp.zeros((PAIR, PAIR), F32)
        if has_state:
            s_scr[d, 0:B_HEAD, 0:B_HEAD] = s0_ref[0, d, 0]
            s_scr[d, B_HEAD:PAIR, B_HEAD:PAIR] = s0_ref[0, d, 1]

    def carry_body(c, carry):
        for d in range(2):
            cc = c if d == 0 else nc - 1 - c
            rows = chunk_rows(cc)
            s0 = s_scr[d]
            y_scr[d, rows, :] += fold(mm_carry(r1_scr[d, cc], s0, _NT))
            s_scr[d] = s0 * gam_scr[d, cc, 0:1, :] - mm_carry(s0, m1_scr[d, cc], _NN) + d0_scr[d, cc]
        return carry

    lax.fori_loop(0, nc, carry_body, 0)
    ys_ref[...] = y_scr[0] + y_scr[1]
    for d in range(2):
        sf_ref[0, d, 0] = s_scr[d, 0:B_HEAD, 0:B_HEAD]
        sf_ref[0, d, 1] = s_scr[d, B_HEAD:PAIR, B_HEAD:PAIR]


def _rwkv_scan(pre, s0, n_seq, seq_len, passes=1, inv_passes=1, carry_passes=1, unroll=4):
    t = pre.shape[1]
    nc = seq_len // CHUNK
    sblk = pl.BlockSpec((1, 2, 2, B_HEAD, B_HEAD), lambda b, p: (b, 0, p, 0, 0))
    mat = pltpu.VMEM((2, nc, PAIR, PAIR), F32)
    return pl.pallas_call(
        functools.partial(_scan_kernel, seq_len=seq_len, has_state=s0 is not None, passes=passes,
                          inv_passes=inv_passes, carry_passes=carry_passes, unroll=unroll),
        grid=(n_seq, N_PAIRS),
        in_specs=[pl.BlockSpec((9, seq_len, PAIR), lambda b, p: (0, b, p))] + [sblk] * (s0 is not None),
        out_specs=[pl.BlockSpec((seq_len, PAIR), lambda b, p: (b, p)), sblk],
        out_shape=[
            jax.ShapeDtypeStruct((t, B_WIDTH), F32),
            jax.ShapeDtypeStruct((n_seq, 2, B_HEADS, B_HEAD, B_HEAD), F32),
        ],
        scratch_shapes=[mat, mat, mat, pltpu.VMEM((2, nc, SUBLANES, PAIR), F32),
                        pltpu.VMEM((2, seq_len, PAIR), F32), pltpu.VMEM((2, PAIR, PAIR), F32)],
        compiler_params=_cparams("parallel", "parallel"),
        name="rwkv_scan",
    )(*((pre,) if s0 is None else (pre, s0)))


def _even_out_kernel(x_ref, mod_ref, gains_ref, ya_ref, ys_ref, post_ref, lnx_ref, bdm_ref, w_ref,
                     o_ref):
    bdm = bdm_ref[...]
    ys = ys_ref[...]
    dlt = ys - _segsum(ys, bdm)
    yn = dlt * lax.rsqrt(_segsum(dlt * dlt, bdm) + LNX_EPS)
    yb = (yn * lnx_ref[0:1] + lnx_ref[1:2] + post_ref[1]) * post_ref[0]
    m = _dot(ya_ref[...], w_ref[0:A_WIDTH, :]) + _dot(yb, w_ref[A_WIDTH:, :])
    mod = mod_ref[0]
    o_ref[...] = x_ref[...] + mod[2:3] * (_rms(m) * gains_ref[1:2])


def _even_out(x, mod, gains, ya, ys, post, lnx, bd_mean, w, seq_len, mod_row0, tm):
    t = x.shape[0]
    row = lambda i: (i, 0)
    return pl.pallas_call(
        _even_out_kernel,
        grid=(t // tm,),
        in_specs=[
            pl.BlockSpec((tm, D_MODEL), row),
            _mod_spec(tm, seq_len, mod_row0),
            pl.BlockSpec((4, D_MODEL), lambda i: (0, 0)),
            pl.BlockSpec((tm, A_WIDTH), row),
            pl.BlockSpec((tm, B_WIDTH), row),
            pl.BlockSpec((2, tm, B_WIDTH), lambda i: (0, i, 0)),
            pl.BlockSpec((2, B_WIDTH), lambda i: (0, 0)),
            pl.BlockSpec((B_WIDTH, B_WIDTH), lambda i: (0, 0)),
            pl.BlockSpec((D_MODEL, D_MODEL), lambda i: (0, 0)),
        ],
        out_specs=pl.BlockSpec((tm, D_MODEL), row),
        out_shape=jax.ShapeDtypeStruct((t, D_MODEL), F32),
        compiler_params=_cparams("parallel"),
        name="even_out",
    )(x, mod, gains, ya, ys, post, lnx, bd_mean, w)


def _odd_kernel(x_ref, mod_ref, gains_ref, csc_ref, cl_ref, sl_ref, w_ref, o_ref,
                tc_scr, ts_scr, m_scr, *, seq_len):
    mod = mod_ref[0]
    gd = C_GROUP_DIM
    n_seq = x_ref.shape[0] // seq_len
    csc = csc_ref[...]

    for s in range(n_seq):
        rows = pl.ds(s * seq_len, seq_len)
        h = _modnorm(x_ref[rows, :], gains_ref[0:1], mod[1:2], mod[0:1]).astype(BF16)
        for g in range(C_GROUPS):
            t = jnp.dot(h[:, g * gd:(g + 1) * gd], csc, preferred_element_type=F32)
            tc_scr[rows, g * gd:(g + 1) * gd] = t[:, 0:gd].astype(BF16)
            ts_scr[rows, g * gd:(g + 1) * gd] = t[:, gd:].astype(BF16)

    def position_dft(s, cb):
        rows = pl.ds(s * seq_len, seq_len)
        cols = pl.ds(cb * ODD_COLS, ODD_COLS)
        f = (jnp.dot(cl_ref[...], tc_scr[rows, cols], preferred_element_type=F32)
             - jnp.dot(sl_ref[...], ts_scr[rows, cols], preferred_element_type=F32))
        yield
        part = jnp.dot(f.astype(BF16), w_ref[cols, :], preferred_element_type=F32)
        yield
        if cb == 0:
            m_scr[rows, :] = part
        else:
            m_scr[rows, :] += part

    _round_robin(position_dft(s, cb) for s in range(n_seq) for cb in range(D_MODEL // ODD_COLS))
    o_ref[...] = x_ref[...] + mod[2:3] * (_rms(m_scr[...]) * gains_ref[1:2])


def _dft_tables(n):
    idx = np.arange(n, dtype=np.int64)
    ang = 2.0 * np.pi * ((idx[:, None] * idx[None, :]) % n).astype(np.float64) / n
    s = 1.0 / math.sqrt(n)
    return np.cos(ang) * s, np.sin(ang) * s


def _odd_mixer(x, mod, gains, w, n_seq, seq_len, mod_row0):
    cc, sc = _dft_tables(C_GROUP_DIM)
    cl, sl = _dft_tables(seq_len)
    csc, cl, sl = (jnp.asarray(a, F32).astype(BF16) for a in (np.concatenate([cc, sc], axis=1), cl, sl))
    full = lambda shape: pl.BlockSpec(shape, lambda i: (0,) * len(shape))
    row = lambda i: (i, 0)
    tm = max(seq_len, ODD_ROWS)
    assert tm % seq_len == 0 and (n_seq * seq_len) % tm == 0
    return pl.pallas_call(
        functools.partial(_odd_kernel, seq_len=seq_len),
        grid=(n_seq * seq_len // tm,),
        in_specs=[
            pl.BlockSpec((tm, D_MODEL), row),
            _mod_spec(tm, seq_len, mod_row0),
            full((4, D_MODEL)), full((C_GROUP_DIM, 2 * C_GROUP_DIM)),
            full((seq_len, seq_len)), full((seq_len, seq_len)), full((D_MODEL, D_MODEL)),
        ],
        out_specs=pl.BlockSpec((tm, D_MODEL), row),
        out_shape=jax.ShapeDtypeStruct(x.shape, F32),
        scratch_shapes=[pltpu.VMEM((tm, D_MODEL), BF16), pltpu.VMEM((tm, D_MODEL), BF16),
                        pltpu.VMEM((tm, D_MODEL), F32)],
        compiler_params=_cparams("parallel"),
        name="odd_mixer",
    )(x, mod, gains, csc, cl, sl, w)


def _ffn_kernel(x_ref, mod_ref, gains_ref, wu_ref, wg_ref, cw_ref, cb_ref, wo_ref, o_ref,
                h_scr, acc_scr, *, seq_len):
    j = pl.program_id(1)
    mod = mod_ref[0]

    @pl.when(j == 0)
    def _():
        h_scr[...] = _modnorm(x_ref[...], gains_ref[2:3], mod[4:5], mod[3:4]).astype(BF16)
        acc_scr[...] = jnp.zeros_like(acc_scr)

    n_sub = h_scr.shape[0] // FF_ROWS
    gate = [None] * n_sub
    zero_row = jnp.zeros((1, FF_CHUNK), F32)

    def sub_tile(i):
        rows = pl.ds(i * FF_ROWS, FF_ROWS)
        h = h_scr[rows, :]
        u = jnp.dot(h, wu_ref[...], preferred_element_type=F32)
        g = gate[i] = jnp.dot(h, wg_ref[...], preferred_element_type=F32)
        yield
        before = zero_row if (i * FF_ROWS) % seq_len == 0 else gate[i - 1][FF_ROWS - 1:FF_ROWS, :]
        after = zero_row if ((i + 1) * FF_ROWS) % seq_len == 0 else gate[i + 1][0:1, :]
        pos = lax.broadcasted_iota(jnp.int32, (FF_ROWS, 1), 0)
        prev = jnp.where(pos == 0, before, pltpu.roll(g, 1, 0))
        nxt = jnp.where(pos == FF_ROWS - 1, after, pltpu.roll(g, FF_ROWS - 1, 0))
        gc = prev * cw_ref[0:1] + g * cw_ref[1:2] + nxt * cw_ref[2:3] + cb_ref[...]
        act = (_silu(gc) * u).astype(BF16)
        yield
        acc_scr[rows, :] += jnp.dot(act, wo_ref[...], preferred_element_type=F32)

    _round_robin(sub_tile(i) for i in range(n_sub))

    @pl.when(j == pl.num_programs(1) - 1)
    def _():
        o_ref[...] = x_ref[...] + mod[5:6] * (_rms(acc_scr[...]) * gains_ref[3:4])


def _conv_ffn(x, mod, gains, w_in, conv_w, conv_b, w_out, seq_len, mod_row0, tm):
    t = x.shape[0]
    nj = D_FF // FF_CHUNK
    assert seq_len % FF_ROWS == 0 and tm % seq_len == 0
    return pl.pallas_call(
        functools.partial(_ffn_kernel, seq_len=seq_len),
        grid=(t // tm, nj),
        in_specs=[
            pl.BlockSpec((tm, D_MODEL), lambda i, j: (i, 0)),
            _mod_spec(tm, seq_len, mod_row0),
            pl.BlockSpec((4, D_MODEL), lambda i, j: (0, 0)),
            pl.BlockSpec((D_MODEL, FF_CHUNK), lambda i, j: (0, j)),
            pl.BlockSpec((D_MODEL, FF_CHUNK), lambda i, j: (0, nj + j)),
            pl.BlockSpec((3, FF_CHUNK), lambda i, j: (0, j)),
            pl.BlockSpec((1, FF_CHUNK), lambda i, j: (0, j)),
            pl.BlockSpec((FF_CHUNK, D_MODEL), lambda i, j: (j, 0)),
        ],
        out_specs=pl.BlockSpec((tm, D_MODEL), lambda i, j: (i, 0)),
        out_shape=jax.ShapeDtypeStruct((t, D_MODEL), F32),
        scratch_shapes=[pltpu.VMEM((tm, D_MODEL), BF16), pltpu.VMEM((tm, D_MODEL), F32)],
        compiler_params=_cparams("parallel", "arbitrary"),
        name="conv_ffn",
    )(x, mod, gains, w_in, w_in, conv_w, conv_b, w_out)


def _rope_tables(seq_len):
    rows = seq_len // GRID_W
    row = jnp.repeat(jnp.arange(rows, dtype=F32), GRID_W)
    col = jnp.tile(jnp.arange(GRID_W, dtype=F32), rows)
    n_freq = A_HALF // 4
    inv = ROPE_BASE ** (-jnp.arange(n_freq, dtype=F32) / n_freq)
    ang = jnp.concatenate([row[:, None] * inv, col[:, None] * inv], axis=-1)
    cos_t = jnp.tile(jnp.repeat(jnp.cos(ang), 2, axis=1), (1, 2))
    sin_t = jnp.tile(jnp.repeat(jnp.sin(ang), 2, axis=1), (1, 2))
    sign = jnp.where(jnp.arange(A_HEAD_DIM) % 2 == 0, -1.0, 1.0).astype(F32)
    return cos_t, sin_t * sign


def _pad_lora(w):
    z = jnp.zeros_like(w[0])
    return jnp.stack([jnp.concatenate([w[0], z], axis=0), jnp.concatenate([z, w[1]], axis=0)])


def _block_diag(value):
    head = np.arange(B_WIDTH) // B_HEAD
    return jnp.asarray(np.where(head[:, None] == head[None, :], value, 0.0), BF16)


def _tile_rows(seq_len):
    return max(seq_len, 1024)


def kernel(x_prompt, x_sample, cache_k, cache_v, state_wkv, c, c_ctx, w_ada, b_ada, norm_gains,
           w_in_even, w_out_even, diff_lambda, diff_subln, rwkv_shift_mu, rwkv_w0, rwkv_w2, rwkv_a0,
           rwkv_a2, rwkv_g2, rwkv_kvec, rwkv_lnx, w_out_odd, w_ffn_in, ffn_conv, ffn_conv_b,
           w_ffn_out):
    n_ctx, l_ctx, _ = x_prompt.shape
    n_lat, l_lat, _ = x_sample.shape
    assert 1 + n_lat <= MOD_ROWS
    cvec = jnp.concatenate(
        [c_ctx[None, :], c, jnp.zeros((MOD_ROWS - 1 - n_lat, D_MODEL), F32)], axis=0)
    mods = _modulation(cvec, w_ada, b_ada).reshape(DEPTH, MOD_ROWS, 6, D_MODEL)

    w_in_even_b = w_in_even.astype(BF16)
    w_out_even_b = w_out_even.astype(BF16)
    w_out_odd_b = w_out_odd.astype(BF16)
    w_ffn_in_b = w_ffn_in.astype(BF16)
    w_ffn_out_b = w_ffn_out.astype(BF16)
    bd_ones = _block_diag(1.0)
    bd_mean = _block_diag(1.0 / B_HEAD)
    cos_t, sin_t = _rope_tables(l_lat)

    def run_group(x, n_seq, seq_len, mod_row0, latent):
        tm = _tile_rows(seq_len)
        ctx_out = None
        for l in range(DEPTH):
            mod = mods[l]
            gains = norm_gains[l]
            if l % 2 == 0:
                e = l // 2
                lam_init = 0.8 - 0.6 * math.exp(-0.3 * l)
                q, k, v, fb = _even_in(x, mod, gains, w_in_even_b[e], seq_len, mod_row0, 256,
                                       rope=(cos_t, sin_t) if latent else None)
                if latent:
                    past = cache_k.shape[2]
                    ctx = (cache_k[:, e].reshape(n_seq * past, A_WIDTH),
                           cache_v[:, e].reshape(n_seq * past, A_WIDTH))
                    s0 = state_wkv[:, e]
                else:
                    ctx = None
                    s0 = None
                ya = _attention(q, k, v, diff_lambda[e], diff_subln[e][None, :], lam_init,
                                n_seq, seq_len, ctx)
                prep = {
                    "mu": rwkv_shift_mu[e], "w0": rwkv_w0[e], "w2": _pad_lora(rwkv_w2[e]).astype(BF16),
                    "a0": rwkv_a0[e], "a2": _pad_lora(rwkv_a2[e]).astype(BF16),
                    "g2": rwkv_g2[e].astype(BF16), "kvec": rwkv_kvec[e], "bd_ones": bd_ones,
                }
                pre, post = _rwkv_prep(fb, prep, seq_len)
                ys, s_fin = _rwkv_scan(pre, s0, n_seq, seq_len)
                x = _even_out(x, mod, gains, ya, ys, post, rwkv_lnx[e], bd_mean, w_out_even_b[e],
                              seq_len, mod_row0, 256)
                if not latent:
                    ctx_out = (k, v, s_fin)
            else:
                x = _odd_mixer(x, mod, gains, w_out_odd_b[l // 2], n_seq, seq_len, mod_row0)
            x = _conv_ffn(x, mod, gains, w_ffn_in_b[l], ffn_conv[l], ffn_conv_b[l][None, :],
                          w_ffn_out_b[l], seq_len, mod_row0, tm)
        return x, ctx_out

    y_ctx, (k_new, v_new, s_new) = run_group(
        x_prompt.reshape(n_ctx * l_ctx, D_MODEL), n_ctx, l_ctx, 0, False)
    y_lat, _ = run_group(x_sample.reshape(n_lat * l_lat, D_MODEL), n_lat, l_lat, 1, True)

    n_even = (DEPTH + 1) // 2
    assert n_even == 1
    return (
        y_ctx.reshape(n_ctx, l_ctx, D_MODEL),
        y_lat.reshape(n_lat, l_lat, D_MODEL),
        k_new.reshape(n_ctx, n_even, l_ctx, A_HEADS, A_HEAD_DIM),
        v_new.reshape(n_ctx, n_even, l_ctx, A_HEADS, A_HEAD_DIM),
        s_new.reshape(n_ctx, n_even, 2, B_HEADS, B_HEAD, B_HEAD),
    )
```

```python
import functools
import math

import numpy as np
import jax
import jax.numpy as jnp
from jax import lax
from jax.experimental import pallas as pl
from jax.experimental.pallas import tpu as pltpu

F32 = jnp.float32
BF16 = jnp.bfloat16

D_MODEL = 1024
DEPTH = 2
GRID_W = 64
A_WIDTH = D_MODEL // 2
A_HEADS = 4
A_HEAD_DIM = A_WIDTH // A_HEADS
A_HALF = A_HEAD_DIM // 2
B_WIDTH = D_MODEL - A_WIDTH
B_HEAD = 64
B_HEADS = B_WIDTH // B_HEAD
LORA_W = 64
LORA_A = 64
LORA_G = 128
IN_B = 3 * B_WIDTH + 2 * LORA_W + 2 * LORA_A + LORA_G
IN_EVEN = 3 * A_WIDTH + IN_B
C_GROUPS = 8
C_GROUP_DIM = D_MODEL // C_GROUPS
D_FF = 2816
ROPE_BASE = 10000.0
RMS_EPS = 1e-6
LNX_EPS = 64e-5

LANES = 128
SUBLANES = 8
PAIR = 2 * B_HEAD
N_PAIRS = B_HEADS // 2
CHUNK = 64
FF_CHUNK = 256
FF_ROWS = 256
ATTN_Q_ROWS = 256
ODD_ROWS = 512
ODD_COLS = 256
VMEM_LIMIT = 48 * 1024 * 1024
MOD_ROWS = 8


def _cparams(*sem):
    return pltpu.CompilerParams(dimension_semantics=sem, vmem_limit_bytes=VMEM_LIMIT)


def _sigmoid(x):
    return 1.0 / (1.0 + jnp.exp(-x))


def _silu(x):
    return x * (0.5 + 0.5 * jnp.tanh(0.5 * x))


def _dot(a, b):
    return jnp.dot(a.astype(BF16), b.astype(BF16), preferred_element_type=F32)


_NN = (((1,), (0,)), ((), ()))
_NT = (((1,), (1,)), ((), ()))
_TN = (((0,), (0,)), ((), ()))


def _split_bf16(x, n):
    parts = []
    rem = x
    for i in range(n):
        p = rem.astype(BF16)
        parts.append(p)
        if i + 1 < n:
            rem = rem - p.astype(F32)
    return parts


def _mm(a, b, dims, passes):
    if passes == 1:
        return lax.dot_general(a.astype(BF16), b.astype(BF16), dims, preferred_element_type=F32)
    a_hi, a_lo = _split_bf16(a, 2)
    b_hi, b_lo = _split_bf16(b, 2)
    dg = functools.partial(lax.dot_general, dimension_numbers=dims, preferred_element_type=F32)
    return dg(a_hi, b_hi) + (dg(a_hi, b_lo) + dg(a_lo, b_hi))


def _segsum(x, bd, pieces=2):
    acc = None
    for p in _split_bf16(x, pieces):
        t = jnp.dot(p, bd, preferred_element_type=F32)
        acc = t if acc is None else acc + t
    return acc


def _rms(x):
    return x * lax.rsqrt(jnp.mean(x * x, axis=-1, keepdims=True) + RMS_EPS)


def _modnorm(x, gain, scale, shift):
    return _rms(x) * gain * (1.0 + scale) + shift


def _mod_kernel(c_ref, w_ref, b_ref, o_ref):
    c = c_ref[...]
    s = c * _sigmoid(c)
    o_ref[0] = _dot(s, w_ref[0]) + b_ref[0]


def _modulation(cvec, w_ada, b_ada):
    tn = 1536
    n = 6 * D_MODEL
    return pl.pallas_call(
        _mod_kernel,
        grid=(DEPTH, n // tn),
        in_specs=[
            pl.BlockSpec((MOD_ROWS, D_MODEL), lambda l, j: (0, 0)),
            pl.BlockSpec((1, D_MODEL, tn), lambda l, j: (l, 0, j)),
            pl.BlockSpec((1, 1, tn), lambda l, j: (l, 0, j)),
        ],
        out_specs=pl.BlockSpec((1, MOD_ROWS, tn), lambda l, j: (l, 0, j)),
        out_shape=jax.ShapeDtypeStruct((DEPTH, MOD_ROWS, n), F32),
        compiler_params=_cparams("parallel", "parallel"),
        name="modulation",
    )(cvec, w_ada, b_ada.reshape(DEPTH, 1, n))


def _mod_spec(tm, seq_len, mod_row0):
    if mod_row0 == 0:
        return pl.BlockSpec((1, 6, D_MODEL), lambda i, *_: (0, 0, 0))
    return pl.BlockSpec((1, 6, D_MODEL), lambda i, *_: (mod_row0 + (i * tm) // seq_len, 0, 0))


def _rope(x, cos_t, sin_t):
    lane = lax.broadcasted_iota(jnp.int32, x.shape, 1)
    nxt = pltpu.roll(x, LANES - 1, 1)
    prv = pltpu.roll(x, 1, 1)
    partner = jnp.where(lane % 2 == 0, nxt, prv)
    return x * cos_t + partner * sin_t


def _even_in_kernel(*refs, rope):
    if rope:
        x_ref, mod_ref, gains_ref, w_ref, cos_ref, sin_ref, q_ref, k_ref, v_ref, fb_ref = refs
    else:
        x_ref, mod_ref, gains_ref, w_ref, q_ref, k_ref, v_ref, fb_ref = refs
    mod = mod_ref[0]
    h = _modnorm(x_ref[...], gains_ref[0:1], mod[1:2], mod[0:1]).astype(BF16)
    a = A_WIDTH
    hd = A_HEAD_DIM
    for out_ref, base in ((q_ref, 0), (k_ref, a)):
        y = jnp.dot(h, w_ref[:, base:base + a], preferred_element_type=F32)
        if rope:
            for hh in range(A_HEADS):
                cols = slice(hh * hd, (hh + 1) * hd)
                out_ref[:, cols] = _rope(y[:, cols], cos_ref[...], sin_ref[...])
        else:
            out_ref[...] = y
    v_ref[...] = jnp.dot(h, w_ref[:, 2 * a:3 * a], preferred_element_type=F32)
    fb_ref[...] = jnp.dot(h, w_ref[:, 3 * a:], preferred_element_type=F32)


def _even_in(x, mod, gains, w, seq_len, mod_row0, tm, rope=None):
    t = x.shape[0]
    row = lambda i: (i, 0)
    operands = [x, mod, gains, w]
    in_specs = [
        pl.BlockSpec((tm, D_MODEL), row),
        _mod_spec(tm, seq_len, mod_row0),
        pl.BlockSpec((4, D_MODEL), lambda i: (0, 0)),
        pl.BlockSpec((D_MODEL, IN_EVEN), lambda i: (0, 0)),
    ]
    if rope is not None:
        tab = pl.BlockSpec((tm, A_HEAD_DIM), lambda i: (i % (seq_len // tm), 0))
        operands += list(rope)
        in_specs += [tab, tab]
    return pl.pallas_call(
        functools.partial(_even_in_kernel, rope=rope is not None),
        grid=(t // tm,),
        in_specs=in_specs,
        out_specs=[
            pl.BlockSpec((tm, A_WIDTH), row),
            pl.BlockSpec((tm, A_WIDTH), row),
            pl.BlockSpec((tm, A_WIDTH), row),
            pl.BlockSpec((tm, IN_B), row),
        ],
        out_shape=[
            jax.ShapeDtypeStruct((t, A_WIDTH), F32),
            jax.ShapeDtypeStruct((t, A_WIDTH), F32),
            jax.ShapeDtypeStruct((t, A_WIDTH), F32),
            jax.ShapeDtypeStruct((t, IN_B), F32),
        ],
        compiler_params=_cparams("parallel"),
        name="even_in",
    )(*operands)


def _attn_kernel(*refs, has_ctx, lam_init):
    if has_ctx:
        lam_ref, sub_ref, q_ref, k_ref, v_ref, ck_ref, cv_ref, o_ref = refs
    else:
        lam_ref, sub_ref, q_ref, k_ref, v_ref, o_ref = refs
    lp = lam_ref[...]
    l1 = jnp.sum(lp[0:1] * lp[1:2], axis=-1, keepdims=True)
    l2 = jnp.sum(lp[2:3] * lp[3:4], axis=-1, keepdims=True)
    lam = jnp.exp(l1) - jnp.exp(l2) + lam_init
    hd = A_HEAD_DIM
    lane = lax.broadcasted_iota(jnp.int32, (q_ref.shape[0], hd), 1)
    scale = A_HALF ** -0.5

    def head(h):
        cols = pl.ds(h * hd, hd)
        q = q_ref[:, cols]
        k = k_ref[:, cols].astype(BF16)
        if has_ctx:
            ck = ck_ref[:, cols].astype(BF16)
        scores = []
        for m in range(2):
            qm = jnp.where((lane < A_HALF) == (m == 0), q, 0.0).astype(BF16)
            s = lax.dot_general(qm, k, _NT, preferred_element_type=F32) * scale
            sc = lax.dot_general(qm, ck, _NT, preferred_element_type=F32) * scale if has_ctx else None
            scores.append((s, sc))
        yield
        probs = []
        for s, sc in scores:
            mx = jnp.max(s, axis=-1, keepdims=True)
            if has_ctx:
                mx = jnp.maximum(mx, jnp.max(sc, axis=-1, keepdims=True))
                ec = jnp.exp(sc - mx)
            e = jnp.exp(s - mx)
            z = jnp.sum(e, axis=-1, keepdims=True)
            if has_ctx:
                z = z + jnp.sum(ec, axis=-1, keepdims=True)
            inv = 1.0 / z
            probs.append((e * inv, ec * inv if has_ctx else None))
        w = (probs[0][0] - lam * probs[1][0]).astype(BF16)
        if has_ctx:
            wc = (probs[0][1] - lam * probs[1][1]).astype(BF16)
        yield
        o = jnp.dot(w, v_ref[:, cols].astype(BF16), preferred_element_type=F32)
        if has_ctx:
            o = o + jnp.dot(wc, cv_ref[:, cols].astype(BF16), preferred_element_type=F32)
        yield
        o_ref[:, cols] = _rms(o) * sub_ref[...] * (1.0 - lam_init)

    _round_robin(head(h) for h in range(A_HEADS))


def _attention(q, k, v, lam_p, subln, lam_init, n_seq, seq_len, ctx=None):
    t = q.shape[0]
    tq = min(seq_len, ATTN_Q_ROWS if ctx is None else ATTN_Q_ROWS // 2)
    nq = seq_len // tq
    small = [pl.BlockSpec((4, A_HALF), lambda b, i: (0, 0)),
             pl.BlockSpec((1, A_HEAD_DIM), lambda b, i: (0, 0))]
    qblk = pl.BlockSpec((tq, A_WIDTH), lambda b, i: (b * nq + i, 0))
    kblk = pl.BlockSpec((seq_len, A_WIDTH), lambda b, i: (b, 0))
    operands = [lam_p, subln, q, k, v]
    in_specs = small + [qblk, kblk, kblk]
    if ctx is not None:
        ck, cv = ctx
        cblk = pl.BlockSpec((ck.shape[0] // n_seq, A_WIDTH), lambda b, i: (b, 0))
        operands += [ck, cv]
        in_specs += [cblk, cblk]
    return pl.pallas_call(
        functools.partial(_attn_kernel, has_ctx=ctx is not None, lam_init=lam_init),
        grid=(n_seq, nq),
        in_specs=in_specs,
        out_specs=qblk,
        out_shape=jax.ShapeDtypeStruct((t, A_WIDTH), F32),
        compiler_params=_cparams("parallel", "parallel"),
        name="diff_attn",
    )(*operands)


def _shift_rows(f, halo_prev, halo_next, start, seq_len):
    tm = f.shape[0]
    local = lax.broadcasted_iota(jnp.int32, (tm, 1), 0)
    halo_prev = jnp.where(start == 0, 0.0, halo_prev)
    halo_next = jnp.where(start + tm == seq_len, 0.0, halo_next)
    prev = jnp.where(local == 0, halo_prev, pltpu.roll(f, 1, 0))
    nxt = jnp.where(local == tm - 1, halo_next, pltpu.roll(f, tm - 1, 0))
    return prev, nxt


def _prep_kernel(fb_ref, fp_ref, fn_ref, mu_ref, w0_ref, w2_ref, a0_ref, a2_ref, g2_ref, kvec_ref,
                 bd_ref, scan_ref, post_ref, *, seq_len):
    f = fb_ref[...]
    tm = f.shape[0]
    start = (pl.program_id(0) * tm) % seq_len
    prev, nxt = _shift_rows(f, fp_ref[SUBLANES - 1:SUBLANES, :], fn_ref[0:1, :], start, seq_len)
    f = f * (1.0 - mu_ref[0:1] - mu_ref[1:2]) + mu_ref[0:1] * prev + mu_ref[1:2] * nxt
    bw = B_WIDTH
    r = f[:, 0:bw]
    k = f[:, bw:2 * bw]
    v = f[:, 2 * bw:3 * bw]
    off = 3 * bw
    wd = jnp.tanh(f[:, off:off + 2 * LORA_W])
    ad = f[:, off + 2 * LORA_W:off + 2 * LORA_W + 2 * LORA_A]
    gd = f[:, off + 2 * LORA_W + 2 * LORA_A:]
    bd = bd_ref[...]
    kv = kvec_ref[...]
    g = _dot(_sigmoid(gd), g2_ref[...])
    kk = k * kv[0:1]
    kk = kk * lax.rsqrt(_segsum(kk * kk, bd, pieces=1) + 1e-12)
    scan_ref[0] = r
    scan_ref[1] = v
    scan_ref[2] = kk
    ksum = None
    for d in range(2):
        w_raw = w0_ref[d:d + 1] + _dot(wd, w2_ref[d])
        scan_ref[7 + d] = -math.exp(-0.5) * _sigmoid(w_raw)
        a = _sigmoid(a0_ref[d:d + 1] + _dot(ad, a2_ref[d]))
        kd = k * (1.0 + (a - 1.0) * kv[1:2])
        scan_ref[3 + d] = kd
        scan_ref[5 + d] = kk * a
        ksum = kd if ksum is None else ksum + kd
    post_ref[0] = g
    post_ref[1] = _segsum(r * ksum * kv[2:3], bd) * v


def _rwkv_prep(fb, p, seq_len, tm=256):
    t = fb.shape[0]
    assert seq_len % tm == 0
    nh = tm // SUBLANES
    last = t // SUBLANES - 1
    full = lambda shape: pl.BlockSpec(shape, lambda i: (0,) * len(shape))
    return pl.pallas_call(
        functools.partial(_prep_kernel, seq_len=seq_len),
        grid=(t // tm,),
        in_specs=[
            pl.BlockSpec((tm, IN_B), lambda i: (i, 0)),
            pl.BlockSpec((SUBLANES, IN_B), lambda i: (jnp.maximum(i * nh - 1, 0), 0)),
            pl.BlockSpec((SUBLANES, IN_B), lambda i: (jnp.minimum((i + 1) * nh, last), 0)),
            full((2, IN_B)), full((2, B_WIDTH)), full((2, 2 * LORA_W, B_WIDTH)),
            full((2, B_WIDTH)), full((2, 2 * LORA_A, B_WIDTH)), full((LORA_G, B_WIDTH)),
            full((3, B_WIDTH)), full((B_WIDTH, B_WIDTH)),
        ],
        out_specs=[
            pl.BlockSpec((9, tm, B_WIDTH), lambda i: (0, i, 0)),
            pl.BlockSpec((2, tm, B_WIDTH), lambda i: (0, i, 0)),
        ],
        out_shape=[
            jax.ShapeDtypeStruct((9, t, B_WIDTH), F32),
            jax.ShapeDtypeStruct((2, t, B_WIDTH), F32),
        ],
        compiler_params=_cparams("parallel"),
        name="rwkv_prep",
    )(fb, fb, fb, p["mu"], p["w0"], p["w2"], p["a0"], p["a2"], p["g2"], p["kvec"], p["bd_ones"])


def _cumsum_rows(x, reverse):
    n = x.shape[0]
    ridx = lax.broadcasted_iota(jnp.int32, x.shape, 0)
    s = 1
    while s < n:
        if reverse:
            x = x + jnp.where(ridx < n - s, pltpu.roll(x, n - s, 0), 0.0)
        else:
            x = x + jnp.where(ridx >= s, pltpu.roll(x, s, 0), 0.0)
        s *= 2
    return x


def _round_robin(gens):
    gens = list(gens)
    while gens:
        alive = []
        for gen in gens:
            try:
                next(gen)
                alive.append(gen)
            except StopIteration:
                pass
        gens = alive


def _scan_kernel(*refs, seq_len, has_state, pairs, passes, inv_passes, carry_passes, unroll):
    if has_state:
        pre_ref, s0_ref, ys_ref, sf_ref, r1_scr, m1_scr, d0_scr, gam_scr, y_scr, s_scr = refs
    else:
        pre_ref, ys_ref, sf_ref, r1_scr, m1_scr, d0_scr, gam_scr, y_scr, s_scr = refs
    nc = seq_len // CHUNK
    c2 = 2 * CHUNK
    lane = lax.broadcasted_iota(jnp.int32, (CHUNK, PAIR), 1)
    head0 = lane < B_HEAD
    row = lax.broadcasted_iota(jnp.int32, (c2, c2), 0)
    col = lax.broadcasted_iota(jnp.int32, (c2, c2), 1)
    same = (row // CHUNK) == (col // CHUNK)
    eye = jnp.where(row == col, 1.0, 0.0)
    xor_rc = jnp.bitwise_xor(row, col)
    mm = functools.partial(_mm, passes=passes)
    mm_inv = functools.partial(_mm, passes=inv_passes)
    mm_carry = functools.partial(_mm, passes=carry_passes)

    def stack(x):
        return jnp.concatenate([jnp.where(head0, x, 0.0), jnp.where(head0, 0.0, x)], axis=0)

    def fold(x):
        return x[0:CHUNK] + x[CHUNK:c2]

    def chunk_rows(cc):
        return pl.ds(pl.multiple_of(cc * CHUNK, CHUNK), CHUNK)

    def chunk_local(cc, d, q):
        rows = chunk_rows(cc)
        lanes = pl.ds(q * PAIR, PAIR)
        z = 2 * q + d
        r = pre_ref[0, rows, lanes]
        v = pre_ref[1, rows, lanes]
        kk = pre_ref[2, rows, lanes]
        k = pre_ref[3 + d, rows, lanes]
        b = pre_ref[5 + d, rows, lanes]
        lw = pre_ref[7 + d, rows, lanes]
        incl = same & ((col <= row) if d == 0 else (col >= row))
        strict = same & ((col < row) if d == 0 else (col > row))
        g = _cumsum_rows(lw, reverse=(d == 1))
        gp = g - lw
        gtot = g[CHUNK - 1:CHUNK, :] if d == 0 else g[0:1, :]
        gm = 0.5 * gtot
        kx = k * jnp.exp(gm - g)
        bx = b * jnp.exp(gm - g)
        lhs = jnp.concatenate([stack(r * jnp.exp(g - gm)), stack(kk * jnp.exp(gp - gm))], axis=0)
        rhs = jnp.concatenate([kx, kx, bx, bx], axis=0)
        a_all = mm(lhs, rhs, _NT)
        yield
        a_rk = jnp.where(incl, a_all[0:c2, 0:c2], 0.0)
        a_rb = jnp.where(incl, a_all[0:c2, c2:2 * c2], 0.0)
        a_kk = jnp.where(strict, a_all[c2:2 * c2, 0:c2], 0.0)
        n_mat = jnp.where(strict, a_all[c2:2 * c2, c2:2 * c2], 0.0)
        vs = stack(v)
        akv = mm(a_kk, vs, _NN)
        y0 = mm(a_rk, vs, _NN)
        lower = (col < row) if d == 0 else (col > row)
        p_inv = eye - jnp.where(lower & (xor_rc == 1), n_mat, 0.0)
        yield
        s = 2
        while s < CHUNK:
            c_s = jnp.where(lower & (xor_rc >= s) & (xor_rc < 2 * s), n_mat, 0.0)
            t = mm_inv(c_s, p_inv, _NN)
            yield
            p_inv = p_inv - mm_inv(p_inv, t, _NN)
            yield
            s *= 2
        x = jnp.concatenate([stack(kk * jnp.exp(gp)), akv], axis=1)
        w12 = mm(p_inv, x, _NN)
        yield
        arb_w = mm(a_rb, w12, _NN)
        et = jnp.exp(gtot - g)
        bes = stack(b * et)
        d0_scr[z, cc] = mm(jnp.concatenate([vs, -w12[:, PAIR:]], axis=0),
                           jnp.concatenate([stack(k * et), bes], axis=0), _TN)
        m1_scr[z, cc] = mm(w12[:, 0:PAIR], bes, _TN)
        yield
        r1_scr[z, cc] = stack(r * jnp.exp(g)) - arb_w[:, 0:PAIR]
        y_scr[z, rows, :] = fold(y0 - arb_w[:, PAIR:])
        gam_scr[z, cc] = jnp.broadcast_to(jnp.exp(gtot), (SUBLANES, PAIR))

    def local_body(grp, carry):
        _round_robin(chunk_local(grp * unroll + u, d, q)
                     for u in range(unroll) for q in range(pairs) for d in range(2))
        return carry

    lax.fori_loop(0, nc // unroll, local_body, 0)

    for z in range(2 * pairs):
        s_scr[z] = jnp.zeros((PAIR, PAIR), F32)
        if has_state:
            s_scr[z, 0:B_HEAD, 0:B_HEAD] = s0_ref[0, z % 2, 2 * (z // 2)]
            s_scr[z, B_HEAD:PAIR, B_HEAD:PAIR] = s0_ref[0, z % 2, 2 * (z // 2) + 1]

    def carry_body(c, carry):
        for z in range(2 * pairs):
            cc = c if z % 2 == 0 else nc - 1 - c
            rows = chunk_rows(cc)
            s0 = s_scr[z]
            y_scr[z, rows, :] += fold(mm_carry(r1_scr[z, cc], s0, _NT))
            s_scr[z] = s0 * gam_scr[z, cc, 0:1, :] - mm_carry(s0, m1_scr[z, cc], _NN) + d0_scr[z, cc]
        return carry

    lax.fori_loop(0, nc, carry_body, 0)
    for q in range(pairs):
        ys_ref[:, q * PAIR:(q + 1) * PAIR] = y_scr[2 * q] + y_scr[2 * q + 1]
    for z in range(2 * pairs):
        sf_ref[0, z % 2, 2 * (z // 2)] = s_scr[z, 0:B_HEAD, 0:B_HEAD]
        sf_ref[0, z % 2, 2 * (z // 2) + 1] = s_scr[z, B_HEAD:PAIR, B_HEAD:PAIR]


def _rwkv_scan(pre, s0, n_seq, seq_len, pairs=2, passes=1, inv_passes=1, carry_passes=1, unroll=4):
    t = pre.shape[1]
    nc = seq_len // CHUNK
    sblk = pl.BlockSpec((1, 2, 2 * pairs, B_HEAD, B_HEAD), lambda b, p: (b, 0, p, 0, 0))
    mat = pltpu.VMEM((2 * pairs, nc, PAIR, PAIR), F32)
    return pl.pallas_call(
        functools.partial(_scan_kernel, seq_len=seq_len, has_state=s0 is not None, pairs=pairs,
                          passes=passes, inv_passes=inv_passes, carry_passes=carry_passes, unroll=unroll),
        grid=(n_seq, N_PAIRS // pairs),
        in_specs=([pl.BlockSpec((9, seq_len, pairs * PAIR), lambda b, p: (0, b, p))]
                  + [sblk] * (s0 is not None)),
        out_specs=[pl.BlockSpec((seq_len, pairs * PAIR), lambda b, p: (b, p)), sblk],
        out_shape=[
            jax.ShapeDtypeStruct((t, B_WIDTH), F32),
            jax.ShapeDtypeStruct((n_seq, 2, B_HEADS, B_HEAD, B_HEAD), F32),
        ],
        scratch_shapes=[mat, mat, mat, pltpu.VMEM((2 * pairs, nc, SUBLANES, PAIR), F32),
                        pltpu.VMEM((2 * pairs, seq_len, PAIR), F32),
                        pltpu.VMEM((2 * pairs, PAIR, PAIR), F32)],
        compiler_params=_cparams("parallel", "parallel"),
        name="rwkv_scan",
    )(*((pre,) if s0 is None else (pre, s0)))


def _even_out_kernel(x_ref, mod_ref, gains_ref, ya_ref, ys_ref, post_ref, lnx_ref, bdm_ref, w_ref,
                     o_ref):
    bdm = bdm_ref[...]
    ys = ys_ref[...]
    dlt = ys - _segsum(ys, bdm)
    yn = dlt * lax.rsqrt(_segsum(dlt * dlt, bdm, pieces=1) + LNX_EPS)
    yb = (yn * lnx_ref[0:1] + lnx_ref[1:2] + post_ref[1]) * post_ref[0]
    m = _dot(ya_ref[...], w_ref[0:A_WIDTH, :]) + _dot(yb, w_ref[A_WIDTH:, :])
    mod = mod_ref[0]
    o_ref[...] = x_ref[...] + mod[2:3] * (_rms(m) * gains_ref[1:2])


def _even_out(x, mod, gains, ya, ys, post, lnx, bd_mean, w, seq_len, mod_row0, tm):
    t = x.shape[0]
    row = lambda i: (i, 0)
    return pl.pallas_call(
        _even_out_kernel,
        grid=(t // tm,),
        in_specs=[
            pl.BlockSpec((tm, D_MODEL), row),
            _mod_spec(tm, seq_len, mod_row0),
            pl.BlockSpec((4, D_MODEL), lambda i: (0, 0)),
            pl.BlockSpec((tm, A_WIDTH), row),
            pl.BlockSpec((tm, B_WIDTH), row),
            pl.BlockSpec((2, tm, B_WIDTH), lambda i: (0, i, 0)),
            pl.BlockSpec((2, B_WIDTH), lambda i: (0, 0)),
            pl.BlockSpec((B_WIDTH, B_WIDTH), lambda i: (0, 0)),
            pl.BlockSpec((D_MODEL, D_MODEL), lambda i: (0, 0)),
        ],
        out_specs=pl.BlockSpec((tm, D_MODEL), row),
        out_shape=jax.ShapeDtypeStruct((t, D_MODEL), F32),
        compiler_params=_cparams("parallel"),
        name="even_out",
    )(x, mod, gains, ya, ys, post, lnx, bd_mean, w)


def _odd_kernel(x_ref, mod_ref, gains_ref, csc_ref, cl_ref, sl_ref, w_ref, o_ref,
                tc_scr, ts_scr, m_scr, *, seq_len):
    mod = mod_ref[0]
    gd = C_GROUP_DIM
    n_seq = x_ref.shape[0] // seq_len
    csc = csc_ref[...]

    for s in range(n_seq):
        rows = pl.ds(s * seq_len, seq_len)
        h = _modnorm(x_ref[rows, :], gains_ref[0:1], mod[1:2], mod[0:1]).astype(BF16)
        for g in range(C_GROUPS):
            t = jnp.dot(h[:, g * gd:(g + 1) * gd], csc, preferred_element_type=F32)
            tc_scr[rows, g * gd:(g + 1) * gd] = t[:, 0:gd].astype(BF16)
            ts_scr[rows, g * gd:(g + 1) * gd] = t[:, gd:].astype(BF16)

    def position_dft(s, cb):
        rows = pl.ds(s * seq_len, seq_len)
        cols = pl.ds(cb * ODD_COLS, ODD_COLS)
        f = (jnp.dot(cl_ref[...], tc_scr[rows, cols], preferred_element_type=F32)
             - jnp.dot(sl_ref[...], ts_scr[rows, cols], preferred_element_type=F32))
        yield
        part = jnp.dot(f.astype(BF16), w_ref[cols, :], preferred_element_type=F32)
        yield
        if cb == 0:
            m_scr[rows, :] = part
        else:
            m_scr[rows, :] += part

    _round_robin(position_dft(s, cb) for s in range(n_seq) for cb in range(D_MODEL // ODD_COLS))
    o_ref[...] = x_ref[...] + mod[2:3] * (_rms(m_scr[...]) * gains_ref[1:2])


def _dft_tables(n):
    idx = np.arange(n, dtype=np.int64)
    ang = 2.0 * np.pi * ((idx[:, None] * idx[None, :]) % n).astype(np.float64) / n
    s = 1.0 / math.sqrt(n)
    return np.cos(ang) * s, np.sin(ang) * s


def _odd_mixer(x, mod, gains, w, n_seq, seq_len, mod_row0):
    cc, sc = _dft_tables(C_GROUP_DIM)
    cl, sl = _dft_tables(seq_len)
    csc, cl, sl = (jnp.asarray(a, F32).astype(BF16) for a in (np.concatenate([cc, sc], axis=1), cl, sl))
    full = lambda shape: pl.BlockSpec(shape, lambda i: (0,) * len(shape))
    row = lambda i: (i, 0)
    tm = max(seq_len, ODD_ROWS)
    assert tm % seq_len == 0 and (n_seq * seq_len) % tm == 0
    return pl.pallas_call(
        functools.partial(_odd_kernel, seq_len=seq_len),
        grid=(n_seq * seq_len // tm,),
        in_specs=[
            pl.BlockSpec((tm, D_MODEL), row),
            _mod_spec(tm, seq_len, mod_row0),
            full((4, D_MODEL)), full((C_GROUP_DIM, 2 * C_GROUP_DIM)),
            full((seq_len, seq_len)), full((seq_len, seq_len)), full((D_MODEL, D_MODEL)),
        ],
        out_specs=pl.BlockSpec((tm, D_MODEL), row),
        out_shape=jax.ShapeDtypeStruct(x.shape, F32),
        scratch_shapes=[pltpu.VMEM((tm, D_MODEL), BF16), pltpu.VMEM((tm, D_MODEL), BF16),
                        pltpu.VMEM((tm, D_MODEL), F32)],
        compiler_params=_cparams("parallel"),
        name="odd_mixer",
    )(x, mod, gains, csc, cl, sl, w)


def _ffn_kernel(x_ref, mod_ref, gains_ref, wu_ref, wg_ref, cw_ref, cb_ref, wo_ref, o_ref,
                h_scr, acc_scr, *, seq_len):
    j = pl.program_id(1)
    mod = mod_ref[0]

    @pl.when(j == 0)
    def _():
        h_scr[...] = _modnorm(x_ref[...], gains_ref[2:3], mod[4:5], mod[3:4]).astype(BF16)
        acc_scr[...] = jnp.zeros_like(acc_scr)

    n_sub = h_scr.shape[0] // FF_ROWS
    gate = [None] * n_sub
    zero_row = jnp.zeros((1, FF_CHUNK), F32)

    def sub_tile(i):
        rows = pl.ds(i * FF_ROWS, FF_ROWS)
        h = h_scr[rows, :]
        u = jnp.dot(h, wu_ref[...], preferred_element_type=F32)
        g = gate[i] = jnp.dot(h, wg_ref[...], preferred_element_type=F32)
        yield
        before = zero_row if (i * FF_ROWS) % seq_len == 0 else gate[i - 1][FF_ROWS - 1:FF_ROWS, :]
        after = zero_row if ((i + 1) * FF_ROWS) % seq_len == 0 else gate[i + 1][0:1, :]
        pos = lax.broadcasted_iota(jnp.int32, (FF_ROWS, 1), 0)
        prev = jnp.where(pos == 0, before, pltpu.roll(g, 1, 0))
        nxt = jnp.where(pos == FF_ROWS - 1, after, pltpu.roll(g, FF_ROWS - 1, 0))
        gc = prev * cw_ref[0:1] + g * cw_ref[1:2] + nxt * cw_ref[2:3] + cb_ref[...]
        act = (_silu(gc) * u).astype(BF16)
        yield
        acc_scr[rows, :] += jnp.dot(act, wo_ref[...], preferred_element_type=F32)

    _round_robin(sub_tile(i) for i in range(n_sub))

    @pl.when(j == pl.num_programs(1) - 1)
    def _():
        o_ref[...] = x_ref[...] + mod[5:6] * (_rms(acc_scr[...]) * gains_ref[3:4])


def _conv_ffn(x, mod, gains, w_in, conv_w, conv_b, w_out, seq_len, mod_row0, tm):
    t = x.shape[0]
    nj = D_FF // FF_CHUNK
    assert seq_len % FF_ROWS == 0 and tm % seq_len == 0
    return pl.pallas_call(
        functools.partial(_ffn_kernel, seq_len=seq_len),
        grid=(t // tm, nj),
        in_specs=[
            pl.BlockSpec((tm, D_MODEL), lambda i, j: (i, 0)),
            _mod_spec(tm, seq_len, mod_row0),
            pl.BlockSpec((4, D_MODEL), lambda i, j: (0, 0)),
            pl.BlockSpec((D_MODEL, FF_CHUNK), lambda i, j: (0, j)),
            pl.BlockSpec((D_MODEL, FF_CHUNK), lambda i, j: (0, nj + j)),
            pl.BlockSpec((3, FF_CHUNK), lambda i, j: (0, j)),
            pl.BlockSpec((1, FF_CHUNK), lambda i, j: (0, j)),
            pl.BlockSpec((FF_CHUNK, D_MODEL), lambda i, j: (j, 0)),
        ],
        out_specs=pl.BlockSpec((tm, D_MODEL), lambda i, j: (i, 0)),
        out_shape=jax.ShapeDtypeStruct((t, D_MODEL), F32),
        scratch_shapes=[pltpu.VMEM((tm, D_MODEL), BF16), pltpu.VMEM((tm, D_MODEL), F32)],
        compiler_params=_cparams("parallel", "arbitrary"),
        name="conv_ffn",
    )(x, mod, gains, w_in, w_in, conv_w, conv_b, w_out)


def _rope_tables(seq_len):
    rows = seq_len // GRID_W
    row = jnp.repeat(jnp.arange(rows, dtype=F32), GRID_W)
    col = jnp.tile(jnp.arange(GRID_W, dtype=F32), rows)
    n_freq = A_HALF // 4
    inv = ROPE_BASE ** (-jnp.arange(n_freq, dtype=F32) / n_freq)
    ang = jnp.concatenate([row[:, None] * inv, col[:, None] * inv], axis=-1)
    cos_t = jnp.tile(jnp.repeat(jnp.cos(ang), 2, axis=1), (1, 2))
    sin_t = jnp.tile(jnp.repeat(jnp.sin(ang), 2, axis=1), (1, 2))
    sign = jnp.where(jnp.arange(A_HEAD_DIM) % 2 == 0, -1.0, 1.0).astype(F32)
    return cos_t, sin_t * sign


def _pad_lora(w):
    z = jnp.zeros_like(w[0])
    return jnp.stack([jnp.concatenate([w[0], z], axis=0), jnp.concatenate([z, w[1]], axis=0)])


def _block_diag(value):
    head = np.arange(B_WIDTH) // B_HEAD
    return jnp.asarray(np.where(head[:, None] == head[None, :], value, 0.0), BF16)


def _tile_rows(seq_len):
    return max(seq_len, 1024)


def kernel(x_prompt, x_sample, cache_k, cache_v, state_wkv, c, c_ctx, w_ada, b_ada, norm_gains,
           w_in_even, w_out_even, diff_lambda, diff_subln, rwkv_shift_mu, rwkv_w0, rwkv_w2, rwkv_a0,
           rwkv_a2, rwkv_g2, rwkv_kvec, rwkv_lnx, w_out_odd, w_ffn_in, ffn_conv, ffn_conv_b,
           w_ffn_out):
    n_ctx, l_ctx, _ = x_prompt.shape
    n_lat, l_lat, _ = x_sample.shape
    assert 1 + n_lat <= MOD_ROWS
    cvec = jnp.concatenate(
        [c_ctx[None, :], c, jnp.zeros((MOD_ROWS - 1 - n_lat, D_MODEL), F32)], axis=0)
    mods = _modulation(cvec, w_ada, b_ada).reshape(DEPTH, MOD_ROWS, 6, D_MODEL)

    per_layer_bf16 = lambda w: [w[i].astype(BF16) for i in range(w.shape[0])]
    w_in_even_b = per_layer_bf16(w_in_even)
    w_out_even_b = per_layer_bf16(w_out_even)
    w_out_odd_b = per_layer_bf16(w_out_odd)
    w_ffn_in_b = per_layer_bf16(w_ffn_in)
    w_ffn_out_b = per_layer_bf16(w_ffn_out)
    bd_ones = _block_diag(1.0)
    bd_mean = _block_diag(1.0 / B_HEAD)
    cos_t, sin_t = _rope_tables(l_lat)

    def run_group(x, n_seq, seq_len, mod_row0, latent):
        tm = _tile_rows(seq_len)
        ctx_out = None
        for l in range(DEPTH):
            mod = mods[l]
            gains = norm_gains[l]
            if l % 2 == 0:
                e = l // 2
                lam_init = 0.8 - 0.6 * math.exp(-0.3 * l)
                q, k, v, fb = _even_in(x, mod, gains, w_in_even_b[e], seq_len, mod_row0, 256,
                                       rope=(cos_t, sin_t) if latent else None)
                if latent:
                    past = cache_k.shape[2]
                    ctx = (cache_k[:, e].reshape(n_seq * past, A_WIDTH),
                           cache_v[:, e].reshape(n_seq * past, A_WIDTH))
                    s0 = state_wkv[:, e]
                else:
                    ctx = None
                    s0 = None
                ya = _attention(q, k, v, diff_lambda[e], diff_subln[e][None, :], lam_init,
                                n_seq, seq_len, ctx)
                prep = {
                    "mu": rwkv_shift_mu[e], "w0": rwkv_w0[e], "w2": _pad_lora(rwkv_w2[e]).astype(BF16),
                    "a0": rwkv_a0[e], "a2": _pad_lora(rwkv_a2[e]).astype(BF16),
                    "g2": rwkv_g2[e].astype(BF16), "kvec": rwkv_kvec[e], "bd_ones": bd_ones,
                }
                pre, post = _rwkv_prep(fb, prep, seq_len)
                ys, s_fin = _rwkv_scan(pre, s0, n_seq, seq_len)
                x = _even_out(x, mod, gains, ya, ys, post, rwkv_lnx[e], bd_mean, w_out_even_b[e],
                              seq_len, mod_row0, 256)
                if not latent:
                    ctx_out = (k, v, s_fin)
            else:
                x = _odd_mixer(x, mod, gains, w_out_odd_b[l // 2], n_seq, seq_len, mod_row0)
            x = _conv_ffn(x, mod, gains, w_ffn_in_b[l], ffn_conv[l], ffn_conv_b[l][None, :],
                          w_ffn_out_b[l], seq_len, mod_row0, tm)
        return x, ctx_out

    y_ctx, (k_new, v_new, s_new) = run_group(
        x_prompt.reshape(n_ctx * l_ctx, D_MODEL), n_ctx, l_ctx, 0, False)
    y_lat, _ = run_group(x_sample.reshape(n_lat * l_lat, D_MODEL), n_lat, l_lat, 1, True)

    n_even = (DEPTH + 1) // 2
    assert n_even == 1
    return (
        y_ctx.reshape(n_ctx, l_ctx, D_MODEL),
        y_lat.reshape(n_lat, l_lat, D_MODEL),
        k_new.reshape(n_ctx, n_even, l_ctx, A_HEADS, A_HEAD_DIM),
        v_new.reshape(n_ctx, n_even, l_ctx, A_HEADS, A_HEAD_DIM),
        s_new.reshape(n_ctx, n_even, 2, B_HEADS, B_HEAD, B_HEAD),
    )
```

```python
import functools
import math

import numpy as np
import jax
import jax.numpy as jnp
from jax import lax
from jax.experimental import pallas as pl
from jax.experimental.pallas import tpu as pltpu

F32 = jnp.float32
BF16 = jnp.bfloat16

D_MODEL = 1024
DEPTH = 2
GRID_W = 64
A_WIDTH = D_MODEL // 2
A_HEADS = 4
A_HEAD_DIM = A_WIDTH // A_HEADS
A_HALF = A_HEAD_DIM // 2
B_WIDTH = D_MODEL - A_WIDTH
B_HEAD = 64
B_HEADS = B_WIDTH // B_HEAD
LORA_W = 64
LORA_A = 64
LORA_G = 128
IN_B = 3 * B_WIDTH + 2 * LORA_W + 2 * LORA_A + LORA_G
IN_EVEN = 3 * A_WIDTH + IN_B
C_GROUPS = 8
C_GROUP_DIM = D_MODEL // C_GROUPS
D_FF = 2816
ROPE_BASE = 10000.0
RMS_EPS = 1e-6
LNX_EPS = 64e-5

LANES = 128
SUBLANES = 8
PAIR = 2 * B_HEAD
N_PAIRS = B_HEADS // 2
CHUNK = 64
FF_CHUNK = 256
FF_ROWS = 256
ATTN_Q_ROWS = 256
ODD_ROWS = 512
ODD_COLS = 256
VMEM_LIMIT = 48 * 1024 * 1024
MOD_ROWS = 8


def _cparams(*sem):
    return pltpu.CompilerParams(dimension_semantics=sem, vmem_limit_bytes=VMEM_LIMIT)


def _sigmoid(x):
    return 1.0 / (1.0 + jnp.exp(-x))


def _silu(x):
    return x * (0.5 + 0.5 * jnp.tanh(0.5 * x))


def _dot(a, b):
    return jnp.dot(a.astype(BF16), b.astype(BF16), preferred_element_type=F32)


_NN = (((1,), (0,)), ((), ()))
_NT = (((1,), (1,)), ((), ()))
_TN = (((0,), (0,)), ((), ()))


def _split_bf16(x, n):
    parts = []
    rem = x
    for i in range(n):
        p = rem.astype(BF16)
        parts.append(p)
        if i + 1 < n:
            rem = rem - p.astype(F32)
    return parts


def _mm(a, b, dims, passes):
    if passes == 1:
        return lax.dot_general(a.astype(BF16), b.astype(BF16), dims, preferred_element_type=F32)
    a_hi, a_lo = _split_bf16(a, 2)
    b_hi, b_lo = _split_bf16(b, 2)
    dg = functools.partial(lax.dot_general, dimension_numbers=dims, preferred_element_type=F32)
    return dg(a_hi, b_hi) + (dg(a_hi, b_lo) + dg(a_lo, b_hi))


def _segsum(x, bd, pieces=2):
    acc = None
    for p in _split_bf16(x, pieces):
        t = jnp.dot(p, bd, preferred_element_type=F32)
        acc = t if acc is None else acc + t
    return acc


def _rms(x):
    return x * lax.rsqrt(jnp.mean(x * x, axis=-1, keepdims=True) + RMS_EPS)


def _modnorm(x, gain, scale, shift):
    return _rms(x) * gain * (1.0 + scale) + shift


def _mod_kernel(c_ref, w_ref, b_ref, o_ref):
    c = c_ref[...]
    s = c * _sigmoid(c)
    o_ref[0] = _dot(s, w_ref[0]) + b_ref[0]


def _modulation(cvec, w_ada, b_ada):
    tn = 1536
    n = 6 * D_MODEL
    return pl.pallas_call(
        _mod_kernel,
        grid=(DEPTH, n // tn),
        in_specs=[
            pl.BlockSpec((MOD_ROWS, D_MODEL), lambda l, j: (0, 0)),
            pl.BlockSpec((1, D_MODEL, tn), lambda l, j: (l, 0, j)),
            pl.BlockSpec((1, 1, tn), lambda l, j: (l, 0, j)),
        ],
        out_specs=pl.BlockSpec((1, MOD_ROWS, tn), lambda l, j: (l, 0, j)),
        out_shape=jax.ShapeDtypeStruct((DEPTH, MOD_ROWS, n), F32),
        compiler_params=_cparams("parallel", "parallel"),
        name="modulation",
    )(cvec, w_ada, b_ada.reshape(DEPTH, 1, n))


def _mod_spec(tm, seq_len, mod_row0):
    if mod_row0 == 0:
        return pl.BlockSpec((1, 6, D_MODEL), lambda i, *_: (0, 0, 0))
    return pl.BlockSpec((1, 6, D_MODEL), lambda i, *_: (mod_row0 + (i * tm) // seq_len, 0, 0))


def _rope(x, cos_t, sin_t):
    lane = lax.broadcasted_iota(jnp.int32, x.shape, 1)
    nxt = pltpu.roll(x, LANES - 1, 1)
    prv = pltpu.roll(x, 1, 1)
    partner = jnp.where(lane % 2 == 0, nxt, prv)
    return x * cos_t + partner * sin_t


def _even_in_kernel(*refs, rope, cache):
    x_ref, mod_ref, gains_ref, w_ref = refs[:4]
    refs = refs[4:]
    if rope:
        cos_ref, sin_ref = refs[:2]
        refs = refs[2:]
    q_ref, k_ref, v_ref, fb_ref = refs[:4]
    cache_refs = refs[4:]
    mod = mod_ref[0]
    h = _modnorm(x_ref[...], gains_ref[0:1], mod[1:2], mod[0:1]).astype(BF16)
    a = A_WIDTH
    hd = A_HEAD_DIM
    for n, out_ref in enumerate((q_ref, k_ref, v_ref)):
        y = jnp.dot(h, w_ref[:, n * a:(n + 1) * a], preferred_element_type=F32)
        if cache and n > 0:
            for hh in range(A_HEADS):
                cache_refs[n - 1][0, 0, :, hh, :] = y[:, hh * hd:(hh + 1) * hd]
        if rope and n < 2:
            for hh in range(A_HEADS):
                cols = slice(hh * hd, (hh + 1) * hd)
                out_ref[:, cols] = _rope(y[:, cols], cos_ref[...], sin_ref[...]).astype(BF16)
        else:
            out_ref[...] = y.astype(BF16)
    fb_ref[...] = jnp.dot(h, w_ref[:, 3 * a:], preferred_element_type=F32)


def _even_in(x, mod, gains, w, seq_len, mod_row0, tm, rope=None, cache=False):
    t = x.shape[0]
    assert seq_len % tm == 0
    per_seq = seq_len // tm
    row = lambda i: (i, 0)
    operands = [x, mod, gains, w]
    in_specs = [
        pl.BlockSpec((tm, D_MODEL), row),
        _mod_spec(tm, seq_len, mod_row0),
        pl.BlockSpec((4, D_MODEL), lambda i: (0, 0)),
        pl.BlockSpec((D_MODEL, IN_EVEN), lambda i: (0, 0)),
    ]
    if rope is not None:
        tab = pl.BlockSpec((tm, A_HEAD_DIM), lambda i: (i % per_seq, 0))
        operands += list(rope)
        in_specs += [tab, tab]
    out_specs = [pl.BlockSpec((tm, A_WIDTH), row)] * 3 + [pl.BlockSpec((tm, IN_B), row)]
    out_shape = [jax.ShapeDtypeStruct((t, A_WIDTH), BF16)] * 3 + [jax.ShapeDtypeStruct((t, IN_B), F32)]
    if cache:
        out_specs += [pl.BlockSpec((1, 1, tm, A_HEADS, A_HEAD_DIM),
                                   lambda i: (i // per_seq, 0, i % per_seq, 0, 0))] * 2
        out_shape += [jax.ShapeDtypeStruct((t // seq_len, 1, seq_len, A_HEADS, A_HEAD_DIM), F32)] * 2
    return pl.pallas_call(
        functools.partial(_even_in_kernel, rope=rope is not None, cache=cache),
        grid=(t // tm,),
        in_specs=in_specs,
        out_specs=out_specs,
        out_shape=out_shape,
        compiler_params=_cparams("parallel"),
        name="even_in",
    )(*operands)


def _attn_kernel(*refs, has_ctx, lam_init):
    if has_ctx:
        lam_ref, sub_ref, q_ref, k_ref, v_ref, ck_ref, cv_ref, o_ref = refs
    else:
        lam_ref, sub_ref, q_ref, k_ref, v_ref, o_ref = refs
    lp = lam_ref[...]
    l1 = jnp.sum(lp[0:1] * lp[1:2], axis=-1, keepdims=True)
    l2 = jnp.sum(lp[2:3] * lp[3:4], axis=-1, keepdims=True)
    lam = jnp.exp(l1) - jnp.exp(l2) + lam_init
    hd = A_HEAD_DIM
    lane = lax.broadcasted_iota(jnp.int32, (q_ref.shape[0], hd), 1)
    scale = A_HALF ** -0.5

    def head(h):
        cols = pl.ds(h * hd, hd)
        q = q_ref[:, cols]
        k = k_ref[:, cols].astype(BF16)
        if has_ctx:
            ck = ck_ref[:, cols].astype(BF16)
        scores = []
        for m in range(2):
            qm = jnp.where((lane < A_HALF) == (m == 0), q, 0.0).astype(BF16)
            s = lax.dot_general(qm, k, _NT, preferred_element_type=F32) * scale
            sc = lax.dot_general(qm, ck, _NT, preferred_element_type=F32) * scale if has_ctx else None
            scores.append((s, sc))
        yield
        probs = []
        for s, sc in scores:
            mx = jnp.max(s, axis=-1, keepdims=True)
            if has_ctx:
                mx = jnp.maximum(mx, jnp.max(sc, axis=-1, keepdims=True))
                ec = jnp.exp(sc - mx)
            e = jnp.exp(s - mx)
            z = jnp.sum(e, axis=-1, keepdims=True)
            if has_ctx:
                z = z + jnp.sum(ec, axis=-1, keepdims=True)
            inv = 1.0 / z
            probs.append((e * inv, ec * inv if has_ctx else None))
        w = (probs[0][0] - lam * probs[1][0]).astype(BF16)
        if has_ctx:
            wc = (probs[0][1] - lam * probs[1][1]).astype(BF16)
        yield
        o = jnp.dot(w, v_ref[:, cols].astype(BF16), preferred_element_type=F32)
        if has_ctx:
            o = o + jnp.dot(wc, cv_ref[:, cols].astype(BF16), preferred_element_type=F32)
        yield
        o_ref[:, cols] = _rms(o) * sub_ref[...] * (1.0 - lam_init)

    _round_robin(head(h) for h in range(A_HEADS))


def _attention(q, k, v, lam_p, subln, lam_init, n_seq, seq_len, ctx=None):
    t = q.shape[0]
    tq = min(seq_len, ATTN_Q_ROWS if ctx is None else ATTN_Q_ROWS // 2)
    nq = seq_len // tq
    small = [pl.BlockSpec((4, A_HALF), lambda b, i: (0, 0)),
             pl.BlockSpec((1, A_HEAD_DIM), lambda b, i: (0, 0))]
    qblk = pl.BlockSpec((tq, A_WIDTH), lambda b, i: (b * nq + i, 0))
    kblk = pl.BlockSpec((seq_len, A_WIDTH), lambda b, i: (b, 0))
    operands = [lam_p, subln, q, k, v]
    in_specs = small + [qblk, kblk, kblk]
    if ctx is not None:
        ck, cv = ctx
        cblk = pl.BlockSpec((ck.shape[0] // n_seq, A_WIDTH), lambda b, i: (b, 0))
        operands += [ck, cv]
        in_specs += [cblk, cblk]
    return pl.pallas_call(
        functools.partial(_attn_kernel, has_ctx=ctx is not None, lam_init=lam_init),
        grid=(n_seq, nq),
        in_specs=in_specs,
        out_specs=qblk,
        out_shape=jax.ShapeDtypeStruct((t, A_WIDTH), F32),
        compiler_params=_cparams("parallel", "parallel"),
        name="diff_attn",
    )(*operands)


def _shift_rows(f, halo_prev, halo_next, start, seq_len):
    tm = f.shape[0]
    local = lax.broadcasted_iota(jnp.int32, (tm, 1), 0)
    halo_prev = jnp.where(start == 0, 0.0, halo_prev)
    halo_next = jnp.where(start + tm == seq_len, 0.0, halo_next)
    prev = jnp.where(local == 0, halo_prev, pltpu.roll(f, 1, 0))
    nxt = jnp.where(local == tm - 1, halo_next, pltpu.roll(f, tm - 1, 0))
    return prev, nxt


def _prep_kernel(fb_ref, fp_ref, fn_ref, mu_ref, w0_ref, w2_ref, a0_ref, a2_ref, g2_ref, kvec_ref,
                 bd_ref, scan_ref, post_ref, *, seq_len):
    f = fb_ref[...]
    tm = f.shape[0]
    start = (pl.program_id(0) * tm) % seq_len
    prev, nxt = _shift_rows(f, fp_ref[SUBLANES - 1:SUBLANES, :], fn_ref[0:1, :], start, seq_len)
    f = f * (1.0 - mu_ref[0:1] - mu_ref[1:2]) + mu_ref[0:1] * prev + mu_ref[1:2] * nxt
    bw = B_WIDTH
    r = f[:, 0:bw]
    k = f[:, bw:2 * bw]
    v = f[:, 2 * bw:3 * bw]
    off = 3 * bw
    wd = jnp.tanh(f[:, off:off + 2 * LORA_W])
    ad = f[:, off + 2 * LORA_W:off + 2 * LORA_W + 2 * LORA_A]
    gd = f[:, off + 2 * LORA_W + 2 * LORA_A:]
    bd = bd_ref[...]
    kv = kvec_ref[...]
    g = _dot(_sigmoid(gd), g2_ref[...])
    kk = k * kv[0:1]
    kk = kk * lax.rsqrt(_segsum(kk * kk, bd, pieces=1) + 1e-12)
    scan_ref[0] = r
    scan_ref[1] = v
    scan_ref[2] = kk
    ksum = None
    for d in range(2):
        w_raw = w0_ref[d:d + 1] + _dot(wd, w2_ref[d])
        scan_ref[7 + d] = -math.exp(-0.5) * _sigmoid(w_raw)
        a = _sigmoid(a0_ref[d:d + 1] + _dot(ad, a2_ref[d]))
        kd = k * (1.0 + (a - 1.0) * kv[1:2])
        scan_ref[3 + d] = kd
        scan_ref[5 + d] = kk * a
        ksum = kd if ksum is None else ksum + kd
    post_ref[0] = g
    post_ref[1] = _segsum(r * ksum * kv[2:3], bd) * v


def _rwkv_prep(fb, p, seq_len, tm=256):
    t = fb.shape[0]
    assert seq_len % tm == 0
    nh = tm // SUBLANES
    last = t // SUBLANES - 1
    full = lambda shape: pl.BlockSpec(shape, lambda i: (0,) * len(shape))
    return pl.pallas_call(
        functools.partial(_prep_kernel, seq_len=seq_len),
        grid=(t // tm,),
        in_specs=[
            pl.BlockSpec((tm, IN_B), lambda i: (i, 0)),
            pl.BlockSpec((SUBLANES, IN_B), lambda i: (jnp.maximum(i * nh - 1, 0), 0)),
            pl.BlockSpec((SUBLANES, IN_B), lambda i: (jnp.minimum((i + 1) * nh, last), 0)),
            full((2, IN_B)), full((2, B_WIDTH)), full((2, 2 * LORA_W, B_WIDTH)),
            full((2, B_WIDTH)), full((2, 2 * LORA_A, B_WIDTH)), full((LORA_G, B_WIDTH)),
            full((3, B_WIDTH)), full((B_WIDTH, B_WIDTH)),
        ],
        out_specs=[
            pl.BlockSpec((9, tm, B_WIDTH), lambda i: (0, i, 0)),
            pl.BlockSpec((2, tm, B_WIDTH), lambda i: (0, i, 0)),
        ],
        out_shape=[
            jax.ShapeDtypeStruct((9, t, B_WIDTH), F32),
            jax.ShapeDtypeStruct((2, t, B_WIDTH), F32),
        ],
        compiler_params=_cparams("parallel"),
        name="rwkv_prep",
    )(fb, fb, fb, p["mu"], p["w0"], p["w2"], p["a0"], p["a2"], p["g2"], p["kvec"], p["bd_ones"])


def _cumsum_rows(x, reverse):
    n = x.shape[0]
    ridx = lax.broadcasted_iota(jnp.int32, x.shape, 0)
    s = 1
    while s < n:
        if reverse:
            x = x + jnp.where(ridx < n - s, pltpu.roll(x, n - s, 0), 0.0)
        else:
            x = x + jnp.where(ridx >= s, pltpu.roll(x, s, 0), 0.0)
        s *= 2
    return x


def _round_robin(gens):
    gens = list(gens)
    while gens:
        alive = []
        for gen in gens:
            try:
                next(gen)
                alive.append(gen)
            except StopIteration:
                pass
        gens = alive


def _scan_kernel(*refs, seq_len, has_state, pairs, passes, inv_passes, carry_passes, unroll):
    if has_state:
        pre_ref, s0_ref, ys_ref, sf_ref, r1_scr, m1_scr, d0_scr, gam_scr, y_scr, s_scr = refs
    else:
        pre_ref, ys_ref, sf_ref, r1_scr, m1_scr, d0_scr, gam_scr, y_scr, s_scr = refs
    nc = seq_len // CHUNK
    c2 = 2 * CHUNK
    lane = lax.broadcasted_iota(jnp.int32, (CHUNK, PAIR), 1)
    head0 = lane < B_HEAD
    row = lax.broadcasted_iota(jnp.int32, (c2, c2), 0)
    col = lax.broadcasted_iota(jnp.int32, (c2, c2), 1)
    same = (row // CHUNK) == (col // CHUNK)
    eye = jnp.where(row == col, 1.0, 0.0)
    xor_rc = jnp.bitwise_xor(row, col)
    mm = functools.partial(_mm, passes=passes)
    mm_inv = functools.partial(_mm, passes=inv_passes)
    mm_carry = functools.partial(_mm, passes=carry_passes)

    def stack(x):
        return jnp.concatenate([jnp.where(head0, x, 0.0), jnp.where(head0, 0.0, x)], axis=0)

    def fold(x):
        return x[0:CHUNK] + x[CHUNK:c2]

    def chunk_rows(cc):
        return pl.ds(pl.multiple_of(cc * CHUNK, CHUNK), CHUNK)

    def chunk_local(cc, d, q):
        rows = chunk_rows(cc)
        lanes = pl.ds(q * PAIR, PAIR)
        z = 2 * q + d
        r = pre_ref[0, rows, lanes]
        v = pre_ref[1, rows, lanes]
        kk = pre_ref[2, rows, lanes]
        k = pre_ref[3 + d, rows, lanes]
        b = pre_ref[5 + d, rows, lanes]
        lw = pre_ref[7 + d, rows, lanes]
        incl = same & ((col <= row) if d == 0 else (col >= row))
        strict = same & ((col < row) if d == 0 else (col > row))
        g = _cumsum_rows(lw, reverse=(d == 1))
        gp = g - lw
        gtot = g[CHUNK - 1:CHUNK, :] if d == 0 else g[0:1, :]
        gm = 0.5 * gtot
        kx = k * jnp.exp(gm - g)
        bx = b * jnp.exp(gm - g)
        lhs = jnp.concatenate([stack(r * jnp.exp(g - gm)), stack(kk * jnp.exp(gp - gm))], axis=0)
        rhs = jnp.concatenate([kx, kx, bx, bx], axis=0)
        a_all = mm(lhs, rhs, _NT)
        yield
        a_rk = jnp.where(incl, a_all[0:c2, 0:c2], 0.0)
        a_rb = jnp.where(incl, a_all[0:c2, c2:2 * c2], 0.0)
        a_kk = jnp.where(strict, a_all[c2:2 * c2, 0:c2], 0.0)
        n_mat = jnp.where(strict, a_all[c2:2 * c2, c2:2 * c2], 0.0)
        vs = stack(v)
        akv = mm(a_kk, vs, _NN)
        y0 = mm(a_rk, vs, _NN)
        lower = (col < row) if d == 0 else (col > row)
        p_inv = eye - jnp.where(lower & (xor_rc == 1), n_mat, 0.0)
        yield
        s = 2
        while s < CHUNK:
            c_s = jnp.where(lower & (xor_rc >= s) & (xor_rc < 2 * s), n_mat, 0.0)
            t = mm_inv(c_s, p_inv, _NN)
            yield
            p_inv = p_inv - mm_inv(p_inv, t, _NN)
            yield
            s *= 2
        x = jnp.concatenate([stack(kk * jnp.exp(gp)), akv], axis=1)
        w12 = mm(p_inv, x, _NN)
        yield
        arb_w = mm(a_rb, w12, _NN)
        et = jnp.exp(gtot - g)
        bes = stack(b * et)
        d0_scr[z, cc] = mm(jnp.concatenate([vs, -w12[:, PAIR:]], axis=0),
                           jnp.concatenate([stack(k * et), bes], axis=0), _TN)
        m1_scr[z, cc] = mm(w12[:, 0:PAIR], bes, _TN)
        yield
        r1_scr[z, cc] = stack(r * jnp.exp(g)) - arb_w[:, 0:PAIR]
        y_scr[z, rows, :] = fold(y0 - arb_w[:, PAIR:])
        gam_scr[z, cc] = jnp.broadcast_to(jnp.exp(gtot), (SUBLANES, PAIR))

    def local_body(grp, carry):
        _round_robin(chunk_local(grp * unroll + u, d, q)
                     for u in range(unroll) for q in range(pairs) for d in range(2))
        return carry

    lax.fori_loop(0, nc // unroll, local_body, 0)

    for z in range(2 * pairs):
        s_scr[z] = jnp.zeros((PAIR, PAIR), F32)
        if has_state:
            s_scr[z, 0:B_HEAD, 0:B_HEAD] = s0_ref[0, z % 2, 2 * (z // 2)]
            s_scr[z, B_HEAD:PAIR, B_HEAD:PAIR] = s0_ref[0, z % 2, 2 * (z // 2) + 1]

    def carry_body(c, carry):
        for z in range(2 * pairs):
            cc = c if z % 2 == 0 else nc - 1 - c
            rows = chunk_rows(cc)
            s0 = s_scr[z]
            y_scr[z, rows, :] += fold(mm_carry(r1_scr[z, cc], s0, _NT))
            s_scr[z] = s0 * gam_scr[z, cc, 0:1, :] - mm_carry(s0, m1_scr[z, cc], _NN) + d0_scr[z, cc]
        return carry

    lax.fori_loop(0, nc, carry_body, 0)
    for q in range(pairs):
        ys_ref[:, q * PAIR:(q + 1) * PAIR] = y_scr[2 * q] + y_scr[2 * q + 1]
    for z in range(2 * pairs):
        sf_ref[0, z % 2, 2 * (z // 2)] = s_scr[z, 0:B_HEAD, 0:B_HEAD]
        sf_ref[0, z % 2, 2 * (z // 2) + 1] = s_scr[z, B_HEAD:PAIR, B_HEAD:PAIR]


def _rwkv_scan(pre, s0, n_seq, seq_len, pairs=2, passes=1, inv_passes=1, carry_passes=1, unroll=4):
    t = pre.shape[1]
    nc = seq_len // CHUNK
    sblk = pl.BlockSpec((1, 2, 2 * pairs, B_HEAD, B_HEAD), lambda b, p: (b, 0, p, 0, 0))
    mat = pltpu.VMEM((2 * pairs, nc, PAIR, PAIR), F32)
    return pl.pallas_call(
        functools.partial(_scan_kernel, seq_len=seq_len, has_state=s0 is not None, pairs=pairs,
                          passes=passes, inv_passes=inv_passes, carry_passes=carry_passes, unroll=unroll),
        grid=(n_seq, N_PAIRS // pairs),
        in_specs=([pl.BlockSpec((9, seq_len, pairs * PAIR), lambda b, p: (0, b, p))]
                  + [sblk] * (s0 is not None)),
        out_specs=[pl.BlockSpec((seq_len, pairs * PAIR), lambda b, p: (b, p)), sblk],
        out_shape=[
            jax.ShapeDtypeStruct((t, B_WIDTH), F32),
            jax.ShapeDtypeStruct((n_seq, 2, B_HEADS, B_HEAD, B_HEAD), F32),
        ],
        scratch_shapes=[mat, mat, mat, pltpu.VMEM((2 * pairs, nc, SUBLANES, PAIR), F32),
                        pltpu.VMEM((2 * pairs, seq_len, PAIR), F32),
                        pltpu.VMEM((2 * pairs, PAIR, PAIR), F32)],
        compiler_params=_cparams("parallel", "parallel"),
        name="rwkv_scan",
    )(*((pre,) if s0 is None else (pre, s0)))


def _even_out_kernel(x_ref, mod_ref, gains_ref, ya_ref, ys_ref, post_ref, lnx_ref, bdm_ref, w_ref,
                     o_ref):
    bdm = bdm_ref[...]
    ys = ys_ref[...]
    dlt = ys - _segsum(ys, bdm)
    yn = dlt * lax.rsqrt(_segsum(dlt * dlt, bdm, pieces=1) + LNX_EPS)
    yb = (yn * lnx_ref[0:1] + lnx_ref[1:2] + post_ref[1]) * post_ref[0]
    m = _dot(ya_ref[...], w_ref[0:A_WIDTH, :]) + _dot(yb, w_ref[A_WIDTH:, :])
    mod = mod_ref[0]
    o_ref[...] = x_ref[...] + mod[2:3] * (_rms(m) * gains_ref[1:2])


def _even_out(x, mod, gains, ya, ys, post, lnx, bd_mean, w, seq_len, mod_row0, tm):
    t = x.shape[0]
    row = lambda i: (i, 0)
    return pl.pallas_call(
        _even_out_kernel,
        grid=(t // tm,),
        in_specs=[
            pl.BlockSpec((tm, D_MODEL), row),
            _mod_spec(tm, seq_len, mod_row0),
            pl.BlockSpec((4, D_MODEL), lambda i: (0, 0)),
            pl.BlockSpec((tm, A_WIDTH), row),
            pl.BlockSpec((tm, B_WIDTH), row),
            pl.BlockSpec((2, tm, B_WIDTH), lambda i: (0, i, 0)),
            pl.BlockSpec((2, B_WIDTH), lambda i: (0, 0)),
            pl.BlockSpec((B_WIDTH, B_WIDTH), lambda i: (0, 0)),
            pl.BlockSpec((D_MODEL, D_MODEL), lambda i: (0, 0)),
        ],
        out_specs=pl.BlockSpec((tm, D_MODEL), row),
        out_shape=jax.ShapeDtypeStruct((t, D_MODEL), F32),
        compiler_params=_cparams("parallel"),
        name="even_out",
    )(x, mod, gains, ya, ys, post, lnx, bd_mean, w)


def _odd_kernel(x_ref, mod_ref, gains_ref, csc_ref, cl_ref, sl_ref, w_ref, o_ref,
                tc_scr, ts_scr, m_scr, *, seq_len):
    mod = mod_ref[0]
    gd = C_GROUP_DIM
    n_seq = x_ref.shape[0] // seq_len
    csc = csc_ref[...]

    for s in range(n_seq):
        rows = pl.ds(s * seq_len, seq_len)
        h = _modnorm(x_ref[rows, :], gains_ref[0:1], mod[1:2], mod[0:1]).astype(BF16)
        for g in range(C_GROUPS):
            t = jnp.dot(h[:, g * gd:(g + 1) * gd], csc, preferred_element_type=F32)
            tc_scr[rows, g * gd:(g + 1) * gd] = t[:, 0:gd].astype(BF16)
            ts_scr[rows, g * gd:(g + 1) * gd] = t[:, gd:].astype(BF16)

    def position_dft(s, cb):
        rows = pl.ds(s * seq_len, seq_len)
        cols = pl.ds(cb * ODD_COLS, ODD_COLS)
        f = (jnp.dot(cl_ref[...], tc_scr[rows, cols], preferred_element_type=F32)
             - jnp.dot(sl_ref[...], ts_scr[rows, cols], preferred_element_type=F32))
        yield
        part = jnp.dot(f.astype(BF16), w_ref[cols, :], preferred_element_type=F32)
        yield
        if cb == 0:
            m_scr[rows, :] = part
        else:
            m_scr[rows, :] += part

    _round_robin(position_dft(s, cb) for s in range(n_seq) for cb in range(D_MODEL // ODD_COLS))
    o_ref[...] = x_ref[...] + mod[2:3] * (_rms(m_scr[...]) * gains_ref[1:2])


def _dft_tables(n):
    idx = np.arange(n, dtype=np.int64)
    ang = 2.0 * np.pi * ((idx[:, None] * idx[None, :]) % n).astype(np.float64) / n
    s = 1.0 / math.sqrt(n)
    return np.cos(ang) * s, np.sin(ang) * s


def _odd_mixer(x, mod, gains, w, n_seq, seq_len, mod_row0):
    cc, sc = _dft_tables(C_GROUP_DIM)
    cl, sl = _dft_tables(seq_len)
    csc, cl, sl = (jnp.asarray(a, F32).astype(BF16) for a in (np.concatenate([cc, sc], axis=1), cl, sl))
    full = lambda shape: pl.BlockSpec(shape, lambda i: (0,) * len(shape))
    row = lambda i: (i, 0)
    tm = max(seq_len, ODD_ROWS)
    assert tm % seq_len == 0 and (n_seq * seq_len) % tm == 0
    return pl.pallas_call(
        functools.partial(_odd_kernel, seq_len=seq_len),
        grid=(n_seq * seq_len // tm,),
        in_specs=[
            pl.BlockSpec((tm, D_MODEL), row),
            _mod_spec(tm, seq_len, mod_row0),
            full((4, D_MODEL)), full((C_GROUP_DIM, 2 * C_GROUP_DIM)),
            full((seq_len, seq_len)), full((seq_len, seq_len)), full((D_MODEL, D_MODEL)),
        ],
        out_specs=pl.BlockSpec((tm, D_MODEL), row),
        out_shape=jax.ShapeDtypeStruct(x.shape, F32),
        scratch_shapes=[pltpu.VMEM((tm, D_MODEL), BF16), pltpu.VMEM((tm, D_MODEL), BF16),
                        pltpu.VMEM((tm, D_MODEL), F32)],
        compiler_params=_cparams("parallel"),
        name="odd_mixer",
    )(x, mod, gains, csc, cl, sl, w)


def _ffn_kernel(x_ref, mod_ref, gains_ref, wu_ref, wg_ref, cw_ref, cb_ref, wo_ref, o_ref,
                h_scr, acc_scr, *, seq_len):
    j = pl.program_id(1)
    mod = mod_ref[0]

    @pl.when(j == 0)
    def _():
        h_scr[...] = _modnorm(x_ref[...], gains_ref[2:3], mod[4:5], mod[3:4]).astype(BF16)
        acc_scr[...] = jnp.zeros_like(acc_scr)

    n_sub = h_scr.shape[0] // FF_ROWS
    gate = [None] * n_sub
    zero_row = jnp.zeros((1, FF_CHUNK), F32)

    def sub_tile(i):
        rows = pl.ds(i * FF_ROWS, FF_ROWS)
        h = h_scr[rows, :]
        u = jnp.dot(h, wu_ref[...], preferred_element_type=F32)
        g = gate[i] = jnp.dot(h, wg_ref[...], preferred_element_type=F32)
        yield
        before = zero_row if (i * FF_ROWS) % seq_len == 0 else gate[i - 1][FF_ROWS - 1:FF_ROWS, :]
        after = zero_row if ((i + 1) * FF_ROWS) % seq_len == 0 else gate[i + 1][0:1, :]
        pos = lax.broadcasted_iota(jnp.int32, (FF_ROWS, 1), 0)
        prev = jnp.where(pos == 0, before, pltpu.roll(g, 1, 0))
        nxt = jnp.where(pos == FF_ROWS - 1, after, pltpu.roll(g, FF_ROWS - 1, 0))
        gc = prev * cw_ref[0:1] + g * cw_ref[1:2] + nxt * cw_ref[2:3] + cb_ref[...]
        act = (_silu(gc) * u).astype(BF16)
        yield
        acc_scr[rows, :] += jnp.dot(act, wo_ref[...], preferred_element_type=F32)

    _round_robin(sub_tile(i) for i in range(n_sub))

    @pl.when(j == pl.num_programs(1) - 1)
    def _():
        o_ref[...] = x_ref[...] + mod[5:6] * (_rms(acc_scr[...]) * gains_ref[3:4])


def _conv_ffn(x, mod, gains, layer, w_in, conv_w, conv_b, w_out, seq_len, mod_row0, tm):
    t = x.shape[0]
    nj = D_FF // FF_CHUNK
    assert seq_len % FF_ROWS == 0 and tm % seq_len == 0
    return pl.pallas_call(
        functools.partial(_ffn_kernel, seq_len=seq_len),
        grid=(t // tm, nj),
        in_specs=[
            pl.BlockSpec((tm, D_MODEL), lambda i, j: (i, 0)),
            _mod_spec(tm, seq_len, mod_row0),
            pl.BlockSpec((4, D_MODEL), lambda i, j: (0, 0)),
            pl.BlockSpec((None, D_MODEL, FF_CHUNK), lambda i, j: (layer, 0, j)),
            pl.BlockSpec((None, D_MODEL, FF_CHUNK), lambda i, j: (layer, 0, nj + j)),
            pl.BlockSpec((3, FF_CHUNK), lambda i, j: (0, j)),
            pl.BlockSpec((1, FF_CHUNK), lambda i, j: (0, j)),
            pl.BlockSpec((None, FF_CHUNK, D_MODEL), lambda i, j: (layer, j, 0)),
        ],
        out_specs=pl.BlockSpec((tm, D_MODEL), lambda i, j: (i, 0)),
        out_shape=jax.ShapeDtypeStruct((t, D_MODEL), F32),
        scratch_shapes=[pltpu.VMEM((tm, D_MODEL), BF16), pltpu.VMEM((tm, D_MODEL), F32)],
        compiler_params=_cparams("parallel", "arbitrary"),
        name="conv_ffn",
    )(x, mod, gains, w_in, w_in, conv_w, conv_b, w_out)


def _rope_tables(seq_len):
    rows = seq_len // GRID_W
    row = jnp.repeat(jnp.arange(rows, dtype=F32), GRID_W)
    col = jnp.tile(jnp.arange(GRID_W, dtype=F32), rows)
    n_freq = A_HALF // 4
    inv = ROPE_BASE ** (-jnp.arange(n_freq, dtype=F32) / n_freq)
    ang = jnp.concatenate([row[:, None] * inv, col[:, None] * inv], axis=-1)
    cos_t = jnp.tile(jnp.repeat(jnp.cos(ang), 2, axis=1), (1, 2))
    sin_t = jnp.tile(jnp.repeat(jnp.sin(ang), 2, axis=1), (1, 2))
    sign = jnp.where(jnp.arange(A_HEAD_DIM) % 2 == 0, -1.0, 1.0).astype(F32)
    return cos_t, sin_t * sign


def _pad_lora(w):
    z = jnp.zeros_like(w[0])
    return jnp.stack([jnp.concatenate([w[0], z], axis=0), jnp.concatenate([z, w[1]], axis=0)])


def _block_diag(value):
    head = np.arange(B_WIDTH) // B_HEAD
    return jnp.asarray(np.where(head[:, None] == head[None, :], value, 0.0), BF16)


def _tile_rows(seq_len):
    return max(seq_len, 1024)


def kernel(x_prompt, x_sample, cache_k, cache_v, state_wkv, c, c_ctx, w_ada, b_ada, norm_gains,
           w_in_even, w_out_even, diff_lambda, diff_subln, rwkv_shift_mu, rwkv_w0, rwkv_w2, rwkv_a0,
           rwkv_a2, rwkv_g2, rwkv_kvec, rwkv_lnx, w_out_odd, w_ffn_in, ffn_conv, ffn_conv_b,
           w_ffn_out):
    n_ctx, l_ctx, _ = x_prompt.shape
    n_lat, l_lat, _ = x_sample.shape
    assert 1 + n_lat <= MOD_ROWS
    cvec = jnp.concatenate(
        [c_ctx[None, :], c, jnp.zeros((MOD_ROWS - 1 - n_lat, D_MODEL), F32)], axis=0)
    mods = _modulation(cvec, w_ada, b_ada).reshape(DEPTH, MOD_ROWS, 6, D_MODEL)

    per_layer_bf16 = lambda w: [w[i].astype(BF16) for i in range(w.shape[0])]
    w_in_even_b = per_layer_bf16(w_in_even)
    w_out_even_b = per_layer_bf16(w_out_even)
    w_out_odd_b = per_layer_bf16(w_out_odd)
    w_ffn_in_b = w_ffn_in.astype(BF16)
    w_ffn_out_b = w_ffn_out.astype(BF16)
    bd_ones = _block_diag(1.0)
    bd_mean = _block_diag(1.0 / B_HEAD)
    cos_t, sin_t = _rope_tables(l_lat)

    def run_group(x, n_seq, seq_len, mod_row0, latent):
        tm = _tile_rows(seq_len)
        ctx_out = None
        for l in range(DEPTH):
            mod = mods[l]
            gains = norm_gains[l]
            if l % 2 == 0:
                e = l // 2
                lam_init = 0.8 - 0.6 * math.exp(-0.3 * l)
                q, k, v, fb, *kv_cache = _even_in(
                    x, mod, gains, w_in_even_b[e], seq_len, mod_row0, 256,
                    rope=(cos_t, sin_t) if latent else None, cache=not latent)
                if latent:
                    past = cache_k.shape[2]
                    ctx = (cache_k[:, e].reshape(n_seq * past, A_WIDTH),
                           cache_v[:, e].reshape(n_seq * past, A_WIDTH))
                    s0 = state_wkv[:, e]
                else:
                    ctx = None
                    s0 = None
                ya = _attention(q, k, v, diff_lambda[e], diff_subln[e][None, :], lam_init,
                                n_seq, seq_len, ctx)
                prep = {
                    "mu": rwkv_shift_mu[e], "w0": rwkv_w0[e], "w2": _pad_lora(rwkv_w2[e]).astype(BF16),
                    "a0": rwkv_a0[e], "a2": _pad_lora(rwkv_a2[e]).astype(BF16),
                    "g2": rwkv_g2[e].astype(BF16), "kvec": rwkv_kvec[e], "bd_ones": bd_ones,
                }
                pre, post = _rwkv_prep(fb, prep, seq_len)
                ys, s_fin = _rwkv_scan(pre, s0, n_seq, seq_len)
                x = _even_out(x, mod, gains, ya, ys, post, rwkv_lnx[e], bd_mean, w_out_even_b[e],
                              seq_len, mod_row0, 256)
                if not latent:
                    ctx_out = (*kv_cache, s_fin)
            else:
                x = _odd_mixer(x, mod, gains, w_out_odd_b[l // 2], n_seq, seq_len, mod_row0)
            x = _conv_ffn(x, mod, gains, l, w_ffn_in_b, ffn_conv[l], ffn_conv_b[l][None, :],
                          w_ffn_out_b, seq_len, mod_row0, tm)
        return x, ctx_out

    y_ctx, (k_new, v_new, s_new) = run_group(
        x_prompt.reshape(n_ctx * l_ctx, D_MODEL), n_ctx, l_ctx, 0, False)
    y_lat, _ = run_group(x_sample.reshape(n_lat * l_lat, D_MODEL), n_lat, l_lat, 1, True)

    n_even = (DEPTH + 1) // 2
    assert n_even == 1
    return (
        y_ctx.reshape(n_ctx, l_ctx, D_MODEL),
        y_lat.reshape(n_lat, l_lat, D_MODEL),
        k_new,
        v_new,
        s_new.reshape(n_ctx, n_even, 2, B_HEADS, B_HEAD, B_HEAD),
    )
```

```python
import functools
import math

import numpy as np
import jax
import jax.numpy as jnp
from jax import lax
from jax.experimental import pallas as pl
from jax.experimental.pallas import tpu as pltpu

F32 = jnp.float32
BF16 = jnp.bfloat16

D_MODEL = 1024
DEPTH = 2
GRID_W = 64
A_WIDTH = D_MODEL // 2
A_HEADS = 4
A_HEAD_DIM = A_WIDTH // A_HEADS
A_HALF = A_HEAD_DIM // 2
B_WIDTH = D_MODEL - A_WIDTH
B_HEAD = 64
B_HEADS = B_WIDTH // B_HEAD
LORA_W = 64
LORA_A = 64
LORA_G = 128
IN_B = 3 * B_WIDTH + 2 * LORA_W + 2 * LORA_A + LORA_G
IN_EVEN = 3 * A_WIDTH + IN_B
C_GROUPS = 8
C_GROUP_DIM = D_MODEL // C_GROUPS
D_FF = 2816
ROPE_BASE = 10000.0
RMS_EPS = 1e-6
LNX_EPS = 64e-5

LANES = 128
SUBLANES = 8
PAIR = 2 * B_HEAD
N_PAIRS = B_HEADS // 2
CHUNK = 64
FF_CHUNK = 256
FF_ROWS = 256
ATTN_Q_ROWS = 256
ODD_ROWS = 512
ODD_COLS = 256
VMEM_LIMIT = 48 * 1024 * 1024
VMEM_LIMIT_LARGE = 56 * 1024 * 1024
MOD_ROWS = 8


def _cparams(*sem):
    return pltpu.CompilerParams(dimension_semantics=sem, vmem_limit_bytes=VMEM_LIMIT)


def _sigmoid(x):
    return 1.0 / (1.0 + jnp.exp(-x))


def _silu(x):
    return x * (0.5 + 0.5 * jnp.tanh(0.5 * x))


def _dot(a, b):
    return jnp.dot(a.astype(BF16), b.astype(BF16), preferred_element_type=F32)


_NN = (((1,), (0,)), ((), ()))
_NT = (((1,), (1,)), ((), ()))
_TN = (((0,), (0,)), ((), ()))


def _split_bf16(x, n):
    parts = []
    rem = x
    for i in range(n):
        p = rem.astype(BF16)
        parts.append(p)
        if i + 1 < n:
            rem = rem - p.astype(F32)
    return parts


def _mm(a, b, dims, passes):
    if passes == 1:
        return lax.dot_general(a.astype(BF16), b.astype(BF16), dims, preferred_element_type=F32)
    a_hi, a_lo = _split_bf16(a, 2)
    b_hi, b_lo = _split_bf16(b, 2)
    dg = functools.partial(lax.dot_general, dimension_numbers=dims, preferred_element_type=F32)
    return dg(a_hi, b_hi) + (dg(a_hi, b_lo) + dg(a_lo, b_hi))


def _segsum(x, bd, pieces=2):
    acc = None
    for p in _split_bf16(x, pieces):
        t = jnp.dot(p, bd, preferred_element_type=F32)
        acc = t if acc is None else acc + t
    return acc


def _rms(x):
    return x * lax.rsqrt(jnp.mean(x * x, axis=-1, keepdims=True) + RMS_EPS)


def _modnorm(x, gain, scale, shift):
    return _rms(x) * gain * (1.0 + scale) + shift


def _mod_kernel(c_ref, w_ref, b_ref, o_ref):
    c = c_ref[...]
    s = c * _sigmoid(c)
    o_ref[0] = _dot(s, w_ref[0]) + b_ref[0]


def _modulation(cvec, w_ada, b_ada):
    tn = 1536
    n = 6 * D_MODEL
    return pl.pallas_call(
        _mod_kernel,
        grid=(DEPTH, n // tn),
        in_specs=[
            pl.BlockSpec((MOD_ROWS, D_MODEL), lambda l, j: (0, 0)),
            pl.BlockSpec((1, D_MODEL, tn), lambda l, j: (l, 0, j)),
            pl.BlockSpec((1, 1, tn), lambda l, j: (l, 0, j)),
        ],
        out_specs=pl.BlockSpec((1, MOD_ROWS, tn), lambda l, j: (l, 0, j)),
        out_shape=jax.ShapeDtypeStruct((DEPTH, MOD_ROWS, n), F32),
        compiler_params=_cparams("parallel", "parallel"),
        name="modulation",
    )(cvec, w_ada, b_ada.reshape(DEPTH, 1, n))


def _mod_spec(tm, seq_len, mod_row0):
    if mod_row0 == 0:
        return pl.BlockSpec((1, 6, D_MODEL), lambda i, *_: (0, 0, 0))
    return pl.BlockSpec((1, 6, D_MODEL), lambda i, *_: (mod_row0 + (i * tm) // seq_len, 0, 0))


def _rope(x, cos_t, sin_t):
    lane = lax.broadcasted_iota(jnp.int32, x.shape, 1)
    nxt = pltpu.roll(x, LANES - 1, 1)
    prv = pltpu.roll(x, 1, 1)
    partner = jnp.where(lane % 2 == 0, nxt, prv)
    return x * cos_t + partner * sin_t


def _even_in_kernel(*refs, rope, cache):
    x_ref, mod_ref, gains_ref, w_ref = refs[:4]
    refs = refs[4:]
    if rope:
        cos_ref, sin_ref = refs[:2]
        refs = refs[2:]
    q_ref, k_ref, v_ref, fb_ref = refs[:4]
    cache_refs = refs[4:]
    mod = mod_ref[0]
    h = _modnorm(x_ref[...], gains_ref[0:1], mod[1:2], mod[0:1]).astype(BF16)
    a = A_WIDTH
    hd = A_HEAD_DIM
    for n, out_ref in enumerate((q_ref, k_ref, v_ref)):
        y = jnp.dot(h, w_ref[:, n * a:(n + 1) * a], preferred_element_type=F32)
        if cache and n > 0:
            for hh in range(A_HEADS):
                cache_refs[n - 1][0, 0, :, hh, :] = y[:, hh * hd:(hh + 1) * hd]
        if rope and n < 2:
            for hh in range(A_HEADS):
                cols = slice(hh * hd, (hh + 1) * hd)
                out_ref[:, cols] = _rope(y[:, cols], cos_ref[...], sin_ref[...]).astype(BF16)
        else:
            out_ref[...] = y.astype(BF16)
    fb_ref[...] = jnp.dot(h, w_ref[:, 3 * a:], preferred_element_type=F32)


def _even_in(x, mod, gains, w, seq_len, mod_row0, tm, rope=None, cache=False):
    t = x.shape[0]
    assert seq_len % tm == 0
    per_seq = seq_len // tm
    row = lambda i: (i, 0)
    operands = [x, mod, gains, w]
    in_specs = [
        pl.BlockSpec((tm, D_MODEL), row),
        _mod_spec(tm, seq_len, mod_row0),
        pl.BlockSpec((4, D_MODEL), lambda i: (0, 0)),
        pl.BlockSpec((D_MODEL, IN_EVEN), lambda i: (0, 0)),
    ]
    if rope is not None:
        tab = pl.BlockSpec((tm, A_HEAD_DIM), lambda i: (i % per_seq, 0))
        operands += list(rope)
        in_specs += [tab, tab]
    out_specs = [pl.BlockSpec((tm, A_WIDTH), row)] * 3 + [pl.BlockSpec((tm, IN_B), row)]
    out_shape = [jax.ShapeDtypeStruct((t, A_WIDTH), BF16)] * 3 + [jax.ShapeDtypeStruct((t, IN_B), F32)]
    if cache:
        out_specs += [pl.BlockSpec((1, 1, tm, A_HEADS, A_HEAD_DIM),
                                   lambda i: (i // per_seq, 0, i % per_seq, 0, 0))] * 2
        out_shape += [jax.ShapeDtypeStruct((t // seq_len, 1, seq_len, A_HEADS, A_HEAD_DIM), F32)] * 2
    return pl.pallas_call(
        functools.partial(_even_in_kernel, rope=rope is not None, cache=cache),
        grid=(t // tm,),
        in_specs=in_specs,
        out_specs=out_specs,
        out_shape=out_shape,
        compiler_params=_cparams("parallel"),
        name="even_in",
    )(*operands)


def _attn_kernel(*refs, has_ctx, lam_init):
    if has_ctx:
        lam_ref, sub_ref, q_ref, k_ref, v_ref, ck_ref, cv_ref, o_ref = refs
    else:
        lam_ref, sub_ref, q_ref, k_ref, v_ref, o_ref = refs
    lp = lam_ref[...]
    l1 = jnp.sum(lp[0:1] * lp[1:2], axis=-1, keepdims=True)
    l2 = jnp.sum(lp[2:3] * lp[3:4], axis=-1, keepdims=True)
    lam = jnp.exp(l1) - jnp.exp(l2) + lam_init
    hd = A_HEAD_DIM
    lane = lax.broadcasted_iota(jnp.int32, (q_ref.shape[0], hd), 1)
    scale = A_HALF ** -0.5

    def head(h):
        cols = pl.ds(h * hd, hd)
        q = q_ref[:, cols]
        k = k_ref[:, cols].astype(BF16)
        if has_ctx:
            ck = ck_ref[:, cols].astype(BF16)
        scores = []
        for m in range(2):
            qm = jnp.where((lane < A_HALF) == (m == 0), q, 0.0).astype(BF16)
            s = lax.dot_general(qm, k, _NT, preferred_element_type=F32) * scale
            sc = lax.dot_general(qm, ck, _NT, preferred_element_type=F32) * scale if has_ctx else None
            scores.append((s, sc))
        yield
        probs = []
        for s, sc in scores:
            mx = jnp.max(s, axis=-1, keepdims=True)
            if has_ctx:
                mx = jnp.maximum(mx, jnp.max(sc, axis=-1, keepdims=True))
                ec = jnp.exp(sc - mx)
            e = jnp.exp(s - mx)
            z = jnp.sum(e, axis=-1, keepdims=True)
            if has_ctx:
                z = z + jnp.sum(ec, axis=-1, keepdims=True)
            inv = 1.0 / z
            probs.append((e * inv, ec * inv if has_ctx else None))
        w = (probs[0][0] - lam * probs[1][0]).astype(BF16)
        if has_ctx:
            wc = (probs[0][1] - lam * probs[1][1]).astype(BF16)
        yield
        o = jnp.dot(w, v_ref[:, cols].astype(BF16), preferred_element_type=F32)
        if has_ctx:
            o = o + jnp.dot(wc, cv_ref[:, cols].astype(BF16), preferred_element_type=F32)
        yield
        o_ref[:, cols] = _rms(o) * sub_ref[...] * (1.0 - lam_init)

    _round_robin(head(h) for h in range(A_HEADS))


def _attention(q, k, v, lam_p, subln, lam_init, n_seq, seq_len, ctx=None):
    t = q.shape[0]
    tq = min(seq_len, ATTN_Q_ROWS if ctx is None else ATTN_Q_ROWS // 2)
    nq = seq_len // tq
    small = [pl.BlockSpec((4, A_HALF), lambda b, i: (0, 0)),
             pl.BlockSpec((1, A_HEAD_DIM), lambda b, i: (0, 0))]
    qblk = pl.BlockSpec((tq, A_WIDTH), lambda b, i: (b * nq + i, 0))
    kblk = pl.BlockSpec((seq_len, A_WIDTH), lambda b, i: (b, 0))
    operands = [lam_p, subln, q, k, v]
    in_specs = small + [qblk, kblk, kblk]
    if ctx is not None:
        ck, cv = ctx
        cblk = pl.BlockSpec((ck.shape[0] // n_seq, A_WIDTH), lambda b, i: (b, 0))
        operands += [ck, cv]
        in_specs += [cblk, cblk]
    return pl.pallas_call(
        functools.partial(_attn_kernel, has_ctx=ctx is not None, lam_init=lam_init),
        grid=(n_seq, nq),
        in_specs=in_specs,
        out_specs=qblk,
        out_shape=jax.ShapeDtypeStruct((t, A_WIDTH), F32),
        compiler_params=_cparams("parallel", "parallel"),
        name="diff_attn",
    )(*operands)


def _shift_rows(f, halo_prev, halo_next, start, seq_len):
    tm = f.shape[0]
    local = lax.broadcasted_iota(jnp.int32, (tm, 1), 0)
    halo_prev = jnp.where(start == 0, 0.0, halo_prev)
    halo_next = jnp.where(start + tm == seq_len, 0.0, halo_next)
    prev = jnp.where(local == 0, halo_prev, pltpu.roll(f, 1, 0))
    nxt = jnp.where(local == tm - 1, halo_next, pltpu.roll(f, tm - 1, 0))
    return prev, nxt


def _prep_kernel(fb_ref, fp_ref, fn_ref, mu_ref, w0_ref, w2_ref, a0_ref, a2_ref, g2_ref, kvec_ref,
                 bd_ref, scan_ref, post_ref, *, seq_len):
    f = fb_ref[...]
    tm = f.shape[0]
    start = (pl.program_id(0) * tm) % seq_len
    prev, nxt = _shift_rows(f, fp_ref[SUBLANES - 1:SUBLANES, :], fn_ref[0:1, :], start, seq_len)
    f = f * (1.0 - mu_ref[0:1] - mu_ref[1:2]) + mu_ref[0:1] * prev + mu_ref[1:2] * nxt
    bw = B_WIDTH
    r = f[:, 0:bw]
    k = f[:, bw:2 * bw]
    v = f[:, 2 * bw:3 * bw]
    off = 3 * bw
    wd = jnp.tanh(f[:, off:off + 2 * LORA_W])
    ad = f[:, off + 2 * LORA_W:off + 2 * LORA_W + 2 * LORA_A]
    gd = f[:, off + 2 * LORA_W + 2 * LORA_A:]
    bd = bd_ref[...]
    kv = kvec_ref[...]
    g = _dot(_sigmoid(gd), g2_ref[...])
    kk = k * kv[0:1]
    kk = kk * lax.rsqrt(_segsum(kk * kk, bd, pieces=1) + 1e-12)
    scan_ref[0] = r
    scan_ref[1] = v
    scan_ref[2] = kk
    ksum = None
    for d in range(2):
        w_raw = w0_ref[d:d + 1] + _dot(wd, w2_ref[d])
        scan_ref[7 + d] = -math.exp(-0.5) * _sigmoid(w_raw)
        a = _sigmoid(a0_ref[d:d + 1] + _dot(ad, a2_ref[d]))
        kd = k * (1.0 + (a - 1.0) * kv[1:2])
        scan_ref[3 + d] = kd
        scan_ref[5 + d] = kk * a
        ksum = kd if ksum is None else ksum + kd
    post_ref[0] = g
    post_ref[1] = _segsum(r * ksum * kv[2:3], bd) * v


def _rwkv_prep(fb, p, seq_len, tm=256):
    t = fb.shape[0]
    assert seq_len % tm == 0
    nh = tm // SUBLANES
    last = t // SUBLANES - 1
    full = lambda shape: pl.BlockSpec(shape, lambda i: (0,) * len(shape))
    return pl.pallas_call(
        functools.partial(_prep_kernel, seq_len=seq_len),
        grid=(t // tm,),
        in_specs=[
            pl.BlockSpec((tm, IN_B), lambda i: (i, 0)),
            pl.BlockSpec((SUBLANES, IN_B), lambda i: (jnp.maximum(i * nh - 1, 0), 0)),
            pl.BlockSpec((SUBLANES, IN_B), lambda i: (jnp.minimum((i + 1) * nh, last), 0)),
            full((2, IN_B)), full((2, B_WIDTH)), full((2, 2 * LORA_W, B_WIDTH)),
            full((2, B_WIDTH)), full((2, 2 * LORA_A, B_WIDTH)), full((LORA_G, B_WIDTH)),
            full((3, B_WIDTH)), full((B_WIDTH, B_WIDTH)),
        ],
        out_specs=[
            pl.BlockSpec((9, tm, B_WIDTH), lambda i: (0, i, 0)),
            pl.BlockSpec((2, tm, B_WIDTH), lambda i: (0, i, 0)),
        ],
        out_shape=[
            jax.ShapeDtypeStruct((9, t, B_WIDTH), F32),
            jax.ShapeDtypeStruct((2, t, B_WIDTH), F32),
        ],
        compiler_params=_cparams("parallel"),
        name="rwkv_prep",
    )(fb, fb, fb, p["mu"], p["w0"], p["w2"], p["a0"], p["a2"], p["g2"], p["kvec"], p["bd_ones"])


def _cumsum_rows(x, reverse):
    n = x.shape[0]
    ridx = lax.broadcasted_iota(jnp.int32, x.shape, 0)
    s = 1
    while s < n:
        if reverse:
            x = x + jnp.where(ridx < n - s, pltpu.roll(x, n - s, 0), 0.0)
        else:
            x = x + jnp.where(ridx >= s, pltpu.roll(x, s, 0), 0.0)
        s *= 2
    return x


def _round_robin(gens):
    gens = list(gens)
    while gens:
        alive = []
        for gen in gens:
            try:
                next(gen)
                alive.append(gen)
            except StopIteration:
                pass
        gens = alive


def _scan_kernel(*refs, seq_len, has_state, pairs, passes, inv_passes, carry_passes, unroll):
    if has_state:
        pre_ref, s0_ref, ys_ref, sf_ref, r1_scr, m1_scr, d0_scr, gam_scr, y_scr, s_scr = refs
    else:
        pre_ref, ys_ref, sf_ref, r1_scr, m1_scr, d0_scr, gam_scr, y_scr, s_scr = refs
    nc = seq_len // CHUNK
    c2 = 2 * CHUNK
    lane = lax.broadcasted_iota(jnp.int32, (CHUNK, PAIR), 1)
    head0 = lane < B_HEAD
    row = lax.broadcasted_iota(jnp.int32, (c2, c2), 0)
    col = lax.broadcasted_iota(jnp.int32, (c2, c2), 1)
    same = (row // CHUNK) == (col // CHUNK)
    eye = jnp.where(row == col, 1.0, 0.0)
    xor_rc = jnp.bitwise_xor(row, col)
    mm = functools.partial(_mm, passes=passes)
    mm_inv = functools.partial(_mm, passes=inv_passes)
    mm_carry = functools.partial(_mm, passes=carry_passes)

    def stack(x):
        return jnp.concatenate([jnp.where(head0, x, 0.0), jnp.where(head0, 0.0, x)], axis=0)

    def fold(x):
        return x[0:CHUNK] + x[CHUNK:c2]

    def chunk_rows(cc):
        return pl.ds(pl.multiple_of(cc * CHUNK, CHUNK), CHUNK)

    def chunk_local(cc, d, q):
        rows = chunk_rows(cc)
        lanes = pl.ds(q * PAIR, PAIR)
        z = 2 * q + d
        r = pre_ref[0, rows, lanes]
        v = pre_ref[1, rows, lanes]
        kk = pre_ref[2, rows, lanes]
        k = pre_ref[3 + d, rows, lanes]
        b = pre_ref[5 + d, rows, lanes]
        lw = pre_ref[7 + d, rows, lanes]
        incl = same & ((col <= row) if d == 0 else (col >= row))
        strict = same & ((col < row) if d == 0 else (col > row))
        g = _cumsum_rows(lw, reverse=(d == 1))
        gp = g - lw
        gtot = g[CHUNK - 1:CHUNK, :] if d == 0 else g[0:1, :]
        gm = 0.5 * gtot
        kx = k * jnp.exp(gm - g)
        bx = b * jnp.exp(gm - g)
        lhs = jnp.concatenate([stack(r * jnp.exp(g - gm)), stack(kk * jnp.exp(gp - gm))], axis=0)
        rhs = jnp.concatenate([kx, kx, bx, bx], axis=0)
        a_all = mm(lhs, rhs, _NT)
        yield
        a_rk = jnp.where(incl, a_all[0:c2, 0:c2], 0.0)
        a_rb = jnp.where(incl, a_all[0:c2, c2:2 * c2], 0.0)
        a_kk = jnp.where(strict, a_all[c2:2 * c2, 0:c2], 0.0)
        n_mat = jnp.where(strict, a_all[c2:2 * c2, c2:2 * c2], 0.0)
        vs = stack(v)
        akv = mm(a_kk, vs, _NN)
        y0 = mm(a_rk, vs, _NN)
        lower = (col < row) if d == 0 else (col > row)
        p_inv = eye - jnp.where(lower & (xor_rc == 1), n_mat, 0.0)
        yield
        s = 2
        while s < CHUNK:
            c_s = jnp.where(lower & (xor_rc >= s) & (xor_rc < 2 * s), n_mat, 0.0)
            t = mm_inv(c_s, p_inv, _NN)
            yield
            p_inv = p_inv - mm_inv(p_inv, t, _NN)
            yield
            s *= 2
        x = jnp.concatenate([stack(kk * jnp.exp(gp)), akv], axis=1)
        w12 = mm(p_inv, x, _NN)
        yield
        arb_w = mm(a_rb, w12, _NN)
        et = jnp.exp(gtot - g)
        bes = stack(b * et)
        d0_scr[z, cc] = mm(jnp.concatenate([vs, -w12[:, PAIR:]], axis=0),
                           jnp.concatenate([stack(k * et), bes], axis=0), _TN)
        m1_scr[z, cc] = mm(w12[:, 0:PAIR], bes, _TN)
        yield
        r1_scr[z, cc] = stack(r * jnp.exp(g)) - arb_w[:, 0:PAIR]
        y_scr[z, rows, :] = fold(y0 - arb_w[:, PAIR:])
        gam_scr[z, cc] = jnp.broadcast_to(jnp.exp(gtot), (SUBLANES, PAIR))

    def local_body(grp, carry):
        _round_robin(chunk_local(grp * unroll + u, d, q)
                     for u in range(unroll) for q in range(pairs) for d in range(2))
        return carry

    lax.fori_loop(0, nc // unroll, local_body, 0)

    for z in range(2 * pairs):
        s_scr[z] = jnp.zeros((PAIR, PAIR), F32)
        if has_state:
            s_scr[z, 0:B_HEAD, 0:B_HEAD] = s0_ref[0, z % 2, 2 * (z // 2)]
            s_scr[z, B_HEAD:PAIR, B_HEAD:PAIR] = s0_ref[0, z % 2, 2 * (z // 2) + 1]

    def carry_body(c, carry):
        for z in range(2 * pairs):
            cc = c if z % 2 == 0 else nc - 1 - c
            rows = chunk_rows(cc)
            s0 = s_scr[z]
            y_scr[z, rows, :] += fold(mm_carry(r1_scr[z, cc], s0, _NT))
            s_scr[z] = s0 * gam_scr[z, cc, 0:1, :] - mm_carry(s0, m1_scr[z, cc], _NN) + d0_scr[z, cc]
        return carry

    lax.fori_loop(0, nc, carry_body, 0)
    for q in range(pairs):
        ys_ref[:, q * PAIR:(q + 1) * PAIR] = y_scr[2 * q] + y_scr[2 * q + 1]
    for z in range(2 * pairs):
        sf_ref[0, z % 2, 2 * (z // 2)] = s_scr[z, 0:B_HEAD, 0:B_HEAD]
        sf_ref[0, z % 2, 2 * (z // 2) + 1] = s_scr[z, B_HEAD:PAIR, B_HEAD:PAIR]


def _rwkv_scan(pre, s0, n_seq, seq_len, pairs=2, passes=1, inv_passes=1, carry_passes=1, unroll=4):
    t = pre.shape[1]
    nc = seq_len // CHUNK
    sblk = pl.BlockSpec((1, 2, 2 * pairs, B_HEAD, B_HEAD), lambda b, p: (b, 0, p, 0, 0))
    mat = pltpu.VMEM((2 * pairs, nc, PAIR, PAIR), F32)
    return pl.pallas_call(
        functools.partial(_scan_kernel, seq_len=seq_len, has_state=s0 is not None, pairs=pairs,
                          passes=passes, inv_passes=inv_passes, carry_passes=carry_passes, unroll=unroll),
        grid=(n_seq, N_PAIRS // pairs),
        in_specs=([pl.BlockSpec((9, seq_len, pairs * PAIR), lambda b, p: (0, b, p))]
                  + [sblk] * (s0 is not None)),
        out_specs=[pl.BlockSpec((seq_len, pairs * PAIR), lambda b, p: (b, p)), sblk],
        out_shape=[
            jax.ShapeDtypeStruct((t, B_WIDTH), F32),
            jax.ShapeDtypeStruct((n_seq, 2, B_HEADS, B_HEAD, B_HEAD), F32),
        ],
        scratch_shapes=[mat, mat, mat, pltpu.VMEM((2 * pairs, nc, SUBLANES, PAIR), F32),
                        pltpu.VMEM((2 * pairs, seq_len, PAIR), F32),
                        pltpu.VMEM((2 * pairs, PAIR, PAIR), F32)],
        compiler_params=_cparams("parallel", "parallel"),
        name="rwkv_scan",
    )(*((pre,) if s0 is None else (pre, s0)))


def _even_out_kernel(x_ref, mod_ref, gains_ref, ya_ref, ys_ref, post_ref, lnx_ref, bdm_ref, w_ref,
                     o_ref):
    bdm = bdm_ref[...]
    ys = ys_ref[...]
    dlt = ys - _segsum(ys, bdm)
    yn = dlt * lax.rsqrt(_segsum(dlt * dlt, bdm, pieces=1) + LNX_EPS)
    yb = (yn * lnx_ref[0:1] + lnx_ref[1:2] + post_ref[1]) * post_ref[0]
    m = _dot(ya_ref[...], w_ref[0:A_WIDTH, :]) + _dot(yb, w_ref[A_WIDTH:, :])
    mod = mod_ref[0]
    o_ref[...] = x_ref[...] + mod[2:3] * (_rms(m) * gains_ref[1:2])


def _even_out(x, mod, gains, ya, ys, post, lnx, bd_mean, w, seq_len, mod_row0, tm):
    t = x.shape[0]
    row = lambda i: (i, 0)
    return pl.pallas_call(
        _even_out_kernel,
        grid=(t // tm,),
        in_specs=[
            pl.BlockSpec((tm, D_MODEL), row),
            _mod_spec(tm, seq_len, mod_row0),
            pl.BlockSpec((4, D_MODEL), lambda i: (0, 0)),
            pl.BlockSpec((tm, A_WIDTH), row),
            pl.BlockSpec((tm, B_WIDTH), row),
            pl.BlockSpec((2, tm, B_WIDTH), lambda i: (0, i, 0)),
            pl.BlockSpec((2, B_WIDTH), lambda i: (0, 0)),
            pl.BlockSpec((B_WIDTH, B_WIDTH), lambda i: (0, 0)),
            pl.BlockSpec((D_MODEL, D_MODEL), lambda i: (0, 0)),
        ],
        out_specs=pl.BlockSpec((tm, D_MODEL), row),
        out_shape=jax.ShapeDtypeStruct((t, D_MODEL), F32),
        compiler_params=_cparams("parallel"),
        name="even_out",
    )(x, mod, gains, ya, ys, post, lnx, bd_mean, w)


def _odd_kernel(x_ref, mod_ref, gains_ref, csc_ref, cl_ref, sl_ref, w_ref, o_ref,
                tc_scr, ts_scr, m_scr, *, seq_len):
    mod = mod_ref[0]
    gd = C_GROUP_DIM
    n_seq = x_ref.shape[0] // seq_len
    csc = csc_ref[...]

    for s in range(n_seq):
        rows = pl.ds(s * seq_len, seq_len)
        h = _modnorm(x_ref[rows, :], gains_ref[0:1], mod[1:2], mod[0:1]).astype(BF16)
        for g in range(C_GROUPS):
            t = jnp.dot(h[:, g * gd:(g + 1) * gd], csc, preferred_element_type=F32)
            tc_scr[rows, g * gd:(g + 1) * gd] = t[:, 0:gd].astype(BF16)
            ts_scr[rows, g * gd:(g + 1) * gd] = t[:, gd:].astype(BF16)

    def position_dft(s, cb):
        rows = pl.ds(s * seq_len, seq_len)
        cols = pl.ds(cb * ODD_COLS, ODD_COLS)
        f = (jnp.dot(cl_ref[...], tc_scr[rows, cols], preferred_element_type=F32)
             - jnp.dot(sl_ref[...], ts_scr[rows, cols], preferred_element_type=F32))
        yield
        part = jnp.dot(f.astype(BF16), w_ref[cols, :], preferred_element_type=F32)
        yield
        if cb == 0:
            m_scr[rows, :] = part
        else:
            m_scr[rows, :] += part

    _round_robin(position_dft(s, cb) for s in range(n_seq) for cb in range(D_MODEL // ODD_COLS))
    o_ref[...] = x_ref[...] + mod[2:3] * (_rms(m_scr[...]) * gains_ref[1:2])


def _dft_tables(n):
    idx = np.arange(n, dtype=np.int64)
    ang = 2.0 * np.pi * ((idx[:, None] * idx[None, :]) % n).astype(np.float64) / n
    s = 1.0 / math.sqrt(n)
    return np.cos(ang) * s, np.sin(ang) * s


def _odd_mixer(x, mod, gains, w, n_seq, seq_len, mod_row0):
    cc, sc = _dft_tables(C_GROUP_DIM)
    cl, sl = _dft_tables(seq_len)
    csc, cl, sl = (jnp.asarray(a, F32).astype(BF16) for a in (np.concatenate([cc, sc], axis=1), cl, sl))
    full = lambda shape: pl.BlockSpec(shape, lambda i: (0,) * len(shape))
    row = lambda i: (i, 0)
    tm = max(seq_len, ODD_ROWS)
    assert tm % seq_len == 0 and (n_seq * seq_len) % tm == 0
    return pl.pallas_call(
        functools.partial(_odd_kernel, seq_len=seq_len),
        grid=(n_seq * seq_len // tm,),
        in_specs=[
            pl.BlockSpec((tm, D_MODEL), row),
            _mod_spec(tm, seq_len, mod_row0),
            full((4, D_MODEL)), full((C_GROUP_DIM, 2 * C_GROUP_DIM)),
            full((seq_len, seq_len)), full((seq_len, seq_len)), full((D_MODEL, D_MODEL)),
        ],
        out_specs=pl.BlockSpec((tm, D_MODEL), row),
        out_shape=jax.ShapeDtypeStruct(x.shape, F32),
        scratch_shapes=[pltpu.VMEM((tm, D_MODEL), BF16), pltpu.VMEM((tm, D_MODEL), BF16),
                        pltpu.VMEM((tm, D_MODEL), F32)],
        compiler_params=_cparams("parallel"),
        name="odd_mixer",
    )(x, mod, gains, csc, cl, sl, w)


def _ffn_kernel(x_ref, mod_ref, gains_ref, win_ref, cw_ref, cb_ref, wo_ref, o_ref, h_scr, acc_scr, *,
                seq_len):
    mod = mod_ref[0]
    nj = D_FF // FF_CHUNK
    h_scr[...] = _modnorm(x_ref[...], gains_ref[2:3], mod[4:5], mod[3:4]).astype(BF16)
    acc_scr[...] = jnp.zeros_like(acc_scr)
    n_sub = h_scr.shape[0] // FF_ROWS
    zero_row = jnp.zeros((1, FF_CHUNK), F32)

    def chunk(j, carry):
        gate = [None] * n_sub
        cw = cw_ref[j]
        cb = cb_ref[j]
        wo_rows = pl.ds(pl.multiple_of(j * FF_CHUNK, FF_CHUNK), FF_CHUNK)

        def sub_tile(i):
            rows = pl.ds(i * FF_ROWS, FF_ROWS)
            h = h_scr[rows, :]
            u = jnp.dot(h, win_ref[j], preferred_element_type=F32)
            g = gate[i] = jnp.dot(h, win_ref[nj + j], preferred_element_type=F32)
            yield
            before = zero_row if (i * FF_ROWS) % seq_len == 0 else gate[i - 1][FF_ROWS - 1:FF_ROWS, :]
            after = zero_row if ((i + 1) * FF_ROWS) % seq_len == 0 else gate[i + 1][0:1, :]
            pos = lax.broadcasted_iota(jnp.int32, (FF_ROWS, 1), 0)
            prev = jnp.where(pos == 0, before, pltpu.roll(g, 1, 0))
            nxt = jnp.where(pos == FF_ROWS - 1, after, pltpu.roll(g, FF_ROWS - 1, 0))
            gc = prev * cw[0:1] + g * cw[1:2] + nxt * cw[2:3] + cb
            act = (_silu(gc) * u).astype(BF16)
            yield
            acc_scr[rows, :] += jnp.dot(act, wo_ref[wo_rows, :], preferred_element_type=F32)

        _round_robin(sub_tile(i) for i in range(n_sub))
        return carry

    lax.fori_loop(0, nj, chunk, 0)
    o_ref[...] = x_ref[...] + mod[5:6] * (_rms(acc_scr[...]) * gains_ref[3:4])


def _ffn_weights(w_in, conv_w, conv_b):
    depth = w_in.shape[0]
    nj = D_FF // FF_CHUNK
    w = w_in.astype(BF16).reshape(depth, D_MODEL, 2 * nj, FF_CHUNK).transpose(0, 2, 1, 3)
    cw = conv_w.reshape(depth, 3, nj, FF_CHUNK).transpose(0, 2, 1, 3)
    cb = conv_b.reshape(depth, nj, 1, FF_CHUNK)
    return w, cw, cb


def _conv_ffn(x, mod, gains, layer, w_in, conv_w, conv_b, w_out, seq_len, mod_row0, tm):
    t = x.shape[0]
    nj = D_FF // FF_CHUNK
    assert seq_len % FF_ROWS == 0 and tm % seq_len == 0
    once = pl.Buffered(1)
    return pl.pallas_call(
        functools.partial(_ffn_kernel, seq_len=seq_len),
        grid=(t // tm,),
        in_specs=[
            pl.BlockSpec((tm, D_MODEL), lambda i: (i, 0)),
            _mod_spec(tm, seq_len, mod_row0),
            pl.BlockSpec((4, D_MODEL), lambda i: (0, 0)),
            pl.BlockSpec((None, 2 * nj, D_MODEL, FF_CHUNK), lambda i: (layer, 0, 0, 0), pipeline_mode=once),
            pl.BlockSpec((None, nj, 3, FF_CHUNK), lambda i: (layer, 0, 0, 0)),
            pl.BlockSpec((None, nj, 1, FF_CHUNK), lambda i: (layer, 0, 0, 0)),
            pl.BlockSpec((None, D_FF, D_MODEL), lambda i: (layer, 0, 0), pipeline_mode=once),
        ],
        out_specs=pl.BlockSpec((tm, D_MODEL), lambda i: (i, 0)),
        out_shape=jax.ShapeDtypeStruct((t, D_MODEL), F32),
        scratch_shapes=[pltpu.VMEM((tm, D_MODEL), BF16), pltpu.VMEM((tm, D_MODEL), F32)],
        compiler_params=pltpu.CompilerParams(dimension_semantics=("parallel",),
                                             vmem_limit_bytes=VMEM_LIMIT_LARGE),
        name="conv_ffn",
    )(x, mod, gains, w_in, conv_w, conv_b, w_out)


def _rope_tables(seq_len):
    rows = seq_len // GRID_W
    row = jnp.repeat(jnp.arange(rows, dtype=F32), GRID_W)
    col = jnp.tile(jnp.arange(GRID_W, dtype=F32), rows)
    n_freq = A_HALF // 4
    inv = ROPE_BASE ** (-jnp.arange(n_freq, dtype=F32) / n_freq)
    ang = jnp.concatenate([row[:, None] * inv, col[:, None] * inv], axis=-1)
    cos_t = jnp.tile(jnp.repeat(jnp.cos(ang), 2, axis=1), (1, 2))
    sin_t = jnp.tile(jnp.repeat(jnp.sin(ang), 2, axis=1), (1, 2))
    sign = jnp.where(jnp.arange(A_HEAD_DIM) % 2 == 0, -1.0, 1.0).astype(F32)
    return cos_t, sin_t * sign


def _pad_lora(w):
    z = jnp.zeros_like(w[0])
    return jnp.stack([jnp.concatenate([w[0], z], axis=0), jnp.concatenate([z, w[1]], axis=0)])


def _block_diag(value):
    head = np.arange(B_WIDTH) // B_HEAD
    return jnp.asarray(np.where(head[:, None] == head[None, :], value, 0.0), BF16)


def _tile_rows(seq_len):
    return max(seq_len, 1024)


def kernel(x_prompt, x_sample, cache_k, cache_v, state_wkv, c, c_ctx, w_ada, b_ada, norm_gains,
           w_in_even, w_out_even, diff_lambda, diff_subln, rwkv_shift_mu, rwkv_w0, rwkv_w2, rwkv_a0,
           rwkv_a2, rwkv_g2, rwkv_kvec, rwkv_lnx, w_out_odd, w_ffn_in, ffn_conv, ffn_conv_b,
           w_ffn_out):
    n_ctx, l_ctx, _ = x_prompt.shape
    n_lat, l_lat, _ = x_sample.shape
    assert 1 + n_lat <= MOD_ROWS
    cvec = jnp.concatenate(
        [c_ctx[None, :], c, jnp.zeros((MOD_ROWS - 1 - n_lat, D_MODEL), F32)], axis=0)
    mods = _modulation(cvec, w_ada, b_ada).reshape(DEPTH, MOD_ROWS, 6, D_MODEL)

    per_layer_bf16 = lambda w: [w[i].astype(BF16) for i in range(w.shape[0])]
    w_in_even_b = per_layer_bf16(w_in_even)
    w_out_even_b = per_layer_bf16(w_out_even)
    w_out_odd_b = per_layer_bf16(w_out_odd)
    w_ffn_in_b, ffn_conv_c, ffn_conv_b_c = _ffn_weights(w_ffn_in, ffn_conv, ffn_conv_b)
    w_ffn_out_b = w_ffn_out.astype(BF16)
    bd_ones = _block_diag(1.0)
    bd_mean = _block_diag(1.0 / B_HEAD)
    cos_t, sin_t = _rope_tables(l_lat)

    def run_group(x, n_seq, seq_len, mod_row0, latent):
        tm = _tile_rows(seq_len)
        ctx_out = None
        for l in range(DEPTH):
            mod = mods[l]
            gains = norm_gains[l]
            if l % 2 == 0:
                e = l // 2
                lam_init = 0.8 - 0.6 * math.exp(-0.3 * l)
                q, k, v, fb, *kv_cache = _even_in(
                    x, mod, gains, w_in_even_b[e], seq_len, mod_row0, 256,
                    rope=(cos_t, sin_t) if latent else None, cache=not latent)
                if latent:
                    past = cache_k.shape[2]
                    ctx = (cache_k[:, e].reshape(n_seq * past, A_WIDTH),
                           cache_v[:, e].reshape(n_seq * past, A_WIDTH))
                    s0 = state_wkv[:, e]
                else:
                    ctx = None
                    s0 = None
                ya = _attention(q, k, v, diff_lambda[e], diff_subln[e][None, :], lam_init,
                                n_seq, seq_len, ctx)
                prep = {
                    "mu": rwkv_shift_mu[e], "w0": rwkv_w0[e], "w2": _pad_lora(rwkv_w2[e]).astype(BF16),
                    "a0": rwkv_a0[e], "a2": _pad_lora(rwkv_a2[e]).astype(BF16),
                    "g2": rwkv_g2[e].astype(BF16), "kvec": rwkv_kvec[e], "bd_ones": bd_ones,
                }
                pre, post = _rwkv_prep(fb, prep, seq_len)
                ys, s_fin = _rwkv_scan(pre, s0, n_seq, seq_len)
                x = _even_out(x, mod, gains, ya, ys, post, rwkv_lnx[e], bd_mean, w_out_even_b[e],
                              seq_len, mod_row0, 256)
                if not latent:
                    ctx_out = (*kv_cache, s_fin)
            else:
                x = _odd_mixer(x, mod, gains, w_out_odd_b[l // 2], n_seq, seq_len, mod_row0)
            x = _conv_ffn(x, mod, gains, l, w_ffn_in_b, ffn_conv_c, ffn_conv_b_c, w_ffn_out_b,
                          seq_len, mod_row0, tm)
        return x, ctx_out

    y_ctx, (k_new, v_new, s_new) = run_group(
        x_prompt.reshape(n_ctx * l_ctx, D_MODEL), n_ctx, l_ctx, 0, False)
    y_lat, _ = run_group(x_sample.reshape(n_lat * l_lat, D_MODEL), n_lat, l_lat, 1, True)

    n_even = (DEPTH + 1) // 2
    assert n_even == 1
    return (
        y_ctx.reshape(n_ctx, l_ctx, D_MODEL),
        y_lat.reshape(n_lat, l_lat, D_MODEL),
        k_new,
        v_new,
        s_new.reshape(n_ctx, n_even, 2, B_HEADS, B_HEAD, B_HEAD),
    )
```

```python
import functools
import math

import numpy as np
import jax
import jax.numpy as jnp
from jax import lax
from jax.experimental import pallas as pl
from jax.experimental.pallas import tpu as pltpu

F32 = jnp.float32
BF16 = jnp.bfloat16

D_MODEL = 1024
DEPTH = 2
GRID_W = 64
A_WIDTH = D_MODEL // 2
A_HEADS = 4
A_HEAD_DIM = A_WIDTH // A_HEADS
A_HALF = A_HEAD_DIM // 2
B_WIDTH = D_MODEL - A_WIDTH
B_HEAD = 64
B_HEADS = B_WIDTH // B_HEAD
LORA_W = 64
LORA_A = 64
LORA_G = 128
IN_B = 3 * B_WIDTH + 2 * LORA_W + 2 * LORA_A + LORA_G
IN_EVEN = 3 * A_WIDTH + IN_B
C_GROUPS = 8
C_GROUP_DIM = D_MODEL // C_GROUPS
D_FF = 2816
ROPE_BASE = 10000.0
RMS_EPS = 1e-6
LNX_EPS = 64e-5

LANES = 128
SUBLANES = 8
PAIR = 2 * B_HEAD
N_PAIRS = B_HEADS // 2
CHUNK = 64
FF_CHUNK = 256
FF_ROWS = 256
ATTN_Q_ROWS = 256
ODD_ROWS = 512
ODD_COLS = 256
VMEM_LIMIT = 48 * 1024 * 1024
MOD_ROWS = 8


def _cparams(*sem):
    return pltpu.CompilerParams(dimension_semantics=sem, vmem_limit_bytes=VMEM_LIMIT)


def _sigmoid(x):
    return 1.0 / (1.0 + jnp.exp(-x))


def _silu(x):
    return x * (0.5 + 0.5 * jnp.tanh(0.5 * x))


def _dot(a, b):
    return jnp.dot(a.astype(BF16), b.astype(BF16), preferred_element_type=F32)


_NN = (((1,), (0,)), ((), ()))
_NT = (((1,), (1,)), ((), ()))
_TN = (((0,), (0,)), ((), ()))


def _split_bf16(x, n):
    parts = []
    rem = x
    for i in range(n):
        p = rem.astype(BF16)
        parts.append(p)
        if i + 1 < n:
            rem = rem - p.astype(F32)
    return parts


def _mm(a, b, dims, passes):
    if passes == 1:
        return lax.dot_general(a.astype(BF16), b.astype(BF16), dims, preferred_element_type=F32)
    a_hi, a_lo = _split_bf16(a, 2)
    b_hi, b_lo = _split_bf16(b, 2)
    dg = functools.partial(lax.dot_general, dimension_numbers=dims, preferred_element_type=F32)
    return dg(a_hi, b_hi) + (dg(a_hi, b_lo) + dg(a_lo, b_hi))


def _segsum(x, bd, pieces=2):
    acc = None
    for p in _split_bf16(x, pieces):
        t = jnp.dot(p, bd, preferred_element_type=F32)
        acc = t if acc is None else acc + t
    return acc


def _rms(x):
    return x * lax.rsqrt(jnp.mean(x * x, axis=-1, keepdims=True) + RMS_EPS)


def _modnorm(x, gain, scale, shift):
    return _rms(x) * gain * (1.0 + scale) + shift


def _mod_kernel(c_ref, w_ref, b_ref, o_ref):
    c = c_ref[...]
    s = c * _sigmoid(c)
    o_ref[0] = _dot(s, w_ref[0]) + b_ref[0]


def _modulation(cvec, w_ada, b_ada):
    tn = 1536
    n = 6 * D_MODEL
    return pl.pallas_call(
        _mod_kernel,
        grid=(DEPTH, n // tn),
        in_specs=[
            pl.BlockSpec((MOD_ROWS, D_MODEL), lambda l, j: (0, 0)),
            pl.BlockSpec((1, D_MODEL, tn), lambda l, j: (l, 0, j)),
            pl.BlockSpec((1, 1, tn), lambda l, j: (l, 0, j)),
        ],
        out_specs=pl.BlockSpec((1, MOD_ROWS, tn), lambda l, j: (l, 0, j)),
        out_shape=jax.ShapeDtypeStruct((DEPTH, MOD_ROWS, n), F32),
        compiler_params=_cparams("parallel", "parallel"),
        name="modulation",
    )(cvec, w_ada, b_ada.reshape(DEPTH, 1, n))


def _mod_spec(tm, seq_len, mod_row0):
    if mod_row0 == 0:
        return pl.BlockSpec((1, 6, D_MODEL), lambda i, *_: (0, 0, 0))
    return pl.BlockSpec((1, 6, D_MODEL), lambda i, *_: (mod_row0 + (i * tm) // seq_len, 0, 0))


def _rope(x, cos_t, sin_t):
    lane = lax.broadcasted_iota(jnp.int32, x.shape, 1)
    nxt = pltpu.roll(x, LANES - 1, 1)
    prv = pltpu.roll(x, 1, 1)
    partner = jnp.where(lane % 2 == 0, nxt, prv)
    return x * cos_t + partner * sin_t


def _even_in_kernel(*refs, rope, cache):
    x_ref, mod_ref, gains_ref, w_ref = refs[:4]
    refs = refs[4:]
    if rope:
        cos_ref, sin_ref = refs[:2]
        refs = refs[2:]
    q_ref, k_ref, v_ref, fb_ref = refs[:4]
    cache_refs = refs[4:]
    mod = mod_ref[0]
    h = _modnorm(x_ref[...], gains_ref[0:1], mod[1:2], mod[0:1]).astype(BF16)
    a = A_WIDTH
    hd = A_HEAD_DIM
    for n, out_ref in enumerate((q_ref, k_ref, v_ref)):
        y = jnp.dot(h, w_ref[:, n * a:(n + 1) * a], preferred_element_type=F32)
        if cache and n > 0:
            for hh in range(A_HEADS):
                cache_refs[n - 1][0, 0, :, hh, :] = y[:, hh * hd:(hh + 1) * hd]
        if rope and n < 2:
            for hh in range(A_HEADS):
                cols = slice(hh * hd, (hh + 1) * hd)
                out_ref[:, cols] = _rope(y[:, cols], cos_ref[...], sin_ref[...]).astype(BF16)
        else:
            out_ref[...] = y.astype(BF16)
    fb_ref[...] = jnp.dot(h, w_ref[:, 3 * a:], preferred_element_type=F32)


def _even_in(x, mod, gains, w, seq_len, mod_row0, tm, rope=None, cache=False):
    t = x.shape[0]
    assert seq_len % tm == 0
    per_seq = seq_len // tm
    row = lambda i: (i, 0)
    operands = [x, mod, gains, w]
    in_specs = [
        pl.BlockSpec((tm, D_MODEL), row),
        _mod_spec(tm, seq_len, mod_row0),
        pl.BlockSpec((4, D_MODEL), lambda i: (0, 0)),
        pl.BlockSpec((D_MODEL, IN_EVEN), lambda i: (0, 0)),
    ]
    if rope is not None:
        tab = pl.BlockSpec((tm, A_HEAD_DIM), lambda i: (i % per_seq, 0))
        operands += list(rope)
        in_specs += [tab, tab]
    out_specs = [pl.BlockSpec((tm, A_WIDTH), row)] * 3 + [pl.BlockSpec((tm, IN_B), row)]
    out_shape = [jax.ShapeDtypeStruct((t, A_WIDTH), BF16)] * 3 + [jax.ShapeDtypeStruct((t, IN_B), F32)]
    if cache:
        out_specs += [pl.BlockSpec((1, 1, tm, A_HEADS, A_HEAD_DIM),
                                   lambda i: (i // per_seq, 0, i % per_seq, 0, 0))] * 2
        out_shape += [jax.ShapeDtypeStruct((t // seq_len, 1, seq_len, A_HEADS, A_HEAD_DIM), F32)] * 2
    return pl.pallas_call(
        functools.partial(_even_in_kernel, rope=rope is not None, cache=cache),
        grid=(t // tm,),
        in_specs=in_specs,
        out_specs=out_specs,
        out_shape=out_shape,
        compiler_params=_cparams("parallel"),
        name="even_in",
    )(*operands)


def _attn_kernel(*refs, has_ctx, lam_init):
    if has_ctx:
        lam_ref, sub_ref, q_ref, k_ref, v_ref, ck_ref, cv_ref, o_ref = refs
    else:
        lam_ref, sub_ref, q_ref, k_ref, v_ref, o_ref = refs
    lp = lam_ref[...]
    l1 = jnp.sum(lp[0:1] * lp[1:2], axis=-1, keepdims=True)
    l2 = jnp.sum(lp[2:3] * lp[3:4], axis=-1, keepdims=True)
    lam = jnp.exp(l1) - jnp.exp(l2) + lam_init
    hd = A_HEAD_DIM
    lane = lax.broadcasted_iota(jnp.int32, (q_ref.shape[0], hd), 1)
    scale = A_HALF ** -0.5

    def head(h):
        cols = pl.ds(h * hd, hd)
        q = q_ref[:, cols]
        k = k_ref[:, cols].astype(BF16)
        if has_ctx:
            ck = ck_ref[:, cols].astype(BF16)
        scores = []
        for m in range(2):
            qm = jnp.where((lane < A_HALF) == (m == 0), q, 0.0).astype(BF16)
            s = lax.dot_general(qm, k, _NT, preferred_element_type=F32) * scale
            sc = lax.dot_general(qm, ck, _NT, preferred_element_type=F32) * scale if has_ctx else None
            scores.append((s, sc))
        yield
        probs = []
        for s, sc in scores:
            mx = jnp.max(s, axis=-1, keepdims=True)
            if has_ctx:
                mx = jnp.maximum(mx, jnp.max(sc, axis=-1, keepdims=True))
                ec = jnp.exp(sc - mx)
            e = jnp.exp(s - mx)
            z = jnp.sum(e, axis=-1, keepdims=True)
            if has_ctx:
                z = z + jnp.sum(ec, axis=-1, keepdims=True)
            inv = 1.0 / z
            probs.append((e * inv, ec * inv if has_ctx else None))
        w = (probs[0][0] - lam * probs[1][0]).astype(BF16)
        if has_ctx:
            wc = (probs[0][1] - lam * probs[1][1]).astype(BF16)
        yield
        o = jnp.dot(w, v_ref[:, cols].astype(BF16), preferred_element_type=F32)
        if has_ctx:
            o = o + jnp.dot(wc, cv_ref[:, cols].astype(BF16), preferred_element_type=F32)
        yield
        o_ref[:, cols] = _rms(o) * sub_ref[...] * (1.0 - lam_init)

    _round_robin(head(h) for h in range(A_HEADS))


def _attention(q, k, v, lam_p, subln, lam_init, n_seq, seq_len, ctx=None):
    t = q.shape[0]
    tq = min(seq_len, ATTN_Q_ROWS if ctx is None else ATTN_Q_ROWS // 2)
    nq = seq_len // tq
    small = [pl.BlockSpec((4, A_HALF), lambda b, i: (0, 0)),
             pl.BlockSpec((1, A_HEAD_DIM), lambda b, i: (0, 0))]
    qblk = pl.BlockSpec((tq, A_WIDTH), lambda b, i: (b * nq + i, 0))
    kblk = pl.BlockSpec((seq_len, A_WIDTH), lambda b, i: (b, 0))
    operands = [lam_p, subln, q, k, v]
    in_specs = small + [qblk, kblk, kblk]
    if ctx is not None:
        ck, cv = ctx
        cblk = pl.BlockSpec((ck.shape[0] // n_seq, A_WIDTH), lambda b, i: (b, 0))
        operands += [ck, cv]
        in_specs += [cblk, cblk]
    return pl.pallas_call(
        functools.partial(_attn_kernel, has_ctx=ctx is not None, lam_init=lam_init),
        grid=(n_seq, nq),
        in_specs=in_specs,
        out_specs=qblk,
        out_shape=jax.ShapeDtypeStruct((t, A_WIDTH), F32),
        compiler_params=_cparams("parallel", "parallel"),
        name="diff_attn",
    )(*operands)


def _shift_rows(f, halo_prev, halo_next, start, seq_len):
    tm = f.shape[0]
    local = lax.broadcasted_iota(jnp.int32, (tm, 1), 0)
    halo_prev = jnp.where(start == 0, 0.0, halo_prev)
    halo_next = jnp.where(start + tm == seq_len, 0.0, halo_next)
    prev = jnp.where(local == 0, halo_prev, pltpu.roll(f, 1, 0))
    nxt = jnp.where(local == tm - 1, halo_next, pltpu.roll(f, tm - 1, 0))
    return prev, nxt


def _prep_kernel(fb_ref, fp_ref, fn_ref, mu_ref, w0_ref, w2_ref, a0_ref, a2_ref, g2_ref, kvec_ref,
                 bd_ref, scan_ref, post_ref, *, seq_len):
    f = fb_ref[...]
    tm = f.shape[0]
    start = (pl.program_id(0) * tm) % seq_len
    prev, nxt = _shift_rows(f, fp_ref[SUBLANES - 1:SUBLANES, :], fn_ref[0:1, :], start, seq_len)
    f = f * (1.0 - mu_ref[0:1] - mu_ref[1:2]) + mu_ref[0:1] * prev + mu_ref[1:2] * nxt
    bw = B_WIDTH
    r = f[:, 0:bw]
    k = f[:, bw:2 * bw]
    v = f[:, 2 * bw:3 * bw]
    off = 3 * bw
    wd = jnp.tanh(f[:, off:off + 2 * LORA_W])
    ad = f[:, off + 2 * LORA_W:off + 2 * LORA_W + 2 * LORA_A]
    gd = f[:, off + 2 * LORA_W + 2 * LORA_A:]
    bd = bd_ref[...]
    kv = kvec_ref[...]
    g = _dot(_sigmoid(gd), g2_ref[...])
    kk = k * kv[0:1]
    kk = kk * lax.rsqrt(_segsum(kk * kk, bd, pieces=1) + 1e-12)
    scan_ref[0] = r
    scan_ref[1] = v
    scan_ref[2] = kk
    ksum = None
    for d in range(2):
        w_raw = w0_ref[d:d + 1] + _dot(wd, w2_ref[d])
        scan_ref[7 + d] = -math.exp(-0.5) * _sigmoid(w_raw)
        a = _sigmoid(a0_ref[d:d + 1] + _dot(ad, a2_ref[d]))
        kd = k * (1.0 + (a - 1.0) * kv[1:2])
        scan_ref[3 + d] = kd
        scan_ref[5 + d] = kk * a
        ksum = kd if ksum is None else ksum + kd
    post_ref[0] = g
    post_ref[1] = _segsum(r * ksum * kv[2:3], bd) * v


def _rwkv_prep(fb, p, seq_len, tm=256):
    t = fb.shape[0]
    assert seq_len % tm == 0
    nh = tm // SUBLANES
    last = t // SUBLANES - 1
    full = lambda shape: pl.BlockSpec(shape, lambda i: (0,) * len(shape))
    return pl.pallas_call(
        functools.partial(_prep_kernel, seq_len=seq_len),
        grid=(t // tm,),
        in_specs=[
            pl.BlockSpec((tm, IN_B), lambda i: (i, 0)),
            pl.BlockSpec((SUBLANES, IN_B), lambda i: (jnp.maximum(i * nh - 1, 0), 0)),
            pl.BlockSpec((SUBLANES, IN_B), lambda i: (jnp.minimum((i + 1) * nh, last), 0)),
            full((2, IN_B)), full((2, B_WIDTH)), full((2, 2 * LORA_W, B_WIDTH)),
            full((2, B_WIDTH)), full((2, 2 * LORA_A, B_WIDTH)), full((LORA_G, B_WIDTH)),
            full((3, B_WIDTH)), full((B_WIDTH, B_WIDTH)),
        ],
        out_specs=[
            pl.BlockSpec((9, tm, B_WIDTH), lambda i: (0, i, 0)),
            pl.BlockSpec((2, tm, B_WIDTH), lambda i: (0, i, 0)),
        ],
        out_shape=[
            jax.ShapeDtypeStruct((9, t, B_WIDTH), F32),
            jax.ShapeDtypeStruct((2, t, B_WIDTH), F32),
        ],
        compiler_params=_cparams("parallel"),
        name="rwkv_prep",
    )(fb, fb, fb, p["mu"], p["w0"], p["w2"], p["a0"], p["a2"], p["g2"], p["kvec"], p["bd_ones"])


def _cumsum_rows(x, reverse):
    n = x.shape[0]
    ridx = lax.broadcasted_iota(jnp.int32, x.shape, 0)
    s = 1
    while s < n:
        if reverse:
            x = x + jnp.where(ridx < n - s, pltpu.roll(x, n - s, 0), 0.0)
        else:
            x = x + jnp.where(ridx >= s, pltpu.roll(x, s, 0), 0.0)
        s *= 2
    return x


def _round_robin(gens):
    gens = list(gens)
    while gens:
        alive = []
        for gen in gens:
            try:
                next(gen)
                alive.append(gen)
            except StopIteration:
                pass
        gens = alive


def _scan_kernel(*refs, seq_len, has_state, pairs, passes, inv_passes, carry_passes, unroll):
    if has_state:
        pre_ref, s0_ref, ys_ref, sf_ref, r1_scr, m1_scr, d0_scr, gam_scr, y_scr, s_scr = refs
    else:
        pre_ref, ys_ref, sf_ref, r1_scr, m1_scr, d0_scr, gam_scr, y_scr, s_scr = refs
    nc = seq_len // CHUNK
    c2 = 2 * CHUNK
    lane = lax.broadcasted_iota(jnp.int32, (CHUNK, PAIR), 1)
    head0 = lane < B_HEAD
    row = lax.broadcasted_iota(jnp.int32, (c2, c2), 0)
    col = lax.broadcasted_iota(jnp.int32, (c2, c2), 1)
    same = (row // CHUNK) == (col // CHUNK)
    eye = jnp.where(row == col, 1.0, 0.0)
    xor_rc = jnp.bitwise_xor(row, col)
    mm = functools.partial(_mm, passes=passes)
    mm_inv = functools.partial(_mm, passes=inv_passes)
    mm_carry = functools.partial(_mm, passes=carry_passes)

    def stack(x):
        return jnp.concatenate([jnp.where(head0, x, 0.0), jnp.where(head0, 0.0, x)], axis=0)

    def fold(x):
        return x[0:CHUNK] + x[CHUNK:c2]

    def chunk_rows(cc):
        return pl.ds(pl.multiple_of(cc * CHUNK, CHUNK), CHUNK)

    def chunk_local(cc, d, q):
        rows = chunk_rows(cc)
        lanes = pl.ds(q * PAIR, PAIR)
        z = 2 * q + d
        r = pre_ref[0, rows, lanes]
        v = pre_ref[1, rows, lanes]
        kk = pre_ref[2, rows, lanes]
        k = pre_ref[3 + d, rows, lanes]
        b = pre_ref[5 + d, rows, lanes]
        lw = pre_ref[7 + d, rows, lanes]
        incl = same & ((col <= row) if d == 0 else (col >= row))
        strict = same & ((col < row) if d == 0 else (col > row))
        g = _cumsum_rows(lw, reverse=(d == 1))
        gp = g - lw
        gtot = g[CHUNK - 1:CHUNK, :] if d == 0 else g[0:1, :]
        gm = 0.5 * gtot
        kx = k * jnp.exp(gm - g)
        bx = b * jnp.exp(gm - g)
        lhs = jnp.concatenate([stack(r * jnp.exp(g - gm)), stack(kk * jnp.exp(gp - gm))], axis=0)
        rhs = jnp.concatenate([kx, kx, bx, bx], axis=0)
        a_all = mm(lhs, rhs, _NT)
        yield
        a_rk = jnp.where(incl, a_all[0:c2, 0:c2], 0.0)
        a_rb = jnp.where(incl, a_all[0:c2, c2:2 * c2], 0.0)
        a_kk = jnp.where(strict, a_all[c2:2 * c2, 0:c2], 0.0)
        n_mat = jnp.where(strict, a_all[c2:2 * c2, c2:2 * c2], 0.0)
        vs = stack(v)
        akv = mm(a_kk, vs, _NN)
        y0 = mm(a_rk, vs, _NN)
        lower = (col < row) if d == 0 else (col > row)
        p_inv = eye - jnp.where(lower & (xor_rc == 1), n_mat, 0.0)
        yield
        s = 2
        while s < CHUNK:
            c_s = jnp.where(lower & (xor_rc >= s) & (xor_rc < 2 * s), n_mat, 0.0)
            t = mm_inv(c_s, p_inv, _NN)
            yield
            p_inv = p_inv - mm_inv(p_inv, t, _NN)
            yield
            s *= 2
        x = jnp.concatenate([stack(kk * jnp.exp(gp)), akv], axis=1)
        w12 = mm(p_inv, x, _NN)
        yield
        arb_w = mm(a_rb, w12, _NN)
        et = jnp.exp(gtot - g)
        bes = stack(b * et)
        d0_scr[z, cc] = mm(jnp.concatenate([vs, -w12[:, PAIR:]], axis=0),
                           jnp.concatenate([stack(k * et), bes], axis=0), _TN)
        m1_scr[z, cc] = mm(w12[:, 0:PAIR], bes, _TN)
        yield
        r1_scr[z, cc] = stack(r * jnp.exp(g)) - arb_w[:, 0:PAIR]
        y_scr[z, rows, :] = fold(y0 - arb_w[:, PAIR:])
        gam_scr[z, cc] = jnp.broadcast_to(jnp.exp(gtot), (SUBLANES, PAIR))

    def local_body(grp, carry):
        _round_robin(chunk_local(grp * unroll + u, d, q)
                     for u in range(unroll) for q in range(pairs) for d in range(2))
        return carry

    lax.fori_loop(0, nc // unroll, local_body, 0)

    for z in range(2 * pairs):
        s_scr[z] = jnp.zeros((PAIR, PAIR), F32)
        if has_state:
            s_scr[z, 0:B_HEAD, 0:B_HEAD] = s0_ref[0, z % 2, 2 * (z // 2)]
            s_scr[z, B_HEAD:PAIR, B_HEAD:PAIR] = s0_ref[0, z % 2, 2 * (z // 2) + 1]

    def carry_body(c, carry):
        for z in range(2 * pairs):
            cc = c if z % 2 == 0 else nc - 1 - c
            rows = chunk_rows(cc)
            s0 = s_scr[z]
            y_scr[z, rows, :] += fold(mm_carry(r1_scr[z, cc], s0, _NT))
            s_scr[z] = s0 * gam_scr[z, cc, 0:1, :] - mm_carry(s0, m1_scr[z, cc], _NN) + d0_scr[z, cc]
        return carry

    lax.fori_loop(0, nc, carry_body, 0)
    for q in range(pairs):
        ys_ref[:, q * PAIR:(q + 1) * PAIR] = y_scr[2 * q] + y_scr[2 * q + 1]
    for z in range(2 * pairs):
        sf_ref[0, z % 2, 2 * (z // 2)] = s_scr[z, 0:B_HEAD, 0:B_HEAD]
        sf_ref[0, z % 2, 2 * (z // 2) + 1] = s_scr[z, B_HEAD:PAIR, B_HEAD:PAIR]


def _rwkv_scan(pre, s0, n_seq, seq_len, pairs=2, passes=1, inv_passes=1, carry_passes=1, unroll=4):
    t = pre.shape[1]
    nc = seq_len // CHUNK
    sblk = pl.BlockSpec((1, 2, 2 * pairs, B_HEAD, B_HEAD), lambda b, p: (b, 0, p, 0, 0))
    mat = pltpu.VMEM((2 * pairs, nc, PAIR, PAIR), F32)
    return pl.pallas_call(
        functools.partial(_scan_kernel, seq_len=seq_len, has_state=s0 is not None, pairs=pairs,
                          passes=passes, inv_passes=inv_passes, carry_passes=carry_passes, unroll=unroll),
        grid=(n_seq, N_PAIRS // pairs),
        in_specs=([pl.BlockSpec((9, seq_len, pairs * PAIR), lambda b, p: (0, b, p))]
                  + [sblk] * (s0 is not None)),
        out_specs=[pl.BlockSpec((seq_len, pairs * PAIR), lambda b, p: (b, p)), sblk],
        out_shape=[
            jax.ShapeDtypeStruct((t, B_WIDTH), F32),
            jax.ShapeDtypeStruct((n_seq, 2, B_HEADS, B_HEAD, B_HEAD), F32),
        ],
        scratch_shapes=[mat, mat, mat, pltpu.VMEM((2 * pairs, nc, SUBLANES, PAIR), F32),
                        pltpu.VMEM((2 * pairs, seq_len, PAIR), F32),
                        pltpu.VMEM((2 * pairs, PAIR, PAIR), F32)],
        compiler_params=_cparams("parallel", "parallel"),
        name="rwkv_scan",
    )(*((pre,) if s0 is None else (pre, s0)))


def _even_out_kernel(x_ref, mod_ref, gains_ref, ya_ref, ys_ref, post_ref, lnx_ref, bdm_ref, w_ref,
                     o_ref):
    bdm = bdm_ref[...]
    ys = ys_ref[...]
    dlt = ys - _segsum(ys, bdm)
    yn = dlt * lax.rsqrt(_segsum(dlt * dlt, bdm, pieces=1) + LNX_EPS)
    yb = (yn * lnx_ref[0:1] + lnx_ref[1:2] + post_ref[1]) * post_ref[0]
    m = _dot(ya_ref[...], w_ref[0:A_WIDTH, :]) + _dot(yb, w_ref[A_WIDTH:, :])
    mod = mod_ref[0]
    o_ref[...] = x_ref[...] + mod[2:3] * (_rms(m) * gains_ref[1:2])


def _even_out(x, mod, gains, ya, ys, post, lnx, bd_mean, w, seq_len, mod_row0, tm):
    t = x.shape[0]
    row = lambda i: (i, 0)
    return pl.pallas_call(
        _even_out_kernel,
        grid=(t // tm,),
        in_specs=[
            pl.BlockSpec((tm, D_MODEL), row),
            _mod_spec(tm, seq_len, mod_row0),
            pl.BlockSpec((4, D_MODEL), lambda i: (0, 0)),
            pl.BlockSpec((tm, A_WIDTH), row),
            pl.BlockSpec((tm, B_WIDTH), row),
            pl.BlockSpec((2, tm, B_WIDTH), lambda i: (0, i, 0)),
            pl.BlockSpec((2, B_WIDTH), lambda i: (0, 0)),
            pl.BlockSpec((B_WIDTH, B_WIDTH), lambda i: (0, 0)),
            pl.BlockSpec((D_MODEL, D_MODEL), lambda i: (0, 0)),
        ],
        out_specs=pl.BlockSpec((tm, D_MODEL), row),
        out_shape=jax.ShapeDtypeStruct((t, D_MODEL), F32),
        compiler_params=_cparams("parallel"),
        name="even_out",
    )(x, mod, gains, ya, ys, post, lnx, bd_mean, w)


def _odd_kernel(x_ref, mod_ref, gains_ref, csc_ref, cl_ref, sl_ref, w_ref, o_ref,
                tc_scr, ts_scr, m_scr, *, seq_len):
    mod = mod_ref[0]
    gd = C_GROUP_DIM
    n_seq = x_ref.shape[0] // seq_len
    csc = csc_ref[...]

    for s in range(n_seq):
        rows = pl.ds(s * seq_len, seq_len)
        h = _modnorm(x_ref[rows, :], gains_ref[0:1], mod[1:2], mod[0:1]).astype(BF16)
        for g in range(C_GROUPS):
            t = jnp.dot(h[:, g * gd:(g + 1) * gd], csc, preferred_element_type=F32)
            tc_scr[rows, g * gd:(g + 1) * gd] = t[:, 0:gd].astype(BF16)
            ts_scr[rows, g * gd:(g + 1) * gd] = t[:, gd:].astype(BF16)

    def position_dft(s, cb):
        rows = pl.ds(s * seq_len, seq_len)
        cols = pl.ds(cb * ODD_COLS, ODD_COLS)
        f = (jnp.dot(cl_ref[...], tc_scr[rows, cols], preferred_element_type=F32)
             - jnp.dot(sl_ref[...], ts_scr[rows, cols], preferred_element_type=F32))
        yield
        part = jnp.dot(f.astype(BF16), w_ref[cols, :], preferred_element_type=F32)
        yield
        if cb == 0:
            m_scr[rows, :] = part
        else:
            m_scr[rows, :] += part

    _round_robin(position_dft(s, cb) for s in range(n_seq) for cb in range(D_MODEL // ODD_COLS))
    o_ref[...] = x_ref[...] + mod[2:3] * (_rms(m_scr[...]) * gains_ref[1:2])


def _dft_tables(n):
    idx = np.arange(n, dtype=np.int64)
    ang = 2.0 * np.pi * ((idx[:, None] * idx[None, :]) % n).astype(np.float64) / n
    s = 1.0 / math.sqrt(n)
    return np.cos(ang) * s, np.sin(ang) * s


def _odd_mixer(x, mod, gains, w, n_seq, seq_len, mod_row0):
    cc, sc = _dft_tables(C_GROUP_DIM)
    cl, sl = _dft_tables(seq_len)
    csc, cl, sl = (jnp.asarray(a, F32).astype(BF16) for a in (np.concatenate([cc, sc], axis=1), cl, sl))
    full = lambda shape: pl.BlockSpec(shape, lambda i: (0,) * len(shape))
    row = lambda i: (i, 0)
    tm = max(seq_len, ODD_ROWS)
    assert tm % seq_len == 0 and (n_seq * seq_len) % tm == 0
    return pl.pallas_call(
        functools.partial(_odd_kernel, seq_len=seq_len),
        grid=(n_seq * seq_len // tm,),
        in_specs=[
            pl.BlockSpec((tm, D_MODEL), row),
            _mod_spec(tm, seq_len, mod_row0),
            full((4, D_MODEL)), full((C_GROUP_DIM, 2 * C_GROUP_DIM)),
            full((seq_len, seq_len)), full((seq_len, seq_len)), full((D_MODEL, D_MODEL)),
        ],
        out_specs=pl.BlockSpec((tm, D_MODEL), row),
        out_shape=jax.ShapeDtypeStruct(x.shape, F32),
        scratch_shapes=[pltpu.VMEM((tm, D_MODEL), BF16), pltpu.VMEM((tm, D_MODEL), BF16),
                        pltpu.VMEM((tm, D_MODEL), F32)],
        compiler_params=_cparams("parallel"),
        name="odd_mixer",
    )(x, mod, gains, csc, cl, sl, w)


def _ffn_kernel(x_ref, mod_ref, gains_ref, wu_ref, wg_ref, cw_ref, cb_ref, wo_ref, o_ref,
                h_scr, acc_scr, *, seq_len):
    j = pl.program_id(1)
    mod = mod_ref[0]

    @pl.when(j == 0)
    def _():
        h_scr[...] = _modnorm(x_ref[...], gains_ref[2:3], mod[4:5], mod[3:4]).astype(BF16)
        acc_scr[...] = jnp.zeros_like(acc_scr)

    n_sub = h_scr.shape[0] // FF_ROWS
    gate = [None] * n_sub
    zero_row = jnp.zeros((1, FF_CHUNK), F32)
    wu = wu_ref[...].astype(BF16)
    wg = wg_ref[...].astype(BF16)
    wo = wo_ref[...].astype(BF16)

    def sub_tile(i):
        rows = pl.ds(i * FF_ROWS, FF_ROWS)
        h = h_scr[rows, :]
        u = jnp.dot(h, wu, preferred_element_type=F32)
        g = gate[i] = jnp.dot(h, wg, preferred_element_type=F32)
        yield
        before = zero_row if (i * FF_ROWS) % seq_len == 0 else gate[i - 1][FF_ROWS - 1:FF_ROWS, :]
        after = zero_row if ((i + 1) * FF_ROWS) % seq_len == 0 else gate[i + 1][0:1, :]
        pos = lax.broadcasted_iota(jnp.int32, (FF_ROWS, 1), 0)
        prev = jnp.where(pos == 0, before, pltpu.roll(g, 1, 0))
        nxt = jnp.where(pos == FF_ROWS - 1, after, pltpu.roll(g, FF_ROWS - 1, 0))
        gc = prev * cw_ref[0:1] + g * cw_ref[1:2] + nxt * cw_ref[2:3] + cb_ref[...]
        act = (_silu(gc) * u).astype(BF16)
        yield
        acc_scr[rows, :] += jnp.dot(act, wo, preferred_element_type=F32)

    _round_robin(sub_tile(i) for i in range(n_sub))

    @pl.when(j == pl.num_programs(1) - 1)
    def _():
        o_ref[...] = x_ref[...] + mod[5:6] * (_rms(acc_scr[...]) * gains_ref[3:4])


def _conv_ffn(x, mod, gains, layer, w_in, conv_w, conv_b, w_out, seq_len, mod_row0, tm):
    t = x.shape[0]
    nj = D_FF // FF_CHUNK
    assert seq_len % FF_ROWS == 0 and tm % seq_len == 0
    return pl.pallas_call(
        functools.partial(_ffn_kernel, seq_len=seq_len),
        grid=(t // tm, nj),
        in_specs=[
            pl.BlockSpec((tm, D_MODEL), lambda i, j: (i, 0)),
            _mod_spec(tm, seq_len, mod_row0),
            pl.BlockSpec((4, D_MODEL), lambda i, j: (0, 0)),
            pl.BlockSpec((None, D_MODEL, FF_CHUNK), lambda i, j: (layer, 0, j)),
            pl.BlockSpec((None, D_MODEL, FF_CHUNK), lambda i, j: (layer, 0, nj + j)),
            pl.BlockSpec((3, FF_CHUNK), lambda i, j: (0, j)),
            pl.BlockSpec((1, FF_CHUNK), lambda i, j: (0, j)),
            pl.BlockSpec((None, FF_CHUNK, D_MODEL), lambda i, j: (layer, j, 0)),
        ],
        out_specs=pl.BlockSpec((tm, D_MODEL), lambda i, j: (i, 0)),
        out_shape=jax.ShapeDtypeStruct((t, D_MODEL), F32),
        scratch_shapes=[pltpu.VMEM((tm, D_MODEL), BF16), pltpu.VMEM((tm, D_MODEL), F32)],
        compiler_params=_cparams("parallel", "arbitrary"),
        name="conv_ffn",
    )(x, mod, gains, w_in, w_in, conv_w, conv_b, w_out)


def _rope_tables(seq_len):
    rows = seq_len // GRID_W
    row = jnp.repeat(jnp.arange(rows, dtype=F32), GRID_W)
    col = jnp.tile(jnp.arange(GRID_W, dtype=F32), rows)
    n_freq = A_HALF // 4
    inv = ROPE_BASE ** (-jnp.arange(n_freq, dtype=F32) / n_freq)
    ang = jnp.concatenate([row[:, None] * inv, col[:, None] * inv], axis=-1)
    cos_t = jnp.tile(jnp.repeat(jnp.cos(ang), 2, axis=1), (1, 2))
    sin_t = jnp.tile(jnp.repeat(jnp.sin(ang), 2, axis=1), (1, 2))
    sign = jnp.where(jnp.arange(A_HEAD_DIM) % 2 == 0, -1.0, 1.0).astype(F32)
    return cos_t, sin_t * sign


def _pad_lora(w):
    z = jnp.zeros_like(w[0])
    return jnp.stack([jnp.concatenate([w[0], z], axis=0), jnp.concatenate([z, w[1]], axis=0)])


def _block_diag(value):
    head = np.arange(B_WIDTH) // B_HEAD
    return jnp.asarray(np.where(head[:, None] == head[None, :], value, 0.0), BF16)


def _tile_rows(seq_len):
    return max(seq_len, 1024)


def kernel(x_prompt, x_sample, cache_k, cache_v, state_wkv, c, c_ctx, w_ada, b_ada, norm_gains,
           w_in_even, w_out_even, diff_lambda, diff_subln, rwkv_shift_mu, rwkv_w0, rwkv_w2, rwkv_a0,
           rwkv_a2, rwkv_g2, rwkv_kvec, rwkv_lnx, w_out_odd, w_ffn_in, ffn_conv, ffn_conv_b,
           w_ffn_out):
    n_ctx, l_ctx, _ = x_prompt.shape
    n_lat, l_lat, _ = x_sample.shape
    assert 1 + n_lat <= MOD_ROWS
    cvec = jnp.concatenate(
        [c_ctx[None, :], c, jnp.zeros((MOD_ROWS - 1 - n_lat, D_MODEL), F32)], axis=0)
    mods = _modulation(cvec, w_ada, b_ada).reshape(DEPTH, MOD_ROWS, 6, D_MODEL)

    per_layer_bf16 = lambda w: [w[i].astype(BF16) for i in range(w.shape[0])]
    w_in_even_b = per_layer_bf16(w_in_even)
    w_out_even_b = per_layer_bf16(w_out_even)
    w_out_odd_b = per_layer_bf16(w_out_odd)
    bd_ones = _block_diag(1.0)
    bd_mean = _block_diag(1.0 / B_HEAD)
    cos_t, sin_t = _rope_tables(l_lat)

    def run_group(x, n_seq, seq_len, mod_row0, latent):
        tm = _tile_rows(seq_len)
        ctx_out = None
        for l in range(DEPTH):
            mod = mods[l]
            gains = norm_gains[l]
            if l % 2 == 0:
                e = l // 2
                lam_init = 0.8 - 0.6 * math.exp(-0.3 * l)
                q, k, v, fb, *kv_cache = _even_in(
                    x, mod, gains, w_in_even_b[e], seq_len, mod_row0, 256,
                    rope=(cos_t, sin_t) if latent else None, cache=not latent)
                if latent:
                    past = cache_k.shape[2]
                    ctx = (cache_k[:, e].reshape(n_seq * past, A_WIDTH),
                           cache_v[:, e].reshape(n_seq * past, A_WIDTH))
                    s0 = state_wkv[:, e]
                else:
                    ctx = None
                    s0 = None
                ya = _attention(q, k, v, diff_lambda[e], diff_subln[e][None, :], lam_init,
                                n_seq, seq_len, ctx)
                prep = {
                    "mu": rwkv_shift_mu[e], "w0": rwkv_w0[e], "w2": _pad_lora(rwkv_w2[e]).astype(BF16),
                    "a0": rwkv_a0[e], "a2": _pad_lora(rwkv_a2[e]).astype(BF16),
                    "g2": rwkv_g2[e].astype(BF16), "kvec": rwkv_kvec[e], "bd_ones": bd_ones,
                }
                pre, post = _rwkv_prep(fb, prep, seq_len)
                ys, s_fin = _rwkv_scan(pre, s0, n_seq, seq_len)
                x = _even_out(x, mod, gains, ya, ys, post, rwkv_lnx[e], bd_mean, w_out_even_b[e],
                              seq_len, mod_row0, 256)
                if not latent:
                    ctx_out = (*kv_cache, s_fin)
            else:
                x = _odd_mixer(x, mod, gains, w_out_odd_b[l // 2], n_seq, seq_len, mod_row0)
            x = _conv_ffn(x, mod, gains, l, w_ffn_in, ffn_conv[l], ffn_conv_b[l][None, :],
                          w_ffn_out, seq_len, mod_row0, tm)
        return x, ctx_out

    y_ctx, (k_new, v_new, s_new) = run_group(
        x_prompt.reshape(n_ctx * l_ctx, D_MODEL), n_ctx, l_ctx, 0, False)
    y_lat, _ = run_group(x_sample.reshape(n_lat * l_lat, D_MODEL), n_lat, l_lat, 1, True)

    n_even = (DEPTH + 1) // 2
    assert n_even == 1
    return (
        y_ctx.reshape(n_ctx, l_ctx, D_MODEL),
        y_lat.reshape(n_lat, l_lat, D_MODEL),
        k_new,
        v_new,
        s_new.reshape(n_ctx, n_even, 2, B_HEADS, B_HEAD, B_HEAD),
    )
```

```python
import functools
import math

import numpy as np
import jax
import jax.numpy as jnp
from jax import lax
from jax.experimental import pallas as pl
from jax.experimental.pallas import tpu as pltpu

F32 = jnp.float32
BF16 = jnp.bfloat16

D_MODEL = 1024
DEPTH = 2
GRID_W = 64
A_WIDTH = D_MODEL // 2
A_HEADS = 4
A_HEAD_DIM = A_WIDTH // A_HEADS
A_HALF = A_HEAD_DIM // 2
B_WIDTH = D_MODEL - A_WIDTH
B_HEAD = 64
B_HEADS = B_WIDTH // B_HEAD
LORA_W = 64
LORA_A = 64
LORA_G = 128
IN_B = 3 * B_WIDTH + 2 * LORA_W + 2 * LORA_A + LORA_G
IN_EVEN = 3 * A_WIDTH + IN_B
C_GROUPS = 8
C_GROUP_DIM = D_MODEL // C_GROUPS
D_FF = 2816
ROPE_BASE = 10000.0
RMS_EPS = 1e-6
LNX_EPS = 64e-5

LANES = 128
SUBLANES = 8
PAIR = 2 * B_HEAD
N_PAIRS = B_HEADS // 2
CHUNK = 64
FF_CHUNK = 256
FF_ROWS = 256
PREP_ROWS = 256
ATTN_Q_ROWS = 256
ODD_ROWS = 512
ODD_COLS = 256
VMEM_LIMIT = 48 * 1024 * 1024
MOD_ROWS = 8


def _cparams(*sem):
    return pltpu.CompilerParams(dimension_semantics=sem, vmem_limit_bytes=VMEM_LIMIT)


def _sigmoid(x):
    return 1.0 / (1.0 + jnp.exp(-x))


def _silu(x):
    return x * (0.5 + 0.5 * jnp.tanh(0.5 * x))


def _dot(a, b):
    return jnp.dot(a.astype(BF16), b.astype(BF16), preferred_element_type=F32)


_NN = (((1,), (0,)), ((), ()))
_NT = (((1,), (1,)), ((), ()))
_TN = (((0,), (0,)), ((), ()))


def _split_bf16(x, n):
    parts = []
    rem = x
    for i in range(n):
        p = rem.astype(BF16)
        parts.append(p)
        if i + 1 < n:
            rem = rem - p.astype(F32)
    return parts


def _mm(a, b, dims, passes):
    if passes == 1:
        return lax.dot_general(a.astype(BF16), b.astype(BF16), dims, preferred_element_type=F32)
    a_hi, a_lo = _split_bf16(a, 2)
    b_hi, b_lo = _split_bf16(b, 2)
    dg = functools.partial(lax.dot_general, dimension_numbers=dims, preferred_element_type=F32)
    return dg(a_hi, b_hi) + (dg(a_hi, b_lo) + dg(a_lo, b_hi))


def _segsum(x, bd, pieces=2):
    acc = None
    for p in _split_bf16(x, pieces):
        t = jnp.dot(p, bd, preferred_element_type=F32)
        acc = t if acc is None else acc + t
    return acc


def _rms(x):
    return x * lax.rsqrt(jnp.mean(x * x, axis=-1, keepdims=True) + RMS_EPS)


def _modnorm(x, gain, scale, shift):
    return _rms(x) * gain * (1.0 + scale) + shift


def _mod_kernel(c_ref, w_ref, b_ref, o_ref):
    c = c_ref[...]
    s = c * _sigmoid(c)
    o_ref[0] = _dot(s, w_ref[0]) + b_ref[0]


def _modulation(cvec, w_ada, b_ada):
    tn = 1536
    n = 6 * D_MODEL
    return pl.pallas_call(
        _mod_kernel,
        grid=(DEPTH, n // tn),
        in_specs=[
            pl.BlockSpec((MOD_ROWS, D_MODEL), lambda l, j: (0, 0)),
            pl.BlockSpec((1, D_MODEL, tn), lambda l, j: (l, 0, j)),
            pl.BlockSpec((1, 1, tn), lambda l, j: (l, 0, j)),
        ],
        out_specs=pl.BlockSpec((1, MOD_ROWS, tn), lambda l, j: (l, 0, j)),
        out_shape=jax.ShapeDtypeStruct((DEPTH, MOD_ROWS, n), F32),
        compiler_params=_cparams("parallel", "parallel"),
        name="modulation",
    )(cvec, w_ada, b_ada.reshape(DEPTH, 1, n))


def _mod_spec(tm, seq_len, mod_row0):
    if mod_row0 == 0:
        return pl.BlockSpec((1, 6, D_MODEL), lambda i, *_: (0, 0, 0))
    return pl.BlockSpec((1, 6, D_MODEL), lambda i, *_: (mod_row0 + (i * tm) // seq_len, 0, 0))


def _rope(x, cos_t, sin_t):
    lane = lax.broadcasted_iota(jnp.int32, x.shape, 1)
    nxt = pltpu.roll(x, LANES - 1, 1)
    prv = pltpu.roll(x, 1, 1)
    partner = jnp.where(lane % 2 == 0, nxt, prv)
    return x * cos_t + partner * sin_t


def _even_in_kernel(*refs, rope, cache):
    x_ref, mod_ref, gains_ref, w_ref = refs[:4]
    refs = refs[4:]
    if rope:
        cos_ref, sin_ref = refs[:2]
        refs = refs[2:]
    q_ref, k_ref, v_ref, fb_ref = refs[:4]
    cache_refs = refs[4:]
    mod = mod_ref[0]
    h = _modnorm(x_ref[...], gains_ref[0:1], mod[1:2], mod[0:1]).astype(BF16)
    a = A_WIDTH
    hd = A_HEAD_DIM
    for n, out_ref in enumerate((q_ref, k_ref, v_ref)):
        y = jnp.dot(h, w_ref[:, n * a:(n + 1) * a], preferred_element_type=F32)
        if cache and n > 0:
            for hh in range(A_HEADS):
                cache_refs[n - 1][0, 0, :, hh, :] = y[:, hh * hd:(hh + 1) * hd]
        if rope and n < 2:
            for hh in range(A_HEADS):
                cols = slice(hh * hd, (hh + 1) * hd)
                out_ref[:, cols] = _rope(y[:, cols], cos_ref[...], sin_ref[...]).astype(BF16)
        else:
            out_ref[...] = y.astype(BF16)
    fb_ref[...] = jnp.dot(h, w_ref[:, 3 * a:], preferred_element_type=F32)


def _even_in(x, mod, gains, w, seq_len, mod_row0, tm, rope=None, cache=False):
    t = x.shape[0]
    assert seq_len % tm == 0
    per_seq = seq_len // tm
    row = lambda i: (i, 0)
    operands = [x, mod, gains, w]
    in_specs = [
        pl.BlockSpec((tm, D_MODEL), row),
        _mod_spec(tm, seq_len, mod_row0),
        pl.BlockSpec((4, D_MODEL), lambda i: (0, 0)),
        pl.BlockSpec((D_MODEL, IN_EVEN), lambda i: (0, 0)),
    ]
    if rope is not None:
        tab = pl.BlockSpec((tm, A_HEAD_DIM), lambda i: (i % per_seq, 0))
        operands += list(rope)
        in_specs += [tab, tab]
    out_specs = [pl.BlockSpec((tm, A_WIDTH), row)] * 3 + [pl.BlockSpec((tm, IN_B), row)]
    out_shape = [jax.ShapeDtypeStruct((t, A_WIDTH), BF16)] * 3 + [jax.ShapeDtypeStruct((t, IN_B), F32)]
    if cache:
        out_specs += [pl.BlockSpec((1, 1, tm, A_HEADS, A_HEAD_DIM),
                                   lambda i: (i // per_seq, 0, i % per_seq, 0, 0))] * 2
        out_shape += [jax.ShapeDtypeStruct((t // seq_len, 1, seq_len, A_HEADS, A_HEAD_DIM), F32)] * 2
    return pl.pallas_call(
        functools.partial(_even_in_kernel, rope=rope is not None, cache=cache),
        grid=(t // tm,),
        in_specs=in_specs,
        out_specs=out_specs,
        out_shape=out_shape,
        compiler_params=_cparams("parallel"),
        name="even_in",
    )(*operands)


def _attn_kernel(*refs, has_ctx, lam_init):
    if has_ctx:
        lam_ref, sub_ref, q_ref, k_ref, v_ref, ck_ref, cv_ref, o_ref = refs
    else:
        lam_ref, sub_ref, q_ref, k_ref, v_ref, o_ref = refs
    lp = lam_ref[...]
    l1 = jnp.sum(lp[0:1] * lp[1:2], axis=-1, keepdims=True)
    l2 = jnp.sum(lp[2:3] * lp[3:4], axis=-1, keepdims=True)
    lam = jnp.exp(l1) - jnp.exp(l2) + lam_init
    hd = A_HEAD_DIM
    lane = lax.broadcasted_iota(jnp.int32, (q_ref.shape[0], hd), 1)
    scale = A_HALF ** -0.5

    def head(h):
        cols = pl.ds(h * hd, hd)
        q = q_ref[:, cols]
        k = k_ref[:, cols].astype(BF16)
        if has_ctx:
            ck = ck_ref[:, cols].astype(BF16)
        scores = []
        for m in range(2):
            qm = jnp.where((lane < A_HALF) == (m == 0), q, 0.0).astype(BF16)
            s = lax.dot_general(qm, k, _NT, preferred_element_type=F32) * scale
            sc = lax.dot_general(qm, ck, _NT, preferred_element_type=F32) * scale if has_ctx else None
            scores.append((s, sc))
        yield
        probs = []
        for s, sc in scores:
            mx = jnp.max(s, axis=-1, keepdims=True)
            if has_ctx:
                mx = jnp.maximum(mx, jnp.max(sc, axis=-1, keepdims=True))
                ec = jnp.exp(sc - mx)
            e = jnp.exp(s - mx)
            z = jnp.sum(e, axis=-1, keepdims=True)
            if has_ctx:
                z = z + jnp.sum(ec, axis=-1, keepdims=True)
            inv = 1.0 / z
            probs.append((e * inv, ec * inv if has_ctx else None))
        w = (probs[0][0] - lam * probs[1][0]).astype(BF16)
        if has_ctx:
            wc = (probs[0][1] - lam * probs[1][1]).astype(BF16)
        yield
        o = jnp.dot(w, v_ref[:, cols].astype(BF16), preferred_element_type=F32)
        if has_ctx:
            o = o + jnp.dot(wc, cv_ref[:, cols].astype(BF16), preferred_element_type=F32)
        yield
        o_ref[:, cols] = (_rms(o) * sub_ref[...] * (1.0 - lam_init)).astype(BF16)

    _round_robin(head(h) for h in range(A_HEADS))


def _attention(q, k, v, lam_p, subln, lam_init, n_seq, seq_len, ctx=None):
    t = q.shape[0]
    tq = min(seq_len, ATTN_Q_ROWS if ctx is None else ATTN_Q_ROWS // 2)
    nq = seq_len // tq
    small = [pl.BlockSpec((4, A_HALF), lambda b, i: (0, 0)),
             pl.BlockSpec((1, A_HEAD_DIM), lambda b, i: (0, 0))]
    qblk = pl.BlockSpec((tq, A_WIDTH), lambda b, i: (b * nq + i, 0))
    kblk = pl.BlockSpec((seq_len, A_WIDTH), lambda b, i: (b, 0))
    operands = [lam_p, subln, q, k, v]
    in_specs = small + [qblk, kblk, kblk]
    if ctx is not None:
        ck, cv = ctx
        cblk = pl.BlockSpec((ck.shape[0] // n_seq, A_WIDTH), lambda b, i: (b, 0))
        operands += [ck, cv]
        in_specs += [cblk, cblk]
    return pl.pallas_call(
        functools.partial(_attn_kernel, has_ctx=ctx is not None, lam_init=lam_init),
        grid=(n_seq, nq),
        in_specs=in_specs,
        out_specs=qblk,
        out_shape=jax.ShapeDtypeStruct((t, A_WIDTH), BF16),
        compiler_params=_cparams("parallel", "parallel"),
        name="diff_attn",
    )(*operands)


def _cumsum_rows(x, reverse):
    n = x.shape[0]
    ridx = lax.broadcasted_iota(jnp.int32, x.shape, 0)
    s = 1
    while s < n:
        if reverse:
            x = x + jnp.where(ridx < n - s, pltpu.roll(x, n - s, 0), 0.0)
        else:
            x = x + jnp.where(ridx >= s, pltpu.roll(x, s, 0), 0.0)
        s *= 2
    return x


def _round_robin(gens):
    gens = list(gens)
    while gens:
        alive = []
        for gen in gens:
            try:
                next(gen)
                alive.append(gen)
            except StopIteration:
                pass
        gens = alive


def _rwkv_kernel(*refs, seq_len, has_state, pairs, passes, inv_passes, carry_passes, unroll):
    (fr_ref, fk_ref, fv_ref, fl_ref, mur_ref, muk_ref, muv_ref, mul_ref, w0_ref, w2_ref, a0_ref,
     a2_ref, g2_ref, kvec_ref, lnx_ref, bd_ref, bdm_ref) = refs[:17]
    refs = refs[17:]
    if has_state:
        s0_ref = refs[0]
        refs = refs[1:]
    yb_ref, sf_ref, pre_ref, g_scr, bonus_scr, r1_scr, m1_scr, d0_scr, gam_scr, y_scr, s_scr = refs

    def shifted(ref, mu_ref, r0):
        f = ref[r0:r0 + PREP_ROWS, :]
        zero = jnp.zeros((1, f.shape[1]), F32)
        before = zero if r0 == 0 else ref[r0 - 1:r0, :]
        after = zero if r0 + PREP_ROWS == seq_len else ref[r0 + PREP_ROWS:r0 + PREP_ROWS + 1, :]
        local = lax.broadcasted_iota(jnp.int32, (PREP_ROWS, 1), 0)
        prev = jnp.where(local == 0, before, pltpu.roll(f, 1, 0))
        nxt = jnp.where(local == PREP_ROWS - 1, after, pltpu.roll(f, PREP_ROWS - 1, 0))
        return f * (1.0 - mu_ref[0:1] - mu_ref[1:2]) + mu_ref[0:1] * prev + mu_ref[1:2] * nxt

    bd = bd_ref[...]
    kv = kvec_ref[...]
    for r0 in range(0, seq_len, PREP_ROWS):
        rows = slice(r0, r0 + PREP_ROWS)
        r = shifted(fr_ref, mur_ref, r0)
        k = shifted(fk_ref, muk_ref, r0)
        v = shifted(fv_ref, muv_ref, r0)
        lora = shifted(fl_ref, mul_ref, r0)
        wd = jnp.tanh(lora[:, 0:2 * LORA_W])
        ad = lora[:, 2 * LORA_W:2 * LORA_W + 2 * LORA_A]
        gd = lora[:, 2 * LORA_W + 2 * LORA_A:]
        g_scr[rows, :] = _dot(_sigmoid(gd), g2_ref[...])
        kk = k * kv[0:1]
        kk = kk * lax.rsqrt(_segsum(kk * kk, bd, pieces=1) + 1e-12)
        pre_ref[0, rows, :] = r
        pre_ref[1, rows, :] = v
        pre_ref[2, rows, :] = kk
        ksum = None
        for d in range(2):
            w_raw = w0_ref[d:d + 1] + _dot(wd, w2_ref[d])
            pre_ref[7 + d, rows, :] = -math.exp(-0.5) * _sigmoid(w_raw)
            a = _sigmoid(a0_ref[d:d + 1] + _dot(ad, a2_ref[d]))
            kd = k * (1.0 + (a - 1.0) * kv[1:2])
            pre_ref[3 + d, rows, :] = kd
            pre_ref[5 + d, rows, :] = kk * a
            ksum = kd if ksum is None else ksum + kd
        bonus_scr[rows, :] = _segsum(r * ksum * kv[2:3], bd) * v

    nc = seq_len // CHUNK
    c2 = 2 * CHUNK
    lane = lax.broadcasted_iota(jnp.int32, (CHUNK, PAIR), 1)
    head0 = lane < B_HEAD
    row = lax.broadcasted_iota(jnp.int32, (c2, c2), 0)
    col = lax.broadcasted_iota(jnp.int32, (c2, c2), 1)
    same = (row // CHUNK) == (col // CHUNK)
    eye = jnp.where(row == col, 1.0, 0.0)
    xor_rc = jnp.bitwise_xor(row, col)
    mm = functools.partial(_mm, passes=passes)
    mm_inv = functools.partial(_mm, passes=inv_passes)
    mm_carry = functools.partial(_mm, passes=carry_passes)

    def stack(x):
        return jnp.concatenate([jnp.where(head0, x, 0.0), jnp.where(head0, 0.0, x)], axis=0)

    def fold(x):
        return x[0:CHUNK] + x[CHUNK:c2]

    def chunk_rows(cc):
        return pl.ds(pl.multiple_of(cc * CHUNK, CHUNK), CHUNK)

    def chunk_local(cc, d, q):
        rows = chunk_rows(cc)
        lanes = pl.ds(q * PAIR, PAIR)
        z = 2 * q + d
        r = pre_ref[0, rows, lanes]
        v = pre_ref[1, rows, lanes]
        kk = pre_ref[2, rows, lanes]
        k = pre_ref[3 + d, rows, lanes]
        b = pre_ref[5 + d, rows, lanes]
        lw = pre_ref[7 + d, rows, lanes]
        incl = same & ((col <= row) if d == 0 else (col >= row))
        strict = same & ((col < row) if d == 0 else (col > row))
        g = _cumsum_rows(lw, reverse=(d == 1))
        gp = g - lw
        gtot = g[CHUNK - 1:CHUNK, :] if d == 0 else g[0:1, :]
        gm = 0.5 * gtot
        kx = k * jnp.exp(gm - g)
        bx = b * jnp.exp(gm - g)
        lhs = jnp.concatenate([stack(r * jnp.exp(g - gm)), stack(kk * jnp.exp(gp - gm))], axis=0)
        rhs = jnp.concatenate([kx, kx, bx, bx], axis=0)
        a_all = mm(lhs, rhs, _NT)
        yield
        a_rk = jnp.where(incl, a_all[0:c2, 0:c2], 0.0)
        a_rb = jnp.where(incl, a_all[0:c2, c2:2 * c2], 0.0)
        a_kk = jnp.where(strict, a_all[c2:2 * c2, 0:c2], 0.0)
        n_mat = jnp.where(strict, a_all[c2:2 * c2, c2:2 * c2], 0.0)
        vs = stack(v)
        akv = mm(a_kk, vs, _NN)
        y0 = mm(a_rk, vs, _NN)
        lower = (col < row) if d == 0 else (col > row)
        p_inv = eye - jnp.where(lower & (xor_rc == 1), n_mat, 0.0)
        yield
        s = 2
        while s < CHUNK:
            c_s = jnp.where(lower & (xor_rc >= s) & (xor_rc < 2 * s), n_mat, 0.0)
            t = mm_inv(c_s, p_inv, _NN)
            yield
            p_inv = p_inv - mm_inv(p_inv, t, _NN)
            yield
            s *= 2
        x = jnp.concatenate([stack(kk * jnp.exp(gp)), akv], axis=1)
        w12 = mm(p_inv, x, _NN)
        yield
        arb_w = mm(a_rb, w12, _NN)
        et = jnp.exp(gtot - g)
        bes = stack(b * et)
        d0_scr[z, cc] = mm(jnp.concatenate([vs, -w12[:, PAIR:]], axis=0),
                           jnp.concatenate([stack(k * et), bes], axis=0), _TN)
        m1_scr[z, cc] = mm(w12[:, 0:PAIR], bes, _TN)
        yield
        r1_scr[z, cc] = stack(r * jnp.exp(g)) - arb_w[:, 0:PAIR]
        y_scr[z, rows, :] = fold(y0 - arb_w[:, PAIR:])
        gam_scr[z, cc] = jnp.broadcast_to(jnp.exp(gtot), (SUBLANES, PAIR))

    def local_body(grp, carry):
        _round_robin(chunk_local(grp * unroll + u, d, q)
                     for u in range(unroll) for q in range(pairs) for d in range(2))
        return carry

    lax.fori_loop(0, nc // unroll, local_body, 0)

    for z in range(2 * pairs):
        s_scr[z] = jnp.zeros((PAIR, PAIR), F32)
        if has_state:
            s_scr[z, 0:B_HEAD, 0:B_HEAD] = s0_ref[0, z % 2, 2 * (z // 2)]
            s_scr[z, B_HEAD:PAIR, B_HEAD:PAIR] = s0_ref[0, z % 2, 2 * (z // 2) + 1]

    def carry_body(c, carry):
        for z in range(2 * pairs):
            cc = c if z % 2 == 0 else nc - 1 - c
            rows = chunk_rows(cc)
            s0 = s_scr[z]
            y_scr[z, rows, :] += fold(mm_carry(r1_scr[z, cc], s0, _NT))
            s_scr[z] = s0 * gam_scr[z, cc, 0:1, :] - mm_carry(s0, m1_scr[z, cc], _NN) + d0_scr[z, cc]
        return carry

    lax.fori_loop(0, nc, carry_body, 0)
    bdm = bdm_ref[...]
    for r0 in range(0, seq_len, PREP_ROWS):
        rows = slice(r0, r0 + PREP_ROWS)
        ys = jnp.concatenate([y_scr[2 * q, rows, :] + y_scr[2 * q + 1, rows, :] for q in range(pairs)],
                             axis=1)
        dlt = ys - _segsum(ys, bdm)
        yn = dlt * lax.rsqrt(_segsum(dlt * dlt, bdm, pieces=1) + LNX_EPS)
        yb = (yn * lnx_ref[0:1] + lnx_ref[1:2] + bonus_scr[rows, :]) * g_scr[rows, :]
        yb_ref[rows, :] = yb.astype(BF16)
    for z in range(2 * pairs):
        sf_ref[0, z % 2, 2 * (z // 2)] = s_scr[z, 0:B_HEAD, 0:B_HEAD]
        sf_ref[0, z % 2, 2 * (z // 2) + 1] = s_scr[z, B_HEAD:PAIR, B_HEAD:PAIR]


def _rwkv_mix(fb, s0, p, n_seq, seq_len, pairs=2, passes=1, inv_passes=1, carry_passes=1, unroll=4):
    t = fb.shape[0]
    nc = seq_len // CHUNK
    w = pairs * PAIR
    nw = B_WIDTH // w
    lora_w = 2 * LORA_W + 2 * LORA_A + LORA_G
    assert seq_len % PREP_ROWS == 0 and (3 * B_WIDTH) % lora_w == 0
    col = lambda rows, base: pl.BlockSpec((rows, w), lambda b, q: (0, base + q))
    seq = lambda base: pl.BlockSpec((seq_len, w), lambda b, q: (b, base + q))
    lora_blk = 3 * B_WIDTH // lora_w
    sblk = pl.BlockSpec((1, 2, 2 * pairs, B_HEAD, B_HEAD), lambda b, q: (b, 0, q, 0, 0))
    mat = pltpu.VMEM((2 * pairs, nc, PAIR, PAIR), F32)
    in_specs = [
        seq(0), seq(nw), seq(2 * nw),
        pl.BlockSpec((seq_len, lora_w), lambda b, q: (b, lora_blk)),
        col(2, 0), col(2, nw), col(2, 2 * nw),
        pl.BlockSpec((2, lora_w), lambda b, q: (0, lora_blk)),
        col(2, 0),
        pl.BlockSpec((2, 2 * LORA_W, w), lambda b, q: (0, 0, q)),
        col(2, 0),
        pl.BlockSpec((2, 2 * LORA_A, w), lambda b, q: (0, 0, q)),
        col(LORA_G, 0), col(3, 0), col(2, 0),
        pl.BlockSpec((w, w), lambda b, q: (0, 0)),
        pl.BlockSpec((w, w), lambda b, q: (0, 0)),
    ]
    operands = [fb, fb, fb, fb, p["mu"], p["mu"], p["mu"], p["mu"], p["w0"], p["w2"], p["a0"], p["a2"],
                p["g2"], p["kvec"], p["lnx"], _block_diag(1.0, w), _block_diag(1.0 / B_HEAD, w)]
    if s0 is not None:
        in_specs.append(sblk)
        operands.append(s0)
    return pl.pallas_call(
        functools.partial(_rwkv_kernel, seq_len=seq_len, has_state=s0 is not None, pairs=pairs,
                          passes=passes, inv_passes=inv_passes, carry_passes=carry_passes, unroll=unroll),
        grid=(n_seq, nw),
        in_specs=in_specs,
        out_specs=[pl.BlockSpec((seq_len, w), lambda b, q: (b, q)), sblk],
        out_shape=[
            jax.ShapeDtypeStruct((t, B_WIDTH), BF16),
            jax.ShapeDtypeStruct((n_seq, 2, B_HEADS, B_HEAD, B_HEAD), F32),
        ],
        scratch_shapes=[pltpu.VMEM((9, seq_len, w), F32), pltpu.VMEM((seq_len, w), F32),
                        pltpu.VMEM((seq_len, w), F32), mat, mat, mat,
                        pltpu.VMEM((2 * pairs, nc, SUBLANES, PAIR), F32),
                        pltpu.VMEM((2 * pairs, seq_len, PAIR), F32),
                        pltpu.VMEM((2 * pairs, PAIR, PAIR), F32)],
        compiler_params=_cparams("parallel", "parallel"),
        name="rwkv_mix",
    )(*operands)


def _even_out_kernel(x_ref, mod_ref, gains_ref, ya_ref, yb_ref, w_ref, o_ref):
    m = (jnp.dot(ya_ref[...], w_ref[0:A_WIDTH, :], preferred_element_type=F32)
         + jnp.dot(yb_ref[...], w_ref[A_WIDTH:, :], preferred_element_type=F32))
    mod = mod_ref[0]
    o_ref[...] = x_ref[...] + mod[2:3] * (_rms(m) * gains_ref[1:2])


def _even_out(x, mod, gains, ya, yb, w, seq_len, mod_row0, tm):
    t = x.shape[0]
    row = lambda i: (i, 0)
    return pl.pallas_call(
        _even_out_kernel,
        grid=(t // tm,),
        in_specs=[
            pl.BlockSpec((tm, D_MODEL), row),
            _mod_spec(tm, seq_len, mod_row0),
            pl.BlockSpec((4, D_MODEL), lambda i: (0, 0)),
            pl.BlockSpec((tm, A_WIDTH), row),
            pl.BlockSpec((tm, B_WIDTH), row),
            pl.BlockSpec((D_MODEL, D_MODEL), lambda i: (0, 0)),
        ],
        out_specs=pl.BlockSpec((tm, D_MODEL), row),
        out_shape=jax.ShapeDtypeStruct((t, D_MODEL), F32),
        compiler_params=_cparams("parallel"),
        name="even_out",
    )(x, mod, gains, ya, yb, w)


def _odd_kernel(x_ref, mod_ref, gains_ref, csc_ref, cl_ref, sl_ref, w_ref, o_ref,
                tc_scr, ts_scr, m_scr, *, seq_len):
    mod = mod_ref[0]
    gd = C_GROUP_DIM
    n_seq = x_ref.shape[0] // seq_len
    csc = csc_ref[...]

    for s in range(n_seq):
        rows = pl.ds(s * seq_len, seq_len)
        h = _modnorm(x_ref[rows, :], gains_ref[0:1], mod[1:2], mod[0:1]).astype(BF16)
        for g in range(C_GROUPS):
            t = jnp.dot(h[:, g * gd:(g + 1) * gd], csc, preferred_element_type=F32)
            tc_scr[rows, g * gd:(g + 1) * gd] = t[:, 0:gd].astype(BF16)
            ts_scr[rows, g * gd:(g + 1) * gd] = t[:, gd:].astype(BF16)

    def position_dft(s, cb):
        rows = pl.ds(s * seq_len, seq_len)
        cols = pl.ds(cb * ODD_COLS, ODD_COLS)
        f = (jnp.dot(cl_ref[...], tc_scr[rows, cols], preferred_element_type=F32)
             - jnp.dot(sl_ref[...], ts_scr[rows, cols], preferred_element_type=F32))
        yield
        part = jnp.dot(f.astype(BF16), w_ref[cols, :], preferred_element_type=F32)
        yield
        if cb == 0:
            m_scr[rows, :] = part
        else:
            m_scr[rows, :] += part

    _round_robin(position_dft(s, cb) for s in range(n_seq) for cb in range(D_MODEL // ODD_COLS))
    o_ref[...] = x_ref[...] + mod[2:3] * (_rms(m_scr[...]) * gains_ref[1:2])


def _dft_tables(n):
    idx = np.arange(n, dtype=np.int64)
    ang = 2.0 * np.pi * ((idx[:, None] * idx[None, :]) % n).astype(np.float64) / n
    s = 1.0 / math.sqrt(n)
    return np.cos(ang) * s, np.sin(ang) * s


def _odd_mixer(x, mod, gains, w, n_seq, seq_len, mod_row0):
    cc, sc = _dft_tables(C_GROUP_DIM)
    cl, sl = _dft_tables(seq_len)
    csc, cl, sl = (jnp.asarray(a, F32).astype(BF16) for a in (np.concatenate([cc, sc], axis=1), cl, sl))
    full = lambda shape: pl.BlockSpec(shape, lambda i: (0,) * len(shape))
    row = lambda i: (i, 0)
    tm = max(seq_len, ODD_ROWS)
    assert tm % seq_len == 0 and (n_seq * seq_len) % tm == 0
    return pl.pallas_call(
        functools.partial(_odd_kernel, seq_len=seq_len),
        grid=(n_seq * seq_len // tm,),
        in_specs=[
            pl.BlockSpec((tm, D_MODEL), row),
            _mod_spec(tm, seq_len, mod_row0),
            full((4, D_MODEL)), full((C_GROUP_DIM, 2 * C_GROUP_DIM)),
            full((seq_len, seq_len)), full((seq_len, seq_len)), full((D_MODEL, D_MODEL)),
        ],
        out_specs=pl.BlockSpec((tm, D_MODEL), row),
        out_shape=jax.ShapeDtypeStruct(x.shape, F32),
        scratch_shapes=[pltpu.VMEM((tm, D_MODEL), BF16), pltpu.VMEM((tm, D_MODEL), BF16),
                        pltpu.VMEM((tm, D_MODEL), F32)],
        compiler_params=_cparams("parallel"),
        name="odd_mixer",
    )(x, mod, gains, csc, cl, sl, w)


def _ffn_kernel(x_ref, mod_ref, gains_ref, wu_ref, wg_ref, cw_ref, cb_ref, wo_ref, o_ref,
                h_scr, acc_scr, *, seq_len):
    j = pl.program_id(1)
    mod = mod_ref[0]

    @pl.when(j == 0)
    def _():
        h_scr[...] = _modnorm(x_ref[...], gains_ref[2:3], mod[4:5], mod[3:4]).astype(BF16)
        acc_scr[...] = jnp.zeros_like(acc_scr)

    n_sub = h_scr.shape[0] // FF_ROWS
    gate = [None] * n_sub
    zero_row = jnp.zeros((1, FF_CHUNK), F32)
    wu = wu_ref[...].astype(BF16)
    wg = wg_ref[...].astype(BF16)
    wo = wo_ref[...].astype(BF16)

    def sub_tile(i):
        rows = pl.ds(i * FF_ROWS, FF_ROWS)
        h = h_scr[rows, :]
        u = jnp.dot(h, wu, preferred_element_type=F32)
        g = gate[i] = jnp.dot(h, wg, preferred_element_type=F32)
        yield
        before = zero_row if (i * FF_ROWS) % seq_len == 0 else gate[i - 1][FF_ROWS - 1:FF_ROWS, :]
        after = zero_row if ((i + 1) * FF_ROWS) % seq_len == 0 else gate[i + 1][0:1, :]
        pos = lax.broadcasted_iota(jnp.int32, (FF_ROWS, 1), 0)
        prev = jnp.where(pos == 0, before, pltpu.roll(g, 1, 0))
        nxt = jnp.where(pos == FF_ROWS - 1, after, pltpu.roll(g, FF_ROWS - 1, 0))
        gc = prev * cw_ref[0:1] + g * cw_ref[1:2] + nxt * cw_ref[2:3] + cb_ref[...]
        act = (_silu(gc) * u).astype(BF16)
        yield
        acc_scr[rows, :] += jnp.dot(act, wo, preferred_element_type=F32)

    _round_robin(sub_tile(i) for i in range(n_sub))

    @pl.when(j == pl.num_programs(1) - 1)
    def _():
        o_ref[...] = x_ref[...] + mod[5:6] * (_rms(acc_scr[...]) * gains_ref[3:4])


def _conv_ffn(x, mod, gains, layer, w_in, conv_w, conv_b, w_out, seq_len, mod_row0, tm):
    t = x.shape[0]
    nj = D_FF // FF_CHUNK
    assert seq_len % FF_ROWS == 0 and tm % seq_len == 0
    return pl.pallas_call(
        functools.partial(_ffn_kernel, seq_len=seq_len),
        grid=(t // tm, nj),
        in_specs=[
            pl.BlockSpec((tm, D_MODEL), lambda i, j: (i, 0)),
            _mod_spec(tm, seq_len, mod_row0),
            pl.BlockSpec((4, D_MODEL), lambda i, j: (0, 0)),
            pl.BlockSpec((None, D_MODEL, FF_CHUNK), lambda i, j: (layer, 0, j)),
            pl.BlockSpec((None, D_MODEL, FF_CHUNK), lambda i, j: (layer, 0, nj + j)),
            pl.BlockSpec((3, FF_CHUNK), lambda i, j: (0, j)),
            pl.BlockSpec((1, FF_CHUNK), lambda i, j: (0, j)),
            pl.BlockSpec((None, FF_CHUNK, D_MODEL), lambda i, j: (layer, j, 0)),
        ],
        out_specs=pl.BlockSpec((tm, D_MODEL), lambda i, j: (i, 0)),
        out_shape=jax.ShapeDtypeStruct((t, D_MODEL), F32),
        scratch_shapes=[pltpu.VMEM((tm, D_MODEL), BF16), pltpu.VMEM((tm, D_MODEL), F32)],
        compiler_params=_cparams("parallel", "arbitrary"),
        name="conv_ffn",
    )(x, mod, gains, w_in, w_in, conv_w, conv_b, w_out)


def _rope_tables(seq_len):
    rows = seq_len // GRID_W
    row = jnp.repeat(jnp.arange(rows, dtype=F32), GRID_W)
    col = jnp.tile(jnp.arange(GRID_W, dtype=F32), rows)
    n_freq = A_HALF // 4
    inv = ROPE_BASE ** (-jnp.arange(n_freq, dtype=F32) / n_freq)
    ang = jnp.concatenate([row[:, None] * inv, col[:, None] * inv], axis=-1)
    cos_t = jnp.tile(jnp.repeat(jnp.cos(ang), 2, axis=1), (1, 2))
    sin_t = jnp.tile(jnp.repeat(jnp.sin(ang), 2, axis=1), (1, 2))
    sign = jnp.where(jnp.arange(A_HEAD_DIM) % 2 == 0, -1.0, 1.0).astype(F32)
    return cos_t, sin_t * sign


def _pad_lora(w):
    z = jnp.zeros_like(w[0])
    return jnp.stack([jnp.concatenate([w[0], z], axis=0), jnp.concatenate([z, w[1]], axis=0)])


def _block_diag(value, width):
    head = np.arange(width) // B_HEAD
    return jnp.asarray(np.where(head[:, None] == head[None, :], value, 0.0), BF16)


def _tile_rows(seq_len):
    return max(seq_len, 1024)


def kernel(x_prompt, x_sample, cache_k, cache_v, state_wkv, c, c_ctx, w_ada, b_ada, norm_gains,
           w_in_even, w_out_even, diff_lambda, diff_subln, rwkv_shift_mu, rwkv_w0, rwkv_w2, rwkv_a0,
           rwkv_a2, rwkv_g2, rwkv_kvec, rwkv_lnx, w_out_odd, w_ffn_in, ffn_conv, ffn_conv_b,
           w_ffn_out):
    n_ctx, l_ctx, _ = x_prompt.shape
    n_lat, l_lat, _ = x_sample.shape
    assert 1 + n_lat <= MOD_ROWS
    cvec = jnp.concatenate(
        [c_ctx[None, :], c, jnp.zeros((MOD_ROWS - 1 - n_lat, D_MODEL), F32)], axis=0)
    mods = _modulation(cvec, w_ada, b_ada).reshape(DEPTH, MOD_ROWS, 6, D_MODEL)

    per_layer_bf16 = lambda w: [w[i].astype(BF16) for i in range(w.shape[0])]
    w_in_even_b = per_layer_bf16(w_in_even)
    w_out_even_b = per_layer_bf16(w_out_even)
    w_out_odd_b = per_layer_bf16(w_out_odd)
    cos_t, sin_t = _rope_tables(l_lat)

    def run_group(x, n_seq, seq_len, mod_row0, latent):
        tm = _tile_rows(seq_len)
        ctx_out = None
        for l in range(DEPTH):
            mod = mods[l]
            gains = norm_gains[l]
            if l % 2 == 0:
                e = l // 2
                lam_init = 0.8 - 0.6 * math.exp(-0.3 * l)
                q, k, v, fb, *kv_cache = _even_in(
                    x, mod, gains, w_in_even_b[e], seq_len, mod_row0, 256,
                    rope=(cos_t, sin_t) if latent else None, cache=not latent)
                if latent:
                    past = cache_k.shape[2]
                    ctx = (cache_k[:, e].reshape(n_seq * past, A_WIDTH),
                           cache_v[:, e].reshape(n_seq * past, A_WIDTH))
                    s0 = state_wkv[:, e]
                else:
                    ctx = None
                    s0 = None
                ya = _attention(q, k, v, diff_lambda[e], diff_subln[e][None, :], lam_init,
                                n_seq, seq_len, ctx)
                rwkv = {
                    "mu": rwkv_shift_mu[e], "w0": rwkv_w0[e], "w2": _pad_lora(rwkv_w2[e]).astype(BF16),
                    "a0": rwkv_a0[e], "a2": _pad_lora(rwkv_a2[e]).astype(BF16),
                    "g2": rwkv_g2[e].astype(BF16), "kvec": rwkv_kvec[e], "lnx": rwkv_lnx[e],
                }
                yb, s_fin = _rwkv_mix(fb, s0, rwkv, n_seq, seq_len)
                x = _even_out(x, mod, gains, ya, yb, w_out_even_b[e], seq_len, mod_row0, 256)
                if not latent:
                    ctx_out = (*kv_cache, s_fin)
            else:
                x = _odd_mixer(x, mod, gains, w_out_odd_b[l // 2], n_seq, seq_len, mod_row0)
            x = _conv_ffn(x, mod, gains, l, w_ffn_in, ffn_conv[l], ffn_conv_b[l][None, :],
                          w_ffn_out, seq_len, mod_row0, tm)
        return x, ctx_out

    y_ctx, (k_new, v_new, s_new) = run_group(
        x_prompt.reshape(n_ctx * l_ctx, D_MODEL), n_ctx, l_ctx, 0, False)
    y_lat, _ = run_group(x_sample.reshape(n_lat * l_lat, D_MODEL), n_lat, l_lat, 1, True)

    n_even = (DEPTH + 1) // 2
    assert n_even == 1
    return (
        y_ctx.reshape(n_ctx, l_ctx, D_MODEL),
        y_lat.reshape(n_lat, l_lat, D_MODEL),
        k_new,
        v_new,
        s_new.reshape(n_ctx, n_even, 2, B_HEADS, B_HEAD, B_HEAD),
    )
```

```python
import functools
import math

import numpy as np
import jax
import jax.numpy as jnp
from jax import lax
from jax.experimental import pallas as pl
from jax.experimental.pallas import tpu as pltpu

F32 = jnp.float32
BF16 = jnp.bfloat16

D_MODEL = 1024
DEPTH = 2
GRID_W = 64
A_WIDTH = D_MODEL // 2
A_HEADS = 4
A_HEAD_DIM = A_WIDTH // A_HEADS
A_HALF = A_HEAD_DIM // 2
B_WIDTH = D_MODEL - A_WIDTH
B_HEAD = 64
B_HEADS = B_WIDTH // B_HEAD
LORA_W = 64
LORA_A = 64
LORA_G = 128
IN_B = 3 * B_WIDTH + 2 * LORA_W + 2 * LORA_A + LORA_G
IN_EVEN = 3 * A_WIDTH + IN_B
C_GROUPS = 8
C_GROUP_DIM = D_MODEL // C_GROUPS
D_FF = 2816
ROPE_BASE = 10000.0
RMS_EPS = 1e-6
LNX_EPS = 64e-5

LANES = 128
SUBLANES = 8
PAIR = 2 * B_HEAD
N_PAIRS = B_HEADS // 2
CHUNK = 64
FF_CHUNK = 256
FF_ROWS = 256
PREP_ROWS = 256
ATTN_Q_ROWS = 256
ODD_ROWS = 512
ODD_COLS = 256
VMEM_LIMIT = 48 * 1024 * 1024
MOD_ROWS = 8


def _cparams(*sem):
    return pltpu.CompilerParams(dimension_semantics=sem, vmem_limit_bytes=VMEM_LIMIT)


def _sigmoid(x):
    return 1.0 / (1.0 + jnp.exp(-x))


def _silu(x):
    return x * (0.5 + 0.5 * jnp.tanh(0.5 * x))


def _dot(a, b):
    return jnp.dot(a.astype(BF16), b.astype(BF16), preferred_element_type=F32)


_NN = (((1,), (0,)), ((), ()))
_NT = (((1,), (1,)), ((), ()))
_TN = (((0,), (0,)), ((), ()))


def _split_bf16(x, n):
    parts = []
    rem = x
    for i in range(n):
        p = rem.astype(BF16)
        parts.append(p)
        if i + 1 < n:
            rem = rem - p.astype(F32)
    return parts


def _mm(a, b, dims, passes):
    if passes == 1:
        return lax.dot_general(a.astype(BF16), b.astype(BF16), dims, preferred_element_type=F32)
    a_hi, a_lo = _split_bf16(a, 2)
    b_hi, b_lo = _split_bf16(b, 2)
    dg = functools.partial(lax.dot_general, dimension_numbers=dims, preferred_element_type=F32)
    return dg(a_hi, b_hi) + (dg(a_hi, b_lo) + dg(a_lo, b_hi))


def _segsum(x, bd, pieces=2):
    acc = None
    for p in _split_bf16(x, pieces):
        t = jnp.dot(p, bd, preferred_element_type=F32)
        acc = t if acc is None else acc + t
    return acc


def _rms(x):
    return x * lax.rsqrt(jnp.mean(x * x, axis=-1, keepdims=True) + RMS_EPS)


def _modnorm(x, gain, scale, shift):
    return _rms(x) * gain * (1.0 + scale) + shift


def _mod_kernel(c_ref, w_ref, b_ref, o_ref):
    c = c_ref[...]
    s = c * _sigmoid(c)
    o_ref[0] = _dot(s, w_ref[0]) + b_ref[0]


def _modulation(cvec, w_ada, b_ada):
    tn = 1536
    n = 6 * D_MODEL
    return pl.pallas_call(
        _mod_kernel,
        grid=(DEPTH, n // tn),
        in_specs=[
            pl.BlockSpec((MOD_ROWS, D_MODEL), lambda l, j: (0, 0)),
            pl.BlockSpec((1, D_MODEL, tn), lambda l, j: (l, 0, j)),
            pl.BlockSpec((1, 1, tn), lambda l, j: (l, 0, j)),
        ],
        out_specs=pl.BlockSpec((1, MOD_ROWS, tn), lambda l, j: (l, 0, j)),
        out_shape=jax.ShapeDtypeStruct((DEPTH, MOD_ROWS, n), F32),
        compiler_params=_cparams("parallel", "parallel"),
        name="modulation",
    )(cvec, w_ada, b_ada.reshape(DEPTH, 1, n))


def _mod_spec(tm, seq_len, mod_row0):
    if mod_row0 == 0:
        return pl.BlockSpec((1, 6, D_MODEL), lambda i, *_: (0, 0, 0))
    return pl.BlockSpec((1, 6, D_MODEL), lambda i, *_: (mod_row0 + (i * tm) // seq_len, 0, 0))


def _rope(x, cos_t, sin_t):
    lane = lax.broadcasted_iota(jnp.int32, x.shape, 1)
    nxt = pltpu.roll(x, LANES - 1, 1)
    prv = pltpu.roll(x, 1, 1)
    partner = jnp.where(lane % 2 == 0, nxt, prv)
    return x * cos_t + partner * sin_t


def _even_in_kernel(*refs, rope, cache):
    x_ref, mod_ref, gains_ref, w_ref = refs[:4]
    refs = refs[4:]
    if rope:
        cos_ref, sin_ref = refs[:2]
        refs = refs[2:]
    q_ref, k_ref, v_ref, fb_ref = refs[:4]
    cache_refs = refs[4:]
    mod = mod_ref[0]
    h = _modnorm(x_ref[...], gains_ref[0:1], mod[1:2], mod[0:1]).astype(BF16)
    a = A_WIDTH
    hd = A_HEAD_DIM
    for n, out_ref in enumerate((q_ref, k_ref, v_ref)):
        y = jnp.dot(h, w_ref[:, n * a:(n + 1) * a], preferred_element_type=F32)
        if cache and n > 0:
            for hh in range(A_HEADS):
                cache_refs[n - 1][0, 0, :, hh, :] = y[:, hh * hd:(hh + 1) * hd]
        if rope and n < 2:
            for hh in range(A_HEADS):
                cols = slice(hh * hd, (hh + 1) * hd)
                out_ref[:, cols] = _rope(y[:, cols], cos_ref[...], sin_ref[...]).astype(BF16)
        else:
            out_ref[...] = y.astype(BF16)
    fb_ref[...] = jnp.dot(h, w_ref[:, 3 * a:], preferred_element_type=F32)


def _even_in(x, mod, gains, w, seq_len, mod_row0, tm, rope=None, cache=False):
    t = x.shape[0]
    assert seq_len % tm == 0
    per_seq = seq_len // tm
    row = lambda i: (i, 0)
    operands = [x, mod, gains, w]
    in_specs = [
        pl.BlockSpec((tm, D_MODEL), row),
        _mod_spec(tm, seq_len, mod_row0),
        pl.BlockSpec((4, D_MODEL), lambda i: (0, 0)),
        pl.BlockSpec((D_MODEL, IN_EVEN), lambda i: (0, 0)),
    ]
    if rope is not None:
        tab = pl.BlockSpec((tm, A_HEAD_DIM), lambda i: (i % per_seq, 0))
        operands += list(rope)
        in_specs += [tab, tab]
    out_specs = [pl.BlockSpec((tm, A_WIDTH), row)] * 3 + [pl.BlockSpec((tm, IN_B), row)]
    out_shape = [jax.ShapeDtypeStruct((t, A_WIDTH), BF16)] * 3 + [jax.ShapeDtypeStruct((t, IN_B), F32)]
    if cache:
        out_specs += [pl.BlockSpec((1, 1, tm, A_HEADS, A_HEAD_DIM),
                                   lambda i: (i // per_seq, 0, i % per_seq, 0, 0))] * 2
        out_shape += [jax.ShapeDtypeStruct((t // seq_len, 1, seq_len, A_HEADS, A_HEAD_DIM), F32)] * 2
    return pl.pallas_call(
        functools.partial(_even_in_kernel, rope=rope is not None, cache=cache),
        grid=(t // tm,),
        in_specs=in_specs,
        out_specs=out_specs,
        out_shape=out_shape,
        compiler_params=_cparams("parallel"),
        name="even_in",
    )(*operands)


def _attn_kernel(*refs, has_ctx, lam_init):
    if has_ctx:
        lam_ref, sub_ref, q_ref, k_ref, v_ref, ck_ref, cv_ref, o_ref = refs
    else:
        lam_ref, sub_ref, q_ref, k_ref, v_ref, o_ref = refs
    lp = lam_ref[...]
    l1 = jnp.sum(lp[0:1] * lp[1:2], axis=-1, keepdims=True)
    l2 = jnp.sum(lp[2:3] * lp[3:4], axis=-1, keepdims=True)
    lam = jnp.exp(l1) - jnp.exp(l2) + lam_init
    hd = A_HEAD_DIM
    lane = lax.broadcasted_iota(jnp.int32, (q_ref.shape[0], hd), 1)
    scale = A_HALF ** -0.5

    def head(h):
        cols = pl.ds(h * hd, hd)
        q = q_ref[:, cols]
        k = k_ref[:, cols].astype(BF16)
        if has_ctx:
            ck = ck_ref[:, cols].astype(BF16)
        scores = []
        for m in range(2):
            qm = jnp.where((lane < A_HALF) == (m == 0), q, 0.0).astype(BF16)
            s = lax.dot_general(qm, k, _NT, preferred_element_type=F32) * scale
            sc = lax.dot_general(qm, ck, _NT, preferred_element_type=F32) * scale if has_ctx else None
            scores.append((s, sc))
        yield
        probs = []
        for s, sc in scores:
            mx = jnp.max(s, axis=-1, keepdims=True)
            if has_ctx:
                mx = jnp.maximum(mx, jnp.max(sc, axis=-1, keepdims=True))
                ec = jnp.exp(sc - mx)
            e = jnp.exp(s - mx)
            z = jnp.sum(e, axis=-1, keepdims=True)
            if has_ctx:
                z = z + jnp.sum(ec, axis=-1, keepdims=True)
            inv = 1.0 / z
            probs.append((e * inv, ec * inv if has_ctx else None))
        w = (probs[0][0] - lam * probs[1][0]).astype(BF16)
        if has_ctx:
            wc = (probs[0][1] - lam * probs[1][1]).astype(BF16)
        yield
        o = jnp.dot(w, v_ref[:, cols].astype(BF16), preferred_element_type=F32)
        if has_ctx:
            o = o + jnp.dot(wc, cv_ref[:, cols].astype(BF16), preferred_element_type=F32)
        yield
        o_ref[:, cols] = (_rms(o) * sub_ref[...] * (1.0 - lam_init)).astype(BF16)

    _round_robin(head(h) for h in range(A_HEADS))


def _attention(q, k, v, lam_p, subln, lam_init, n_seq, seq_len, ctx=None):
    t = q.shape[0]
    tq = min(seq_len, ATTN_Q_ROWS if ctx is None else ATTN_Q_ROWS // 2)
    nq = seq_len // tq
    small = [pl.BlockSpec((4, A_HALF), lambda b, i: (0, 0)),
             pl.BlockSpec((1, A_HEAD_DIM), lambda b, i: (0, 0))]
    qblk = pl.BlockSpec((tq, A_WIDTH), lambda b, i: (b * nq + i, 0))
    kblk = pl.BlockSpec((seq_len, A_WIDTH), lambda b, i: (b, 0))
    operands = [lam_p, subln, q, k, v]
    in_specs = small + [qblk, kblk, kblk]
    if ctx is not None:
        ck, cv = ctx
        cblk = pl.BlockSpec((ck.shape[0] // n_seq, A_WIDTH), lambda b, i: (b, 0))
        operands += [ck, cv]
        in_specs += [cblk, cblk]
    return pl.pallas_call(
        functools.partial(_attn_kernel, has_ctx=ctx is not None, lam_init=lam_init),
        grid=(n_seq, nq),
        in_specs=in_specs,
        out_specs=qblk,
        out_shape=jax.ShapeDtypeStruct((t, A_WIDTH), BF16),
        compiler_params=_cparams("parallel", "parallel"),
        name="diff_attn",
    )(*operands)


def _cumsum_rows(x, reverse):
    n = x.shape[0]
    ridx = lax.broadcasted_iota(jnp.int32, x.shape, 0)
    s = 1
    while s < n:
        if reverse:
            x = x + jnp.where(ridx < n - s, pltpu.roll(x, n - s, 0), 0.0)
        else:
            x = x + jnp.where(ridx >= s, pltpu.roll(x, s, 0), 0.0)
        s *= 2
    return x


def _half_rows(m, s, first):
    off = 0 if first else s
    return jnp.concatenate([m[b + off:b + off + s] for b in range(0, m.shape[0], 2 * s)], axis=0)


def _spread_rows(h, s, first):
    zero = jnp.zeros((s, h.shape[1]), h.dtype)
    pieces = []
    for i in range(h.shape[0] // s):
        blk = h[i * s:(i + 1) * s]
        pieces += [blk, zero] if first else [zero, blk]
    return jnp.concatenate(pieces, axis=0)


def _round_robin(gens):
    gens = list(gens)
    while gens:
        alive = []
        for gen in gens:
            try:
                next(gen)
                alive.append(gen)
            except StopIteration:
                pass
        gens = alive


def _rwkv_kernel(*refs, seq_len, has_state, pairs, passes, inv_passes, carry_passes, unroll):
    (fr_ref, fk_ref, fv_ref, fl_ref, mur_ref, muk_ref, muv_ref, mul_ref, w0_ref, w2_ref, a0_ref,
     a2_ref, g2_ref, kvec_ref, lnx_ref, bd_ref, bdm_ref) = refs[:17]
    refs = refs[17:]
    if has_state:
        s0_ref = refs[0]
        refs = refs[1:]
    yb_ref, sf_ref, pre_ref, g_scr, bonus_scr, r1_scr, m1_scr, d0_scr, gam_scr, y_scr, s_scr = refs

    def shifted(ref, mu_ref, r0):
        f = ref[r0:r0 + PREP_ROWS, :]
        zero = jnp.zeros((1, f.shape[1]), F32)
        before = zero if r0 == 0 else ref[r0 - 1:r0, :]
        after = zero if r0 + PREP_ROWS == seq_len else ref[r0 + PREP_ROWS:r0 + PREP_ROWS + 1, :]
        local = lax.broadcasted_iota(jnp.int32, (PREP_ROWS, 1), 0)
        prev = jnp.where(local == 0, before, pltpu.roll(f, 1, 0))
        nxt = jnp.where(local == PREP_ROWS - 1, after, pltpu.roll(f, PREP_ROWS - 1, 0))
        return f * (1.0 - mu_ref[0:1] - mu_ref[1:2]) + mu_ref[0:1] * prev + mu_ref[1:2] * nxt

    bd = bd_ref[...]
    kv = kvec_ref[...]
    for r0 in range(0, seq_len, PREP_ROWS):
        rows = slice(r0, r0 + PREP_ROWS)
        r = shifted(fr_ref, mur_ref, r0)
        k = shifted(fk_ref, muk_ref, r0)
        v = shifted(fv_ref, muv_ref, r0)
        lora = shifted(fl_ref, mul_ref, r0)
        wd = jnp.tanh(lora[:, 0:2 * LORA_W])
        ad = lora[:, 2 * LORA_W:2 * LORA_W + 2 * LORA_A]
        gd = lora[:, 2 * LORA_W + 2 * LORA_A:]
        g_scr[rows, :] = _dot(_sigmoid(gd), g2_ref[...])
        kk = k * kv[0:1]
        kk = kk * lax.rsqrt(_segsum(kk * kk, bd, pieces=1) + 1e-12)
        pre_ref[0, rows, :] = r
        pre_ref[1, rows, :] = v
        pre_ref[2, rows, :] = kk
        ksum = None
        for d in range(2):
            w_raw = w0_ref[d:d + 1] + _dot(wd, w2_ref[d])
            pre_ref[7 + d, rows, :] = -math.exp(-0.5) * _sigmoid(w_raw)
            a = _sigmoid(a0_ref[d:d + 1] + _dot(ad, a2_ref[d]))
            kd = k * (1.0 + (a - 1.0) * kv[1:2])
            pre_ref[3 + d, rows, :] = kd
            pre_ref[5 + d, rows, :] = kk * a
            ksum = kd if ksum is None else ksum + kd
        bonus_scr[rows, :] = _segsum(r * ksum * kv[2:3], bd) * v

    nc = seq_len // CHUNK
    c2 = 2 * CHUNK
    lane = lax.broadcasted_iota(jnp.int32, (CHUNK, PAIR), 1)
    head0 = lane < B_HEAD
    row = lax.broadcasted_iota(jnp.int32, (c2, c2), 0)
    col = lax.broadcasted_iota(jnp.int32, (c2, c2), 1)
    same = (row // CHUNK) == (col // CHUNK)
    eye = jnp.where(row == col, 1.0, 0.0)
    xor_rc = jnp.bitwise_xor(row, col)
    mm = functools.partial(_mm, passes=passes)
    mm_inv = functools.partial(_mm, passes=inv_passes)
    mm_carry = functools.partial(_mm, passes=carry_passes)

    def stack(x):
        return jnp.concatenate([jnp.where(head0, x, 0.0), jnp.where(head0, 0.0, x)], axis=0)

    def fold(x):
        return x[0:CHUNK] + x[CHUNK:c2]

    def chunk_rows(cc):
        return pl.ds(pl.multiple_of(cc * CHUNK, CHUNK), CHUNK)

    def chunk_local(cc, d, q):
        rows = chunk_rows(cc)
        lanes = pl.ds(q * PAIR, PAIR)
        z = 2 * q + d
        r = pre_ref[0, rows, lanes]
        v = pre_ref[1, rows, lanes]
        kk = pre_ref[2, rows, lanes]
        k = pre_ref[3 + d, rows, lanes]
        b = pre_ref[5 + d, rows, lanes]
        lw = pre_ref[7 + d, rows, lanes]
        incl = same & ((col <= row) if d == 0 else (col >= row))
        strict = same & ((col < row) if d == 0 else (col > row))
        g = _cumsum_rows(lw, reverse=(d == 1))
        gp = g - lw
        gtot = g[CHUNK - 1:CHUNK, :] if d == 0 else g[0:1, :]
        gm = 0.5 * gtot
        kx = k * jnp.exp(gm - g)
        bx = b * jnp.exp(gm - g)
        lhs = jnp.concatenate([stack(r * jnp.exp(g - gm)), stack(kk * jnp.exp(gp - gm))], axis=0)
        rhs = jnp.concatenate([kx, kx, bx, bx], axis=0)
        a_all = mm(lhs, rhs, _NT)
        yield
        a_rk = jnp.where(incl, a_all[0:c2, 0:c2], 0.0)
        a_rb = jnp.where(incl, a_all[0:c2, c2:2 * c2], 0.0)
        a_kk = jnp.where(strict, a_all[c2:2 * c2, 0:c2], 0.0)
        n_mat = jnp.where(strict, a_all[c2:2 * c2, c2:2 * c2], 0.0)
        vs = stack(v)
        akv = mm(a_kk, vs, _NN)
        y0 = mm(a_rk, vs, _NN)
        lower = (col < row) if d == 0 else (col > row)
        p_inv = eye - jnp.where(lower & (xor_rc == 1), n_mat, 0.0)
        yield
        s = 2
        while s < SUBLANES:
            c_s = jnp.where(lower & (xor_rc >= s) & (xor_rc < 2 * s), n_mat, 0.0)
            t = mm_inv(c_s, p_inv, _NN)
            yield
            p_inv = p_inv - mm_inv(p_inv, t, _NN)
            yield
            s *= 2
        while s < CHUNK:
            c_s = jnp.where(lower & (xor_rc >= s) & (xor_rc < 2 * s), n_mat, 0.0)
            t = _spread_rows(mm_inv(_half_rows(c_s, s, d == 1), p_inv, _NN), s, d == 1)
            yield
            upd = mm_inv(_half_rows(p_inv, s, d == 1), t, _NN)
            p_inv = p_inv - _spread_rows(upd, s, d == 1)
            yield
            s *= 2
        x = jnp.concatenate([stack(kk * jnp.exp(gp)), akv], axis=1)
        w12 = mm(p_inv, x, _NN)
        yield
        arb_w = mm(a_rb, w12, _NN)
        et = jnp.exp(gtot - g)
        bes = stack(b * et)
        d0_scr[z, cc] = mm(jnp.concatenate([vs, -w12[:, PAIR:]], axis=0),
                           jnp.concatenate([stack(k * et), bes], axis=0), _TN)
        m1_scr[z, cc] = mm(w12[:, 0:PAIR], bes, _TN)
        yield
        r1_scr[z, cc] = stack(r * jnp.exp(g)) - arb_w[:, 0:PAIR]
        y_scr[z, rows, :] = fold(y0 - arb_w[:, PAIR:])
        gam_scr[z, cc] = jnp.broadcast_to(jnp.exp(gtot), (SUBLANES, PAIR))

    def local_body(grp, carry):
        _round_robin(chunk_local(grp * unroll + u, d, q)
                     for u in range(unroll) for q in range(pairs) for d in range(2))
        return carry

    lax.fori_loop(0, nc // unroll, local_body, 0)

    for z in range(2 * pairs):
        s_scr[z] = jnp.zeros((PAIR, PAIR), F32)
        if has_state:
            s_scr[z, 0:B_HEAD, 0:B_HEAD] = s0_ref[0, z % 2, 2 * (z // 2)]
            s_scr[z, B_HEAD:PAIR, B_HEAD:PAIR] = s0_ref[0, z % 2, 2 * (z // 2) + 1]

    def carry_body(c, carry):
        for z in range(2 * pairs):
            cc = c if z % 2 == 0 else nc - 1 - c
            rows = chunk_rows(cc)
            s0 = s_scr[z]
            y_scr[z, rows, :] += fold(mm_carry(r1_scr[z, cc], s0, _NT))
            s_scr[z] = s0 * gam_scr[z, cc, 0:1, :] - mm_carry(s0, m1_scr[z, cc], _NN) + d0_scr[z, cc]
        return carry

    lax.fori_loop(0, nc, carry_body, 0)
    bdm = bdm_ref[...]
    for r0 in range(0, seq_len, PREP_ROWS):
        rows = slice(r0, r0 + PREP_ROWS)
        ys = jnp.concatenate([y_scr[2 * q, rows, :] + y_scr[2 * q + 1, rows, :] for q in range(pairs)],
                             axis=1)
        dlt = ys - _segsum(ys, bdm)
        yn = dlt * lax.rsqrt(_segsum(dlt * dlt, bdm, pieces=1) + LNX_EPS)
        yb = (yn * lnx_ref[0:1] + lnx_ref[1:2] + bonus_scr[rows, :]) * g_scr[rows, :]
        yb_ref[rows, :] = yb.astype(BF16)
    for z in range(2 * pairs):
        sf_ref[0, z % 2, 2 * (z // 2)] = s_scr[z, 0:B_HEAD, 0:B_HEAD]
        sf_ref[0, z % 2, 2 * (z // 2) + 1] = s_scr[z, B_HEAD:PAIR, B_HEAD:PAIR]


def _rwkv_mix(fb, s0, p, n_seq, seq_len, pairs=2, passes=1, inv_passes=1, carry_passes=1, unroll=4):
    t = fb.shape[0]
    nc = seq_len // CHUNK
    w = pairs * PAIR
    nw = B_WIDTH // w
    lora_w = 2 * LORA_W + 2 * LORA_A + LORA_G
    assert seq_len % PREP_ROWS == 0 and (3 * B_WIDTH) % lora_w == 0
    col = lambda rows, base: pl.BlockSpec((rows, w), lambda b, q: (0, base + q))
    seq = lambda base: pl.BlockSpec((seq_len, w), lambda b, q: (b, base + q))
    lora_blk = 3 * B_WIDTH // lora_w
    sblk = pl.BlockSpec((1, 2, 2 * pairs, B_HEAD, B_HEAD), lambda b, q: (b, 0, q, 0, 0))
    mat = pltpu.VMEM((2 * pairs, nc, PAIR, PAIR), F32)
    in_specs = [
        seq(0), seq(nw), seq(2 * nw),
        pl.BlockSpec((seq_len, lora_w), lambda b, q: (b, lora_blk)),
        col(2, 0), col(2, nw), col(2, 2 * nw),
        pl.BlockSpec((2, lora_w), lambda b, q: (0, lora_blk)),
        col(2, 0),
        pl.BlockSpec((2, 2 * LORA_W, w), lambda b, q: (0, 0, q)),
        col(2, 0),
        pl.BlockSpec((2, 2 * LORA_A, w), lambda b, q: (0, 0, q)),
        col(LORA_G, 0), col(3, 0), col(2, 0),
        pl.BlockSpec((w, w), lambda b, q: (0, 0)),
        pl.BlockSpec((w, w), lambda b, q: (0, 0)),
    ]
    operands = [fb, fb, fb, fb, p["mu"], p["mu"], p["mu"], p["mu"], p["w0"], p["w2"], p["a0"], p["a2"],
                p["g2"], p["kvec"], p["lnx"], _block_diag(1.0, w), _block_diag(1.0 / B_HEAD, w)]
    if s0 is not None:
        in_specs.append(sblk)
        operands.append(s0)
    return pl.pallas_call(
        functools.partial(_rwkv_kernel, seq_len=seq_len, has_state=s0 is not None, pairs=pairs,
                          passes=passes, inv_passes=inv_passes, carry_passes=carry_passes, unroll=unroll),
        grid=(n_seq, nw),
        in_specs=in_specs,
        out_specs=[pl.BlockSpec((seq_len, w), lambda b, q: (b, q)), sblk],
        out_shape=[
            jax.ShapeDtypeStruct((t, B_WIDTH), BF16),
            jax.ShapeDtypeStruct((n_seq, 2, B_HEADS, B_HEAD, B_HEAD), F32),
        ],
        scratch_shapes=[pltpu.VMEM((9, seq_len, w), F32), pltpu.VMEM((seq_len, w), F32),
                        pltpu.VMEM((seq_len, w), F32), mat, mat, mat,
                        pltpu.VMEM((2 * pairs, nc, SUBLANES, PAIR), F32),
                        pltpu.VMEM((2 * pairs, seq_len, PAIR), F32),
                        pltpu.VMEM((2 * pairs, PAIR, PAIR), F32)],
        compiler_params=_cparams("parallel", "parallel"),
        name="rwkv_mix",
    )(*operands)


def _even_out_kernel(x_ref, mod_ref, gains_ref, ya_ref, yb_ref, w_ref, o_ref):
    m = (jnp.dot(ya_ref[...], w_ref[0:A_WIDTH, :], preferred_element_type=F32)
         + jnp.dot(yb_ref[...], w_ref[A_WIDTH:, :], preferred_element_type=F32))
    mod = mod_ref[0]
    o_ref[...] = x_ref[...] + mod[2:3] * (_rms(m) * gains_ref[1:2])


def _even_out(x, mod, gains, ya, yb, w, seq_len, mod_row0, tm):
    t = x.shape[0]
    row = lambda i: (i, 0)
    return pl.pallas_call(
        _even_out_kernel,
        grid=(t // tm,),
        in_specs=[
            pl.BlockSpec((tm, D_MODEL), row),
            _mod_spec(tm, seq_len, mod_row0),
            pl.BlockSpec((4, D_MODEL), lambda i: (0, 0)),
            pl.BlockSpec((tm, A_WIDTH), row),
            pl.BlockSpec((tm, B_WIDTH), row),
            pl.BlockSpec((D_MODEL, D_MODEL), lambda i: (0, 0)),
        ],
        out_specs=pl.BlockSpec((tm, D_MODEL), row),
        out_shape=jax.ShapeDtypeStruct((t, D_MODEL), F32),
        compiler_params=_cparams("parallel"),
        name="even_out",
    )(x, mod, gains, ya, yb, w)


def _odd_kernel(x_ref, mod_ref, gains_ref, csc_ref, cl_ref, sl_ref, w_ref, o_ref,
                tc_scr, ts_scr, m_scr, *, seq_len):
    mod = mod_ref[0]
    gd = C_GROUP_DIM
    n_seq = x_ref.shape[0] // seq_len
    csc = csc_ref[...]

    for s in range(n_seq):
        rows = pl.ds(s * seq_len, seq_len)
        h = _modnorm(x_ref[rows, :], gains_ref[0:1], mod[1:2], mod[0:1]).astype(BF16)
        for g in range(C_GROUPS):
            t = jnp.dot(h[:, g * gd:(g + 1) * gd], csc, preferred_element_type=F32)
            tc_scr[rows, g * gd:(g + 1) * gd] = t[:, 0:gd].astype(BF16)
            ts_scr[rows, g * gd:(g + 1) * gd] = t[:, gd:].astype(BF16)

    def position_dft(s, cb):
        rows = pl.ds(s * seq_len, seq_len)
        cols = pl.ds(cb * ODD_COLS, ODD_COLS)
        f = (jnp.dot(cl_ref[...], tc_scr[rows, cols], preferred_element_type=F32)
             - jnp.dot(sl_ref[...], ts_scr[rows, cols], preferred_element_type=F32))
        yield
        part = jnp.dot(f.astype(BF16), w_ref[cols, :], preferred_element_type=F32)
        yield
        if cb == 0:
            m_scr[rows, :] = part
        else:
            m_scr[rows, :] += part

    _round_robin(position_dft(s, cb) for s in range(n_seq) for cb in range(D_MODEL // ODD_COLS))
    o_ref[...] = x_ref[...] + mod[2:3] * (_rms(m_scr[...]) * gains_ref[1:2])


def _dft_tables(n):
    idx = np.arange(n, dtype=np.int64)
    ang = 2.0 * np.pi * ((idx[:, None] * idx[None, :]) % n).astype(np.float64) / n
    s = 1.0 / math.sqrt(n)
    return np.cos(ang) * s, np.sin(ang) * s


def _odd_mixer(x, mod, gains, w, n_seq, seq_len, mod_row0):
    cc, sc = _dft_tables(C_GROUP_DIM)
    cl, sl = _dft_tables(seq_len)
    csc, cl, sl = (jnp.asarray(a, F32).astype(BF16) for a in (np.concatenate([cc, sc], axis=1), cl, sl))
    full = lambda shape: pl.BlockSpec(shape, lambda i: (0,) * len(shape))
    row = lambda i: (i, 0)
    tm = max(seq_len, ODD_ROWS)
    assert tm % seq_len == 0 and (n_seq * seq_len) % tm == 0
    return pl.pallas_call(
        functools.partial(_odd_kernel, seq_len=seq_len),
        grid=(n_seq * seq_len // tm,),
        in_specs=[
            pl.BlockSpec((tm, D_MODEL), row),
            _mod_spec(tm, seq_len, mod_row0),
            full((4, D_MODEL)), full((C_GROUP_DIM, 2 * C_GROUP_DIM)),
            full((seq_len, seq_len)), full((seq_len, seq_len)), full((D_MODEL, D_MODEL)),
        ],
        out_specs=pl.BlockSpec((tm, D_MODEL), row),
        out_shape=jax.ShapeDtypeStruct(x.shape, F32),
        scratch_shapes=[pltpu.VMEM((tm, D_MODEL), BF16), pltpu.VMEM((tm, D_MODEL), BF16),
                        pltpu.VMEM((tm, D_MODEL), F32)],
        compiler_params=_cparams("parallel"),
        name="odd_mixer",
    )(x, mod, gains, csc, cl, sl, w)


def _ffn_kernel(x_ref, mod_ref, gains_ref, wu_ref, wg_ref, cw_ref, cb_ref, wo_ref, o_ref,
                h_scr, acc_scr, *, seq_len):
    j = pl.program_id(1)
    mod = mod_ref[0]

    @pl.when(j == 0)
    def _():
        h_scr[...] = _modnorm(x_ref[...], gains_ref[2:3], mod[4:5], mod[3:4]).astype(BF16)
        acc_scr[...] = jnp.zeros_like(acc_scr)

    n_sub = h_scr.shape[0] // FF_ROWS
    gate = [None] * n_sub
    zero_row = jnp.zeros((1, FF_CHUNK), F32)
    wu = wu_ref[...].astype(BF16)
    wg = wg_ref[...].astype(BF16)
    wo = wo_ref[...].astype(BF16)

    def sub_tile(i):
        rows = pl.ds(i * FF_ROWS, FF_ROWS)
        h = h_scr[rows, :]
        u = jnp.dot(h, wu, preferred_element_type=F32)
        g = gate[i] = jnp.dot(h, wg, preferred_element_type=F32)
        yield
        before = zero_row if (i * FF_ROWS) % seq_len == 0 else gate[i - 1][FF_ROWS - 1:FF_ROWS, :]
        after = zero_row if ((i + 1) * FF_ROWS) % seq_len == 0 else gate[i + 1][0:1, :]
        pos = lax.broadcasted_iota(jnp.int32, (FF_ROWS, 1), 0)
        prev = jnp.where(pos == 0, before, pltpu.roll(g, 1, 0))
        nxt = jnp.where(pos == FF_ROWS - 1, after, pltpu.roll(g, FF_ROWS - 1, 0))
        gc = prev * cw_ref[0:1] + g * cw_ref[1:2] + nxt * cw_ref[2:3] + cb_ref[...]
        act = (_silu(gc) * u).astype(BF16)
        yield
        acc_scr[rows, :] += jnp.dot(act, wo, preferred_element_type=F32)

    _round_robin(sub_tile(i) for i in range(n_sub))

    @pl.when(j == pl.num_programs(1) - 1)
    def _():
        o_ref[...] = x_ref[...] + mod[5:6] * (_rms(acc_scr[...]) * gains_ref[3:4])


def _conv_ffn(x, mod, gains, layer, w_in, conv_w, conv_b, w_out, seq_len, mod_row0, tm):
    t = x.shape[0]
    nj = D_FF // FF_CHUNK
    assert seq_len % FF_ROWS == 0 and tm % seq_len == 0
    return pl.pallas_call(
        functools.partial(_ffn_kernel, seq_len=seq_len),
        grid=(t // tm, nj),
        in_specs=[
            pl.BlockSpec((tm, D_MODEL), lambda i, j: (i, 0)),
            _mod_spec(tm, seq_len, mod_row0),
            pl.BlockSpec((4, D_MODEL), lambda i, j: (0, 0)),
            pl.BlockSpec((None, D_MODEL, FF_CHUNK), lambda i, j: (layer, 0, j)),
            pl.BlockSpec((None, D_MODEL, FF_CHUNK), lambda i, j: (layer, 0, nj + j)),
            pl.BlockSpec((3, FF_CHUNK), lambda i, j: (0, j)),
            pl.BlockSpec((1, FF_CHUNK), lambda i, j: (0, j)),
            pl.BlockSpec((None, FF_CHUNK, D_MODEL), lambda i, j: (layer, j, 0)),
        ],
        out_specs=pl.BlockSpec((tm, D_MODEL), lambda i, j: (i, 0)),
        out_shape=jax.ShapeDtypeStruct((t, D_MODEL), F32),
        scratch_shapes=[pltpu.VMEM((tm, D_MODEL), BF16), pltpu.VMEM((tm, D_MODEL), F32)],
        compiler_params=_cparams("parallel", "arbitrary"),
        name="conv_ffn",
    )(x, mod, gains, w_in, w_in, conv_w, conv_b, w_out)


def _rope_tables(seq_len):
    rows = seq_len // GRID_W
    row = jnp.repeat(jnp.arange(rows, dtype=F32), GRID_W)
    col = jnp.tile(jnp.arange(GRID_W, dtype=F32), rows)
    n_freq = A_HALF // 4
    inv = ROPE_BASE ** (-jnp.arange(n_freq, dtype=F32) / n_freq)
    ang = jnp.concatenate([row[:, None] * inv, col[:, None] * inv], axis=-1)
    cos_t = jnp.tile(jnp.repeat(jnp.cos(ang), 2, axis=1), (1, 2))
    sin_t = jnp.tile(jnp.repeat(jnp.sin(ang), 2, axis=1), (1, 2))
    sign = jnp.where(jnp.arange(A_HEAD_DIM) % 2 == 0, -1.0, 1.0).astype(F32)
    return cos_t, sin_t * sign


def _pad_lora(w):
    z = jnp.zeros_like(w[0])
    return jnp.stack([jnp.concatenate([w[0], z], axis=0), jnp.concatenate([z, w[1]], axis=0)])


def _block_diag(value, width):
    head = np.arange(width) // B_HEAD
    return jnp.asarray(np.where(head[:, None] == head[None, :], value, 0.0), BF16)


def _tile_rows(seq_len):
    return max(seq_len, 1024)


def kernel(x_prompt, x_sample, cache_k, cache_v, state_wkv, c, c_ctx, w_ada, b_ada, norm_gains,
           w_in_even, w_out_even, diff_lambda, diff_subln, rwkv_shift_mu, rwkv_w0, rwkv_w2, rwkv_a0,
           rwkv_a2, rwkv_g2, rwkv_kvec, rwkv_lnx, w_out_odd, w_ffn_in, ffn_conv, ffn_conv_b,
           w_ffn_out):
    n_ctx, l_ctx, _ = x_prompt.shape
    n_lat, l_lat, _ = x_sample.shape
    assert 1 + n_lat <= MOD_ROWS
    cvec = jnp.concatenate(
        [c_ctx[None, :], c, jnp.zeros((MOD_ROWS - 1 - n_lat, D_MODEL), F32)], axis=0)
    mods = _modulation(cvec, w_ada, b_ada).reshape(DEPTH, MOD_ROWS, 6, D_MODEL)

    per_layer_bf16 = lambda w: [w[i].astype(BF16) for i in range(w.shape[0])]
    w_in_even_b = per_layer_bf16(w_in_even)
    w_out_even_b = per_layer_bf16(w_out_even)
    w_out_odd_b = per_layer_bf16(w_out_odd)
    cos_t, sin_t = _rope_tables(l_lat)

    def run_group(x, n_seq, seq_len, mod_row0, latent):
        tm = _tile_rows(seq_len)
        ctx_out = None
        for l in range(DEPTH):
            mod = mods[l]
            gains = norm_gains[l]
            if l % 2 == 0:
                e = l // 2
                lam_init = 0.8 - 0.6 * math.exp(-0.3 * l)
                q, k, v, fb, *kv_cache = _even_in(
                    x, mod, gains, w_in_even_b[e], seq_len, mod_row0, 256,
                    rope=(cos_t, sin_t) if latent else None, cache=not latent)
                if latent:
                    past = cache_k.shape[2]
                    ctx = (cache_k[:, e].reshape(n_seq * past, A_WIDTH),
                           cache_v[:, e].reshape(n_seq * past, A_WIDTH))
                    s0 = state_wkv[:, e]
                else:
                    ctx = None
                    s0 = None
                ya = _attention(q, k, v, diff_lambda[e], diff_subln[e][None, :], lam_init,
                                n_seq, seq_len, ctx)
                rwkv = {
                    "mu": rwkv_shift_mu[e], "w0": rwkv_w0[e], "w2": _pad_lora(rwkv_w2[e]).astype(BF16),
                    "a0": rwkv_a0[e], "a2": _pad_lora(rwkv_a2[e]).astype(BF16),
                    "g2": rwkv_g2[e].astype(BF16), "kvec": rwkv_kvec[e], "lnx": rwkv_lnx[e],
                }
                yb, s_fin = _rwkv_mix(fb, s0, rwkv, n_seq, seq_len)
                x = _even_out(x, mod, gains, ya, yb, w_out_even_b[e], seq_len, mod_row0, 256)
                if not latent:
                    ctx_out = (*kv_cache, s_fin)
            else:
                x = _odd_mixer(x, mod, gains, w_out_odd_b[l // 2], n_seq, seq_len, mod_row0)
            x = _conv_ffn(x, mod, gains, l, w_ffn_in, ffn_conv[l], ffn_conv_b[l][None, :],
                          w_ffn_out, seq_len, mod_row0, tm)
        return x, ctx_out

    y_ctx, (k_new, v_new, s_new) = run_group(
        x_prompt.reshape(n_ctx * l_ctx, D_MODEL), n_ctx, l_ctx, 0, False)
    y_lat, _ = run_group(x_sample.reshape(n_lat * l_lat, D_MODEL), n_lat, l_lat, 1, True)

    n_even = (DEPTH + 1) // 2
    assert n_even == 1
    return (
        y_ctx.reshape(n_ctx, l_ctx, D_MODEL),
        y_lat.reshape(n_lat, l_lat, D_MODEL),
        k_new,
        v_new,
        s_new.reshape(n_ctx, n_even, 2, B_HEADS, B_HEAD, B_HEAD),
    )
```

```python
import functools
import math

import numpy as np
import jax
import jax.numpy as jnp
from jax import lax
from jax.experimental import pallas as pl
from jax.experimental.pallas import tpu as pltpu

F32 = jnp.float32
BF16 = jnp.bfloat16

D_MODEL = 1024
DEPTH = 2
GRID_W = 64
A_WIDTH = D_MODEL // 2
A_HEADS = 4
A_HEAD_DIM = A_WIDTH // A_HEADS
A_HALF = A_HEAD_DIM // 2
B_WIDTH = D_MODEL - A_WIDTH
B_HEAD = 64
B_HEADS = B_WIDTH // B_HEAD
LORA_W = 64
LORA_A = 64
LORA_G = 128
IN_B = 3 * B_WIDTH + 2 * LORA_W + 2 * LORA_A + LORA_G
IN_EVEN = 3 * A_WIDTH + IN_B
C_GROUPS = 8
C_GROUP_DIM = D_MODEL // C_GROUPS
D_FF = 2816
ROPE_BASE = 10000.0
RMS_EPS = 1e-6
LNX_EPS = 64e-5

LANES = 128
SUBLANES = 8
PAIR = 2 * B_HEAD
N_PAIRS = B_HEADS // 2
CHUNK = 64
FF_CHUNK = 256
FF_ROWS = 512
PREP_ROWS = 256
ATTN_Q_ROWS = 256
ODD_ROWS = 512
ODD_COLS = 256
VMEM_LIMIT = 48 * 1024 * 1024
MOD_ROWS = 8


def _cparams(*sem):
    return pltpu.CompilerParams(dimension_semantics=sem, vmem_limit_bytes=VMEM_LIMIT)


def _sigmoid(x):
    return 1.0 / (1.0 + jnp.exp(-x))


def _silu(x):
    return x * (0.5 + 0.5 * jnp.tanh(0.5 * x))


def _dot(a, b):
    return jnp.dot(a.astype(BF16), b.astype(BF16), preferred_element_type=F32)


_NN = (((1,), (0,)), ((), ()))
_NT = (((1,), (1,)), ((), ()))
_TN = (((0,), (0,)), ((), ()))


def _split_bf16(x, n):
    parts = []
    rem = x
    for i in range(n):
        p = rem.astype(BF16)
        parts.append(p)
        if i + 1 < n:
            rem = rem - p.astype(F32)
    return parts


def _mm(a, b, dims, passes):
    if passes == 1:
        return lax.dot_general(a.astype(BF16), b.astype(BF16), dims, preferred_element_type=F32)
    a_hi, a_lo = _split_bf16(a, 2)
    b_hi, b_lo = _split_bf16(b, 2)
    dg = functools.partial(lax.dot_general, dimension_numbers=dims, preferred_element_type=F32)
    return dg(a_hi, b_hi) + (dg(a_hi, b_lo) + dg(a_lo, b_hi))


def _segsum(x, bd, pieces=2):
    acc = None
    for p in _split_bf16(x, pieces):
        t = jnp.dot(p, bd, preferred_element_type=F32)
        acc = t if acc is None else acc + t
    return acc


def _rms(x):
    return x * lax.rsqrt(jnp.mean(x * x, axis=-1, keepdims=True) + RMS_EPS)


def _modnorm(x, gain, scale, shift):
    return _rms(x) * gain * (1.0 + scale) + shift


def _mod_kernel(c_ref, w_ref, b_ref, o_ref):
    c = c_ref[...]
    s = c * _sigmoid(c)
    o_ref[0] = _dot(s, w_ref[0]) + b_ref[0]


def _modulation(cvec, w_ada, b_ada):
    tn = 1536
    n = 6 * D_MODEL
    return pl.pallas_call(
        _mod_kernel,
        grid=(DEPTH, n // tn),
        in_specs=[
            pl.BlockSpec((MOD_ROWS, D_MODEL), lambda l, j: (0, 0)),
            pl.BlockSpec((1, D_MODEL, tn), lambda l, j: (l, 0, j)),
            pl.BlockSpec((1, 1, tn), lambda l, j: (l, 0, j)),
        ],
        out_specs=pl.BlockSpec((1, MOD_ROWS, tn), lambda l, j: (l, 0, j)),
        out_shape=jax.ShapeDtypeStruct((DEPTH, MOD_ROWS, n), F32),
        compiler_params=_cparams("parallel", "parallel"),
        name="modulation",
    )(cvec, w_ada, b_ada.reshape(DEPTH, 1, n))


def _mod_spec(tm, seq_len, mod_row0):
    if mod_row0 == 0:
        return pl.BlockSpec((1, 6, D_MODEL), lambda i, *_: (0, 0, 0))
    return pl.BlockSpec((1, 6, D_MODEL), lambda i, *_: (mod_row0 + (i * tm) // seq_len, 0, 0))


def _rope(x, cos_t, sin_t):
    lane = lax.broadcasted_iota(jnp.int32, x.shape, 1)
    nxt = pltpu.roll(x, LANES - 1, 1)
    prv = pltpu.roll(x, 1, 1)
    partner = jnp.where(lane % 2 == 0, nxt, prv)
    return x * cos_t + partner * sin_t


def _even_in_kernel(*refs, rope, cache):
    x_ref, mod_ref, gains_ref, w_ref = refs[:4]
    refs = refs[4:]
    if rope:
        cos_ref, sin_ref = refs[:2]
        refs = refs[2:]
    q_ref, k_ref, v_ref, fb_ref = refs[:4]
    cache_refs = refs[4:]
    mod = mod_ref[0]
    h = _modnorm(x_ref[...], gains_ref[0:1], mod[1:2], mod[0:1]).astype(BF16)
    a = A_WIDTH
    hd = A_HEAD_DIM
    for n, out_ref in enumerate((q_ref, k_ref, v_ref)):
        y = jnp.dot(h, w_ref[:, n * a:(n + 1) * a], preferred_element_type=F32)
        if cache and n > 0:
            for hh in range(A_HEADS):
                cache_refs[n - 1][0, 0, :, hh, :] = y[:, hh * hd:(hh + 1) * hd]
        if rope and n < 2:
            for hh in range(A_HEADS):
                cols = slice(hh * hd, (hh + 1) * hd)
                out_ref[:, cols] = _rope(y[:, cols], cos_ref[...], sin_ref[...]).astype(BF16)
        else:
            out_ref[...] = y.astype(BF16)
    fb_ref[...] = jnp.dot(h, w_ref[:, 3 * a:], preferred_element_type=F32)


def _even_in(x, mod, gains, w, seq_len, mod_row0, tm, rope=None, cache=False):
    t = x.shape[0]
    assert seq_len % tm == 0
    per_seq = seq_len // tm
    row = lambda i: (i, 0)
    operands = [x, mod, gains, w]
    in_specs = [
        pl.BlockSpec((tm, D_MODEL), row),
        _mod_spec(tm, seq_len, mod_row0),
        pl.BlockSpec((4, D_MODEL), lambda i: (0, 0)),
        pl.BlockSpec((D_MODEL, IN_EVEN), lambda i: (0, 0)),
    ]
    if rope is not None:
        tab = pl.BlockSpec((tm, A_HEAD_DIM), lambda i: (i % per_seq, 0))
        operands += list(rope)
        in_specs += [tab, tab]
    out_specs = [pl.BlockSpec((tm, A_WIDTH), row)] * 3 + [pl.BlockSpec((tm, IN_B), row)]
    out_shape = [jax.ShapeDtypeStruct((t, A_WIDTH), BF16)] * 3 + [jax.ShapeDtypeStruct((t, IN_B), F32)]
    if cache:
        out_specs += [pl.BlockSpec((1, 1, tm, A_HEADS, A_HEAD_DIM),
                                   lambda i: (i // per_seq, 0, i % per_seq, 0, 0))] * 2
        out_shape += [jax.ShapeDtypeStruct((t // seq_len, 1, seq_len, A_HEADS, A_HEAD_DIM), F32)] * 2
    return pl.pallas_call(
        functools.partial(_even_in_kernel, rope=rope is not None, cache=cache),
        grid=(t // tm,),
        in_specs=in_specs,
        out_specs=out_specs,
        out_shape=out_shape,
        compiler_params=_cparams("parallel"),
        name="even_in",
    )(*operands)


def _attn_kernel(*refs, has_ctx, lam_init):
    if has_ctx:
        lam_ref, sub_ref, q_ref, k_ref, v_ref, ck_ref, cv_ref, o_ref = refs
    else:
        lam_ref, sub_ref, q_ref, k_ref, v_ref, o_ref = refs
    lp = lam_ref[...]
    l1 = jnp.sum(lp[0:1] * lp[1:2], axis=-1, keepdims=True)
    l2 = jnp.sum(lp[2:3] * lp[3:4], axis=-1, keepdims=True)
    lam = jnp.exp(l1) - jnp.exp(l2) + lam_init
    hd = A_HEAD_DIM
    lane = lax.broadcasted_iota(jnp.int32, (q_ref.shape[0], hd), 1)
    scale = A_HALF ** -0.5

    def head(h):
        cols = pl.ds(h * hd, hd)
        q = q_ref[:, cols]
        k = k_ref[:, cols].astype(BF16)
        if has_ctx:
            ck = ck_ref[:, cols].astype(BF16)
        scores = []
        for m in range(2):
            qm = jnp.where((lane < A_HALF) == (m == 0), q, 0.0).astype(BF16)
            s = lax.dot_general(qm, k, _NT, preferred_element_type=F32) * scale
            sc = lax.dot_general(qm, ck, _NT, preferred_element_type=F32) * scale if has_ctx else None
            scores.append((s, sc))
        yield
        probs = []
        for s, sc in scores:
            mx = jnp.max(s, axis=-1, keepdims=True)
            if has_ctx:
                mx = jnp.maximum(mx, jnp.max(sc, axis=-1, keepdims=True))
                ec = jnp.exp(sc - mx)
            e = jnp.exp(s - mx)
            z = jnp.sum(e, axis=-1, keepdims=True)
            if has_ctx:
                z = z + jnp.sum(ec, axis=-1, keepdims=True)
            inv = 1.0 / z
            probs.append((e * inv, ec * inv if has_ctx else None))
        w = (probs[0][0] - lam * probs[1][0]).astype(BF16)
        if has_ctx:
            wc = (probs[0][1] - lam * probs[1][1]).astype(BF16)
        yield
        o = jnp.dot(w, v_ref[:, cols].astype(BF16), preferred_element_type=F32)
        if has_ctx:
            o = o + jnp.dot(wc, cv_ref[:, cols].astype(BF16), preferred_element_type=F32)
        yield
        o_ref[:, cols] = (_rms(o) * sub_ref[...] * (1.0 - lam_init)).astype(BF16)

    _round_robin(head(h) for h in range(A_HEADS))


def _attention(q, k, v, lam_p, subln, lam_init, n_seq, seq_len, ctx=None):
    t = q.shape[0]
    tq = min(seq_len, ATTN_Q_ROWS if ctx is None else ATTN_Q_ROWS // 2)
    nq = seq_len // tq
    small = [pl.BlockSpec((4, A_HALF), lambda b, i: (0, 0)),
             pl.BlockSpec((1, A_HEAD_DIM), lambda b, i: (0, 0))]
    qblk = pl.BlockSpec((tq, A_WIDTH), lambda b, i: (b * nq + i, 0))
    kblk = pl.BlockSpec((seq_len, A_WIDTH), lambda b, i: (b, 0))
    operands = [lam_p, subln, q, k, v]
    in_specs = small + [qblk, kblk, kblk]
    if ctx is not None:
        ck, cv = ctx
        cblk = pl.BlockSpec((ck.shape[0] // n_seq, A_WIDTH), lambda b, i: (b, 0))
        operands += [ck, cv]
        in_specs += [cblk, cblk]
    return pl.pallas_call(
        functools.partial(_attn_kernel, has_ctx=ctx is not None, lam_init=lam_init),
        grid=(n_seq, nq),
        in_specs=in_specs,
        out_specs=qblk,
        out_shape=jax.ShapeDtypeStruct((t, A_WIDTH), BF16),
        compiler_params=_cparams("parallel", "parallel"),
        name="diff_attn",
    )(*operands)


def _cumsum_rows(x, reverse):
    n = x.shape[0]
    ridx = lax.broadcasted_iota(jnp.int32, x.shape, 0)
    s = 1
    while s < n:
        if reverse:
            x = x + jnp.where(ridx < n - s, pltpu.roll(x, n - s, 0), 0.0)
        else:
            x = x + jnp.where(ridx >= s, pltpu.roll(x, s, 0), 0.0)
        s *= 2
    return x


def _round_robin(gens):
    gens = list(gens)
    while gens:
        alive = []
        for gen in gens:
            try:
                next(gen)
                alive.append(gen)
            except StopIteration:
                pass
        gens = alive


def _rwkv_kernel(*refs, seq_len, has_state, pairs, passes, inv_passes, carry_passes, unroll):
    (fr_ref, fk_ref, fv_ref, fl_ref, mur_ref, muk_ref, muv_ref, mul_ref, w0_ref, w2_ref, a0_ref,
     a2_ref, g2_ref, kvec_ref, lnx_ref, bd_ref, bdm_ref) = refs[:17]
    refs = refs[17:]
    if has_state:
        s0_ref = refs[0]
        refs = refs[1:]
    yb_ref, sf_ref, pre_ref, g_scr, bonus_scr, r1_scr, m1_scr, d0_scr, gam_scr, y_scr, s_scr = refs

    def shifted(ref, mu_ref, r0):
        f = ref[r0:r0 + PREP_ROWS, :]
        zero = jnp.zeros((1, f.shape[1]), F32)
        before = zero if r0 == 0 else ref[r0 - 1:r0, :]
        after = zero if r0 + PREP_ROWS == seq_len else ref[r0 + PREP_ROWS:r0 + PREP_ROWS + 1, :]
        local = lax.broadcasted_iota(jnp.int32, (PREP_ROWS, 1), 0)
        prev = jnp.where(local == 0, before, pltpu.roll(f, 1, 0))
        nxt = jnp.where(local == PREP_ROWS - 1, after, pltpu.roll(f, PREP_ROWS - 1, 0))
        return f * (1.0 - mu_ref[0:1] - mu_ref[1:2]) + mu_ref[0:1] * prev + mu_ref[1:2] * nxt

    bd = bd_ref[...]
    kv = kvec_ref[...]
    for r0 in range(0, seq_len, PREP_ROWS):
        rows = slice(r0, r0 + PREP_ROWS)
        r = shifted(fr_ref, mur_ref, r0)
        k = shifted(fk_ref, muk_ref, r0)
        v = shifted(fv_ref, muv_ref, r0)
        lora = shifted(fl_ref, mul_ref, r0)
        wd = jnp.tanh(lora[:, 0:2 * LORA_W])
        ad = lora[:, 2 * LORA_W:2 * LORA_W + 2 * LORA_A]
        gd = lora[:, 2 * LORA_W + 2 * LORA_A:]
        g_scr[rows, :] = _dot(_sigmoid(gd), g2_ref[...])
        kk = k * kv[0:1]
        kk = kk * lax.rsqrt(_segsum(kk * kk, bd, pieces=1) + 1e-12)
        pre_ref[0, rows, :] = r
        pre_ref[1, rows, :] = v
        pre_ref[2, rows, :] = kk
        ksum = None
        for d in range(2):
            w_raw = w0_ref[d:d + 1] + _dot(wd, w2_ref[d])
            pre_ref[7 + d, rows, :] = -math.exp(-0.5) * _sigmoid(w_raw)
            a = _sigmoid(a0_ref[d:d + 1] + _dot(ad, a2_ref[d]))
            kd = k * (1.0 + (a - 1.0) * kv[1:2])
            pre_ref[3 + d, rows, :] = kd
            pre_ref[5 + d, rows, :] = kk * a
            ksum = kd if ksum is None else ksum + kd
        bonus_scr[rows, :] = _segsum(r * ksum * kv[2:3], bd) * v

    nc = seq_len // CHUNK
    c2 = 2 * CHUNK
    lane = lax.broadcasted_iota(jnp.int32, (CHUNK, PAIR), 1)
    head0 = lane < B_HEAD
    row = lax.broadcasted_iota(jnp.int32, (c2, c2), 0)
    col = lax.broadcasted_iota(jnp.int32, (c2, c2), 1)
    same = (row // CHUNK) == (col // CHUNK)
    eye = jnp.where(row == col, 1.0, 0.0)
    xor_rc = jnp.bitwise_xor(row, col)
    mm = functools.partial(_mm, passes=passes)
    mm_inv = functools.partial(_mm, passes=inv_passes)
    mm_carry = functools.partial(_mm, passes=carry_passes)

    def stack(x):
        return jnp.concatenate([jnp.where(head0, x, 0.0), jnp.where(head0, 0.0, x)], axis=0)

    def fold(x):
        return x[0:CHUNK] + x[CHUNK:c2]

    def chunk_rows(cc):
        return pl.ds(pl.multiple_of(cc * CHUNK, CHUNK), CHUNK)

    def chunk_local(cc, d, q):
        rows = chunk_rows(cc)
        lanes = pl.ds(q * PAIR, PAIR)
        z = 2 * q + d
        r = pre_ref[0, rows, lanes]
        v = pre_ref[1, rows, lanes]
        kk = pre_ref[2, rows, lanes]
        k = pre_ref[3 + d, rows, lanes]
        b = pre_ref[5 + d, rows, lanes]
        lw = pre_ref[7 + d, rows, lanes]
        incl = same & ((col <= row) if d == 0 else (col >= row))
        strict = same & ((col < row) if d == 0 else (col > row))
        g = _cumsum_rows(lw, reverse=(d == 1))
        gp = g - lw
        gtot = g[CHUNK - 1:CHUNK, :] if d == 0 else g[0:1, :]
        gm = 0.5 * gtot
        kx = k * jnp.exp(gm - g)
        bx = b * jnp.exp(gm - g)
        lhs = jnp.concatenate([stack(r * jnp.exp(g - gm)), stack(kk * jnp.exp(gp - gm))], axis=0)
        rhs = jnp.concatenate([kx, kx, bx, bx], axis=0)
        a_all = mm(lhs, rhs, _NT)
        yield
        a_rk = jnp.where(incl, a_all[0:c2, 0:c2], 0.0)
        a_rb = jnp.where(incl, a_all[0:c2, c2:2 * c2], 0.0)
        a_kk = jnp.where(strict, a_all[c2:2 * c2, 0:c2], 0.0)
        n_mat = jnp.where(strict, a_all[c2:2 * c2, c2:2 * c2], 0.0)
        vs = stack(v)
        akv = mm(a_kk, vs, _NN)
        y0 = mm(a_rk, vs, _NN)
        lower = (col < row) if d == 0 else (col > row)
        p_inv = eye - jnp.where(lower & (xor_rc == 1), n_mat, 0.0)
        yield
        s = 2
        while s < CHUNK:
            c_s = jnp.where(lower & (xor_rc >= s) & (xor_rc < 2 * s), n_mat, 0.0)
            t = mm_inv(c_s, p_inv, _NN)
            yield
            p_inv = p_inv - mm_inv(p_inv, t, _NN)
            yield
            s *= 2
        x = jnp.concatenate([stack(kk * jnp.exp(gp)), akv], axis=1)
        w12 = mm(p_inv, x, _NN)
        yield
        arb_w = mm(a_rb, w12, _NN)
        et = jnp.exp(gtot - g)
        bes = stack(b * et)
        d0_scr[z, cc] = mm(jnp.concatenate([vs, -w12[:, PAIR:]], axis=0),
                           jnp.concatenate([stack(k * et), bes], axis=0), _TN)
        m1_scr[z, cc] = mm(w12[:, 0:PAIR], bes, _TN)
        yield
        r1_scr[z, cc] = stack(r * jnp.exp(g)) - arb_w[:, 0:PAIR]
        y_scr[z, rows, :] = fold(y0 - arb_w[:, PAIR:])
        gam_scr[z, cc] = jnp.broadcast_to(jnp.exp(gtot), (SUBLANES, PAIR))

    def local_body(grp, carry):
        _round_robin(chunk_local(grp * unroll + u, d, q)
                     for u in range(unroll) for q in range(pairs) for d in range(2))
        return carry

    lax.fori_loop(0, nc // unroll, local_body, 0)

    for z in range(2 * pairs):
        s_scr[z] = jnp.zeros((PAIR, PAIR), F32)
        if has_state:
            s_scr[z, 0:B_HEAD, 0:B_HEAD] = s0_ref[0, z % 2, 2 * (z // 2)]
            s_scr[z, B_HEAD:PAIR, B_HEAD:PAIR] = s0_ref[0, z % 2, 2 * (z // 2) + 1]

    def carry_body(c, carry):
        for z in range(2 * pairs):
            cc = c if z % 2 == 0 else nc - 1 - c
            rows = chunk_rows(cc)
            s0 = s_scr[z]
            y_scr[z, rows, :] += fold(mm_carry(r1_scr[z, cc], s0, _NT))
            s_scr[z] = s0 * gam_scr[z, cc, 0:1, :] - mm_carry(s0, m1_scr[z, cc], _NN) + d0_scr[z, cc]
        return carry

    lax.fori_loop(0, nc, carry_body, 0)
    bdm = bdm_ref[...]
    for r0 in range(0, seq_len, PREP_ROWS):
        rows = slice(r0, r0 + PREP_ROWS)
        ys = jnp.concatenate([y_scr[2 * q, rows, :] + y_scr[2 * q + 1, rows, :] for q in range(pairs)],
                             axis=1)
        dlt = ys - _segsum(ys, bdm)
        yn = dlt * lax.rsqrt(_segsum(dlt * dlt, bdm, pieces=1) + LNX_EPS)
        yb = (yn * lnx_ref[0:1] + lnx_ref[1:2] + bonus_scr[rows, :]) * g_scr[rows, :]
        yb_ref[rows, :] = yb.astype(BF16)
    for z in range(2 * pairs):
        sf_ref[0, z % 2, 2 * (z // 2)] = s_scr[z, 0:B_HEAD, 0:B_HEAD]
        sf_ref[0, z % 2, 2 * (z // 2) + 1] = s_scr[z, B_HEAD:PAIR, B_HEAD:PAIR]


def _rwkv_mix(fb, s0, p, n_seq, seq_len, pairs=2, passes=1, inv_passes=1, carry_passes=1, unroll=4):
    t = fb.shape[0]
    nc = seq_len // CHUNK
    w = pairs * PAIR
    nw = B_WIDTH // w
    lora_w = 2 * LORA_W + 2 * LORA_A + LORA_G
    assert seq_len % PREP_ROWS == 0 and (3 * B_WIDTH) % lora_w == 0
    col = lambda rows, base: pl.BlockSpec((rows, w), lambda b, q: (0, base + q))
    seq = lambda base: pl.BlockSpec((seq_len, w), lambda b, q: (b, base + q))
    lora_blk = 3 * B_WIDTH // lora_w
    sblk = pl.BlockSpec((1, 2, 2 * pairs, B_HEAD, B_HEAD), lambda b, q: (b, 0, q, 0, 0))
    mat = pltpu.VMEM((2 * pairs, nc, PAIR, PAIR), F32)
    in_specs = [
        seq(0), seq(nw), seq(2 * nw),
        pl.BlockSpec((seq_len, lora_w), lambda b, q: (b, lora_blk)),
        col(2, 0), col(2, nw), col(2, 2 * nw),
        pl.BlockSpec((2, lora_w), lambda b, q: (0, lora_blk)),
        col(2, 0),
        pl.BlockSpec((2, 2 * LORA_W, w), lambda b, q: (0, 0, q)),
        col(2, 0),
        pl.BlockSpec((2, 2 * LORA_A, w), lambda b, q: (0, 0, q)),
        col(LORA_G, 0), col(3, 0), col(2, 0),
        pl.BlockSpec((w, w), lambda b, q: (0, 0)),
        pl.BlockSpec((w, w), lambda b, q: (0, 0)),
    ]
    operands = [fb, fb, fb, fb, p["mu"], p["mu"], p["mu"], p["mu"], p["w0"], p["w2"], p["a0"], p["a2"],
                p["g2"], p["kvec"], p["lnx"], _block_diag(1.0, w), _block_diag(1.0 / B_HEAD, w)]
    if s0 is not None:
        in_specs.append(sblk)
        operands.append(s0)
    return pl.pallas_call(
        functools.partial(_rwkv_kernel, seq_len=seq_len, has_state=s0 is not None, pairs=pairs,
                          passes=passes, inv_passes=inv_passes, carry_passes=carry_passes, unroll=unroll),
        grid=(n_seq, nw),
        in_specs=in_specs,
        out_specs=[pl.BlockSpec((seq_len, w), lambda b, q: (b, q)), sblk],
        out_shape=[
            jax.ShapeDtypeStruct((t, B_WIDTH), BF16),
            jax.ShapeDtypeStruct((n_seq, 2, B_HEADS, B_HEAD, B_HEAD), F32),
        ],
        scratch_shapes=[pltpu.VMEM((9, seq_len, w), F32), pltpu.VMEM((seq_len, w), F32),
                        pltpu.VMEM((seq_len, w), F32), mat, mat, mat,
                        pltpu.VMEM((2 * pairs, nc, SUBLANES, PAIR), F32),
                        pltpu.VMEM((2 * pairs, seq_len, PAIR), F32),
                        pltpu.VMEM((2 * pairs, PAIR, PAIR), F32)],
        compiler_params=_cparams("parallel", "parallel"),
        name="rwkv_mix",
    )(*operands)


def _even_out_kernel(x_ref, mod_ref, gains_ref, ya_ref, yb_ref, w_ref, o_ref):
    m = (jnp.dot(ya_ref[...], w_ref[0:A_WIDTH, :], preferred_element_type=F32)
         + jnp.dot(yb_ref[...], w_ref[A_WIDTH:, :], preferred_element_type=F32))
    mod = mod_ref[0]
    o_ref[...] = x_ref[...] + mod[2:3] * (_rms(m) * gains_ref[1:2])


def _even_out(x, mod, gains, ya, yb, w, seq_len, mod_row0, tm):
    t = x.shape[0]
    row = lambda i: (i, 0)
    return pl.pallas_call(
        _even_out_kernel,
        grid=(t // tm,),
        in_specs=[
            pl.BlockSpec((tm, D_MODEL), row),
            _mod_spec(tm, seq_len, mod_row0),
            pl.BlockSpec((4, D_MODEL), lambda i: (0, 0)),
            pl.BlockSpec((tm, A_WIDTH), row),
            pl.BlockSpec((tm, B_WIDTH), row),
            pl.BlockSpec((D_MODEL, D_MODEL), lambda i: (0, 0)),
        ],
        out_specs=pl.BlockSpec((tm, D_MODEL), row),
        out_shape=jax.ShapeDtypeStruct((t, D_MODEL), F32),
        compiler_params=_cparams("parallel"),
        name="even_out",
    )(x, mod, gains, ya, yb, w)


def _odd_kernel(x_ref, mod_ref, gains_ref, csc_ref, cl_ref, sl_ref, w_ref, o_ref,
                tc_scr, ts_scr, m_scr, *, seq_len):
    mod = mod_ref[0]
    gd = C_GROUP_DIM
    n_seq = x_ref.shape[0] // seq_len
    csc = csc_ref[...]

    for s in range(n_seq):
        rows = pl.ds(s * seq_len, seq_len)
        h = _modnorm(x_ref[rows, :], gains_ref[0:1], mod[1:2], mod[0:1]).astype(BF16)
        for g in range(C_GROUPS):
            t = jnp.dot(h[:, g * gd:(g + 1) * gd], csc, preferred_element_type=F32)
            tc_scr[rows, g * gd:(g + 1) * gd] = t[:, 0:gd].astype(BF16)
            ts_scr[rows, g * gd:(g + 1) * gd] = t[:, gd:].astype(BF16)

    def position_dft(s, cb):
        rows = pl.ds(s * seq_len, seq_len)
        cols = pl.ds(cb * ODD_COLS, ODD_COLS)
        f = (jnp.dot(cl_ref[...], tc_scr[rows, cols], preferred_element_type=F32)
             - jnp.dot(sl_ref[...], ts_scr[rows, cols], preferred_element_type=F32))
        yield
        part = jnp.dot(f.astype(BF16), w_ref[cols, :], preferred_element_type=F32)
        yield
        if cb == 0:
            m_scr[rows, :] = part
        else:
            m_scr[rows, :] += part

    _round_robin(position_dft(s, cb) for s in range(n_seq) for cb in range(D_MODEL // ODD_COLS))
    o_ref[...] = x_ref[...] + mod[2:3] * (_rms(m_scr[...]) * gains_ref[1:2])


def _dft_tables(n):
    idx = np.arange(n, dtype=np.int64)
    ang = 2.0 * np.pi * ((idx[:, None] * idx[None, :]) % n).astype(np.float64) / n
    s = 1.0 / math.sqrt(n)
    return np.cos(ang) * s, np.sin(ang) * s


def _odd_mixer(x, mod, gains, w, n_seq, seq_len, mod_row0):
    cc, sc = _dft_tables(C_GROUP_DIM)
    cl, sl = _dft_tables(seq_len)
    csc, cl, sl = (jnp.asarray(a, F32).astype(BF16) for a in (np.concatenate([cc, sc], axis=1), cl, sl))
    full = lambda shape: pl.BlockSpec(shape, lambda i: (0,) * len(shape))
    row = lambda i: (i, 0)
    tm = max(seq_len, ODD_ROWS)
    assert tm % seq_len == 0 and (n_seq * seq_len) % tm == 0
    return pl.pallas_call(
        functools.partial(_odd_kernel, seq_len=seq_len),
        grid=(n_seq * seq_len // tm,),
        in_specs=[
            pl.BlockSpec((tm, D_MODEL), row),
            _mod_spec(tm, seq_len, mod_row0),
            full((4, D_MODEL)), full((C_GROUP_DIM, 2 * C_GROUP_DIM)),
            full((seq_len, seq_len)), full((seq_len, seq_len)), full((D_MODEL, D_MODEL)),
        ],
        out_specs=pl.BlockSpec((tm, D_MODEL), row),
        out_shape=jax.ShapeDtypeStruct(x.shape, F32),
        scratch_shapes=[pltpu.VMEM((tm, D_MODEL), BF16), pltpu.VMEM((tm, D_MODEL), BF16),
                        pltpu.VMEM((tm, D_MODEL), F32)],
        compiler_params=_cparams("parallel"),
        name="odd_mixer",
    )(x, mod, gains, csc, cl, sl, w)


def _ffn_kernel(x_ref, mod_ref, gains_ref, wu_ref, wg_ref, cw_ref, cb_ref, wo_ref, o_ref,
                h_scr, acc_scr, *, seq_len):
    j = pl.program_id(1)
    mod = mod_ref[0]

    @pl.when(j == 0)
    def _():
        h_scr[...] = _modnorm(x_ref[...], gains_ref[2:3], mod[4:5], mod[3:4]).astype(BF16)
        acc_scr[...] = jnp.zeros_like(acc_scr)

    n_sub = h_scr.shape[0] // FF_ROWS
    gate = [None] * n_sub
    zero_row = jnp.zeros((1, FF_CHUNK), F32)
    wu = wu_ref[...].astype(BF16)
    wg = wg_ref[...].astype(BF16)
    wo = wo_ref[...].astype(BF16)

    def sub_tile(i):
        rows = pl.ds(i * FF_ROWS, FF_ROWS)
        h = h_scr[rows, :]
        u = jnp.dot(h, wu, preferred_element_type=F32)
        g = gate[i] = jnp.dot(h, wg, preferred_element_type=F32)
        yield
        before = zero_row if (i * FF_ROWS) % seq_len == 0 else gate[i - 1][FF_ROWS - 1:FF_ROWS, :]
        after = zero_row if ((i + 1) * FF_ROWS) % seq_len == 0 else gate[i + 1][0:1, :]
        pos = lax.broadcasted_iota(jnp.int32, (FF_ROWS, 1), 0)
        prev = jnp.where(pos == 0, before, pltpu.roll(g, 1, 0))
        nxt = jnp.where(pos == FF_ROWS - 1, after, pltpu.roll(g, FF_ROWS - 1, 0))
        if seq_len < FF_ROWS:
            prev = jnp.where(pos % seq_len == 0, 0.0, prev)
            nxt = jnp.where(pos % seq_len == seq_len - 1, 0.0, nxt)
        gc = prev * cw_ref[0:1] + g * cw_ref[1:2] + nxt * cw_ref[2:3] + cb_ref[...]
        act = (_silu(gc) * u).astype(BF16)
        yield
        acc_scr[rows, :] += jnp.dot(act, wo, preferred_element_type=F32)

    _round_robin(sub_tile(i) for i in range(n_sub))

    @pl.when(j == pl.num_programs(1) - 1)
    def _():
        o_ref[...] = x_ref[...] + mod[5:6] * (_rms(acc_scr[...]) * gains_ref[3:4])


def _conv_ffn(x, mod, gains, layer, w_in, conv_w, conv_b, w_out, seq_len, mod_row0, tm):
    t = x.shape[0]
    nj = D_FF // FF_CHUNK
    assert (seq_len % FF_ROWS == 0 or FF_ROWS % seq_len == 0) and tm % seq_len == 0
    return pl.pallas_call(
        functools.partial(_ffn_kernel, seq_len=seq_len),
        grid=(t // tm, nj),
        in_specs=[
            pl.BlockSpec((tm, D_MODEL), lambda i, j: (i, 0)),
            _mod_spec(tm, seq_len, mod_row0),
            pl.BlockSpec((4, D_MODEL), lambda i, j: (0, 0)),
            pl.BlockSpec((None, D_MODEL, FF_CHUNK), lambda i, j: (layer, 0, j)),
            pl.BlockSpec((None, D_MODEL, FF_CHUNK), lambda i, j: (layer, 0, nj + j)),
            pl.BlockSpec((3, FF_CHUNK), lambda i, j: (0, j)),
            pl.BlockSpec((1, FF_CHUNK), lambda i, j: (0, j)),
            pl.BlockSpec((None, FF_CHUNK, D_MODEL), lambda i, j: (layer, j, 0)),
        ],
        out_specs=pl.BlockSpec((tm, D_MODEL), lambda i, j: (i, 0)),
        out_shape=jax.ShapeDtypeStruct((t, D_MODEL), F32),
        scratch_shapes=[pltpu.VMEM((tm, D_MODEL), BF16), pltpu.VMEM((tm, D_MODEL), F32)],
        compiler_params=_cparams("parallel", "arbitrary"),
        name="conv_ffn",
    )(x, mod, gains, w_in, w_in, conv_w, conv_b, w_out)


def _rope_tables(seq_len):
    rows = seq_len // GRID_W
    row = jnp.repeat(jnp.arange(rows, dtype=F32), GRID_W)
    col = jnp.tile(jnp.arange(GRID_W, dtype=F32), rows)
    n_freq = A_HALF // 4
    inv = ROPE_BASE ** (-jnp.arange(n_freq, dtype=F32) / n_freq)
    ang = jnp.concatenate([row[:, None] * inv, col[:, None] * inv], axis=-1)
    cos_t = jnp.tile(jnp.repeat(jnp.cos(ang), 2, axis=1), (1, 2))
    sin_t = jnp.tile(jnp.repeat(jnp.sin(ang), 2, axis=1), (1, 2))
    sign = jnp.where(jnp.arange(A_HEAD_DIM) % 2 == 0, -1.0, 1.0).astype(F32)
    return cos_t, sin_t * sign


def _pad_lora(w):
    z = jnp.zeros_like(w[0])
    return jnp.stack([jnp.concatenate([w[0], z], axis=0), jnp.concatenate([z, w[1]], axis=0)])


def _block_diag(value, width):
    head = np.arange(width) // B_HEAD
    return jnp.asarray(np.where(head[:, None] == head[None, :], value, 0.0), BF16)


def _tile_rows(seq_len):
    return max(seq_len, 1024)


def kernel(x_prompt, x_sample, cache_k, cache_v, state_wkv, c, c_ctx, w_ada, b_ada, norm_gains,
           w_in_even, w_out_even, diff_lambda, diff_subln, rwkv_shift_mu, rwkv_w0, rwkv_w2, rwkv_a0,
           rwkv_a2, rwkv_g2, rwkv_kvec, rwkv_lnx, w_out_odd, w_ffn_in, ffn_conv, ffn_conv_b,
           w_ffn_out):
    n_ctx, l_ctx, _ = x_prompt.shape
    n_lat, l_lat, _ = x_sample.shape
    assert 1 + n_lat <= MOD_ROWS
    cvec = jnp.concatenate(
        [c_ctx[None, :], c, jnp.zeros((MOD_ROWS - 1 - n_lat, D_MODEL), F32)], axis=0)
    mods = _modulation(cvec, w_ada, b_ada).reshape(DEPTH, MOD_ROWS, 6, D_MODEL)

    per_layer_bf16 = lambda w: [w[i].astype(BF16) for i in range(w.shape[0])]
    w_in_even_b = per_layer_bf16(w_in_even)
    w_out_even_b = per_layer_bf16(w_out_even)
    w_out_odd_b = per_layer_bf16(w_out_odd)
    cos_t, sin_t = _rope_tables(l_lat)

    def run_group(x, n_seq, seq_len, mod_row0, latent):
        tm = _tile_rows(seq_len)
        ctx_out = None
        for l in range(DEPTH):
            mod = mods[l]
            gains = norm_gains[l]
            if l % 2 == 0:
                e = l // 2
                lam_init = 0.8 - 0.6 * math.exp(-0.3 * l)
                q, k, v, fb, *kv_cache = _even_in(
                    x, mod, gains, w_in_even_b[e], seq_len, mod_row0, 256,
                    rope=(cos_t, sin_t) if latent else None, cache=not latent)
                if latent:
                    past = cache_k.shape[2]
                    ctx = (cache_k[:, e].reshape(n_seq * past, A_WIDTH),
                           cache_v[:, e].reshape(n_seq * past, A_WIDTH))
                    s0 = state_wkv[:, e]
                else:
                    ctx = None
                    s0 = None
                ya = _attention(q, k, v, diff_lambda[e], diff_subln[e][None, :], lam_init,
                                n_seq, seq_len, ctx)
                rwkv = {
                    "mu": rwkv_shift_mu[e], "w0": rwkv_w0[e], "w2": _pad_lora(rwkv_w2[e]).astype(BF16),
                    "a0": rwkv_a0[e], "a2": _pad_lora(rwkv_a2[e]).astype(BF16),
                    "g2": rwkv_g2[e].astype(BF16), "kvec": rwkv_kvec[e], "lnx": rwkv_lnx[e],
                }
                yb, s_fin = _rwkv_mix(fb, s0, rwkv, n_seq, seq_len)
                x = _even_out(x, mod, gains, ya, yb, w_out_even_b[e], seq_len, mod_row0, 256)
                if not latent:
                    ctx_out = (*kv_cache, s_fin)
            else:
                x = _odd_mixer(x, mod, gains, w_out_odd_b[l // 2], n_seq, seq_len, mod_row0)
            x = _conv_ffn(x, mod, gains, l, w_ffn_in, ffn_conv[l], ffn_conv_b[l][None, :],
                          w_ffn_out, seq_len, mod_row0, tm)
        return x, ctx_out

    y_ctx, (k_new, v_new, s_new) = run_group(
        x_prompt.reshape(n_ctx * l_ctx, D_MODEL), n_ctx, l_ctx, 0, False)
    y_lat, _ = run_group(x_sample.reshape(n_lat * l_lat, D_MODEL), n_lat, l_lat, 1, True)

    n_even = (DEPTH + 1) // 2
    assert n_even == 1
    return (
        y_ctx.reshape(n_ctx, l_ctx, D_MODEL),
        y_lat.reshape(n_lat, l_lat, D_MODEL),
        k_new,
        v_new,
        s_new.reshape(n_ctx, n_even, 2, B_HEADS, B_HEAD, B_HEAD),
    )
```

```python
import functools
import math

import numpy as np
import jax
import jax.numpy as jnp
from jax import lax
from jax.experimental import pallas as pl
from jax.experimental.pallas import tpu as pltpu

F32 = jnp.float32
BF16 = jnp.bfloat16

D_MODEL = 1024
DEPTH = 2
GRID_W = 64
A_WIDTH = D_MODEL // 2
A_HEADS = 4
A_HEAD_DIM = A_WIDTH // A_HEADS
A_HALF = A_HEAD_DIM // 2
B_WIDTH = D_MODEL - A_WIDTH
B_HEAD = 64
B_HEADS = B_WIDTH // B_HEAD
LORA_W = 64
LORA_A = 64
LORA_G = 128
IN_B = 3 * B_WIDTH + 2 * LORA_W + 2 * LORA_A + LORA_G
IN_EVEN = 3 * A_WIDTH + IN_B
C_GROUPS = 8
C_GROUP_DIM = D_MODEL // C_GROUPS
D_FF = 2816
ROPE_BASE = 10000.0
RMS_EPS = 1e-6
LNX_EPS = 64e-5

LANES = 128
SUBLANES = 8
PAIR = 2 * B_HEAD
N_PAIRS = B_HEADS // 2
CHUNK = 64
FF_CHUNK = 256
FF_ROWS = 256
PREP_ROWS = 256
ATTN_Q_ROWS = 256
ODD_ROWS = 1024
ODD_COLS = 256
VMEM_LIMIT = 48 * 1024 * 1024
MOD_ROWS = 8


def _cparams(*sem):
    return pltpu.CompilerParams(dimension_semantics=sem, vmem_limit_bytes=VMEM_LIMIT)


def _sigmoid(x):
    return 1.0 / (1.0 + jnp.exp(-x))


def _silu(x):
    return x * (0.5 + 0.5 * jnp.tanh(0.5 * x))


def _dot(a, b):
    return jnp.dot(a.astype(BF16), b.astype(BF16), preferred_element_type=F32)


_NN = (((1,), (0,)), ((), ()))
_NT = (((1,), (1,)), ((), ()))
_TN = (((0,), (0,)), ((), ()))


def _split_bf16(x, n):
    parts = []
    rem = x
    for i in range(n):
        p = rem.astype(BF16)
        parts.append(p)
        if i + 1 < n:
            rem = rem - p.astype(F32)
    return parts


def _mm(a, b, dims, passes):
    if passes == 1:
        return lax.dot_general(a.astype(BF16), b.astype(BF16), dims, preferred_element_type=F32)
    a_hi, a_lo = _split_bf16(a, 2)
    b_hi, b_lo = _split_bf16(b, 2)
    dg = functools.partial(lax.dot_general, dimension_numbers=dims, preferred_element_type=F32)
    return dg(a_hi, b_hi) + (dg(a_hi, b_lo) + dg(a_lo, b_hi))


def _segsum(x, bd, pieces=2):
    acc = None
    for p in _split_bf16(x, pieces):
        t = jnp.dot(p, bd, preferred_element_type=F32)
        acc = t if acc is None else acc + t
    return acc


def _rms(x):
    return x * lax.rsqrt(jnp.mean(x * x, axis=-1, keepdims=True) + RMS_EPS)


def _modnorm(x, gain, scale, shift):
    return _rms(x) * gain * (1.0 + scale) + shift


def _mod_kernel(c_ref, w_ref, b_ref, o_ref):
    c = c_ref[...]
    s = c * _sigmoid(c)
    o_ref[0] = _dot(s, w_ref[0]) + b_ref[0]


def _modulation(cvec, w_ada, b_ada):
    tn = 1536
    n = 6 * D_MODEL
    return pl.pallas_call(
        _mod_kernel,
        grid=(DEPTH, n // tn),
        in_specs=[
            pl.BlockSpec((MOD_ROWS, D_MODEL), lambda l, j: (0, 0)),
            pl.BlockSpec((1, D_MODEL, tn), lambda l, j: (l, 0, j)),
            pl.BlockSpec((1, 1, tn), lambda l, j: (l, 0, j)),
        ],
        out_specs=pl.BlockSpec((1, MOD_ROWS, tn), lambda l, j: (l, 0, j)),
        out_shape=jax.ShapeDtypeStruct((DEPTH, MOD_ROWS, n), F32),
        compiler_params=_cparams("parallel", "parallel"),
        name="modulation",
    )(cvec, w_ada, b_ada.reshape(DEPTH, 1, n))


def _mod_spec(tm, seq_len, mod_row0):
    if mod_row0 == 0:
        return pl.BlockSpec((1, 6, D_MODEL), lambda i, *_: (0, 0, 0))
    return pl.BlockSpec((1, 6, D_MODEL), lambda i, *_: (mod_row0 + (i * tm) // seq_len, 0, 0))


def _rope(x, cos_t, sin_t):
    lane = lax.broadcasted_iota(jnp.int32, x.shape, 1)
    nxt = pltpu.roll(x, LANES - 1, 1)
    prv = pltpu.roll(x, 1, 1)
    partner = jnp.where(lane % 2 == 0, nxt, prv)
    return x * cos_t + partner * sin_t


def _even_in_kernel(*refs, rope, cache):
    x_ref, mod_ref, gains_ref, w_ref = refs[:4]
    refs = refs[4:]
    if rope:
        cos_ref, sin_ref = refs[:2]
        refs = refs[2:]
    q_ref, k_ref, v_ref, fb_ref = refs[:4]
    cache_refs = refs[4:]
    mod = mod_ref[0]
    h = _modnorm(x_ref[...], gains_ref[0:1], mod[1:2], mod[0:1]).astype(BF16)
    a = A_WIDTH
    hd = A_HEAD_DIM
    for n, out_ref in enumerate((q_ref, k_ref, v_ref)):
        y = jnp.dot(h, w_ref[:, n * a:(n + 1) * a], preferred_element_type=F32)
        if cache and n > 0:
            for hh in range(A_HEADS):
                cache_refs[n - 1][0, 0, :, hh, :] = y[:, hh * hd:(hh + 1) * hd]
        if rope and n < 2:
            for hh in range(A_HEADS):
                cols = slice(hh * hd, (hh + 1) * hd)
                out_ref[:, cols] = _rope(y[:, cols], cos_ref[...], sin_ref[...]).astype(BF16)
        else:
            out_ref[...] = y.astype(BF16)
    fb_ref[...] = jnp.dot(h, w_ref[:, 3 * a:], preferred_element_type=F32)


def _even_in(x, mod, gains, w, seq_len, mod_row0, tm, rope=None, cache=False):
    t = x.shape[0]
    assert seq_len % tm == 0
    per_seq = seq_len // tm
    row = lambda i: (i, 0)
    operands = [x, mod, gains, w]
    in_specs = [
        pl.BlockSpec((tm, D_MODEL), row),
        _mod_spec(tm, seq_len, mod_row0),
        pl.BlockSpec((4, D_MODEL), lambda i: (0, 0)),
        pl.BlockSpec((D_MODEL, IN_EVEN), lambda i: (0, 0)),
    ]
    if rope is not None:
        tab = pl.BlockSpec((tm, A_HEAD_DIM), lambda i: (i % per_seq, 0))
        operands += list(rope)
        in_specs += [tab, tab]
    out_specs = [pl.BlockSpec((tm, A_WIDTH), row)] * 3 + [pl.BlockSpec((tm, IN_B), row)]
    out_shape = [jax.ShapeDtypeStruct((t, A_WIDTH), BF16)] * 3 + [jax.ShapeDtypeStruct((t, IN_B), F32)]
    if cache:
        out_specs += [pl.BlockSpec((1, 1, tm, A_HEADS, A_HEAD_DIM),
                                   lambda i: (i // per_seq, 0, i % per_seq, 0, 0))] * 2
        out_shape += [jax.ShapeDtypeStruct((t // seq_len, 1, seq_len, A_HEADS, A_HEAD_DIM), F32)] * 2
    return pl.pallas_call(
        functools.partial(_even_in_kernel, rope=rope is not None, cache=cache),
        grid=(t // tm,),
        in_specs=in_specs,
        out_specs=out_specs,
        out_shape=out_shape,
        compiler_params=_cparams("parallel"),
        name="even_in",
    )(*operands)


def _attn_kernel(*refs, has_ctx, lam_init):
    if has_ctx:
        lam_ref, sub_ref, q_ref, k_ref, v_ref, ck_ref, cv_ref, o_ref = refs
    else:
        lam_ref, sub_ref, q_ref, k_ref, v_ref, o_ref = refs
    lp = lam_ref[...]
    l1 = jnp.sum(lp[0:1] * lp[1:2], axis=-1, keepdims=True)
    l2 = jnp.sum(lp[2:3] * lp[3:4], axis=-1, keepdims=True)
    lam = jnp.exp(l1) - jnp.exp(l2) + lam_init
    hd = A_HEAD_DIM
    lane = lax.broadcasted_iota(jnp.int32, (q_ref.shape[0], hd), 1)
    scale = A_HALF ** -0.5

    def head(h):
        cols = pl.ds(h * hd, hd)
        q = q_ref[:, cols]
        k = k_ref[:, cols].astype(BF16)
        if has_ctx:
            ck = ck_ref[:, cols].astype(BF16)
        scores = []
        for m in range(2):
            qm = jnp.where((lane < A_HALF) == (m == 0), q, 0.0).astype(BF16)
            s = lax.dot_general(qm, k, _NT, preferred_element_type=F32) * scale
            sc = lax.dot_general(qm, ck, _NT, preferred_element_type=F32) * scale if has_ctx else None
            scores.append((s, sc))
        yield
        probs = []
        for s, sc in scores:
            mx = jnp.max(s, axis=-1, keepdims=True)
            if has_ctx:
                mx = jnp.maximum(mx, jnp.max(sc, axis=-1, keepdims=True))
                ec = jnp.exp(sc - mx)
            e = jnp.exp(s - mx)
            z = jnp.sum(e, axis=-1, keepdims=True)
            if has_ctx:
                z = z + jnp.sum(ec, axis=-1, keepdims=True)
            inv = 1.0 / z
            probs.append((e * inv, ec * inv if has_ctx else None))
        w = (probs[0][0] - lam * probs[1][0]).astype(BF16)
        if has_ctx:
            wc = (probs[0][1] - lam * probs[1][1]).astype(BF16)
        yield
        o = jnp.dot(w, v_ref[:, cols].astype(BF16), preferred_element_type=F32)
        if has_ctx:
            o = o + jnp.dot(wc, cv_ref[:, cols].astype(BF16), preferred_element_type=F32)
        yield
        o_ref[:, cols] = (_rms(o) * sub_ref[...] * (1.0 - lam_init)).astype(BF16)

    _round_robin(head(h) for h in range(A_HEADS))


def _attention(q, k, v, lam_p, subln, lam_init, n_seq, seq_len, ctx=None):
    t = q.shape[0]
    tq = min(seq_len, ATTN_Q_ROWS if ctx is None else ATTN_Q_ROWS // 2)
    nq = seq_len // tq
    small = [pl.BlockSpec((4, A_HALF), lambda b, i: (0, 0)),
             pl.BlockSpec((1, A_HEAD_DIM), lambda b, i: (0, 0))]
    qblk = pl.BlockSpec((tq, A_WIDTH), lambda b, i: (b * nq + i, 0))
    kblk = pl.BlockSpec((seq_len, A_WIDTH), lambda b, i: (b, 0))
    operands = [lam_p, subln, q, k, v]
    in_specs = small + [qblk, kblk, kblk]
    if ctx is not None:
        ck, cv = ctx
        cblk = pl.BlockSpec((ck.shape[0] // n_seq, A_WIDTH), lambda b, i: (b, 0))
        operands += [ck, cv]
        in_specs += [cblk, cblk]
    return pl.pallas_call(
        functools.partial(_attn_kernel, has_ctx=ctx is not None, lam_init=lam_init),
        grid=(n_seq, nq),
        in_specs=in_specs,
        out_specs=qblk,
        out_shape=jax.ShapeDtypeStruct((t, A_WIDTH), BF16),
        compiler_params=_cparams("parallel", "parallel"),
        name="diff_attn",
    )(*operands)


def _cumsum_rows(x, reverse):
    n = x.shape[0]
    ridx = lax.broadcasted_iota(jnp.int32, x.shape, 0)
    s = 1
    while s < n:
        if reverse:
            x = x + jnp.where(ridx < n - s, pltpu.roll(x, n - s, 0), 0.0)
        else:
            x = x + jnp.where(ridx >= s, pltpu.roll(x, s, 0), 0.0)
        s *= 2
    return x


def _round_robin(gens):
    gens = list(gens)
    while gens:
        alive = []
        for gen in gens:
            try:
                next(gen)
                alive.append(gen)
            except StopIteration:
                pass
        gens = alive


def _rwkv_kernel(*refs, seq_len, has_state, pairs, passes, inv_passes, carry_passes, unroll):
    (fr_ref, fk_ref, fv_ref, fl_ref, mur_ref, muk_ref, muv_ref, mul_ref, w0_ref, w2_ref, a0_ref,
     a2_ref, g2_ref, kvec_ref, lnx_ref, bd_ref, bdm_ref) = refs[:17]
    refs = refs[17:]
    if has_state:
        s0_ref = refs[0]
        refs = refs[1:]
    yb_ref, sf_ref, pre_ref, g_scr, bonus_scr, r1_scr, m1_scr, d0_scr, gam_scr, y_scr, s_scr = refs

    def shifted(ref, mu_ref, r0):
        f = ref[r0:r0 + PREP_ROWS, :]
        zero = jnp.zeros((1, f.shape[1]), F32)
        before = zero if r0 == 0 else ref[r0 - 1:r0, :]
        after = zero if r0 + PREP_ROWS == seq_len else ref[r0 + PREP_ROWS:r0 + PREP_ROWS + 1, :]
        local = lax.broadcasted_iota(jnp.int32, (PREP_ROWS, 1), 0)
        prev = jnp.where(local == 0, before, pltpu.roll(f, 1, 0))
        nxt = jnp.where(local == PREP_ROWS - 1, after, pltpu.roll(f, PREP_ROWS - 1, 0))
        return f * (1.0 - mu_ref[0:1] - mu_ref[1:2]) + mu_ref[0:1] * prev + mu_ref[1:2] * nxt

    bd = bd_ref[...]
    kv = kvec_ref[...]
    for r0 in range(0, seq_len, PREP_ROWS):
        rows = slice(r0, r0 + PREP_ROWS)
        r = shifted(fr_ref, mur_ref, r0)
        k = shifted(fk_ref, muk_ref, r0)
        v = shifted(fv_ref, muv_ref, r0)
        lora = shifted(fl_ref, mul_ref, r0)
        wd = jnp.tanh(lora[:, 0:2 * LORA_W])
        ad = lora[:, 2 * LORA_W:2 * LORA_W + 2 * LORA_A]
        gd = lora[:, 2 * LORA_W + 2 * LORA_A:]
        g_scr[rows, :] = _dot(_sigmoid(gd), g2_ref[...])
        kk = k * kv[0:1]
        kk = kk * lax.rsqrt(_segsum(kk * kk, bd, pieces=1) + 1e-12)
        pre_ref[0, rows, :] = r
        pre_ref[1, rows, :] = v
        pre_ref[2, rows, :] = kk
        ksum = None
        for d in range(2):
            w_raw = w0_ref[d:d + 1] + _dot(wd, w2_ref[d])
            pre_ref[7 + d, rows, :] = -math.exp(-0.5) * _sigmoid(w_raw)
            a = _sigmoid(a0_ref[d:d + 1] + _dot(ad, a2_ref[d]))
            kd = k * (1.0 + (a - 1.0) * kv[1:2])
            pre_ref[3 + d, rows, :] = kd
            pre_ref[5 + d, rows, :] = kk * a
            ksum = kd if ksum is None else ksum + kd
        bonus_scr[rows, :] = _segsum(r * ksum * kv[2:3], bd) * v

    nc = seq_len // CHUNK
    c2 = 2 * CHUNK
    lane = lax.broadcasted_iota(jnp.int32, (CHUNK, PAIR), 1)
    head0 = lane < B_HEAD
    row = lax.broadcasted_iota(jnp.int32, (c2, c2), 0)
    col = lax.broadcasted_iota(jnp.int32, (c2, c2), 1)
    same = (row // CHUNK) == (col // CHUNK)
    eye = jnp.where(row == col, 1.0, 0.0)
    xor_rc = jnp.bitwise_xor(row, col)
    mm = functools.partial(_mm, passes=passes)
    mm_inv = functools.partial(_mm, passes=inv_passes)
    mm_carry = functools.partial(_mm, passes=carry_passes)

    def stack(x):
        return jnp.concatenate([jnp.where(head0, x, 0.0), jnp.where(head0, 0.0, x)], axis=0)

    def fold(x):
        return x[0:CHUNK] + x[CHUNK:c2]

    def chunk_rows(cc):
        return pl.ds(pl.multiple_of(cc * CHUNK, CHUNK), CHUNK)

    def chunk_local(cc, d, q):
        rows = chunk_rows(cc)
        lanes = pl.ds(q * PAIR, PAIR)
        z = 2 * q + d
        r = pre_ref[0, rows, lanes]
        v = pre_ref[1, rows, lanes]
        kk = pre_ref[2, rows, lanes]
        k = pre_ref[3 + d, rows, lanes]
        b = pre_ref[5 + d, rows, lanes]
        lw = pre_ref[7 + d, rows, lanes]
        incl = same & ((col <= row) if d == 0 else (col >= row))
        strict = same & ((col < row) if d == 0 else (col > row))
        g = _cumsum_rows(lw, reverse=(d == 1))
        gp = g - lw
        gtot = g[CHUNK - 1:CHUNK, :] if d == 0 else g[0:1, :]
        gm = 0.5 * gtot
        kx = k * jnp.exp(gm - g)
        bx = b * jnp.exp(gm - g)
        lhs = jnp.concatenate([stack(r * jnp.exp(g - gm)), stack(kk * jnp.exp(gp - gm))], axis=0)
        rhs = jnp.concatenate([kx, kx, bx, bx], axis=0)
        a_all = mm(lhs, rhs, _NT)
        yield
        a_rk = jnp.where(incl, a_all[0:c2, 0:c2], 0.0)
        a_rb = jnp.where(incl, a_all[0:c2, c2:2 * c2], 0.0)
        a_kk = jnp.where(strict, a_all[c2:2 * c2, 0:c2], 0.0)
        n_mat = jnp.where(strict, a_all[c2:2 * c2, c2:2 * c2], 0.0)
        vs = stack(v)
        akv = mm(a_kk, vs, _NN)
        y0 = mm(a_rk, vs, _NN)
        lower = (col < row) if d == 0 else (col > row)
        p_inv = eye - jnp.where(lower & (xor_rc == 1), n_mat, 0.0)
        yield
        s = 2
        while s < CHUNK:
            c_s = jnp.where(lower & (xor_rc >= s) & (xor_rc < 2 * s), n_mat, 0.0)
            t = mm_inv(c_s, p_inv, _NN)
            yield
            p_inv = p_inv - mm_inv(p_inv, t, _NN)
            yield
            s *= 2
        x = jnp.concatenate([stack(kk * jnp.exp(gp)), akv], axis=1)
        w12 = mm(p_inv, x, _NN)
        yield
        arb_w = mm(a_rb, w12, _NN)
        et = jnp.exp(gtot - g)
        bes = stack(b * et)
        d0_scr[z, cc] = mm(jnp.concatenate([vs, -w12[:, PAIR:]], axis=0),
                           jnp.concatenate([stack(k * et), bes], axis=0), _TN)
        m1_scr[z, cc] = mm(w12[:, 0:PAIR], bes, _TN)
        yield
        r1_scr[z, cc] = stack(r * jnp.exp(g)) - arb_w[:, 0:PAIR]
        y_scr[z, rows, :] = fold(y0 - arb_w[:, PAIR:])
        gam_scr[z, cc] = jnp.broadcast_to(jnp.exp(gtot), (SUBLANES, PAIR))

    def local_body(grp, carry):
        _round_robin(chunk_local(grp * unroll + u, d, q)
                     for u in range(unroll) for q in range(pairs) for d in range(2))
        return carry

    lax.fori_loop(0, nc // unroll, local_body, 0)

    for z in range(2 * pairs):
        s_scr[z] = jnp.zeros((PAIR, PAIR), F32)
        if has_state:
            s_scr[z, 0:B_HEAD, 0:B_HEAD] = s0_ref[0, z % 2, 2 * (z // 2)]
            s_scr[z, B_HEAD:PAIR, B_HEAD:PAIR] = s0_ref[0, z % 2, 2 * (z // 2) + 1]

    def carry_body(c, carry):
        for z in range(2 * pairs):
            cc = c if z % 2 == 0 else nc - 1 - c
            rows = chunk_rows(cc)
            s0 = s_scr[z]
            y_scr[z, rows, :] += fold(mm_carry(r1_scr[z, cc], s0, _NT))
            s_scr[z] = s0 * gam_scr[z, cc, 0:1, :] - mm_carry(s0, m1_scr[z, cc], _NN) + d0_scr[z, cc]
        return carry

    lax.fori_loop(0, nc, carry_body, 0)
    bdm = bdm_ref[...]
    for r0 in range(0, seq_len, PREP_ROWS):
        rows = slice(r0, r0 + PREP_ROWS)
        ys = jnp.concatenate([y_scr[2 * q, rows, :] + y_scr[2 * q + 1, rows, :] for q in range(pairs)],
                             axis=1)
        dlt = ys - _segsum(ys, bdm)
        yn = dlt * lax.rsqrt(_segsum(dlt * dlt, bdm, pieces=1) + LNX_EPS)
        yb = (yn * lnx_ref[0:1] + lnx_ref[1:2] + bonus_scr[rows, :]) * g_scr[rows, :]
        yb_ref[rows, :] = yb.astype(BF16)
    for z in range(2 * pairs):
        sf_ref[0, z % 2, 2 * (z // 2)] = s_scr[z, 0:B_HEAD, 0:B_HEAD]
        sf_ref[0, z % 2, 2 * (z // 2) + 1] = s_scr[z, B_HEAD:PAIR, B_HEAD:PAIR]


def _rwkv_mix(fb, s0, p, n_seq, seq_len, pairs=2, passes=1, inv_passes=1, carry_passes=1, unroll=4):
    t = fb.shape[0]
    nc = seq_len // CHUNK
    w = pairs * PAIR
    nw = B_WIDTH // w
    lora_w = 2 * LORA_W + 2 * LORA_A + LORA_G
    assert seq_len % PREP_ROWS == 0 and (3 * B_WIDTH) % lora_w == 0
    col = lambda rows, base: pl.BlockSpec((rows, w), lambda b, q: (0, base + q))
    seq = lambda base: pl.BlockSpec((seq_len, w), lambda b, q: (b, base + q))
    lora_blk = 3 * B_WIDTH // lora_w
    sblk = pl.BlockSpec((1, 2, 2 * pairs, B_HEAD, B_HEAD), lambda b, q: (b, 0, q, 0, 0))
    mat = pltpu.VMEM((2 * pairs, nc, PAIR, PAIR), F32)
    in_specs = [
        seq(0), seq(nw), seq(2 * nw),
        pl.BlockSpec((seq_len, lora_w), lambda b, q: (b, lora_blk)),
        col(2, 0), col(2, nw), col(2, 2 * nw),
        pl.BlockSpec((2, lora_w), lambda b, q: (0, lora_blk)),
        col(2, 0),
        pl.BlockSpec((2, 2 * LORA_W, w), lambda b, q: (0, 0, q)),
        col(2, 0),
        pl.BlockSpec((2, 2 * LORA_A, w), lambda b, q: (0, 0, q)),
        col(LORA_G, 0), col(3, 0), col(2, 0),
        pl.BlockSpec((w, w), lambda b, q: (0, 0)),
        pl.BlockSpec((w, w), lambda b, q: (0, 0)),
    ]
    operands = [fb, fb, fb, fb, p["mu"], p["mu"], p["mu"], p["mu"], p["w0"], p["w2"], p["a0"], p["a2"],
                p["g2"], p["kvec"], p["lnx"], _block_diag(1.0, w), _block_diag(1.0 / B_HEAD, w)]
    if s0 is not None:
        in_specs.append(sblk)
        operands.append(s0)
    return pl.pallas_call(
        functools.partial(_rwkv_kernel, seq_len=seq_len, has_state=s0 is not None, pairs=pairs,
                          passes=passes, inv_passes=inv_passes, carry_passes=carry_passes, unroll=unroll),
        grid=(n_seq, nw),
        in_specs=in_specs,
        out_specs=[pl.BlockSpec((seq_len, w), lambda b, q: (b, q)), sblk],
        out_shape=[
            jax.ShapeDtypeStruct((t, B_WIDTH), BF16),
            jax.ShapeDtypeStruct((n_seq, 2, B_HEADS, B_HEAD, B_HEAD), F32),
        ],
        scratch_shapes=[pltpu.VMEM((9, seq_len, w), F32), pltpu.VMEM((seq_len, w), F32),
                        pltpu.VMEM((seq_len, w), F32), mat, mat, mat,
                        pltpu.VMEM((2 * pairs, nc, SUBLANES, PAIR), F32),
                        pltpu.VMEM((2 * pairs, seq_len, PAIR), F32),
                        pltpu.VMEM((2 * pairs, PAIR, PAIR), F32)],
        compiler_params=_cparams("parallel", "parallel"),
        name="rwkv_mix",
    )(*operands)


def _even_out_kernel(x_ref, mod_ref, gains_ref, ya_ref, yb_ref, w_ref, o_ref):
    m = (jnp.dot(ya_ref[...], w_ref[0:A_WIDTH, :], preferred_element_type=F32)
         + jnp.dot(yb_ref[...], w_ref[A_WIDTH:, :], preferred_element_type=F32))
    mod = mod_ref[0]
    o_ref[...] = x_ref[...] + mod[2:3] * (_rms(m) * gains_ref[1:2])


def _even_out(x, mod, gains, ya, yb, w, seq_len, mod_row0, tm):
    t = x.shape[0]
    row = lambda i: (i, 0)
    return pl.pallas_call(
        _even_out_kernel,
        grid=(t // tm,),
        in_specs=[
            pl.BlockSpec((tm, D_MODEL), row),
            _mod_spec(tm, seq_len, mod_row0),
            pl.BlockSpec((4, D_MODEL), lambda i: (0, 0)),
            pl.BlockSpec((tm, A_WIDTH), row),
            pl.BlockSpec((tm, B_WIDTH), row),
            pl.BlockSpec((D_MODEL, D_MODEL), lambda i: (0, 0)),
        ],
        out_specs=pl.BlockSpec((tm, D_MODEL), row),
        out_shape=jax.ShapeDtypeStruct((t, D_MODEL), F32),
        compiler_params=_cparams("parallel"),
        name="even_out",
    )(x, mod, gains, ya, yb, w)


def _odd_kernel(x_ref, mod_ref, gains_ref, csc_ref, cl_ref, sl_ref, w_ref, o_ref,
                tc_scr, ts_scr, m_scr, *, seq_len):
    mod = mod_ref[0]
    gd = C_GROUP_DIM
    n_seq = x_ref.shape[0] // seq_len
    csc = csc_ref[...]

    for s in range(n_seq):
        rows = pl.ds(s * seq_len, seq_len)
        h = _modnorm(x_ref[rows, :], gains_ref[0:1], mod[1:2], mod[0:1]).astype(BF16)
        for g in range(C_GROUPS):
            t = jnp.dot(h[:, g * gd:(g + 1) * gd], csc, preferred_element_type=F32)
            tc_scr[rows, g * gd:(g + 1) * gd] = t[:, 0:gd].astype(BF16)
            ts_scr[rows, g * gd:(g + 1) * gd] = t[:, gd:].astype(BF16)

    def position_dft(s, cb):
        rows = pl.ds(s * seq_len, seq_len)
        cols = pl.ds(cb * ODD_COLS, ODD_COLS)
        f = (jnp.dot(cl_ref[...], tc_scr[rows, cols], preferred_element_type=F32)
             - jnp.dot(sl_ref[...], ts_scr[rows, cols], preferred_element_type=F32))
        yield
        part = jnp.dot(f.astype(BF16), w_ref[cols, :], preferred_element_type=F32)
        yield
        if cb == 0:
            m_scr[rows, :] = part
        else:
            m_scr[rows, :] += part

    _round_robin(position_dft(s, cb) for s in range(n_seq) for cb in range(D_MODEL // ODD_COLS))
    o_ref[...] = x_ref[...] + mod[2:3] * (_rms(m_scr[...]) * gains_ref[1:2])


def _dft_tables(n):
    idx = np.arange(n, dtype=np.int64)
    ang = 2.0 * np.pi * ((idx[:, None] * idx[None, :]) % n).astype(np.float64) / n
    s = 1.0 / math.sqrt(n)
    return np.cos(ang) * s, np.sin(ang) * s


def _odd_mixer(x, mod, gains, w, n_seq, seq_len, mod_row0):
    cc, sc = _dft_tables(C_GROUP_DIM)
    cl, sl = _dft_tables(seq_len)
    csc, cl, sl = (jnp.asarray(a, F32).astype(BF16) for a in (np.concatenate([cc, sc], axis=1), cl, sl))
    full = lambda shape: pl.BlockSpec(shape, lambda i: (0,) * len(shape))
    row = lambda i: (i, 0)
    tm = max(seq_len, ODD_ROWS)
    assert tm % seq_len == 0 and (n_seq * seq_len) % tm == 0
    return pl.pallas_call(
        functools.partial(_odd_kernel, seq_len=seq_len),
        grid=(n_seq * seq_len // tm,),
        in_specs=[
            pl.BlockSpec((tm, D_MODEL), row),
            _mod_spec(tm, seq_len, mod_row0),
            full((4, D_MODEL)), full((C_GROUP_DIM, 2 * C_GROUP_DIM)),
            full((seq_len, seq_len)), full((seq_len, seq_len)), full((D_MODEL, D_MODEL)),
        ],
        out_specs=pl.BlockSpec((tm, D_MODEL), row),
        out_shape=jax.ShapeDtypeStruct(x.shape, F32),
        scratch_shapes=[pltpu.VMEM((tm, D_MODEL), BF16), pltpu.VMEM((tm, D_MODEL), BF16),
                        pltpu.VMEM((tm, D_MODEL), F32)],
        compiler_params=_cparams("parallel"),
        name="odd_mixer",
    )(x, mod, gains, csc, cl, sl, w)


def _ffn_kernel(x_ref, mod_ref, gains_ref, wu_ref, wg_ref, cw_ref, cb_ref, wo_ref, o_ref,
                h_scr, acc_scr, *, seq_len):
    j = pl.program_id(1)
    mod = mod_ref[0]

    @pl.when(j == 0)
    def _():
        h_scr[...] = _modnorm(x_ref[...], gains_ref[2:3], mod[4:5], mod[3:4]).astype(BF16)
        acc_scr[...] = jnp.zeros_like(acc_scr)

    n_sub = h_scr.shape[0] // FF_ROWS
    gate = [None] * n_sub
    zero_row = jnp.zeros((1, FF_CHUNK), F32)
    wu = wu_ref[...].astype(BF16)
    wg = wg_ref[...].astype(BF16)
    wo = wo_ref[...].astype(BF16)

    def sub_tile(i):
        rows = pl.ds(i * FF_ROWS, FF_ROWS)
        h = h_scr[rows, :]
        u = jnp.dot(h, wu, preferred_element_type=F32)
        g = gate[i] = jnp.dot(h, wg, preferred_element_type=F32)
        yield
        before = zero_row if (i * FF_ROWS) % seq_len == 0 else gate[i - 1][FF_ROWS - 1:FF_ROWS, :]
        after = zero_row if ((i + 1) * FF_ROWS) % seq_len == 0 else gate[i + 1][0:1, :]
        pos = lax.broadcasted_iota(jnp.int32, (FF_ROWS, 1), 0)
        prev = jnp.where(pos == 0, before, pltpu.roll(g, 1, 0))
        nxt = jnp.where(pos == FF_ROWS - 1, after, pltpu.roll(g, FF_ROWS - 1, 0))
        gc = prev * cw_ref[0:1] + g * cw_ref[1:2] + nxt * cw_ref[2:3] + cb_ref[...]
        act = (_silu(gc) * u).astype(BF16)
        yield
        acc_scr[rows, :] += jnp.dot(act, wo, preferred_element_type=F32)

    _round_robin(sub_tile(i) for i in range(n_sub))

    @pl.when(j == pl.num_programs(1) - 1)
    def _():
        o_ref[...] = x_ref[...] + mod[5:6] * (_rms(acc_scr[...]) * gains_ref[3:4])


def _conv_ffn(x, mod, gains, layer, w_in, conv_w, conv_b, w_out, seq_len, mod_row0, tm):
    t = x.shape[0]
    nj = D_FF // FF_CHUNK
    assert seq_len % FF_ROWS == 0 and tm % seq_len == 0
    return pl.pallas_call(
        functools.partial(_ffn_kernel, seq_len=seq_len),
        grid=(t // tm, nj),
        in_specs=[
            pl.BlockSpec((tm, D_MODEL), lambda i, j: (i, 0)),
            _mod_spec(tm, seq_len, mod_row0),
            pl.BlockSpec((4, D_MODEL), lambda i, j: (0, 0)),
            pl.BlockSpec((None, D_MODEL, FF_CHUNK), lambda i, j: (layer, 0, j)),
            pl.BlockSpec((None, D_MODEL, FF_CHUNK), lambda i, j: (layer, 0, nj + j)),
            pl.BlockSpec((3, FF_CHUNK), lambda i, j: (0, j)),
            pl.BlockSpec((1, FF_CHUNK), lambda i, j: (0, j)),
            pl.BlockSpec((None, FF_CHUNK, D_MODEL), lambda i, j: (layer, j, 0)),
        ],
        out_specs=pl.BlockSpec((tm, D_MODEL), lambda i, j: (i, 0)),
        out_shape=jax.ShapeDtypeStruct((t, D_MODEL), F32),
        scratch_shapes=[pltpu.VMEM((tm, D_MODEL), BF16), pltpu.VMEM((tm, D_MODEL), F32)],
        compiler_params=_cparams("parallel", "arbitrary"),
        name="conv_ffn",
    )(x, mod, gains, w_in, w_in, conv_w, conv_b, w_out)


def _rope_tables(seq_len):
    rows = seq_len // GRID_W
    row = jnp.repeat(jnp.arange(rows, dtype=F32), GRID_W)
    col = jnp.tile(jnp.arange(GRID_W, dtype=F32), rows)
    n_freq = A_HALF // 4
    inv = ROPE_BASE ** (-jnp.arange(n_freq, dtype=F32) / n_freq)
    ang = jnp.concatenate([row[:, None] * inv, col[:, None] * inv], axis=-1)
    cos_t = jnp.tile(jnp.repeat(jnp.cos(ang), 2, axis=1), (1, 2))
    sin_t = jnp.tile(jnp.repeat(jnp.sin(ang), 2, axis=1), (1, 2))
    sign = jnp.where(jnp.arange(A_HEAD_DIM) % 2 == 0, -1.0, 1.0).astype(F32)
    return cos_t, sin_t * sign


def _pad_lora(w):
    z = jnp.zeros_like(w[0])
    return jnp.stack([jnp.concatenate([w[0], z], axis=0), jnp.concatenate([z, w[1]], axis=0)])


def _block_diag(value, width):
    head = np.arange(width) // B_HEAD
    return jnp.asarray(np.where(head[:, None] == head[None, :], value, 0.0), BF16)


def _tile_rows(seq_len):
    return max(seq_len, 1024)


def kernel(x_prompt, x_sample, cache_k, cache_v, state_wkv, c, c_ctx, w_ada, b_ada, norm_gains,
           w_in_even, w_out_even, diff_lambda, diff_subln, rwkv_shift_mu, rwkv_w0, rwkv_w2, rwkv_a0,
           rwkv_a2, rwkv_g2, rwkv_kvec, rwkv_lnx, w_out_odd, w_ffn_in, ffn_conv, ffn_conv_b,
           w_ffn_out):
    n_ctx, l_ctx, _ = x_prompt.shape
    n_lat, l_lat, _ = x_sample.shape
    assert 1 + n_lat <= MOD_ROWS
    cvec = jnp.concatenate(
        [c_ctx[None, :], c, jnp.zeros((MOD_ROWS - 1 - n_lat, D_MODEL), F32)], axis=0)
    mods = _modulation(cvec, w_ada, b_ada).reshape(DEPTH, MOD_ROWS, 6, D_MODEL)

    per_layer_bf16 = lambda w: [w[i].astype(BF16) for i in range(w.shape[0])]
    w_in_even_b = per_layer_bf16(w_in_even)
    w_out_even_b = per_layer_bf16(w_out_even)
    w_out_odd_b = per_layer_bf16(w_out_odd)
    cos_t, sin_t = _rope_tables(l_lat)

    def run_group(x, n_seq, seq_len, mod_row0, latent):
        tm = _tile_rows(seq_len)
        ctx_out = None
        for l in range(DEPTH):
            mod = mods[l]
            gains = norm_gains[l]
            if l % 2 == 0:
                e = l // 2
                lam_init = 0.8 - 0.6 * math.exp(-0.3 * l)
                q, k, v, fb, *kv_cache = _even_in(
                    x, mod, gains, w_in_even_b[e], seq_len, mod_row0, 256,
                    rope=(cos_t, sin_t) if latent else None, cache=not latent)
                if latent:
                    past = cache_k.shape[2]
                    ctx = (cache_k[:, e].reshape(n_seq * past, A_WIDTH),
                           cache_v[:, e].reshape(n_seq * past, A_WIDTH))
                    s0 = state_wkv[:, e]
                else:
                    ctx = None
                    s0 = None
                ya = _attention(q, k, v, diff_lambda[e], diff_subln[e][None, :], lam_init,
                                n_seq, seq_len, ctx)
                rwkv = {
                    "mu": rwkv_shift_mu[e], "w0": rwkv_w0[e], "w2": _pad_lora(rwkv_w2[e]).astype(BF16),
                    "a0": rwkv_a0[e], "a2": _pad_lora(rwkv_a2[e]).astype(BF16),
                    "g2": rwkv_g2[e].astype(BF16), "kvec": rwkv_kvec[e], "lnx": rwkv_lnx[e],
                }
                yb, s_fin = _rwkv_mix(fb, s0, rwkv, n_seq, seq_len)
                x = _even_out(x, mod, gains, ya, yb, w_out_even_b[e], seq_len, mod_row0, 512)
                if not latent:
                    ctx_out = (*kv_cache, s_fin)
            else:
                x = _odd_mixer(x, mod, gains, w_out_odd_b[l // 2], n_seq, seq_len, mod_row0)
            x = _conv_ffn(x, mod, gains, l, w_ffn_in, ffn_conv[l], ffn_conv_b[l][None, :],
                          w_ffn_out, seq_len, mod_row0, tm)
        return x, ctx_out

    y_ctx, (k_new, v_new, s_new) = run_group(
        x_prompt.reshape(n_ctx * l_ctx, D_MODEL), n_ctx, l_ctx, 0, False)
    y_lat, _ = run_group(x_sample.reshape(n_lat * l_lat, D_MODEL), n_lat, l_lat, 1, True)

    n_even = (DEPTH + 1) // 2
    assert n_even == 1
    return (
        y_ctx.reshape(n_ctx, l_ctx, D_MODEL),
        y_lat.reshape(n_lat, l_lat, D_MODEL),
        k_new,
        v_new,
        s_new.reshape(n_ctx, n_even, 2, B_HEADS, B_HEAD, B_HEAD),
    )
```

```python
import functools
import math

import numpy as np
import jax
import jax.numpy as jnp
from jax import lax
from jax.experimental import pallas as pl
from jax.experimental.pallas import tpu as pltpu

F32 = jnp.float32
BF16 = jnp.bfloat16

D_MODEL = 1024
DEPTH = 2
GRID_W = 64
A_WIDTH = D_MODEL // 2
A_HEADS = 4
A_HEAD_DIM = A_WIDTH // A_HEADS
A_HALF = A_HEAD_DIM // 2
B_WIDTH = D_MODEL - A_WIDTH
B_HEAD = 64
B_HEADS = B_WIDTH // B_HEAD
LORA_W = 64
LORA_A = 64
LORA_G = 128
IN_B = 3 * B_WIDTH + 2 * LORA_W + 2 * LORA_A + LORA_G
IN_EVEN = 3 * A_WIDTH + IN_B
C_GROUPS = 8
C_GROUP_DIM = D_MODEL // C_GROUPS
D_FF = 2816
ROPE_BASE = 10000.0
RMS_EPS = 1e-6
LNX_EPS = 64e-5

LANES = 128
SUBLANES = 8
PAIR = 2 * B_HEAD
N_PAIRS = B_HEADS // 2
CHUNK = 64
FF_CHUNK = 256
FF_ROWS = 256
PREP_ROWS = 256
ATTN_Q_ROWS = 256
ODD_ROWS = 1024
ODD_COLS = 256
VMEM_LIMIT = 48 * 1024 * 1024
MOD_ROWS = 8


def _cparams(*sem):
    return pltpu.CompilerParams(dimension_semantics=sem, vmem_limit_bytes=VMEM_LIMIT)


def _sigmoid(x):
    return 1.0 / (1.0 + jnp.exp(-x))


def _silu(x):
    return x * (0.5 + 0.5 * jnp.tanh(0.5 * x))


def _dot(a, b):
    return jnp.dot(a.astype(BF16), b.astype(BF16), preferred_element_type=F32)


_NN = (((1,), (0,)), ((), ()))
_NT = (((1,), (1,)), ((), ()))
_TN = (((0,), (0,)), ((), ()))


def _split_bf16(x, n):
    parts = []
    rem = x
    for i in range(n):
        p = rem.astype(BF16)
        parts.append(p)
        if i + 1 < n:
            rem = rem - p.astype(F32)
    return parts


def _mm(a, b, dims):
    return lax.dot_general(a.astype(BF16), b.astype(BF16), dims, preferred_element_type=F32)


def _segsum(x, bd, pieces=2):
    acc = None
    for p in _split_bf16(x, pieces):
        t = jnp.dot(p, bd, preferred_element_type=F32)
        acc = t if acc is None else acc + t
    return acc


def _rms(x):
    return x * lax.rsqrt(jnp.mean(x * x, axis=-1, keepdims=True) + RMS_EPS)


def _modnorm(x, gain, scale, shift):
    return _rms(x) * gain * (1.0 + scale) + shift


def _mod_kernel(c_ref, w_ref, b_ref, o_ref):
    c = c_ref[...]
    s = c * _sigmoid(c)
    o_ref[0] = _dot(s, w_ref[0]) + b_ref[0]


def _modulation(cvec, w_ada, b_ada):
    tn = 1536
    n = 6 * D_MODEL
    return pl.pallas_call(
        _mod_kernel,
        grid=(DEPTH, n // tn),
        in_specs=[
            pl.BlockSpec((MOD_ROWS, D_MODEL), lambda l, j: (0, 0)),
            pl.BlockSpec((1, D_MODEL, tn), lambda l, j: (l, 0, j)),
            pl.BlockSpec((1, 1, tn), lambda l, j: (l, 0, j)),
        ],
        out_specs=pl.BlockSpec((1, MOD_ROWS, tn), lambda l, j: (l, 0, j)),
        out_shape=jax.ShapeDtypeStruct((DEPTH, MOD_ROWS, n), F32),
        compiler_params=_cparams("parallel", "parallel"),
        name="modulation",
    )(cvec, w_ada, b_ada.reshape(DEPTH, 1, n))


def _mod_spec(tm, seq_len, mod_row0):
    if mod_row0 == 0:
        return pl.BlockSpec((1, 6, D_MODEL), lambda i, *_: (0, 0, 0))
    return pl.BlockSpec((1, 6, D_MODEL), lambda i, *_: (mod_row0 + (i * tm) // seq_len, 0, 0))


def _rope(x, cos_t, sin_t):
    lane = lax.broadcasted_iota(jnp.int32, x.shape, 1)
    nxt = pltpu.roll(x, LANES - 1, 1)
    prv = pltpu.roll(x, 1, 1)
    partner = jnp.where(lane % 2 == 0, nxt, prv)
    return x * cos_t + partner * sin_t


def _even_in_kernel(*refs, rope, cache):
    x_ref, mod_ref, gains_ref, w_ref = refs[:4]
    refs = refs[4:]
    if rope:
        cos_ref, sin_ref = refs[:2]
        refs = refs[2:]
    q_ref, k_ref, v_ref, fb_ref = refs[:4]
    cache_refs = refs[4:]
    mod = mod_ref[0]
    h = _modnorm(x_ref[...], gains_ref[0:1], mod[1:2], mod[0:1]).astype(BF16)
    a = A_WIDTH
    hd = A_HEAD_DIM
    for n, out_ref in enumerate((q_ref, k_ref, v_ref)):
        y = jnp.dot(h, w_ref[:, n * a:(n + 1) * a], preferred_element_type=F32)
        if cache and n > 0:
            for hh in range(A_HEADS):
                cache_refs[n - 1][0, 0, :, hh, :] = y[:, hh * hd:(hh + 1) * hd]
        if rope and n < 2:
            for hh in range(A_HEADS):
                cols = slice(hh * hd, (hh + 1) * hd)
                out_ref[:, cols] = _rope(y[:, cols], cos_ref[...], sin_ref[...]).astype(BF16)
        else:
            out_ref[...] = y.astype(BF16)
    fb_ref[...] = jnp.dot(h, w_ref[:, 3 * a:], preferred_element_type=F32)


def _even_in(x, mod, gains, w, seq_len, mod_row0, tm, rope=None, cache=False):
    t = x.shape[0]
    assert seq_len % tm == 0
    per_seq = seq_len // tm
    row = lambda i: (i, 0)
    operands = [x, mod, gains, w]
    in_specs = [
        pl.BlockSpec((tm, D_MODEL), row),
        _mod_spec(tm, seq_len, mod_row0),
        pl.BlockSpec((4, D_MODEL), lambda i: (0, 0)),
        pl.BlockSpec((D_MODEL, IN_EVEN), lambda i: (0, 0)),
    ]
    if rope is not None:
        tab = pl.BlockSpec((tm, A_HEAD_DIM), lambda i: (i % per_seq, 0))
        operands += list(rope)
        in_specs += [tab, tab]
    out_specs = [pl.BlockSpec((tm, A_WIDTH), row)] * 3 + [pl.BlockSpec((tm, IN_B), row)]
    out_shape = [jax.ShapeDtypeStruct((t, A_WIDTH), BF16)] * 3 + [jax.ShapeDtypeStruct((t, IN_B), F32)]
    if cache:
        out_specs += [pl.BlockSpec((1, 1, tm, A_HEADS, A_HEAD_DIM),
                                   lambda i: (i // per_seq, 0, i % per_seq, 0, 0))] * 2
        out_shape += [jax.ShapeDtypeStruct((t // seq_len, 1, seq_len, A_HEADS, A_HEAD_DIM), F32)] * 2
    return pl.pallas_call(
        functools.partial(_even_in_kernel, rope=rope is not None, cache=cache),
        grid=(t // tm,),
        in_specs=in_specs,
        out_specs=out_specs,
        out_shape=out_shape,
        compiler_params=_cparams("parallel"),
        name="even_in",
    )(*operands)


def _attn_kernel(*refs, has_ctx, lam_init):
    if has_ctx:
        lam_ref, sub_ref, q_ref, k_ref, v_ref, ck_ref, cv_ref, o_ref = refs
    else:
        lam_ref, sub_ref, q_ref, k_ref, v_ref, o_ref = refs
    lp = lam_ref[...]
    l1 = jnp.sum(lp[0:1] * lp[1:2], axis=-1, keepdims=True)
    l2 = jnp.sum(lp[2:3] * lp[3:4], axis=-1, keepdims=True)
    lam = jnp.exp(l1) - jnp.exp(l2) + lam_init
    hd = A_HEAD_DIM
    lane = lax.broadcasted_iota(jnp.int32, (q_ref.shape[0], hd), 1)
    scale = A_HALF ** -0.5

    def head(h):
        cols = pl.ds(h * hd, hd)
        q = q_ref[:, cols]
        k = k_ref[:, cols].astype(BF16)
        if has_ctx:
            ck = ck_ref[:, cols].astype(BF16)
        scores = []
        for m in range(2):
            qm = jnp.where((lane < A_HALF) == (m == 0), q, 0.0).astype(BF16)
            s = lax.dot_general(qm, k, _NT, preferred_element_type=F32) * scale
            sc = lax.dot_general(qm, ck, _NT, preferred_element_type=F32) * scale if has_ctx else None
            scores.append((s, sc))
        yield
        probs = []
        for s, sc in scores:
            mx = jnp.max(s, axis=-1, keepdims=True)
            if has_ctx:
                mx = jnp.maximum(mx, jnp.max(sc, axis=-1, keepdims=True))
                ec = jnp.exp(sc - mx)
            e = jnp.exp(s - mx)
            z = jnp.sum(e, axis=-1, keepdims=True)
            if has_ctx:
                z = z + jnp.sum(ec, axis=-1, keepdims=True)
            inv = 1.0 / z
            probs.append((e * inv, ec * inv if has_ctx else None))
        w = (probs[0][0] - lam * probs[1][0]).astype(BF16)
        if has_ctx:
            wc = (probs[0][1] - lam * probs[1][1]).astype(BF16)
        yield
        o = jnp.dot(w, v_ref[:, cols].astype(BF16), preferred_element_type=F32)
        if has_ctx:
            o = o + jnp.dot(wc, cv_ref[:, cols].astype(BF16), preferred_element_type=F32)
        yield
        o_ref[:, cols] = (_rms(o) * sub_ref[...] * (1.0 - lam_init)).astype(BF16)

    _round_robin(head(h) for h in range(A_HEADS))


def _attention(q, k, v, lam_p, subln, lam_init, n_seq, seq_len, ctx=None):
    t = q.shape[0]
    tq = min(seq_len, ATTN_Q_ROWS if ctx is None else ATTN_Q_ROWS // 2)
    nq = seq_len // tq
    small = [pl.BlockSpec((4, A_HALF), lambda b, i: (0, 0)),
             pl.BlockSpec((1, A_HEAD_DIM), lambda b, i: (0, 0))]
    qblk = pl.BlockSpec((tq, A_WIDTH), lambda b, i: (b * nq + i, 0))
    kblk = pl.BlockSpec((seq_len, A_WIDTH), lambda b, i: (b, 0))
    operands = [lam_p, subln, q, k, v]
    in_specs = small + [qblk, kblk, kblk]
    if ctx is not None:
        ck, cv = ctx
        cblk = pl.BlockSpec((ck.shape[0] // n_seq, A_WIDTH), lambda b, i: (b, 0))
        operands += [ck, cv]
        in_specs += [cblk, cblk]
    return pl.pallas_call(
        functools.partial(_attn_kernel, has_ctx=ctx is not None, lam_init=lam_init),
        grid=(n_seq, nq),
        in_specs=in_specs,
        out_specs=qblk,
        out_shape=jax.ShapeDtypeStruct((t, A_WIDTH), BF16),
        compiler_params=_cparams("parallel", "parallel"),
        name="diff_attn",
    )(*operands)


def _cumsum_rows(x, reverse):
    n = x.shape[0]
    ridx = lax.broadcasted_iota(jnp.int32, x.shape, 0)
    s = 1
    while s < n:
        if reverse:
            x = x + jnp.where(ridx < n - s, pltpu.roll(x, n - s, 0), 0.0)
        else:
            x = x + jnp.where(ridx >= s, pltpu.roll(x, s, 0), 0.0)
        s *= 2
    return x


def _round_robin(gens):
    gens = list(gens)
    while gens:
        alive = []
        for gen in gens:
            try:
                next(gen)
                alive.append(gen)
            except StopIteration:
                pass
        gens = alive


def _rwkv_kernel(*refs, seq_len, has_state, pairs, unroll):
    (fr_ref, fk_ref, fv_ref, fl_ref, mur_ref, muk_ref, muv_ref, mul_ref, w0_ref, w2_ref, a0_ref,
     a2_ref, g2_ref, kvec_ref, lnx_ref, bd_ref, bdm_ref) = refs[:17]
    refs = refs[17:]
    if has_state:
        s0_ref = refs[0]
        refs = refs[1:]
    yb_ref, sf_ref, pre_ref, g_scr, bonus_scr, r1_scr, m1_scr, d0_scr, gam_scr, y_scr, s_scr = refs

    def shifted(ref, mu_ref, r0):
        f = ref[r0:r0 + PREP_ROWS, :]
        zero = jnp.zeros((1, f.shape[1]), F32)
        before = zero if r0 == 0 else ref[r0 - 1:r0, :]
        after = zero if r0 + PREP_ROWS == seq_len else ref[r0 + PREP_ROWS:r0 + PREP_ROWS + 1, :]
        local = lax.broadcasted_iota(jnp.int32, (PREP_ROWS, 1), 0)
        prev = jnp.where(local == 0, before, pltpu.roll(f, 1, 0))
        nxt = jnp.where(local == PREP_ROWS - 1, after, pltpu.roll(f, PREP_ROWS - 1, 0))
        return f * (1.0 - mu_ref[0:1] - mu_ref[1:2]) + mu_ref[0:1] * prev + mu_ref[1:2] * nxt

    bd = bd_ref[...]
    kv = kvec_ref[...]
    for r0 in range(0, seq_len, PREP_ROWS):
        rows = slice(r0, r0 + PREP_ROWS)
        r = shifted(fr_ref, mur_ref, r0)
        k = shifted(fk_ref, muk_ref, r0)
        v = shifted(fv_ref, muv_ref, r0)
        lora = shifted(fl_ref, mul_ref, r0)
        wd = jnp.tanh(lora[:, 0:2 * LORA_W])
        ad = lora[:, 2 * LORA_W:2 * LORA_W + 2 * LORA_A]
        gd = lora[:, 2 * LORA_W + 2 * LORA_A:]
        g_scr[rows, :] = _dot(_sigmoid(gd), g2_ref[...])
        kk = k * kv[0:1]
        kk = kk * lax.rsqrt(_segsum(kk * kk, bd, pieces=1) + 1e-12)
        pre_ref[0, rows, :] = r
        pre_ref[1, rows, :] = v
        pre_ref[2, rows, :] = kk
        ksum = None
        for d in range(2):
            w_raw = w0_ref[d:d + 1] + _dot(wd, w2_ref[d])
            pre_ref[7 + d, rows, :] = -math.exp(-0.5) * _sigmoid(w_raw)
            a = _sigmoid(a0_ref[d:d + 1] + _dot(ad, a2_ref[d]))
            kd = k * (1.0 + (a - 1.0) * kv[1:2])
            pre_ref[3 + d, rows, :] = kd
            pre_ref[5 + d, rows, :] = kk * a
            ksum = kd if ksum is None else ksum + kd
        bonus_scr[rows, :] = _segsum(r * ksum * kv[2:3], bd) * v

    nc = seq_len // CHUNK
    c2 = 2 * CHUNK
    lane = lax.broadcasted_iota(jnp.int32, (CHUNK, PAIR), 1)
    head0 = lane < B_HEAD
    row = lax.broadcasted_iota(jnp.int32, (c2, c2), 0)
    col = lax.broadcasted_iota(jnp.int32, (c2, c2), 1)
    same = (row // CHUNK) == (col // CHUNK)
    eye = jnp.where(row == col, 1.0, 0.0)
    xor_rc = jnp.bitwise_xor(row, col)
    mm = _mm

    def stack(x):
        return jnp.concatenate([jnp.where(head0, x, 0.0), jnp.where(head0, 0.0, x)], axis=0)

    def fold(x):
        return x[0:CHUNK] + x[CHUNK:c2]

    def chunk_rows(cc):
        return pl.ds(pl.multiple_of(cc * CHUNK, CHUNK), CHUNK)

    def chunk_local(cc, d, q):
        rows = chunk_rows(cc)
        lanes = pl.ds(q * PAIR, PAIR)
        z = 2 * q + d
        r = pre_ref[0, rows, lanes]
        v = pre_ref[1, rows, lanes]
        kk = pre_ref[2, rows, lanes]
        k = pre_ref[3 + d, rows, lanes]
        b = pre_ref[5 + d, rows, lanes]
        lw = pre_ref[7 + d, rows, lanes]
        incl = same & ((col <= row) if d == 0 else (col >= row))
        strict = same & ((col < row) if d == 0 else (col > row))
        g = _cumsum_rows(lw, reverse=(d == 1))
        gp = g - lw
        gtot = g[CHUNK - 1:CHUNK, :] if d == 0 else g[0:1, :]
        gm = 0.5 * gtot
        kx = k * jnp.exp(gm - g)
        bx = b * jnp.exp(gm - g)
        lhs = jnp.concatenate([stack(r * jnp.exp(g - gm)), stack(kk * jnp.exp(gp - gm))], axis=0)
        rhs = jnp.concatenate([kx, kx, bx, bx], axis=0)
        a_all = mm(lhs, rhs, _NT)
        yield
        a_rk = jnp.where(incl, a_all[0:c2, 0:c2], 0.0)
        a_rb = jnp.where(incl, a_all[0:c2, c2:2 * c2], 0.0)
        a_kk = jnp.where(strict, a_all[c2:2 * c2, 0:c2], 0.0)
        n_mat = jnp.where(strict, a_all[c2:2 * c2, c2:2 * c2], 0.0)
        vs = stack(v)
        akv = mm(a_kk, vs, _NN)
        y0 = mm(a_rk, vs, _NN)
        lower = (col < row) if d == 0 else (col > row)
        p_inv = eye - jnp.where(lower & (xor_rc == 1), n_mat, 0.0)
        yield
        s = 2
        while s < CHUNK:
            c_s = jnp.where(lower & (xor_rc >= s) & (xor_rc < 2 * s), n_mat, 0.0)
            t = mm(c_s, p_inv, _NN)
            yield
            p_inv = p_inv - mm(p_inv, t, _NN)
            yield
            s *= 2
        x = jnp.concatenate([stack(kk * jnp.exp(gp)), akv], axis=1)
        w12 = mm(p_inv, x, _NN)
        yield
        arb_w = mm(a_rb, w12, _NN)
        et = jnp.exp(gtot - g)
        bes = stack(b * et)
        d0_scr[z, cc] = mm(jnp.concatenate([vs, -w12[:, PAIR:]], axis=0),
                           jnp.concatenate([stack(k * et), bes], axis=0), _TN)
        m1_scr[z, cc] = mm(w12[:, 0:PAIR], bes, _TN).astype(BF16)
        yield
        r1_scr[z, cc] = (stack(r * jnp.exp(g)) - arb_w[:, 0:PAIR]).astype(BF16)
        y_scr[z, rows, :] = fold(y0 - arb_w[:, PAIR:])
        gam_scr[z, cc] = jnp.broadcast_to(jnp.exp(gtot), (SUBLANES, PAIR))

    def local_body(grp, carry):
        _round_robin(chunk_local(grp * unroll + u, d, q)
                     for u in range(unroll) for q in range(pairs) for d in range(2))
        return carry

    lax.fori_loop(0, nc // unroll, local_body, 0)

    for z in range(2 * pairs):
        s_scr[z] = jnp.zeros((PAIR, PAIR), F32)
        if has_state:
            s_scr[z, 0:B_HEAD, 0:B_HEAD] = s0_ref[0, z % 2, 2 * (z // 2)]
            s_scr[z, B_HEAD:PAIR, B_HEAD:PAIR] = s0_ref[0, z % 2, 2 * (z // 2) + 1]

    def carry_body(grp, carry):
        states = [s_scr[z] for z in range(2 * pairs)]
        for u in range(unroll):
            c = grp * unroll + u
            for z in range(2 * pairs):
                cc = c if z % 2 == 0 else nc - 1 - c
                rows = chunk_rows(cc)
                s0 = states[z]
                y_scr[z, rows, :] += fold(mm(r1_scr[z, cc], s0, _NT))
                states[z] = s0 * gam_scr[z, cc, 0:1, :] - mm(s0, m1_scr[z, cc], _NN) + d0_scr[z, cc]
        for z in range(2 * pairs):
            s_scr[z] = states[z]
        return carry

    lax.fori_loop(0, nc // unroll, carry_body, 0)
    bdm = bdm_ref[...]
    for r0 in range(0, seq_len, PREP_ROWS):
        rows = slice(r0, r0 + PREP_ROWS)
        ys = jnp.concatenate([y_scr[2 * q, rows, :] + y_scr[2 * q + 1, rows, :] for q in range(pairs)],
                             axis=1)
        dlt = ys - _segsum(ys, bdm)
        yn = dlt * lax.rsqrt(_segsum(dlt * dlt, bdm, pieces=1) + LNX_EPS)
        yb = (yn * lnx_ref[0:1] + lnx_ref[1:2] + bonus_scr[rows, :]) * g_scr[rows, :]
        yb_ref[rows, :] = yb.astype(BF16)
    for z in range(2 * pairs):
        sf_ref[0, z % 2, 2 * (z // 2)] = s_scr[z, 0:B_HEAD, 0:B_HEAD]
        sf_ref[0, z % 2, 2 * (z // 2) + 1] = s_scr[z, B_HEAD:PAIR, B_HEAD:PAIR]


def _rwkv_mix(fb, s0, p, n_seq, seq_len, pairs=2, unroll=4):
    t = fb.shape[0]
    nc = seq_len // CHUNK
    w = pairs * PAIR
    nw = B_WIDTH // w
    lora_w = 2 * LORA_W + 2 * LORA_A + LORA_G
    assert seq_len % PREP_ROWS == 0 and (3 * B_WIDTH) % lora_w == 0
    col = lambda rows, base: pl.BlockSpec((rows, w), lambda b, q: (0, base + q))
    seq = lambda base: pl.BlockSpec((seq_len, w), lambda b, q: (b, base + q))
    lora_blk = 3 * B_WIDTH // lora_w
    sblk = pl.BlockSpec((1, 2, 2 * pairs, B_HEAD, B_HEAD), lambda b, q: (b, 0, q, 0, 0))
    mat = lambda dtype: pltpu.VMEM((2 * pairs, nc, PAIR, PAIR), dtype)
    in_specs = [
        seq(0), seq(nw), seq(2 * nw),
        pl.BlockSpec((seq_len, lora_w), lambda b, q: (b, lora_blk)),
        col(2, 0), col(2, nw), col(2, 2 * nw),
        pl.BlockSpec((2, lora_w), lambda b, q: (0, lora_blk)),
        col(2, 0),
        pl.BlockSpec((2, 2 * LORA_W, w), lambda b, q: (0, 0, q)),
        col(2, 0),
        pl.BlockSpec((2, 2 * LORA_A, w), lambda b, q: (0, 0, q)),
        col(LORA_G, 0), col(3, 0), col(2, 0),
        pl.BlockSpec((w, w), lambda b, q: (0, 0)),
        pl.BlockSpec((w, w), lambda b, q: (0, 0)),
    ]
    operands = [fb, fb, fb, fb, p["mu"], p["mu"], p["mu"], p["mu"], p["w0"], p["w2"], p["a0"], p["a2"],
                p["g2"], p["kvec"], p["lnx"], _block_diag(1.0, w), _block_diag(1.0 / B_HEAD, w)]
    if s0 is not None:
        in_specs.append(sblk)
        operands.append(s0)
    return pl.pallas_call(
        functools.partial(_rwkv_kernel, seq_len=seq_len, has_state=s0 is not None, pairs=pairs,
                          unroll=unroll),
        grid=(n_seq, nw),
        in_specs=in_specs,
        out_specs=[pl.BlockSpec((seq_len, w), lambda b, q: (b, q)), sblk],
        out_shape=[
            jax.ShapeDtypeStruct((t, B_WIDTH), BF16),
            jax.ShapeDtypeStruct((n_seq, 2, B_HEADS, B_HEAD, B_HEAD), F32),
        ],
        scratch_shapes=[pltpu.VMEM((9, seq_len, w), F32), pltpu.VMEM((seq_len, w), F32),
                        pltpu.VMEM((seq_len, w), F32), mat(BF16), mat(BF16), mat(F32),
                        pltpu.VMEM((2 * pairs, nc, SUBLANES, PAIR), F32),
                        pltpu.VMEM((2 * pairs, seq_len, PAIR), F32),
                        pltpu.VMEM((2 * pairs, PAIR, PAIR), F32)],
        compiler_params=_cparams("parallel", "parallel"),
        name="rwkv_mix",
    )(*operands)


def _even_out_kernel(x_ref, mod_ref, gains_ref, ya_ref, yb_ref, w_ref, o_ref):
    m = (jnp.dot(ya_ref[...], w_ref[0:A_WIDTH, :], preferred_element_type=F32)
         + jnp.dot(yb_ref[...], w_ref[A_WIDTH:, :], preferred_element_type=F32))
    mod = mod_ref[0]
    o_ref[...] = x_ref[...] + mod[2:3] * (_rms(m) * gains_ref[1:2])


def _even_out(x, mod, gains, ya, yb, w, seq_len, mod_row0, tm):
    t = x.shape[0]
    row = lambda i: (i, 0)
    return pl.pallas_call(
        _even_out_kernel,
        grid=(t // tm,),
        in_specs=[
            pl.BlockSpec((tm, D_MODEL), row),
            _mod_spec(tm, seq_len, mod_row0),
            pl.BlockSpec((4, D_MODEL), lambda i: (0, 0)),
            pl.BlockSpec((tm, A_WIDTH), row),
            pl.BlockSpec((tm, B_WIDTH), row),
            pl.BlockSpec((D_MODEL, D_MODEL), lambda i: (0, 0)),
        ],
        out_specs=pl.BlockSpec((tm, D_MODEL), row),
        out_shape=jax.ShapeDtypeStruct((t, D_MODEL), F32),
        compiler_params=_cparams("parallel"),
        name="even_out",
    )(x, mod, gains, ya, yb, w)


def _odd_kernel(x_ref, mod_ref, gains_ref, csc_ref, cl_ref, sl_ref, w_ref, o_ref,
                tc_scr, ts_scr, m_scr, *, seq_len):
    mod = mod_ref[0]
    gd = C_GROUP_DIM
    n_seq = x_ref.shape[0] // seq_len
    csc = csc_ref[...]

    for s in range(n_seq):
        rows = pl.ds(s * seq_len, seq_len)
        h = _modnorm(x_ref[rows, :], gains_ref[0:1], mod[1:2], mod[0:1]).astype(BF16)
        for g in range(C_GROUPS):
            t = jnp.dot(h[:, g * gd:(g + 1) * gd], csc, preferred_element_type=F32)
            tc_scr[rows, g * gd:(g + 1) * gd] = t[:, 0:gd].astype(BF16)
            ts_scr[rows, g * gd:(g + 1) * gd] = t[:, gd:].astype(BF16)

    def position_dft(s, cb):
        rows = pl.ds(s * seq_len, seq_len)
        cols = pl.ds(cb * ODD_COLS, ODD_COLS)
        f = (jnp.dot(cl_ref[...], tc_scr[rows, cols], preferred_element_type=F32)
             - jnp.dot(sl_ref[...], ts_scr[rows, cols], preferred_element_type=F32))
        yield
        part = jnp.dot(f.astype(BF16), w_ref[cols, :], preferred_element_type=F32)
        yield
        if cb == 0:
            m_scr[rows, :] = part
        else:
            m_scr[rows, :] += part

    _round_robin(position_dft(s, cb) for s in range(n_seq) for cb in range(D_MODEL // ODD_COLS))
    o_ref[...] = x_ref[...] + mod[2:3] * (_rms(m_scr[...]) * gains_ref[1:2])


def _dft_tables(n):
    idx = np.arange(n, dtype=np.int64)
    ang = 2.0 * np.pi * ((idx[:, None] * idx[None, :]) % n).astype(np.float64) / n
    s = 1.0 / math.sqrt(n)
    return np.cos(ang) * s, np.sin(ang) * s


def _odd_mixer(x, mod, gains, w, n_seq, seq_len, mod_row0):
    cc, sc = _dft_tables(C_GROUP_DIM)
    cl, sl = _dft_tables(seq_len)
    csc, cl, sl = (jnp.asarray(a, F32).astype(BF16) for a in (np.concatenate([cc, sc], axis=1), cl, sl))
    full = lambda shape: pl.BlockSpec(shape, lambda i: (0,) * len(shape))
    row = lambda i: (i, 0)
    tm = max(seq_len, ODD_ROWS)
    assert tm % seq_len == 0 and (n_seq * seq_len) % tm == 0
    return pl.pallas_call(
        functools.partial(_odd_kernel, seq_len=seq_len),
        grid=(n_seq * seq_len // tm,),
        in_specs=[
            pl.BlockSpec((tm, D_MODEL), row),
            _mod_spec(tm, seq_len, mod_row0),
            full((4, D_MODEL)), full((C_GROUP_DIM, 2 * C_GROUP_DIM)),
            full((seq_len, seq_len)), full((seq_len, seq_len)), full((D_MODEL, D_MODEL)),
        ],
        out_specs=pl.BlockSpec((tm, D_MODEL), row),
        out_shape=jax.ShapeDtypeStruct(x.shape, F32),
        scratch_shapes=[pltpu.VMEM((tm, D_MODEL), BF16), pltpu.VMEM((tm, D_MODEL), BF16),
                        pltpu.VMEM((tm, D_MODEL), F32)],
        compiler_params=_cparams("parallel"),
        name="odd_mixer",
    )(x, mod, gains, csc, cl, sl, w)


def _ffn_kernel(x_ref, mod_ref, gains_ref, wu_ref, wg_ref, cw_ref, cb_ref, wo_ref, o_ref,
                h_scr, acc_scr, *, seq_len):
    j = pl.program_id(1)
    mod = mod_ref[0]

    @pl.when(j == 0)
    def _():
        h_scr[...] = _modnorm(x_ref[...], gains_ref[2:3], mod[4:5], mod[3:4]).astype(BF16)
        acc_scr[...] = jnp.zeros_like(acc_scr)

    n_sub = h_scr.shape[0] // FF_ROWS
    gate = [None] * n_sub
    zero_row = jnp.zeros((1, FF_CHUNK), F32)
    wu = wu_ref[...].astype(BF16)
    wg = wg_ref[...].astype(BF16)
    wo = wo_ref[...].astype(BF16)

    def sub_tile(i):
        rows = pl.ds(i * FF_ROWS, FF_ROWS)
        h = h_scr[rows, :]
        u = jnp.dot(h, wu, preferred_element_type=F32)
        g = gate[i] = jnp.dot(h, wg, preferred_element_type=F32)
        yield
        before = zero_row if (i * FF_ROWS) % seq_len == 0 else gate[i - 1][FF_ROWS - 1:FF_ROWS, :]
        after = zero_row if ((i + 1) * FF_ROWS) % seq_len == 0 else gate[i + 1][0:1, :]
        pos = lax.broadcasted_iota(jnp.int32, (FF_ROWS, 1), 0)
        prev = jnp.where(pos == 0, before, pltpu.roll(g, 1, 0))
        nxt = jnp.where(pos == FF_ROWS - 1, after, pltpu.roll(g, FF_ROWS - 1, 0))
        gc = prev * cw_ref[0:1] + g * cw_ref[1:2] + nxt * cw_ref[2:3] + cb_ref[...]
        act = (_silu(gc) * u).astype(BF16)
        yield
        acc_scr[rows, :] += jnp.dot(act, wo, preferred_element_type=F32)

    _round_robin(sub_tile(i) for i in range(n_sub))

    @pl.when(j == pl.num_programs(1) - 1)
    def _():
        o_ref[...] = x_ref[...] + mod[5:6] * (_rms(acc_scr[...]) * gains_ref[3:4])


def _conv_ffn(x, mod, gains, layer, w_in, conv_w, conv_b, w_out, seq_len, mod_row0, tm):
    t = x.shape[0]
    nj = D_FF // FF_CHUNK
    assert seq_len % FF_ROWS == 0 and tm % seq_len == 0
    return pl.pallas_call(
        functools.partial(_ffn_kernel, seq_len=seq_len),
        grid=(t // tm, nj),
        in_specs=[
            pl.BlockSpec((tm, D_MODEL), lambda i, j: (i, 0)),
            _mod_spec(tm, seq_len, mod_row0),
            pl.BlockSpec((4, D_MODEL), lambda i, j: (0, 0)),
            pl.BlockSpec((None, D_MODEL, FF_CHUNK), lambda i, j: (layer, 0, j)),
            pl.BlockSpec((None, D_MODEL, FF_CHUNK), lambda i, j: (layer, 0, nj + j)),
            pl.BlockSpec((3, FF_CHUNK), lambda i, j: (0, j)),
            pl.BlockSpec((1, FF_CHUNK), lambda i, j: (0, j)),
            pl.BlockSpec((None, FF_CHUNK, D_MODEL), lambda i, j: (layer, j, 0)),
        ],
        out_specs=pl.BlockSpec((tm, D_MODEL), lambda i, j: (i, 0)),
        out_shape=jax.ShapeDtypeStruct((t, D_MODEL), F32),
        scratch_shapes=[pltpu.VMEM((tm, D_MODEL), BF16), pltpu.VMEM((tm, D_MODEL), F32)],
        compiler_params=_cparams("parallel", "arbitrary"),
        name="conv_ffn",
    )(x, mod, gains, w_in, w_in, conv_w, conv_b, w_out)


def _rope_tables(seq_len):
    rows = seq_len // GRID_W
    row = jnp.repeat(jnp.arange(rows, dtype=F32), GRID_W)
    col = jnp.tile(jnp.arange(GRID_W, dtype=F32), rows)
    n_freq = A_HALF // 4
    inv = ROPE_BASE ** (-jnp.arange(n_freq, dtype=F32) / n_freq)
    ang = jnp.concatenate([row[:, None] * inv, col[:, None] * inv], axis=-1)
    cos_t = jnp.tile(jnp.repeat(jnp.cos(ang), 2, axis=1), (1, 2))
    sin_t = jnp.tile(jnp.repeat(jnp.sin(ang), 2, axis=1), (1, 2))
    sign = jnp.where(jnp.arange(A_HEAD_DIM) % 2 == 0, -1.0, 1.0).astype(F32)
    return cos_t, sin_t * sign


def _pad_lora(w):
    z = jnp.zeros_like(w[0])
    return jnp.stack([jnp.concatenate([w[0], z], axis=0), jnp.concatenate([z, w[1]], axis=0)])


def _block_diag(value, width):
    head = np.arange(width) // B_HEAD
    return jnp.asarray(np.where(head[:, None] == head[None, :], value, 0.0), BF16)


def _tile_rows(seq_len):
    return max(seq_len, 1024)


def kernel(x_prompt, x_sample, cache_k, cache_v, state_wkv, c, c_ctx, w_ada, b_ada, norm_gains,
           w_in_even, w_out_even, diff_lambda, diff_subln, rwkv_shift_mu, rwkv_w0, rwkv_w2, rwkv_a0,
           rwkv_a2, rwkv_g2, rwkv_kvec, rwkv_lnx, w_out_odd, w_ffn_in, ffn_conv, ffn_conv_b,
           w_ffn_out):
    n_ctx, l_ctx, _ = x_prompt.shape
    n_lat, l_lat, _ = x_sample.shape
    assert 1 + n_lat <= MOD_ROWS
    cvec = jnp.concatenate(
        [c_ctx[None, :], c, jnp.zeros((MOD_ROWS - 1 - n_lat, D_MODEL), F32)], axis=0)
    mods = _modulation(cvec, w_ada, b_ada).reshape(DEPTH, MOD_ROWS, 6, D_MODEL)

    per_layer_bf16 = lambda w: [w[i].astype(BF16) for i in range(w.shape[0])]
    w_in_even_b = per_layer_bf16(w_in_even)
    w_out_even_b = per_layer_bf16(w_out_even)
    w_out_odd_b = per_layer_bf16(w_out_odd)
    cos_t, sin_t = _rope_tables(l_lat)

    def run_group(x, n_seq, seq_len, mod_row0, latent):
        tm = _tile_rows(seq_len)
        ctx_out = None
        for l in range(DEPTH):
            mod = mods[l]
            gains = norm_gains[l]
            if l % 2 == 0:
                e = l // 2
                lam_init = 0.8 - 0.6 * math.exp(-0.3 * l)
                q, k, v, fb, *kv_cache = _even_in(
                    x, mod, gains, w_in_even_b[e], seq_len, mod_row0, min(seq_len, 512),
                    rope=(cos_t, sin_t) if latent else None, cache=not latent)
                if latent:
                    past = cache_k.shape[2]
                    ctx = (cache_k[:, e].reshape(n_seq * past, A_WIDTH),
                           cache_v[:, e].reshape(n_seq * past, A_WIDTH))
                    s0 = state_wkv[:, e]
                else:
                    ctx = None
                    s0 = None
                ya = _attention(q, k, v, diff_lambda[e], diff_subln[e][None, :], lam_init,
                                n_seq, seq_len, ctx)
                rwkv = {
                    "mu": rwkv_shift_mu[e], "w0": rwkv_w0[e], "w2": _pad_lora(rwkv_w2[e]).astype(BF16),
                    "a0": rwkv_a0[e], "a2": _pad_lora(rwkv_a2[e]).astype(BF16),
                    "g2": rwkv_g2[e].astype(BF16), "kvec": rwkv_kvec[e], "lnx": rwkv_lnx[e],
                }
                yb, s_fin = _rwkv_mix(fb, s0, rwkv, n_seq, seq_len)
                x = _even_out(x, mod, gains, ya, yb, w_out_even_b[e], seq_len, mod_row0, 1024)
                if not latent:
                    ctx_out = (*kv_cache, s_fin)
            else:
                x = _odd_mixer(x, mod, gains, w_out_odd_b[l // 2], n_seq, seq_len, mod_row0)
            x = _conv_ffn(x, mod, gains, l, w_ffn_in, ffn_conv[l], ffn_conv_b[l][None, :],
                          w_ffn_out, seq_len, mod_row0, tm)
        return x, ctx_out

    y_ctx, (k_new, v_new, s_new) = run_group(
        x_prompt.reshape(n_ctx * l_ctx, D_MODEL), n_ctx, l_ctx, 0, False)
    y_lat, _ = run_group(x_sample.reshape(n_lat * l_lat, D_MODEL), n_lat, l_lat, 1, True)

    n_even = (DEPTH + 1) // 2
    assert n_even == 1
    return (
        y_ctx.reshape(n_ctx, l_ctx, D_MODEL),
        y_lat.reshape(n_lat, l_lat, D_MODEL),
        k_new,
        v_new,
        s_new.reshape(n_ctx, n_even, 2, B_HEADS, B_HEAD, B_HEAD),
    )
```

```python
import functools
import math

import numpy as np
import jax
import jax.numpy as jnp
from jax import lax
from jax.experimental import pallas as pl
from jax.experimental.pallas import tpu as pltpu

F32 = jnp.float32
BF16 = jnp.bfloat16

D_MODEL = 1024
DEPTH = 2
GRID_W = 64
A_WIDTH = D_MODEL // 2
A_HEADS = 4
A_HEAD_DIM = A_WIDTH // A_HEADS
A_HALF = A_HEAD_DIM // 2
B_WIDTH = D_MODEL - A_WIDTH
B_HEAD = 64
B_HEADS = B_WIDTH // B_HEAD
LORA_W = 64
LORA_A = 64
LORA_G = 128
IN_B = 3 * B_WIDTH + 2 * LORA_W + 2 * LORA_A + LORA_G
IN_EVEN = 3 * A_WIDTH + IN_B
C_GROUPS = 8
C_GROUP_DIM = D_MODEL // C_GROUPS
D_FF = 2816
ROPE_BASE = 10000.0
RMS_EPS = 1e-6
LNX_EPS = 64e-5

LANES = 128
SUBLANES = 8
PAIR = 2 * B_HEAD
N_PAIRS = B_HEADS // 2
CHUNK = 64
FF_CHUNK = 256
FF_ROWS = 256
PREP_ROWS = 256
ATTN_Q_ROWS = 256
ODD_ROWS = 1024
ODD_COLS = 256
VMEM_LIMIT = 48 * 1024 * 1024
MOD_ROWS = 8


def _cparams(*sem):
    return pltpu.CompilerParams(dimension_semantics=sem, vmem_limit_bytes=VMEM_LIMIT)


def _sigmoid(x):
    return 1.0 / (1.0 + jnp.exp(-x))


def _silu(x):
    return x * (0.5 + 0.5 * jnp.tanh(0.5 * x))


def _dot(a, b):
    return jnp.dot(a.astype(BF16), b.astype(BF16), preferred_element_type=F32)


_NN = (((1,), (0,)), ((), ()))
_NT = (((1,), (1,)), ((), ()))
_TN = (((0,), (0,)), ((), ()))


def _split_bf16(x, n):
    parts = []
    rem = x
    for i in range(n):
        p = rem.astype(BF16)
        parts.append(p)
        if i + 1 < n:
            rem = rem - p.astype(F32)
    return parts


def _mm(a, b, dims):
    return lax.dot_general(a.astype(BF16), b.astype(BF16), dims, preferred_element_type=F32)


def _segsum(x, bd, pieces=2):
    acc = None
    for p in _split_bf16(x, pieces):
        t = jnp.dot(p, bd, preferred_element_type=F32)
        acc = t if acc is None else acc + t
    return acc


def _rms(x):
    return x * lax.rsqrt(jnp.mean(x * x, axis=-1, keepdims=True) + RMS_EPS)


def _modnorm(x, gain, scale, shift):
    return _rms(x) * gain * (1.0 + scale) + shift


def _mod_kernel(c_ref, w_ref, b_ref, o_ref):
    c = c_ref[...]
    s = c * _sigmoid(c)
    o_ref[0] = _dot(s, w_ref[0]) + b_ref[0]


def _modulation(cvec, w_ada, b_ada):
    tn = 1536
    n = 6 * D_MODEL
    return pl.pallas_call(
        _mod_kernel,
        grid=(DEPTH, n // tn),
        in_specs=[
            pl.BlockSpec((MOD_ROWS, D_MODEL), lambda l, j: (0, 0)),
            pl.BlockSpec((1, D_MODEL, tn), lambda l, j: (l, 0, j)),
            pl.BlockSpec((1, 1, tn), lambda l, j: (l, 0, j)),
        ],
        out_specs=pl.BlockSpec((1, MOD_ROWS, tn), lambda l, j: (l, 0, j)),
        out_shape=jax.ShapeDtypeStruct((DEPTH, MOD_ROWS, n), F32),
        compiler_params=_cparams("parallel", "parallel"),
        name="modulation",
    )(cvec, w_ada, b_ada.reshape(DEPTH, 1, n))


def _mod_spec(tm, seq_len, mod_row0):
    if mod_row0 == 0:
        return pl.BlockSpec((1, 6, D_MODEL), lambda i, *_: (0, 0, 0))
    return pl.BlockSpec((1, 6, D_MODEL), lambda i, *_: (mod_row0 + (i * tm) // seq_len, 0, 0))


def _rope(x, cos_t, sin_t):
    lane = lax.broadcasted_iota(jnp.int32, x.shape, 1)
    nxt = pltpu.roll(x, LANES - 1, 1)
    prv = pltpu.roll(x, 1, 1)
    partner = jnp.where(lane % 2 == 0, nxt, prv)
    return x * cos_t + partner * sin_t


def _even_in_kernel(*refs, rope, cache):
    x_ref, mod_ref, gains_ref, w_ref = refs[:4]
    refs = refs[4:]
    if rope:
        cos_ref, sin_ref = refs[:2]
        refs = refs[2:]
    q_ref, k_ref, v_ref, fb_ref = refs[:4]
    cache_refs = refs[4:]
    mod = mod_ref[0]
    h = _modnorm(x_ref[...], gains_ref[0:1], mod[1:2], mod[0:1]).astype(BF16)
    a = A_WIDTH
    hd = A_HEAD_DIM
    for n, out_ref in enumerate((q_ref, k_ref, v_ref)):
        y = jnp.dot(h, w_ref[:, n * a:(n + 1) * a], preferred_element_type=F32)
        if cache and n > 0:
            for hh in range(A_HEADS):
                cache_refs[n - 1][0, 0, :, hh, :] = y[:, hh * hd:(hh + 1) * hd]
        if rope and n < 2:
            for hh in range(A_HEADS):
                cols = slice(hh * hd, (hh + 1) * hd)
                out_ref[:, cols] = _rope(y[:, cols], cos_ref[...], sin_ref[...]).astype(BF16)
        else:
            out_ref[...] = y.astype(BF16)
    fb_ref[...] = jnp.dot(h, w_ref[:, 3 * a:], preferred_element_type=F32)


def _even_in(x, mod, gains, w, seq_len, mod_row0, tm, rope=None, cache=False):
    t = x.shape[0]
    assert seq_len % tm == 0
    per_seq = seq_len // tm
    row = lambda i: (i, 0)
    operands = [x, mod, gains, w]
    in_specs = [
        pl.BlockSpec((tm, D_MODEL), row),
        _mod_spec(tm, seq_len, mod_row0),
        pl.BlockSpec((4, D_MODEL), lambda i: (0, 0)),
        pl.BlockSpec((D_MODEL, IN_EVEN), lambda i: (0, 0)),
    ]
    if rope is not None:
        tab = pl.BlockSpec((tm, A_HEAD_DIM), lambda i: (i % per_seq, 0))
        operands += list(rope)
        in_specs += [tab, tab]
    out_specs = [pl.BlockSpec((tm, A_WIDTH), row)] * 3 + [pl.BlockSpec((tm, IN_B), row)]
    out_shape = [jax.ShapeDtypeStruct((t, A_WIDTH), BF16)] * 3 + [jax.ShapeDtypeStruct((t, IN_B), F32)]
    if cache:
        out_specs += [pl.BlockSpec((1, 1, tm, A_HEADS, A_HEAD_DIM),
                                   lambda i: (i // per_seq, 0, i % per_seq, 0, 0))] * 2
        out_shape += [jax.ShapeDtypeStruct((t // seq_len, 1, seq_len, A_HEADS, A_HEAD_DIM), F32)] * 2
    return pl.pallas_call(
        functools.partial(_even_in_kernel, rope=rope is not None, cache=cache),
        grid=(t // tm,),
        in_specs=in_specs,
        out_specs=out_specs,
        out_shape=out_shape,
        compiler_params=_cparams("parallel"),
        name="even_in",
    )(*operands)


def _attn_kernel(*refs, has_ctx, lam_init):
    if has_ctx:
        lam_ref, sub_ref, q_ref, k_ref, v_ref, ck_ref, cv_ref, o_ref = refs
    else:
        lam_ref, sub_ref, q_ref, k_ref, v_ref, o_ref = refs
    lp = lam_ref[...]
    l1 = jnp.sum(lp[0:1] * lp[1:2], axis=-1, keepdims=True)
    l2 = jnp.sum(lp[2:3] * lp[3:4], axis=-1, keepdims=True)
    lam = jnp.exp(l1) - jnp.exp(l2) + lam_init
    hd = A_HEAD_DIM
    lane = lax.broadcasted_iota(jnp.int32, (q_ref.shape[0], hd), 1)
    scale = A_HALF ** -0.5

    def head(h):
        cols = pl.ds(h * hd, hd)
        q = q_ref[:, cols]
        k = k_ref[:, cols].astype(BF16)
        if has_ctx:
            ck = ck_ref[:, cols].astype(BF16)
        scores = []
        for m in range(2):
            qm = jnp.where((lane < A_HALF) == (m == 0), q, 0.0).astype(BF16)
            s = lax.dot_general(qm, k, _NT, preferred_element_type=F32) * scale
            sc = lax.dot_general(qm, ck, _NT, preferred_element_type=F32) * scale if has_ctx else None
            scores.append((s, sc))
        yield
        probs = []
        for s, sc in scores:
            mx = jnp.max(s, axis=-1, keepdims=True)
            if has_ctx:
                mx = jnp.maximum(mx, jnp.max(sc, axis=-1, keepdims=True))
                ec = jnp.exp(sc - mx)
            e = jnp.exp(s - mx)
            z = jnp.sum(e, axis=-1, keepdims=True)
            if has_ctx:
                z = z + jnp.sum(ec, axis=-1, keepdims=True)
            inv = 1.0 / z
            probs.append((e * inv, ec * inv if has_ctx else None))
        w = (probs[0][0] - lam * probs[1][0]).astype(BF16)
        if has_ctx:
            wc = (probs[0][1] - lam * probs[1][1]).astype(BF16)
        yield
        o = jnp.dot(w, v_ref[:, cols].astype(BF16), preferred_element_type=F32)
        if has_ctx:
            o = o + jnp.dot(wc, cv_ref[:, cols].astype(BF16), preferred_element_type=F32)
        yield
        o_ref[:, cols] = (_rms(o) * sub_ref[...] * (1.0 - lam_init)).astype(BF16)

    _round_robin(head(h) for h in range(A_HEADS))


def _attention(q, k, v, lam_p, subln, lam_init, n_seq, seq_len, ctx=None):
    t = q.shape[0]
    tq = min(seq_len, ATTN_Q_ROWS if ctx is None else ATTN_Q_ROWS // 2)
    nq = seq_len // tq
    small = [pl.BlockSpec((4, A_HALF), lambda b, i: (0, 0)),
             pl.BlockSpec((1, A_HEAD_DIM), lambda b, i: (0, 0))]
    qblk = pl.BlockSpec((tq, A_WIDTH), lambda b, i: (b * nq + i, 0))
    kblk = pl.BlockSpec((seq_len, A_WIDTH), lambda b, i: (b, 0))
    operands = [lam_p, subln, q, k, v]
    in_specs = small + [qblk, kblk, kblk]
    if ctx is not None:
        ck, cv = ctx
        cblk = pl.BlockSpec((ck.shape[0] // n_seq, A_WIDTH), lambda b, i: (b, 0))
        operands += [ck, cv]
        in_specs += [cblk, cblk]
    return pl.pallas_call(
        functools.partial(_attn_kernel, has_ctx=ctx is not None, lam_init=lam_init),
        grid=(n_seq, nq),
        in_specs=in_specs,
        out_specs=qblk,
        out_shape=jax.ShapeDtypeStruct((t, A_WIDTH), BF16),
        compiler_params=_cparams("parallel", "parallel"),
        name="diff_attn",
    )(*operands)


def _cumsum_rows(x, reverse):
    n = x.shape[0]
    ridx = lax.broadcasted_iota(jnp.int32, x.shape, 0)
    s = 1
    while s < n:
        if reverse:
            x = x + jnp.where(ridx < n - s, pltpu.roll(x, n - s, 0), 0.0)
        else:
            x = x + jnp.where(ridx >= s, pltpu.roll(x, s, 0), 0.0)
        s *= 2
    return x


def _round_robin(gens):
    gens = list(gens)
    while gens:
        alive = []
        for gen in gens:
            try:
                next(gen)
                alive.append(gen)
            except StopIteration:
                pass
        gens = alive


def _rwkv_kernel(*refs, seq_len, has_state, pairs, unroll):
    (fr_ref, fk_ref, fv_ref, fl_ref, mur_ref, muk_ref, muv_ref, mul_ref, w0_ref, w2_ref, a0_ref,
     a2_ref, g2_ref, kvec_ref, lnx_ref, bd_ref, bdm_ref) = refs[:17]
    refs = refs[17:]
    if has_state:
        s0_ref = refs[0]
        refs = refs[1:]
    yb_ref, sf_ref, pre_ref, g_scr, bonus_scr, r1_scr, m1_scr, d0_scr, gam_scr, y_scr, s_scr = refs

    def shifted(ref, mu_ref, r0):
        f = ref[r0:r0 + PREP_ROWS, :]
        zero = jnp.zeros((1, f.shape[1]), F32)
        before = zero if r0 == 0 else ref[r0 - 1:r0, :]
        after = zero if r0 + PREP_ROWS == seq_len else ref[r0 + PREP_ROWS:r0 + PREP_ROWS + 1, :]
        local = lax.broadcasted_iota(jnp.int32, (PREP_ROWS, 1), 0)
        prev = jnp.where(local == 0, before, pltpu.roll(f, 1, 0))
        nxt = jnp.where(local == PREP_ROWS - 1, after, pltpu.roll(f, PREP_ROWS - 1, 0))
        return f * (1.0 - mu_ref[0:1] - mu_ref[1:2]) + mu_ref[0:1] * prev + mu_ref[1:2] * nxt

    bd = bd_ref[...]
    kv = kvec_ref[...]
    for r0 in range(0, seq_len, PREP_ROWS):
        rows = slice(r0, r0 + PREP_ROWS)
        r = shifted(fr_ref, mur_ref, r0)
        k = shifted(fk_ref, muk_ref, r0)
        v = shifted(fv_ref, muv_ref, r0)
        lora = shifted(fl_ref, mul_ref, r0)
        wd = jnp.tanh(lora[:, 0:2 * LORA_W])
        ad = lora[:, 2 * LORA_W:2 * LORA_W + 2 * LORA_A]
        gd = lora[:, 2 * LORA_W + 2 * LORA_A:]
        g_scr[rows, :] = _dot(_sigmoid(gd), g2_ref[...])
        kk = k * kv[0:1]
        kk = kk * lax.rsqrt(_segsum(kk * kk, bd, pieces=1) + 1e-12)
        pre_ref[0, rows, :] = r
        pre_ref[1, rows, :] = v
        pre_ref[2, rows, :] = kk
        ksum = None
        for d in range(2):
            w_raw = w0_ref[d:d + 1] + _dot(wd, w2_ref[d])
            pre_ref[7 + d, rows, :] = -math.exp(-0.5) * _sigmoid(w_raw)
            a = _sigmoid(a0_ref[d:d + 1] + _dot(ad, a2_ref[d]))
            kd = k * (1.0 + (a - 1.0) * kv[1:2])
            pre_ref[3 + d, rows, :] = kd
            pre_ref[5 + d, rows, :] = kk * a
            ksum = kd if ksum is None else ksum + kd
        bonus_scr[rows, :] = _segsum(r * ksum * kv[2:3], bd) * v

    nc = seq_len // CHUNK
    c2 = 2 * CHUNK
    lane = lax.broadcasted_iota(jnp.int32, (CHUNK, PAIR), 1)
    head0 = lane < B_HEAD
    row = lax.broadcasted_iota(jnp.int32, (c2, c2), 0)
    col = lax.broadcasted_iota(jnp.int32, (c2, c2), 1)
    same = (row // CHUNK) == (col // CHUNK)
    eye = jnp.where(row == col, 1.0, 0.0)
    xor_rc = jnp.bitwise_xor(row, col)
    mm = _mm

    def stack(x):
        return jnp.concatenate([jnp.where(head0, x, 0.0), jnp.where(head0, 0.0, x)], axis=0)

    def fold(x):
        return x[0:CHUNK] + x[CHUNK:c2]

    def chunk_rows(cc):
        return pl.ds(pl.multiple_of(cc * CHUNK, CHUNK), CHUNK)

    def chunk_local(cc, d, q):
        rows = chunk_rows(cc)
        lanes = pl.ds(q * PAIR, PAIR)
        z = 2 * q + d
        r = pre_ref[0, rows, lanes]
        v = pre_ref[1, rows, lanes]
        kk = pre_ref[2, rows, lanes]
        k = pre_ref[3 + d, rows, lanes]
        b = pre_ref[5 + d, rows, lanes]
        lw = pre_ref[7 + d, rows, lanes]
        incl = same & ((col <= row) if d == 0 else (col >= row))
        strict = same & ((col < row) if d == 0 else (col > row))
        g = _cumsum_rows(lw, reverse=(d == 1))
        gp = g - lw
        gtot = g[CHUNK - 1:CHUNK, :] if d == 0 else g[0:1, :]
        gm = 0.5 * gtot
        kx = k * jnp.exp(gm - g)
        bx = b * jnp.exp(gm - g)
        lhs = jnp.concatenate([stack(r * jnp.exp(g - gm)), stack(kk * jnp.exp(gp - gm))], axis=0)
        rhs = jnp.concatenate([kx, kx, bx, bx], axis=0)
        a_all = mm(lhs, rhs, _NT)
        yield
        a_rk = jnp.where(incl, a_all[0:c2, 0:c2], 0.0)
        a_rb = jnp.where(incl, a_all[0:c2, c2:2 * c2], 0.0)
        a_kk = jnp.where(strict, a_all[c2:2 * c2, 0:c2], 0.0)
        n_mat = jnp.where(strict, a_all[c2:2 * c2, c2:2 * c2], 0.0)
        vs = stack(v)
        akv = mm(a_kk, vs, _NN)
        y0 = mm(a_rk, vs, _NN)
        lower = (col < row) if d == 0 else (col > row)
        p_inv = eye - jnp.where(lower & (xor_rc == 1), n_mat, 0.0)
        yield
        s = 2
        while s < CHUNK:
            c_s = jnp.where(lower & (xor_rc >= s) & (xor_rc < 2 * s), n_mat, 0.0)
            t = mm(c_s, p_inv, _NN)
            yield
            p_inv = p_inv - mm(p_inv, t, _NN)
            yield
            s *= 2
        x = jnp.concatenate([stack(kk * jnp.exp(gp)), akv], axis=1)
        w12 = mm(p_inv, x, _NN)
        yield
        arb_w = mm(a_rb, w12, _NN)
        et = jnp.exp(gtot - g)
        bes = stack(b * et)
        d0_scr[z, cc] = mm(jnp.concatenate([vs, -w12[:, PAIR:]], axis=0),
                           jnp.concatenate([stack(k * et), bes], axis=0), _TN)
        m1_scr[z, cc] = mm(w12[:, 0:PAIR], bes, _TN).astype(BF16)
        yield
        r1_scr[z, cc] = (stack(r * jnp.exp(g)) - arb_w[:, 0:PAIR]).astype(BF16)
        y_scr[z, rows, :] = fold(y0 - arb_w[:, PAIR:])
        gam_scr[z, cc] = jnp.broadcast_to(jnp.exp(gtot), (SUBLANES, PAIR))

    def local_body(grp, carry):
        _round_robin(chunk_local(grp * unroll + u, d, q)
                     for u in range(unroll) for q in range(pairs) for d in range(2))
        return carry

    lax.fori_loop(0, nc // unroll, local_body, 0)

    for z in range(2 * pairs):
        s_scr[z] = jnp.zeros((PAIR, PAIR), F32)
        if has_state:
            s_scr[z, 0:B_HEAD, 0:B_HEAD] = s0_ref[0, z % 2, 2 * (z // 2)]
            s_scr[z, B_HEAD:PAIR, B_HEAD:PAIR] = s0_ref[0, z % 2, 2 * (z // 2) + 1]

    def carry_body(grp, carry):
        states = [s_scr[z] for z in range(2 * pairs)]
        for u in range(unroll):
            c = grp * unroll + u
            for z in range(2 * pairs):
                cc = c if z % 2 == 0 else nc - 1 - c
                rows = chunk_rows(cc)
                s0 = states[z]
                y_scr[z, rows, :] += fold(mm(r1_scr[z, cc], s0, _NT))
                states[z] = s0 * gam_scr[z, cc, 0:1, :] - mm(s0, m1_scr[z, cc], _NN) + d0_scr[z, cc]
        for z in range(2 * pairs):
            s_scr[z] = states[z]
        return carry

    lax.fori_loop(0, nc // unroll, carry_body, 0)
    bdm = bdm_ref[...]
    for r0 in range(0, seq_len, PREP_ROWS):
        rows = slice(r0, r0 + PREP_ROWS)
        ys = jnp.concatenate([y_scr[2 * q, rows, :] + y_scr[2 * q + 1, rows, :] for q in range(pairs)],
                             axis=1)
        dlt = ys - _segsum(ys, bdm)
        yn = dlt * lax.rsqrt(_segsum(dlt * dlt, bdm, pieces=1) + LNX_EPS)
        yb = (yn * lnx_ref[0:1] + lnx_ref[1:2] + bonus_scr[rows, :]) * g_scr[rows, :]
        yb_ref[rows, :] = yb.astype(BF16)
    for z in range(2 * pairs):
        sf_ref[0, z % 2, 2 * (z // 2)] = s_scr[z, 0:B_HEAD, 0:B_HEAD]
        sf_ref[0, z % 2, 2 * (z // 2) + 1] = s_scr[z, B_HEAD:PAIR, B_HEAD:PAIR]


def _rwkv_mix(fb, s0, p, n_seq, seq_len, pairs=2, unroll=4):
    t = fb.shape[0]
    nc = seq_len // CHUNK
    w = pairs * PAIR
    nw = B_WIDTH // w
    lora_w = 2 * LORA_W + 2 * LORA_A + LORA_G
    assert seq_len % PREP_ROWS == 0 and (3 * B_WIDTH) % lora_w == 0
    col = lambda rows, base: pl.BlockSpec((rows, w), lambda b, q: (0, base + q))
    seq = lambda base: pl.BlockSpec((seq_len, w), lambda b, q: (b, base + q))
    lora_blk = 3 * B_WIDTH // lora_w
    sblk = pl.BlockSpec((1, 2, 2 * pairs, B_HEAD, B_HEAD), lambda b, q: (b, 0, q, 0, 0))
    mat = lambda dtype: pltpu.VMEM((2 * pairs, nc, PAIR, PAIR), dtype)
    in_specs = [
        seq(0), seq(nw), seq(2 * nw),
        pl.BlockSpec((seq_len, lora_w), lambda b, q: (b, lora_blk)),
        col(2, 0), col(2, nw), col(2, 2 * nw),
        pl.BlockSpec((2, lora_w), lambda b, q: (0, lora_blk)),
        col(2, 0),
        pl.BlockSpec((2, 2 * LORA_W, w), lambda b, q: (0, 0, q)),
        col(2, 0),
        pl.BlockSpec((2, 2 * LORA_A, w), lambda b, q: (0, 0, q)),
        col(LORA_G, 0), col(3, 0), col(2, 0),
        pl.BlockSpec((w, w), lambda b, q: (0, 0)),
        pl.BlockSpec((w, w), lambda b, q: (0, 0)),
    ]
    operands = [fb, fb, fb, fb, p["mu"], p["mu"], p["mu"], p["mu"], p["w0"], p["w2"], p["a0"], p["a2"],
                p["g2"], p["kvec"], p["lnx"], _block_diag(1.0, w), _block_diag(1.0 / B_HEAD, w)]
    if s0 is not None:
        in_specs.append(sblk)
        operands.append(s0)
    return pl.pallas_call(
        functools.partial(_rwkv_kernel, seq_len=seq_len, has_state=s0 is not None, pairs=pairs,
                          unroll=unroll),
        grid=(n_seq, nw),
        in_specs=in_specs,
        out_specs=[pl.BlockSpec((seq_len, w), lambda b, q: (b, q)), sblk],
        out_shape=[
            jax.ShapeDtypeStruct((t, B_WIDTH), BF16),
            jax.ShapeDtypeStruct((n_seq, 2, B_HEADS, B_HEAD, B_HEAD), F32),
        ],
        scratch_shapes=[pltpu.VMEM((9, seq_len, w), F32), pltpu.VMEM((seq_len, w), F32),
                        pltpu.VMEM((seq_len, w), F32), mat(BF16), mat(BF16), mat(F32),
                        pltpu.VMEM((2 * pairs, nc, SUBLANES, PAIR), F32),
                        pltpu.VMEM((2 * pairs, seq_len, PAIR), F32),
                        pltpu.VMEM((2 * pairs, PAIR, PAIR), F32)],
        compiler_params=_cparams("parallel", "parallel"),
        name="rwkv_mix",
    )(*operands)


def _even_out_kernel(x_ref, mod_ref, gains_ref, ya_ref, yb_ref, w_ref, o_ref):
    m = (jnp.dot(ya_ref[...], w_ref[0:A_WIDTH, :], preferred_element_type=F32)
         + jnp.dot(yb_ref[...], w_ref[A_WIDTH:, :], preferred_element_type=F32))
    mod = mod_ref[0]
    o_ref[...] = x_ref[...] + mod[2:3] * (_rms(m) * gains_ref[1:2])


def _even_out(x, mod, gains, ya, yb, w, seq_len, mod_row0, tm):
    t = x.shape[0]
    row = lambda i: (i, 0)
    return pl.pallas_call(
        _even_out_kernel,
        grid=(t // tm,),
        in_specs=[
            pl.BlockSpec((tm, D_MODEL), row),
            _mod_spec(tm, seq_len, mod_row0),
            pl.BlockSpec((4, D_MODEL), lambda i: (0, 0)),
            pl.BlockSpec((tm, A_WIDTH), row),
            pl.BlockSpec((tm, B_WIDTH), row),
            pl.BlockSpec((D_MODEL, D_MODEL), lambda i: (0, 0)),
        ],
        out_specs=pl.BlockSpec((tm, D_MODEL), row),
        out_shape=jax.ShapeDtypeStruct((t, D_MODEL), F32),
        compiler_params=_cparams("parallel"),
        name="even_out",
    )(x, mod, gains, ya, yb, w)


def _odd_kernel(x_ref, mod_ref, gains_ref, csc_ref, cl_ref, sl_ref, w_ref, o_ref,
                tc_scr, ts_scr, m_scr, *, seq_len):
    mod = mod_ref[0]
    gd = C_GROUP_DIM
    n_seq = x_ref.shape[0] // seq_len
    csc = csc_ref[...]

    for s in range(n_seq):
        rows = pl.ds(s * seq_len, seq_len)
        h = _modnorm(x_ref[rows, :], gains_ref[0:1], mod[1:2], mod[0:1]).astype(BF16)
        for g in range(C_GROUPS):
            t = jnp.dot(h[:, g * gd:(g + 1) * gd], csc, preferred_element_type=F32)
            tc_scr[rows, g * gd:(g + 1) * gd] = t[:, 0:gd].astype(BF16)
            ts_scr[rows, g * gd:(g + 1) * gd] = t[:, gd:].astype(BF16)

    def position_dft(s, cb):
        rows = pl.ds(s * seq_len, seq_len)
        cols = pl.ds(cb * ODD_COLS, ODD_COLS)
        f = (jnp.dot(cl_ref[...], tc_scr[rows, cols], preferred_element_type=F32)
             - jnp.dot(sl_ref[...], ts_scr[rows, cols], preferred_element_type=F32))
        yield
        part = jnp.dot(f.astype(BF16), w_ref[cols, :], preferred_element_type=F32)
        yield
        if cb == 0:
            m_scr[rows, :] = part
        else:
            m_scr[rows, :] += part

    _round_robin(position_dft(s, cb) for s in range(n_seq) for cb in range(D_MODEL // ODD_COLS))
    o_ref[...] = x_ref[...] + mod[2:3] * (_rms(m_scr[...]) * gains_ref[1:2])


def _dft_tables(n):
    idx = np.arange(n, dtype=np.int64)
    ang = 2.0 * np.pi * ((idx[:, None] * idx[None, :]) % n).astype(np.float64) / n
    s = 1.0 / math.sqrt(n)
    return np.cos(ang) * s, np.sin(ang) * s


def _odd_mixer(x, mod, gains, w, n_seq, seq_len, mod_row0):
    cc, sc = _dft_tables(C_GROUP_DIM)
    cl, sl = _dft_tables(seq_len)
    csc, cl, sl = (jnp.asarray(a, F32).astype(BF16) for a in (np.concatenate([cc, sc], axis=1), cl, sl))
    full = lambda shape: pl.BlockSpec(shape, lambda i: (0,) * len(shape))
    row = lambda i: (i, 0)
    tm = max(seq_len, ODD_ROWS)
    assert tm % seq_len == 0 and (n_seq * seq_len) % tm == 0
    return pl.pallas_call(
        functools.partial(_odd_kernel, seq_len=seq_len),
        grid=(n_seq * seq_len // tm,),
        in_specs=[
            pl.BlockSpec((tm, D_MODEL), row),
            _mod_spec(tm, seq_len, mod_row0),
            full((4, D_MODEL)), full((C_GROUP_DIM, 2 * C_GROUP_DIM)),
            full((seq_len, seq_len)), full((seq_len, seq_len)), full((D_MODEL, D_MODEL)),
        ],
        out_specs=pl.BlockSpec((tm, D_MODEL), row),
        out_shape=jax.ShapeDtypeStruct(x.shape, F32),
        scratch_shapes=[pltpu.VMEM((tm, D_MODEL), BF16), pltpu.VMEM((tm, D_MODEL), BF16),
                        pltpu.VMEM((tm, D_MODEL), F32)],
        compiler_params=_cparams("parallel"),
        name="odd_mixer",
    )(x, mod, gains, csc, cl, sl, w)


def _ffn_kernel(x_ref, mod_ref, gains_ref, wu_ref, wg_ref, cw_ref, cb_ref, wo_ref, o_ref,
                h_scr, acc_scr, *, seq_len):
    j = pl.program_id(1)
    mod = mod_ref[0]

    @pl.when(j == 0)
    def _():
        h_scr[...] = _modnorm(x_ref[...], gains_ref[2:3], mod[4:5], mod[3:4]).astype(BF16)
        acc_scr[...] = jnp.zeros_like(acc_scr)

    n_sub = h_scr.shape[0] // FF_ROWS
    gate = [None] * n_sub
    zero_row = jnp.zeros((1, FF_CHUNK), F32)
    wu = wu_ref[...].astype(BF16)
    wg = wg_ref[...].astype(BF16)
    wo = wo_ref[...].astype(BF16)

    def sub_tile(i):
        rows = pl.ds(i * FF_ROWS, FF_ROWS)
        h = h_scr[rows, :]
        u = jnp.dot(h, wu, preferred_element_type=F32)
        g = gate[i] = jnp.dot(h, wg, preferred_element_type=F32)
        yield
        before = zero_row if (i * FF_ROWS) % seq_len == 0 else gate[i - 1][FF_ROWS - 1:FF_ROWS, :]
        after = zero_row if ((i + 1) * FF_ROWS) % seq_len == 0 else gate[i + 1][0:1, :]
        pos = lax.broadcasted_iota(jnp.int32, (FF_ROWS, 1), 0)
        prev = jnp.where(pos == 0, before, pltpu.roll(g, 1, 0))
        nxt = jnp.where(pos == FF_ROWS - 1, after, pltpu.roll(g, FF_ROWS - 1, 0))
        gc = prev * cw_ref[0:1] + g * cw_ref[1:2] + nxt * cw_ref[2:3] + cb_ref[...]
        act = (_silu(gc) * u).astype(BF16)
        yield
        acc_scr[rows, :] += jnp.dot(act, wo, preferred_element_type=F32)

    _round_robin(sub_tile(i) for i in range(n_sub))

    @pl.when(j == pl.num_programs(1) - 1)
    def _():
        o_ref[...] = x_ref[...] + mod[5:6] * (_rms(acc_scr[...]) * gains_ref[3:4])


def _conv_ffn(x, mod, gains, layer, w_in, conv_w, conv_b, w_out, seq_len, mod_row0, tm):
    t = x.shape[0]
    nj = D_FF // FF_CHUNK
    assert seq_len % FF_ROWS == 0 and tm % seq_len == 0
    return pl.pallas_call(
        functools.partial(_ffn_kernel, seq_len=seq_len),
        grid=(t // tm, nj),
        in_specs=[
            pl.BlockSpec((tm, D_MODEL), lambda i, j: (i, 0)),
            _mod_spec(tm, seq_len, mod_row0),
            pl.BlockSpec((4, D_MODEL), lambda i, j: (0, 0)),
            pl.BlockSpec((None, D_MODEL, FF_CHUNK), lambda i, j: (layer, 0, j)),
            pl.BlockSpec((None, D_MODEL, FF_CHUNK), lambda i, j: (layer, 0, nj + j)),
            pl.BlockSpec((3, FF_CHUNK), lambda i, j: (0, j)),
            pl.BlockSpec((1, FF_CHUNK), lambda i, j: (0, j)),
            pl.BlockSpec((None, FF_CHUNK, D_MODEL), lambda i, j: (layer, j, 0)),
        ],
        out_specs=pl.BlockSpec((tm, D_MODEL), lambda i, j: (i, 0)),
        out_shape=jax.ShapeDtypeStruct((t, D_MODEL), F32),
        scratch_shapes=[pltpu.VMEM((tm, D_MODEL), BF16), pltpu.VMEM((tm, D_MODEL), F32)],
        compiler_params=_cparams("parallel", "arbitrary"),
        name="conv_ffn",
    )(x, mod, gains, w_in, w_in, conv_w, conv_b, w_out)


def _rope_tables(seq_len):
    rows = seq_len // GRID_W
    row = jnp.repeat(jnp.arange(rows, dtype=F32), GRID_W)
    col = jnp.tile(jnp.arange(GRID_W, dtype=F32), rows)
    n_freq = A_HALF // 4
    inv = ROPE_BASE ** (-jnp.arange(n_freq, dtype=F32) / n_freq)
    ang = jnp.concatenate([row[:, None] * inv, col[:, None] * inv], axis=-1)
    cos_t = jnp.tile(jnp.repeat(jnp.cos(ang), 2, axis=1), (1, 2))
    sin_t = jnp.tile(jnp.repeat(jnp.sin(ang), 2, axis=1), (1, 2))
    sign = jnp.where(jnp.arange(A_HEAD_DIM) % 2 == 0, -1.0, 1.0).astype(F32)
    return cos_t, sin_t * sign


def _pad_lora(w):
    z = jnp.zeros_like(w[0])
    return jnp.stack([jnp.concatenate([w[0], z], axis=0), jnp.concatenate([z, w[1]], axis=0)])


def _block_diag(value, width):
    head = np.arange(width) // B_HEAD
    return jnp.asarray(np.where(head[:, None] == head[None, :], value, 0.0), BF16)


def _tile_rows(seq_len):
    return max(seq_len, 1024)


def kernel(x_prompt, x_sample, cache_k, cache_v, state_wkv, c, c_ctx, w_ada, b_ada, norm_gains,
           w_in_even, w_out_even, diff_lambda, diff_subln, rwkv_shift_mu, rwkv_w0, rwkv_w2, rwkv_a0,
           rwkv_a2, rwkv_g2, rwkv_kvec, rwkv_lnx, w_out_odd, w_ffn_in, ffn_conv, ffn_conv_b,
           w_ffn_out):
    n_ctx, l_ctx, _ = x_prompt.shape
    n_lat, l_lat, _ = x_sample.shape
    assert 1 + n_lat <= MOD_ROWS
    cvec = jnp.concatenate(
        [c_ctx[None, :], c, jnp.zeros((MOD_ROWS - 1 - n_lat, D_MODEL), F32)], axis=0)
    mods = _modulation(cvec, w_ada, b_ada).reshape(DEPTH, MOD_ROWS, 6, D_MODEL)

    per_layer_bf16 = lambda w: [w[i].astype(BF16) for i in range(w.shape[0])]
    w_in_even_b = per_layer_bf16(w_in_even)
    w_out_even_b = per_layer_bf16(w_out_even)
    w_out_odd_b = per_layer_bf16(w_out_odd)
    cos_t, sin_t = _rope_tables(l_lat)

    def run_group(x, n_seq, seq_len, mod_row0, latent):
        tm = _tile_rows(seq_len)
        ctx_out = None
        for l in range(DEPTH):
            mod = mods[l]
            gains = norm_gains[l]
            if l % 2 == 0:
                e = l // 2
                lam_init = 0.8 - 0.6 * math.exp(-0.3 * l)
                q, k, v, fb, *kv_cache = _even_in(
                    x, mod, gains, w_in_even_b[e], seq_len, mod_row0, min(seq_len, 512),
                    rope=(cos_t, sin_t) if latent else None, cache=not latent)
                if latent:
                    past = cache_k.shape[2]
                    ctx = (cache_k[:, e].reshape(n_seq * past, A_WIDTH),
                           cache_v[:, e].reshape(n_seq * past, A_WIDTH))
                    s0 = state_wkv[:, e]
                else:
                    ctx = None
                    s0 = None
                ya = _attention(q, k, v, diff_lambda[e], diff_subln[e][None, :], lam_init,
                                n_seq, seq_len, ctx)
                rwkv = {
                    "mu": rwkv_shift_mu[e], "w0": rwkv_w0[e], "w2": _pad_lora(rwkv_w2[e]).astype(BF16),
                    "a0": rwkv_a0[e], "a2": _pad_lora(rwkv_a2[e]).astype(BF16),
                    "g2": rwkv_g2[e].astype(BF16), "kvec": rwkv_kvec[e], "lnx": rwkv_lnx[e],
                }
                yb, s_fin = _rwkv_mix(fb, s0, rwkv, n_seq, seq_len, pairs=4 if seq_len <= 256 else 2)
                x = _even_out(x, mod, gains, ya, yb, w_out_even_b[e], seq_len, mod_row0, 1024)
                if not latent:
                    ctx_out = (*kv_cache, s_fin)
            else:
                x = _odd_mixer(x, mod, gains, w_out_odd_b[l // 2], n_seq, seq_len, mod_row0)
            x = _conv_ffn(x, mod, gains, l, w_ffn_in, ffn_conv[l], ffn_conv_b[l][None, :],
                          w_ffn_out, seq_len, mod_row0, tm)
        return x, ctx_out

    y_ctx, (k_new, v_new, s_new) = run_group(
        x_prompt.reshape(n_ctx * l_ctx, D_MODEL), n_ctx, l_ctx, 0, False)
    y_lat, _ = run_group(x_sample.reshape(n_lat * l_lat, D_MODEL), n_lat, l_lat, 1, True)

    n_even = (DEPTH + 1) // 2
    assert n_even == 1
    return (
        y_ctx.reshape(n_ctx, l_ctx, D_MODEL),
        y_lat.reshape(n_lat, l_lat, D_MODEL),
        k_new,
        v_new,
        s_new.reshape(n_ctx, n_even, 2, B_HEADS, B_HEAD, B_HEAD),
    )
```

```python
import functools
import math

import numpy as np
import jax
import jax.numpy as jnp
from jax import lax
from jax.experimental import pallas as pl
from jax.experimental.pallas import tpu as pltpu

F32 = jnp.float32
BF16 = jnp.bfloat16

D_MODEL = 1024
DEPTH = 2
GRID_W = 64
A_WIDTH = D_MODEL // 2
A_HEADS = 4
A_HEAD_DIM = A_WIDTH // A_HEADS
A_HALF = A_HEAD_DIM // 2
B_WIDTH = D_MODEL - A_WIDTH
B_HEAD = 64
B_HEADS = B_WIDTH // B_HEAD
LORA_W = 64
LORA_A = 64
LORA_G = 128
IN_B = 3 * B_WIDTH + 2 * LORA_W + 2 * LORA_A + LORA_G
IN_EVEN = 3 * A_WIDTH + IN_B
C_GROUPS = 8
C_GROUP_DIM = D_MODEL // C_GROUPS
D_FF = 2816
ROPE_BASE = 10000.0
RMS_EPS = 1e-6
LNX_EPS = 64e-5

LANES = 128
SUBLANES = 8
PAIR = 2 * B_HEAD
N_PAIRS = B_HEADS // 2
CHUNK = 64
FF_CHUNK = 256
FF_ROWS = 256
PREP_ROWS = 256
ATTN_Q_ROWS = 256
ODD_ROWS = 1024
ODD_COLS = 256
VMEM_LIMIT = 48 * 1024 * 1024
MOD_ROWS = 8


def _cparams(*sem):
    return pltpu.CompilerParams(dimension_semantics=sem, vmem_limit_bytes=VMEM_LIMIT)


def _sigmoid(x):
    return 1.0 / (1.0 + jnp.exp(-x))


def _silu(x):
    return x * (0.5 + 0.5 * jnp.tanh(0.5 * x))


def _dot(a, b):
    return jnp.dot(a.astype(BF16), b.astype(BF16), preferred_element_type=F32)


_NN = (((1,), (0,)), ((), ()))
_NT = (((1,), (1,)), ((), ()))
_TN = (((0,), (0,)), ((), ()))


def _split_bf16(x, n):
    parts = []
    rem = x
    for i in range(n):
        p = rem.astype(BF16)
        parts.append(p)
        if i + 1 < n:
            rem = rem - p.astype(F32)
    return parts


def _mm(a, b, dims):
    return lax.dot_general(a.astype(BF16), b.astype(BF16), dims, preferred_element_type=F32)


def _segsum(x, bd, pieces=2):
    acc = None
    for p in _split_bf16(x, pieces):
        t = jnp.dot(p, bd, preferred_element_type=F32)
        acc = t if acc is None else acc + t
    return acc


def _rms(x):
    return x * lax.rsqrt(jnp.mean(x * x, axis=-1, keepdims=True) + RMS_EPS)


def _modnorm(x, gain, scale, shift):
    return _rms(x) * gain * (1.0 + scale) + shift


def _mod_kernel(c_ref, w_ref, b_ref, o_ref):
    c = c_ref[...]
    s = c * _sigmoid(c)
    o_ref[0] = _dot(s, w_ref[0]) + b_ref[0]


def _modulation(cvec, w_ada, b_ada):
    tn = 1536
    n = 6 * D_MODEL
    return pl.pallas_call(
        _mod_kernel,
        grid=(DEPTH, n // tn),
        in_specs=[
            pl.BlockSpec((MOD_ROWS, D_MODEL), lambda l, j: (0, 0)),
            pl.BlockSpec((1, D_MODEL, tn), lambda l, j: (l, 0, j)),
            pl.BlockSpec((1, 1, tn), lambda l, j: (l, 0, j)),
        ],
        out_specs=pl.BlockSpec((1, MOD_ROWS, tn), lambda l, j: (l, 0, j)),
        out_shape=jax.ShapeDtypeStruct((DEPTH, MOD_ROWS, n), F32),
        compiler_params=_cparams("parallel", "parallel"),
        name="modulation",
    )(cvec, w_ada, b_ada.reshape(DEPTH, 1, n))


def _mod_spec(tm, seq_len, mod_row0):
    if mod_row0 == 0:
        return pl.BlockSpec((1, 6, D_MODEL), lambda i, *_: (0, 0, 0))
    return pl.BlockSpec((1, 6, D_MODEL), lambda i, *_: (mod_row0 + (i * tm) // seq_len, 0, 0))


def _rope(x, cos_t, sin_t):
    lane = lax.broadcasted_iota(jnp.int32, x.shape, 1)
    nxt = pltpu.roll(x, LANES - 1, 1)
    prv = pltpu.roll(x, 1, 1)
    partner = jnp.where(lane % 2 == 0, nxt, prv)
    return x * cos_t + partner * sin_t


def _even_in_kernel(*refs, rope, cache):
    x_ref, mod_ref, gains_ref, w_ref = refs[:4]
    refs = refs[4:]
    if rope:
        cos_ref, sin_ref = refs[:2]
        refs = refs[2:]
    q_ref, k_ref, v_ref, fb_ref = refs[:4]
    cache_refs = refs[4:]
    mod = mod_ref[0]
    h = _modnorm(x_ref[...], gains_ref[0:1], mod[1:2], mod[0:1]).astype(BF16)
    a = A_WIDTH
    hd = A_HEAD_DIM
    for n, out_ref in enumerate((q_ref, k_ref, v_ref)):
        y = jnp.dot(h, w_ref[:, n * a:(n + 1) * a], preferred_element_type=F32)
        if cache and n > 0:
            for hh in range(A_HEADS):
                cache_refs[n - 1][0, 0, :, hh, :] = y[:, hh * hd:(hh + 1) * hd]
        if rope and n < 2:
            for hh in range(A_HEADS):
                cols = slice(hh * hd, (hh + 1) * hd)
                out_ref[:, cols] = _rope(y[:, cols], cos_ref[...], sin_ref[...]).astype(BF16)
        else:
            out_ref[...] = y.astype(BF16)
    fb_ref[...] = jnp.dot(h, w_ref[:, 3 * a:], preferred_element_type=F32)


def _even_in(x, mod, gains, w, seq_len, mod_row0, tm, rope=None, cache=False):
    t = x.shape[0]
    assert seq_len % tm == 0
    per_seq = seq_len // tm
    row = lambda i: (i, 0)
    operands = [x, mod, gains, w]
    in_specs = [
        pl.BlockSpec((tm, D_MODEL), row),
        _mod_spec(tm, seq_len, mod_row0),
        pl.BlockSpec((4, D_MODEL), lambda i: (0, 0)),
        pl.BlockSpec((D_MODEL, IN_EVEN), lambda i: (0, 0)),
    ]
    if rope is not None:
        tab = pl.BlockSpec((tm, A_HEAD_DIM), lambda i: (i % per_seq, 0))
        operands += list(rope)
        in_specs += [tab, tab]
    out_specs = [pl.BlockSpec((tm, A_WIDTH), row)] * 3 + [pl.BlockSpec((tm, IN_B), row)]
    out_shape = [jax.ShapeDtypeStruct((t, A_WIDTH), BF16)] * 3 + [jax.ShapeDtypeStruct((t, IN_B), F32)]
    if cache:
        out_specs += [pl.BlockSpec((1, 1, tm, A_HEADS, A_HEAD_DIM),
                                   lambda i: (i // per_seq, 0, i % per_seq, 0, 0))] * 2
        out_shape += [jax.ShapeDtypeStruct((t // seq_len, 1, seq_len, A_HEADS, A_HEAD_DIM), F32)] * 2
    return pl.pallas_call(
        functools.partial(_even_in_kernel, rope=rope is not None, cache=cache),
        grid=(t // tm,),
        in_specs=in_specs,
        out_specs=out_specs,
        out_shape=out_shape,
        compiler_params=_cparams("parallel"),
        name="even_in",
    )(*operands)


def _attn_kernel(*refs, has_ctx, lam_init):
    if has_ctx:
        lam_ref, sub_ref, q_ref, k_ref, v_ref, ck_ref, cv_ref, o_ref = refs
    else:
        lam_ref, sub_ref, q_ref, k_ref, v_ref, o_ref = refs
    lp = lam_ref[...]
    l1 = jnp.sum(lp[0:1] * lp[1:2], axis=-1, keepdims=True)
    l2 = jnp.sum(lp[2:3] * lp[3:4], axis=-1, keepdims=True)
    lam = jnp.exp(l1) - jnp.exp(l2) + lam_init
    hd = A_HEAD_DIM
    lane = lax.broadcasted_iota(jnp.int32, (q_ref.shape[0], hd), 1)
    scale = A_HALF ** -0.5

    def head(h):
        cols = pl.ds(h * hd, hd)
        q = q_ref[:, cols]
        k = k_ref[:, cols].astype(BF16)
        if has_ctx:
            ck = ck_ref[:, cols].astype(BF16)
        scores = []
        for m in range(2):
            qm = jnp.where((lane < A_HALF) == (m == 0), q, 0.0).astype(BF16)
            s = lax.dot_general(qm, k, _NT, preferred_element_type=F32) * scale
            sc = lax.dot_general(qm, ck, _NT, preferred_element_type=F32) * scale if has_ctx else None
            scores.append((s, sc))
        yield
        probs = []
        for s, sc in scores:
            mx = jnp.max(s, axis=-1, keepdims=True)
            if has_ctx:
                mx = jnp.maximum(mx, jnp.max(sc, axis=-1, keepdims=True))
                ec = jnp.exp(sc - mx)
            e = jnp.exp(s - mx)
            z = jnp.sum(e, axis=-1, keepdims=True)
            if has_ctx:
                z = z + jnp.sum(ec, axis=-1, keepdims=True)
            inv = 1.0 / z
            probs.append((e * inv, ec * inv if has_ctx else None))
        w = (probs[0][0] - lam * probs[1][0]).astype(BF16)
        if has_ctx:
            wc = (probs[0][1] - lam * probs[1][1]).astype(BF16)
        yield
        o = jnp.dot(w, v_ref[:, cols].astype(BF16), preferred_element_type=F32)
        if has_ctx:
            o = o + jnp.dot(wc, cv_ref[:, cols].astype(BF16), preferred_element_type=F32)
        yield
        o_ref[:, cols] = (_rms(o) * sub_ref[...] * (1.0 - lam_init)).astype(BF16)

    _round_robin(head(h) for h in range(A_HEADS))


def _attention(q, k, v, lam_p, subln, lam_init, n_seq, seq_len, ctx=None):
    t = q.shape[0]
    tq = min(seq_len, ATTN_Q_ROWS if ctx is None else ATTN_Q_ROWS // 2)
    nq = seq_len // tq
    small = [pl.BlockSpec((4, A_HALF), lambda b, i: (0, 0)),
             pl.BlockSpec((1, A_HEAD_DIM), lambda b, i: (0, 0))]
    qblk = pl.BlockSpec((tq, A_WIDTH), lambda b, i: (b * nq + i, 0))
    kblk = pl.BlockSpec((seq_len, A_WIDTH), lambda b, i: (b, 0))
    operands = [lam_p, subln, q, k, v]
    in_specs = small + [qblk, kblk, kblk]
    if ctx is not None:
        ck, cv = ctx
        cblk = pl.BlockSpec((ck.shape[0] // n_seq, A_WIDTH), lambda b, i: (b, 0))
        operands += [ck, cv]
        in_specs += [cblk, cblk]
    return pl.pallas_call(
        functools.partial(_attn_kernel, has_ctx=ctx is not None, lam_init=lam_init),
        grid=(n_seq, nq),
        in_specs=in_specs,
        out_specs=qblk,
        out_shape=jax.ShapeDtypeStruct((t, A_WIDTH), BF16),
        compiler_params=_cparams("parallel", "parallel"),
        name="diff_attn",
    )(*operands)


def _cumsum_rows(x, reverse):
    n = x.shape[0]
    ridx = lax.broadcasted_iota(jnp.int32, x.shape, 0)
    s = 1
    while s < n:
        if reverse:
            x = x + jnp.where(ridx < n - s, pltpu.roll(x, n - s, 0), 0.0)
        else:
            x = x + jnp.where(ridx >= s, pltpu.roll(x, s, 0), 0.0)
        s *= 2
    return x


def _half_rows(m, s, first):
    off = 0 if first else s
    return jnp.concatenate([m[b + off:b + off + s] for b in range(0, m.shape[0], 2 * s)], axis=0)


def _spread_rows(h, s, first):
    zero = jnp.zeros((s, h.shape[1]), h.dtype)
    pieces = []
    for i in range(h.shape[0] // s):
        blk = h[i * s:(i + 1) * s]
        pieces += [blk, zero] if first else [zero, blk]
    return jnp.concatenate(pieces, axis=0)


def _round_robin(gens):
    gens = list(gens)
    while gens:
        alive = []
        for gen in gens:
            try:
                next(gen)
                alive.append(gen)
            except StopIteration:
                pass
        gens = alive


def _rwkv_kernel(*refs, seq_len, has_state, pairs, unroll):
    (fr_ref, fk_ref, fv_ref, fl_ref, mur_ref, muk_ref, muv_ref, mul_ref, w0_ref, w2_ref, a0_ref,
     a2_ref, g2_ref, kvec_ref, lnx_ref, bd_ref, bdm_ref) = refs[:17]
    refs = refs[17:]
    if has_state:
        s0_ref = refs[0]
        refs = refs[1:]
    yb_ref, sf_ref, pre_ref, g_scr, bonus_scr, r1_scr, m1_scr, d0_scr, gam_scr, y_scr, s_scr = refs

    def shifted(ref, mu_ref, r0):
        f = ref[r0:r0 + PREP_ROWS, :]
        zero = jnp.zeros((1, f.shape[1]), F32)
        before = zero if r0 == 0 else ref[r0 - 1:r0, :]
        after = zero if r0 + PREP_ROWS == seq_len else ref[r0 + PREP_ROWS:r0 + PREP_ROWS + 1, :]
        local = lax.broadcasted_iota(jnp.int32, (PREP_ROWS, 1), 0)
        prev = jnp.where(local == 0, before, pltpu.roll(f, 1, 0))
        nxt = jnp.where(local == PREP_ROWS - 1, after, pltpu.roll(f, PREP_ROWS - 1, 0))
        return f * (1.0 - mu_ref[0:1] - mu_ref[1:2]) + mu_ref[0:1] * prev + mu_ref[1:2] * nxt

    bd = bd_ref[...]
    kv = kvec_ref[...]
    for r0 in range(0, seq_len, PREP_ROWS):
        rows = slice(r0, r0 + PREP_ROWS)
        r = shifted(fr_ref, mur_ref, r0)
        k = shifted(fk_ref, muk_ref, r0)
        v = shifted(fv_ref, muv_ref, r0)
        lora = shifted(fl_ref, mul_ref, r0)
        wd = jnp.tanh(lora[:, 0:2 * LORA_W])
        ad = lora[:, 2 * LORA_W:2 * LORA_W + 2 * LORA_A]
        gd = lora[:, 2 * LORA_W + 2 * LORA_A:]
        g_scr[rows, :] = _dot(_sigmoid(gd), g2_ref[...])
        kk = k * kv[0:1]
        kk = kk * lax.rsqrt(_segsum(kk * kk, bd, pieces=1) + 1e-12)
        pre_ref[0, rows, :] = r
        pre_ref[1, rows, :] = v
        pre_ref[2, rows, :] = kk
        ksum = None
        for d in range(2):
            w_raw = w0_ref[d:d + 1] + _dot(wd, w2_ref[d])
            pre_ref[7 + d, rows, :] = -math.exp(-0.5) * _sigmoid(w_raw)
            a = _sigmoid(a0_ref[d:d + 1] + _dot(ad, a2_ref[d]))
            kd = k * (1.0 + (a - 1.0) * kv[1:2])
            pre_ref[3 + d, rows, :] = kd
            pre_ref[5 + d, rows, :] = kk * a
            ksum = kd if ksum is None else ksum + kd
        bonus_scr[rows, :] = _segsum(r * ksum * kv[2:3], bd) * v

    nc = seq_len // CHUNK
    c2 = 2 * CHUNK
    lane = lax.broadcasted_iota(jnp.int32, (CHUNK, PAIR), 1)
    head0 = lane < B_HEAD
    row = lax.broadcasted_iota(jnp.int32, (c2, c2), 0)
    col = lax.broadcasted_iota(jnp.int32, (c2, c2), 1)
    same = (row // CHUNK) == (col // CHUNK)
    eye = jnp.where(row == col, 1.0, 0.0)
    xor_rc = jnp.bitwise_xor(row, col)
    mm = _mm

    def stack(x):
        return jnp.concatenate([jnp.where(head0, x, 0.0), jnp.where(head0, 0.0, x)], axis=0)

    def fold(x):
        return x[0:CHUNK] + x[CHUNK:c2]

    def chunk_rows(cc):
        return pl.ds(pl.multiple_of(cc * CHUNK, CHUNK), CHUNK)

    def chunk_local(cc, d, q):
        rows = chunk_rows(cc)
        lanes = pl.ds(q * PAIR, PAIR)
        z = 2 * q + d
        r = pre_ref[0, rows, lanes]
        v = pre_ref[1, rows, lanes]
        kk = pre_ref[2, rows, lanes]
        k = pre_ref[3 + d, rows, lanes]
        b = pre_ref[5 + d, rows, lanes]
        lw = pre_ref[7 + d, rows, lanes]
        incl = same & ((col <= row) if d == 0 else (col >= row))
        strict = same & ((col < row) if d == 0 else (col > row))
        g = _cumsum_rows(lw, reverse=(d == 1))
        gp = g - lw
        gtot = g[CHUNK - 1:CHUNK, :] if d == 0 else g[0:1, :]
        gm = 0.5 * gtot
        kx = k * jnp.exp(gm - g)
        bx = b * jnp.exp(gm - g)
        lhs = jnp.concatenate([stack(r * jnp.exp(g - gm)), stack(kk * jnp.exp(gp - gm))], axis=0)
        rhs = jnp.concatenate([kx, kx, bx, bx], axis=0)
        a_all = mm(lhs, rhs, _NT)
        yield
        a_rk = jnp.where(incl, a_all[0:c2, 0:c2], 0.0)
        a_rb = jnp.where(incl, a_all[0:c2, c2:2 * c2], 0.0)
        a_kk = jnp.where(strict, a_all[c2:2 * c2, 0:c2], 0.0)
        n_mat = jnp.where(strict, a_all[c2:2 * c2, c2:2 * c2], 0.0)
        vs = stack(v)
        akv = mm(a_kk, vs, _NN)
        y0 = mm(a_rk, vs, _NN)
        lower = (col < row) if d == 0 else (col > row)
        p_inv = eye - jnp.where(lower & (xor_rc == 1), n_mat, 0.0)
        yield
        s = 2
        while s < SUBLANES:
            c_s = jnp.where(lower & (xor_rc >= s) & (xor_rc < 2 * s), n_mat, 0.0)
            t = mm(c_s, p_inv, _NN)
            yield
            p_inv = p_inv - mm(p_inv, t, _NN)
            yield
            s *= 2
        while s < CHUNK:
            c_s = jnp.where(lower & (xor_rc >= s) & (xor_rc < 2 * s), n_mat, 0.0)
            t = _spread_rows(mm(_half_rows(c_s, s, d == 1), p_inv, _NN), s, d == 1)
            yield
            upd = mm(_half_rows(p_inv, s, d == 1), t, _NN)
            p_inv = p_inv - _spread_rows(upd, s, d == 1)
            yield
            s *= 2
        x = jnp.concatenate([stack(kk * jnp.exp(gp)), akv], axis=1)
        w12 = mm(p_inv, x, _NN)
        yield
        arb_w = mm(a_rb, w12, _NN)
        et = jnp.exp(gtot - g)
        bes = stack(b * et)
        d0_scr[z, cc] = mm(jnp.concatenate([vs, -w12[:, PAIR:]], axis=0),
                           jnp.concatenate([stack(k * et), bes], axis=0), _TN)
        m1_scr[z, cc] = mm(w12[:, 0:PAIR], bes, _TN).astype(BF16)
        yield
        r1_scr[z, cc] = (stack(r * jnp.exp(g)) - arb_w[:, 0:PAIR]).astype(BF16)
        y_scr[z, rows, :] = fold(y0 - arb_w[:, PAIR:])
        gam_scr[z, cc] = jnp.broadcast_to(jnp.exp(gtot), (SUBLANES, PAIR))

    def local_body(grp, carry):
        _round_robin(chunk_local(grp * unroll + u, d, q)
                     for u in range(unroll) for q in range(pairs) for d in range(2))
        return carry

    lax.fori_loop(0, nc // unroll, local_body, 0)

    for z in range(2 * pairs):
        s_scr[z] = jnp.zeros((PAIR, PAIR), F32)
        if has_state:
            s_scr[z, 0:B_HEAD, 0:B_HEAD] = s0_ref[0, z % 2, 2 * (z // 2)]
            s_scr[z, B_HEAD:PAIR, B_HEAD:PAIR] = s0_ref[0, z % 2, 2 * (z // 2) + 1]

    def carry_body(grp, carry):
        states = [s_scr[z] for z in range(2 * pairs)]
        for u in range(unroll):
            c = grp * unroll + u
            for z in range(2 * pairs):
                cc = c if z % 2 == 0 else nc - 1 - c
                rows = chunk_rows(cc)
                s0 = states[z]
                y_scr[z, rows, :] += fold(mm(r1_scr[z, cc], s0, _NT))
                states[z] = s0 * gam_scr[z, cc, 0:1, :] - mm(s0, m1_scr[z, cc], _NN) + d0_scr[z, cc]
        for z in range(2 * pairs):
            s_scr[z] = states[z]
        return carry

    lax.fori_loop(0, nc // unroll, carry_body, 0)
    bdm = bdm_ref[...]
    for r0 in range(0, seq_len, PREP_ROWS):
        rows = slice(r0, r0 + PREP_ROWS)
        ys = jnp.concatenate([y_scr[2 * q, rows, :] + y_scr[2 * q + 1, rows, :] for q in range(pairs)],
                             axis=1)
        dlt = ys - _segsum(ys, bdm)
        yn = dlt * lax.rsqrt(_segsum(dlt * dlt, bdm, pieces=1) + LNX_EPS)
        yb = (yn * lnx_ref[0:1] + lnx_ref[1:2] + bonus_scr[rows, :]) * g_scr[rows, :]
        yb_ref[rows, :] = yb.astype(BF16)
    for z in range(2 * pairs):
        sf_ref[0, z % 2, 2 * (z // 2)] = s_scr[z, 0:B_HEAD, 0:B_HEAD]
        sf_ref[0, z % 2, 2 * (z // 2) + 1] = s_scr[z, B_HEAD:PAIR, B_HEAD:PAIR]


def _rwkv_mix(fb, s0, p, n_seq, seq_len, pairs=2, unroll=4):
    t = fb.shape[0]
    nc = seq_len // CHUNK
    w = pairs * PAIR
    nw = B_WIDTH // w
    lora_w = 2 * LORA_W + 2 * LORA_A + LORA_G
    assert seq_len % PREP_ROWS == 0 and (3 * B_WIDTH) % lora_w == 0
    col = lambda rows, base: pl.BlockSpec((rows, w), lambda b, q: (0, base + q))
    seq = lambda base: pl.BlockSpec((seq_len, w), lambda b, q: (b, base + q))
    lora_blk = 3 * B_WIDTH // lora_w
    sblk = pl.BlockSpec((1, 2, 2 * pairs, B_HEAD, B_HEAD), lambda b, q: (b, 0, q, 0, 0))
    mat = lambda dtype: pltpu.VMEM((2 * pairs, nc, PAIR, PAIR), dtype)
    in_specs = [
        seq(0), seq(nw), seq(2 * nw),
        pl.BlockSpec((seq_len, lora_w), lambda b, q: (b, lora_blk)),
        col(2, 0), col(2, nw), col(2, 2 * nw),
        pl.BlockSpec((2, lora_w), lambda b, q: (0, lora_blk)),
        col(2, 0),
        pl.BlockSpec((2, 2 * LORA_W, w), lambda b, q: (0, 0, q)),
        col(2, 0),
        pl.BlockSpec((2, 2 * LORA_A, w), lambda b, q: (0, 0, q)),
        col(LORA_G, 0), col(3, 0), col(2, 0),
        pl.BlockSpec((w, w), lambda b, q: (0, 0)),
        pl.BlockSpec((w, w), lambda b, q: (0, 0)),
    ]
    operands = [fb, fb, fb, fb, p["mu"], p["mu"], p["mu"], p["mu"], p["w0"], p["w2"], p["a0"], p["a2"],
                p["g2"], p["kvec"], p["lnx"], _block_diag(1.0, w), _block_diag(1.0 / B_HEAD, w)]
    if s0 is not None:
        in_specs.append(sblk)
        operands.append(s0)
    return pl.pallas_call(
        functools.partial(_rwkv_kernel, seq_len=seq_len, has_state=s0 is not None, pairs=pairs,
                          unroll=unroll),
        grid=(n_seq, nw),
        in_specs=in_specs,
        out_specs=[pl.BlockSpec((seq_len, w), lambda b, q: (b, q)), sblk],
        out_shape=[
            jax.ShapeDtypeStruct((t, B_WIDTH), BF16),
            jax.ShapeDtypeStruct((n_seq, 2, B_HEADS, B_HEAD, B_HEAD), F32),
        ],
        scratch_shapes=[pltpu.VMEM((9, seq_len, w), F32), pltpu.VMEM((seq_len, w), F32),
                        pltpu.VMEM((seq_len, w), F32), mat(BF16), mat(BF16), mat(F32),
                        pltpu.VMEM((2 * pairs, nc, SUBLANES, PAIR), F32),
                        pltpu.VMEM((2 * pairs, seq_len, PAIR), F32),
                        pltpu.VMEM((2 * pairs, PAIR, PAIR), F32)],
        compiler_params=_cparams("parallel", "parallel"),
        name="rwkv_mix",
    )(*operands)


def _even_out_kernel(x_ref, mod_ref, gains_ref, ya_ref, yb_ref, w_ref, o_ref):
    m = (jnp.dot(ya_ref[...], w_ref[0:A_WIDTH, :], preferred_element_type=F32)
         + jnp.dot(yb_ref[...], w_ref[A_WIDTH:, :], preferred_element_type=F32))
    mod = mod_ref[0]
    o_ref[...] = x_ref[...] + mod[2:3] * (_rms(m) * gains_ref[1:2])


def _even_out(x, mod, gains, ya, yb, w, seq_len, mod_row0, tm):
    t = x.shape[0]
    row = lambda i: (i, 0)
    return pl.pallas_call(
        _even_out_kernel,
        grid=(t // tm,),
        in_specs=[
            pl.BlockSpec((tm, D_MODEL), row),
            _mod_spec(tm, seq_len, mod_row0),
            pl.BlockSpec((4, D_MODEL), lambda i: (0, 0)),
            pl.BlockSpec((tm, A_WIDTH), row),
            pl.BlockSpec((tm, B_WIDTH), row),
            pl.BlockSpec((D_MODEL, D_MODEL), lambda i: (0, 0)),
        ],
        out_specs=pl.BlockSpec((tm, D_MODEL), row),
        out_shape=jax.ShapeDtypeStruct((t, D_MODEL), F32),
        compiler_params=_cparams("parallel"),
        name="even_out",
    )(x, mod, gains, ya, yb, w)


def _odd_kernel(x_ref, mod_ref, gains_ref, csc_ref, cl_ref, sl_ref, w_ref, o_ref,
                tc_scr, ts_scr, m_scr, *, seq_len):
    mod = mod_ref[0]
    gd = C_GROUP_DIM
    n_seq = x_ref.shape[0] // seq_len
    csc = csc_ref[...]

    for s in range(n_seq):
        rows = pl.ds(s * seq_len, seq_len)
        h = _modnorm(x_ref[rows, :], gains_ref[0:1], mod[1:2], mod[0:1]).astype(BF16)
        for g in range(C_GROUPS):
            t = jnp.dot(h[:, g * gd:(g + 1) * gd], csc, preferred_element_type=F32)
            tc_scr[rows, g * gd:(g + 1) * gd] = t[:, 0:gd].astype(BF16)
            ts_scr[rows, g * gd:(g + 1) * gd] = t[:, gd:].astype(BF16)

    def position_dft(s, cb):
        rows = pl.ds(s * seq_len, seq_len)
        cols = pl.ds(cb * ODD_COLS, ODD_COLS)
        f = (jnp.dot(cl_ref[...], tc_scr[rows, cols], preferred_element_type=F32)
             - jnp.dot(sl_ref[...], ts_scr[rows, cols], preferred_element_type=F32))
        yield
        part = jnp.dot(f.astype(BF16), w_ref[cols, :], preferred_element_type=F32)
        yield
        if cb == 0:
            m_scr[rows, :] = part
        else:
            m_scr[rows, :] += part

    _round_robin(position_dft(s, cb) for s in range(n_seq) for cb in range(D_MODEL // ODD_COLS))
    o_ref[...] = x_ref[...] + mod[2:3] * (_rms(m_scr[...]) * gains_ref[1:2])


def _dft_tables(n):
    idx = np.arange(n, dtype=np.int64)
    ang = 2.0 * np.pi * ((idx[:, None] * idx[None, :]) % n).astype(np.float64) / n
    s = 1.0 / math.sqrt(n)
    return np.cos(ang) * s, np.sin(ang) * s


def _odd_mixer(x, mod, gains, w, n_seq, seq_len, mod_row0):
    cc, sc = _dft_tables(C_GROUP_DIM)
    cl, sl = _dft_tables(seq_len)
    csc, cl, sl = (jnp.asarray(a, F32).astype(BF16) for a in (np.concatenate([cc, sc], axis=1), cl, sl))
    full = lambda shape: pl.BlockSpec(shape, lambda i: (0,) * len(shape))
    row = lambda i: (i, 0)
    tm = max(seq_len, ODD_ROWS)
    assert tm % seq_len == 0 and (n_seq * seq_len) % tm == 0
    return pl.pallas_call(
        functools.partial(_odd_kernel, seq_len=seq_len),
        grid=(n_seq * seq_len // tm,),
        in_specs=[
            pl.BlockSpec((tm, D_MODEL), row),
            _mod_spec(tm, seq_len, mod_row0),
            full((4, D_MODEL)), full((C_GROUP_DIM, 2 * C_GROUP_DIM)),
            full((seq_len, seq_len)), full((seq_len, seq_len)), full((D_MODEL, D_MODEL)),
        ],
        out_specs=pl.BlockSpec((tm, D_MODEL), row),
        out_shape=jax.ShapeDtypeStruct(x.shape, F32),
        scratch_shapes=[pltpu.VMEM((tm, D_MODEL), BF16), pltpu.VMEM((tm, D_MODEL), BF16),
                        pltpu.VMEM((tm, D_MODEL), F32)],
        compiler_params=_cparams("parallel"),
        name="odd_mixer",
    )(x, mod, gains, csc, cl, sl, w)


def _ffn_kernel(x_ref, mod_ref, gains_ref, wu_ref, wg_ref, cw_ref, cb_ref, wo_ref, o_ref,
                h_scr, acc_scr, *, seq_len):
    j = pl.program_id(1)
    mod = mod_ref[0]

    @pl.when(j == 0)
    def _():
        h_scr[...] = _modnorm(x_ref[...], gains_ref[2:3], mod[4:5], mod[3:4]).astype(BF16)
        acc_scr[...] = jnp.zeros_like(acc_scr)

    n_sub = h_scr.shape[0] // FF_ROWS
    gate = [None] * n_sub
    zero_row = jnp.zeros((1, FF_CHUNK), F32)
    wu = wu_ref[...].astype(BF16)
    wg = wg_ref[...].astype(BF16)
    wo = wo_ref[...].astype(BF16)

    def sub_tile(i):
        rows = pl.ds(i * FF_ROWS, FF_ROWS)
        h = h_scr[rows, :]
        u = jnp.dot(h, wu, preferred_element_type=F32)
        g = gate[i] = jnp.dot(h, wg, preferred_element_type=F32)
        yield
        before = zero_row if (i * FF_ROWS) % seq_len == 0 else gate[i - 1][FF_ROWS - 1:FF_ROWS, :]
        after = zero_row if ((i + 1) * FF_ROWS) % seq_len == 0 else gate[i + 1][0:1, :]
        pos = lax.broadcasted_iota(jnp.int32, (FF_ROWS, 1), 0)
        prev = jnp.where(pos == 0, before, pltpu.roll(g, 1, 0))
        nxt = jnp.where(pos == FF_ROWS - 1, after, pltpu.roll(g, FF_ROWS - 1, 0))
        gc = prev * cw_ref[0:1] + g * cw_ref[1:2] + nxt * cw_ref[2:3] + cb_ref[...]
        act = (_silu(gc) * u).astype(BF16)
        yield
        acc_scr[rows, :] += jnp.dot(act, wo, preferred_element_type=F32)

    _round_robin(sub_tile(i) for i in range(n_sub))

    @pl.when(j == pl.num_programs(1) - 1)
    def _():
        o_ref[...] = x_ref[...] + mod[5:6] * (_rms(acc_scr[...]) * gains_ref[3:4])


def _conv_ffn(x, mod, gains, layer, w_in, conv_w, conv_b, w_out, seq_len, mod_row0, tm):
    t = x.shape[0]
    nj = D_FF // FF_CHUNK
    assert seq_len % FF_ROWS == 0 and tm % seq_len == 0
    return pl.pallas_call(
        functools.partial(_ffn_kernel, seq_len=seq_len),
        grid=(t // tm, nj),
        in_specs=[
            pl.BlockSpec((tm, D_MODEL), lambda i, j: (i, 0)),
            _mod_spec(tm, seq_len, mod_row0),
            pl.BlockSpec((4, D_MODEL), lambda i, j: (0, 0)),
            pl.BlockSpec((None, D_MODEL, FF_CHUNK), lambda i, j: (layer, 0, j)),
            pl.BlockSpec((None, D_MODEL, FF_CHUNK), lambda i, j: (layer, 0, nj + j)),
            pl.BlockSpec((3, FF_CHUNK), lambda i, j: (0, j)),
            pl.BlockSpec((1, FF_CHUNK), lambda i, j: (0, j)),
            pl.BlockSpec((None, FF_CHUNK, D_MODEL), lambda i, j: (layer, j, 0)),
        ],
        out_specs=pl.BlockSpec((tm, D_MODEL), lambda i, j: (i, 0)),
        out_shape=jax.ShapeDtypeStruct((t, D_MODEL), F32),
        scratch_shapes=[pltpu.VMEM((tm, D_MODEL), BF16), pltpu.VMEM((tm, D_MODEL), F32)],
        compiler_params=_cparams("parallel", "arbitrary"),
        name="conv_ffn",
    )(x, mod, gains, w_in, w_in, conv_w, conv_b, w_out)


def _rope_tables(seq_len):
    rows = seq_len // GRID_W
    row = jnp.repeat(jnp.arange(rows, dtype=F32), GRID_W)
    col = jnp.tile(jnp.arange(GRID_W, dtype=F32), rows)
    n_freq = A_HALF // 4
    inv = ROPE_BASE ** (-jnp.arange(n_freq, dtype=F32) / n_freq)
    ang = jnp.concatenate([row[:, None] * inv, col[:, None] * inv], axis=-1)
    cos_t = jnp.tile(jnp.repeat(jnp.cos(ang), 2, axis=1), (1, 2))
    sin_t = jnp.tile(jnp.repeat(jnp.sin(ang), 2, axis=1), (1, 2))
    sign = jnp.where(jnp.arange(A_HEAD_DIM) % 2 == 0, -1.0, 1.0).astype(F32)
    return cos_t, sin_t * sign


def _pad_lora(w):
    z = jnp.zeros_like(w[0])
    return jnp.stack([jnp.concatenate([w[0], z], axis=0), jnp.concatenate([z, w[1]], axis=0)])


def _block_diag(value, width):
    head = np.arange(width) // B_HEAD
    return jnp.asarray(np.where(head[:, None] == head[None, :], value, 0.0), BF16)


def _tile_rows(seq_len):
    return max(seq_len, 1024)


def kernel(x_prompt, x_sample, cache_k, cache_v, state_wkv, c, c_ctx, w_ada, b_ada, norm_gains,
           w_in_even, w_out_even, diff_lambda, diff_subln, rwkv_shift_mu, rwkv_w0, rwkv_w2, rwkv_a0,
           rwkv_a2, rwkv_g2, rwkv_kvec, rwkv_lnx, w_out_odd, w_ffn_in, ffn_conv, ffn_conv_b,
           w_ffn_out):
    n_ctx, l_ctx, _ = x_prompt.shape
    n_lat, l_lat, _ = x_sample.shape
    assert 1 + n_lat <= MOD_ROWS
    cvec = jnp.concatenate(
        [c_ctx[None, :], c, jnp.zeros((MOD_ROWS - 1 - n_lat, D_MODEL), F32)], axis=0)
    mods = _modulation(cvec, w_ada, b_ada).reshape(DEPTH, MOD_ROWS, 6, D_MODEL)

    per_layer_bf16 = lambda w: [w[i].astype(BF16) for i in range(w.shape[0])]
    w_in_even_b = per_layer_bf16(w_in_even)
    w_out_even_b = per_layer_bf16(w_out_even)
    w_out_odd_b = per_layer_bf16(w_out_odd)
    cos_t, sin_t = _rope_tables(l_lat)

    def run_group(x, n_seq, seq_len, mod_row0, latent):
        tm = _tile_rows(seq_len)
        ctx_out = None
        for l in range(DEPTH):
            mod = mods[l]
            gains = norm_gains[l]
            if l % 2 == 0:
                e = l // 2
                lam_init = 0.8 - 0.6 * math.exp(-0.3 * l)
                q, k, v, fb, *kv_cache = _even_in(
                    x, mod, gains, w_in_even_b[e], seq_len, mod_row0, min(seq_len, 512),
                    rope=(cos_t, sin_t) if latent else None, cache=not latent)
                if latent:
                    past = cache_k.shape[2]
                    ctx = (cache_k[:, e].reshape(n_seq * past, A_WIDTH),
                           cache_v[:, e].reshape(n_seq * past, A_WIDTH))
                    s0 = state_wkv[:, e]
                else:
                    ctx = None
                    s0 = None
                ya = _attention(q, k, v, diff_lambda[e], diff_subln[e][None, :], lam_init,
                                n_seq, seq_len, ctx)
                rwkv = {
                    "mu": rwkv_shift_mu[e], "w0": rwkv_w0[e], "w2": _pad_lora(rwkv_w2[e]).astype(BF16),
                    "a0": rwkv_a0[e], "a2": _pad_lora(rwkv_a2[e]).astype(BF16),
                    "g2": rwkv_g2[e].astype(BF16), "kvec": rwkv_kvec[e], "lnx": rwkv_lnx[e],
                }
                yb, s_fin = _rwkv_mix(fb, s0, rwkv, n_seq, seq_len, pairs=4 if seq_len <= 256 else 2)
                x = _even_out(x, mod, gains, ya, yb, w_out_even_b[e], seq_len, mod_row0, 1024)
                if not latent:
                    ctx_out = (*kv_cache, s_fin)
            else:
                x = _odd_mixer(x, mod, gains, w_out_odd_b[l // 2], n_seq, seq_len, mod_row0)
            x = _conv_ffn(x, mod, gains, l, w_ffn_in, ffn_conv[l], ffn_conv_b[l][None, :],
                          w_ffn_out, seq_len, mod_row0, tm)
        return x, ctx_out

    y_ctx, (k_new, v_new, s_new) = run_group(
        x_prompt.reshape(n_ctx * l_ctx, D_MODEL), n_ctx, l_ctx, 0, False)
    y_lat, _ = run_group(x_sample.reshape(n_lat * l_lat, D_MODEL), n_lat, l_lat, 1, True)

    n_even = (DEPTH + 1) // 2
    assert n_even == 1
    return (
        y_ctx.reshape(n_ctx, l_ctx, D_MODEL),
        y_lat.reshape(n_lat, l_lat, D_MODEL),
        k_new,
        v_new,
        s_new.reshape(n_ctx, n_even, 2, B_HEADS, B_HEAD, B_HEAD),
    )
```

```python
import functools
import math

import numpy as np
import jax
import jax.numpy as jnp
from jax import lax
from jax.experimental import pallas as pl
from jax.experimental.pallas import tpu as pltpu

F32 = jnp.float32
BF16 = jnp.bfloat16

D_MODEL = 1024
DEPTH = 2
GRID_W = 64
A_WIDTH = D_MODEL // 2
A_HEADS = 4
A_HEAD_DIM = A_WIDTH // A_HEADS
A_HALF = A_HEAD_DIM // 2
B_WIDTH = D_MODEL - A_WIDTH
B_HEAD = 64
B_HEADS = B_WIDTH // B_HEAD
LORA_W = 64
LORA_A = 64
LORA_G = 128
IN_B = 3 * B_WIDTH + 2 * LORA_W + 2 * LORA_A + LORA_G
IN_EVEN = 3 * A_WIDTH + IN_B
C_GROUPS = 8
C_GROUP_DIM = D_MODEL // C_GROUPS
D_FF = 2816
ROPE_BASE = 10000.0
RMS_EPS = 1e-6
LNX_EPS = 64e-5

LANES = 128
SUBLANES = 8
PAIR = 2 * B_HEAD
CHUNK = 64
MOD_COLS = 1536
IN_TILE_ROWS = 512
OUT_TILE_ROWS = 1024
FFN_TILE_ROWS = 1024
FF_CHUNK = 256
FF_ROWS = 256
PREP_ROWS = 256
ATTN_Q_ROWS = 256
ODD_ROWS = 1024
ODD_COLS = 256
VMEM_LIMIT = 48 * 1024 * 1024
MOD_ROWS = 8


def _cparams(*sem):
    return pltpu.CompilerParams(dimension_semantics=sem, vmem_limit_bytes=VMEM_LIMIT)


def _sigmoid(x):
    return 0.5 + 0.5 * jnp.tanh(0.5 * x)


def _silu(x):
    return x * _sigmoid(x)


def _dot(a, b):
    return jnp.dot(a.astype(BF16), b.astype(BF16), preferred_element_type=F32)


_NN = (((1,), (0,)), ((), ()))
_NT = (((1,), (1,)), ((), ()))
_TN = (((0,), (0,)), ((), ()))


def _split_bf16(x, n):
    parts = []
    rem = x
    for i in range(n):
        p = rem.astype(BF16)
        parts.append(p)
        if i + 1 < n:
            rem = rem - p.astype(F32)
    return parts


def _mm(a, b, dims):
    return lax.dot_general(a.astype(BF16), b.astype(BF16), dims, preferred_element_type=F32)


def _segsum(x, bd, pieces=2):
    acc = None
    for p in _split_bf16(x, pieces):
        t = jnp.dot(p, bd, preferred_element_type=F32)
        acc = t if acc is None else acc + t
    return acc


def _rms(x):
    return x * lax.rsqrt(jnp.mean(x * x, axis=-1, keepdims=True) + RMS_EPS)


def _modnorm(x, gain, scale, shift):
    return _rms(x) * gain * (1.0 + scale) + shift


def _mod_kernel(c_ref, w_ref, b_ref, o_ref):
    c = c_ref[...]
    s = c * _sigmoid(c)
    o_ref[0] = _dot(s, w_ref[0]) + b_ref[0]


def _modulation(cvec, w_ada, b_ada):
    tn = MOD_COLS
    n = 6 * D_MODEL
    assert n % tn == 0
    return pl.pallas_call(
        _mod_kernel,
        grid=(DEPTH, n // tn),
        in_specs=[
            pl.BlockSpec((MOD_ROWS, D_MODEL), lambda l, j: (0, 0)),
            pl.BlockSpec((1, D_MODEL, tn), lambda l, j: (l, 0, j)),
            pl.BlockSpec((1, 1, tn), lambda l, j: (l, 0, j)),
        ],
        out_specs=pl.BlockSpec((1, MOD_ROWS, tn), lambda l, j: (l, 0, j)),
        out_shape=jax.ShapeDtypeStruct((DEPTH, MOD_ROWS, n), F32),
        compiler_params=_cparams("parallel", "parallel"),
        name="modulation",
    )(cvec, w_ada, b_ada.reshape(DEPTH, 1, n))


def _mod_spec(tm, seq_len, mod_row0):
    if mod_row0 == 0:
        return pl.BlockSpec((1, 6, D_MODEL), lambda i, *_: (0, 0, 0))
    return pl.BlockSpec((1, 6, D_MODEL), lambda i, *_: (mod_row0 + (i * tm) // seq_len, 0, 0))


def _rope(x, cos_t, sin_t):
    lane = lax.broadcasted_iota(jnp.int32, x.shape, 1)
    nxt = pltpu.roll(x, LANES - 1, 1)
    prv = pltpu.roll(x, 1, 1)
    partner = jnp.where(lane % 2 == 0, nxt, prv)
    return x * cos_t + partner * sin_t


def _even_in_kernel(*refs, rope, cache):
    x_ref, mod_ref, gains_ref, w_ref = refs[:4]
    refs = refs[4:]
    if rope:
        cos_ref, sin_ref = refs[:2]
        refs = refs[2:]
    q_ref, k_ref, v_ref, fb_ref = refs[:4]
    cache_refs = refs[4:]
    mod = mod_ref[0]
    h = _modnorm(x_ref[...], gains_ref[0:1], mod[1:2], mod[0:1]).astype(BF16)
    a = A_WIDTH
    hd = A_HEAD_DIM
    for n, out_ref in enumerate((q_ref, k_ref, v_ref)):
        y = jnp.dot(h, w_ref[:, n * a:(n + 1) * a], preferred_element_type=F32)
        if cache and n > 0:
            for hh in range(A_HEADS):
                cache_refs[n - 1][0, 0, :, hh, :] = y[:, hh * hd:(hh + 1) * hd]
        if rope and n < 2:
            for hh in range(A_HEADS):
                cols = slice(hh * hd, (hh + 1) * hd)
                out_ref[:, cols] = _rope(y[:, cols], cos_ref[...], sin_ref[...]).astype(BF16)
        else:
            out_ref[...] = y.astype(BF16)
    fb_ref[...] = jnp.dot(h, w_ref[:, 3 * a:], preferred_element_type=F32)


def _even_in(x, mod, gains, w, seq_len, mod_row0, tm, rope=None, cache=False):
    t = x.shape[0]
    assert seq_len % tm == 0
    per_seq = seq_len // tm
    row = lambda i: (i, 0)
    operands = [x, mod, gains, w]
    in_specs = [
        pl.BlockSpec((tm, D_MODEL), row),
        _mod_spec(tm, seq_len, mod_row0),
        pl.BlockSpec((4, D_MODEL), lambda i: (0, 0)),
        pl.BlockSpec((D_MODEL, IN_EVEN), lambda i: (0, 0)),
    ]
    if rope is not None:
        tab = pl.BlockSpec((tm, A_HEAD_DIM), lambda i: (i % per_seq, 0))
        operands += list(rope)
        in_specs += [tab, tab]
    out_specs = [pl.BlockSpec((tm, A_WIDTH), row)] * 3 + [pl.BlockSpec((tm, IN_B), row)]
    out_shape = [jax.ShapeDtypeStruct((t, A_WIDTH), BF16)] * 3 + [jax.ShapeDtypeStruct((t, IN_B), F32)]
    if cache:
        out_specs += [pl.BlockSpec((1, 1, tm, A_HEADS, A_HEAD_DIM),
                                   lambda i: (i // per_seq, 0, i % per_seq, 0, 0))] * 2
        out_shape += [jax.ShapeDtypeStruct((t // seq_len, 1, seq_len, A_HEADS, A_HEAD_DIM), F32)] * 2
    return pl.pallas_call(
        functools.partial(_even_in_kernel, rope=rope is not None, cache=cache),
        grid=(t // tm,),
        in_specs=in_specs,
        out_specs=out_specs,
        out_shape=out_shape,
        compiler_params=_cparams("parallel"),
        name="even_in",
    )(*operands)


def _attn_kernel(*refs, has_ctx, lam_init):
    if has_ctx:
        lam_ref, sub_ref, q_ref, k_ref, v_ref, ck_ref, cv_ref, o_ref = refs
    else:
        lam_ref, sub_ref, q_ref, k_ref, v_ref, o_ref = refs
    lp = lam_ref[...]
    l1 = jnp.sum(lp[0:1] * lp[1:2], axis=-1, keepdims=True)
    l2 = jnp.sum(lp[2:3] * lp[3:4], axis=-1, keepdims=True)
    lam = jnp.exp(l1) - jnp.exp(l2) + lam_init
    hd = A_HEAD_DIM
    lane = lax.broadcasted_iota(jnp.int32, (q_ref.shape[0], hd), 1)
    scale = A_HALF ** -0.5

    def head(h):
        cols = pl.ds(h * hd, hd)
        q = q_ref[:, cols]
        k = k_ref[:, cols].astype(BF16)
        if has_ctx:
            ck = ck_ref[:, cols].astype(BF16)
        scores = []
        for m in range(2):
            qm = jnp.where((lane < A_HALF) == (m == 0), q, 0.0).astype(BF16)
            s = lax.dot_general(qm, k, _NT, preferred_element_type=F32) * scale
            sc = lax.dot_general(qm, ck, _NT, preferred_element_type=F32) * scale if has_ctx else None
            scores.append((s, sc))
        yield
        probs = []
        for s, sc in scores:
            mx = jnp.max(s, axis=-1, keepdims=True)
            if has_ctx:
                mx = jnp.maximum(mx, jnp.max(sc, axis=-1, keepdims=True))
                ec = jnp.exp(sc - mx)
            e = jnp.exp(s - mx)
            z = jnp.sum(e, axis=-1, keepdims=True)
            if has_ctx:
                z = z + jnp.sum(ec, axis=-1, keepdims=True)
            inv = 1.0 / z
            probs.append((e * inv, ec * inv if has_ctx else None))
        w = (probs[0][0] - lam * probs[1][0]).astype(BF16)
        if has_ctx:
            wc = (probs[0][1] - lam * probs[1][1]).astype(BF16)
        yield
        o = jnp.dot(w, v_ref[:, cols].astype(BF16), preferred_element_type=F32)
        if has_ctx:
            o = o + jnp.dot(wc, cv_ref[:, cols].astype(BF16), preferred_element_type=F32)
        yield
        o_ref[:, cols] = (_rms(o) * sub_ref[...] * (1.0 - lam_init)).astype(BF16)

    _round_robin(head(h) for h in range(A_HEADS))


def _attention(q, k, v, lam_p, subln, lam_init, n_seq, seq_len, ctx=None):
    t = q.shape[0]
    tq = min(seq_len, ATTN_Q_ROWS if ctx is None else ATTN_Q_ROWS // 2)
    nq = seq_len // tq
    small = [pl.BlockSpec((4, A_HALF), lambda b, i: (0, 0)),
             pl.BlockSpec((1, A_HEAD_DIM), lambda b, i: (0, 0))]
    qblk = pl.BlockSpec((tq, A_WIDTH), lambda b, i: (b * nq + i, 0))
    kblk = pl.BlockSpec((seq_len, A_WIDTH), lambda b, i: (b, 0))
    operands = [lam_p, subln, q, k, v]
    in_specs = small + [qblk, kblk, kblk]
    if ctx is not None:
        ck, cv = ctx
        cblk = pl.BlockSpec((ck.shape[0] // n_seq, A_WIDTH), lambda b, i: (b, 0))
        operands += [ck, cv]
        in_specs += [cblk, cblk]
    return pl.pallas_call(
        functools.partial(_attn_kernel, has_ctx=ctx is not None, lam_init=lam_init),
        grid=(n_seq, nq),
        in_specs=in_specs,
        out_specs=qblk,
        out_shape=jax.ShapeDtypeStruct((t, A_WIDTH), BF16),
        compiler_params=_cparams("parallel", "parallel"),
        name="diff_attn",
    )(*operands)


def _cumsum_rows(x, reverse):
    n = x.shape[0]
    ridx = lax.broadcasted_iota(jnp.int32, x.shape, 0)
    s = 1
    while s < n:
        if reverse:
            x = x + jnp.where(ridx < n - s, pltpu.roll(x, n - s, 0), 0.0)
        else:
            x = x + jnp.where(ridx >= s, pltpu.roll(x, s, 0), 0.0)
        s *= 2
    return x


def _round_robin(gens):
    gens = list(gens)
    while gens:
        alive = []
        for gen in gens:
            try:
                next(gen)
                alive.append(gen)
            except StopIteration:
                pass
        gens = alive


def _rwkv_kernel(*refs, seq_len, has_state, pairs, unroll):
    (fr_ref, fk_ref, fv_ref, fl_ref, mur_ref, muk_ref, muv_ref, mul_ref, w0_ref, w2_ref, a0_ref,
     a2_ref, g2_ref, kvec_ref, lnx_ref, bd_ref, bdm_ref) = refs[:17]
    refs = refs[17:]
    if has_state:
        s0_ref = refs[0]
        refs = refs[1:]
    yb_ref, sf_ref, pre_ref, g_scr, bonus_scr, r1_scr, m1_scr, d0_scr, gam_scr, y_scr, s_scr = refs

    def shifted(ref, mu_ref, r0):
        f = ref[r0:r0 + PREP_ROWS, :]
        zero = jnp.zeros((1, f.shape[1]), F32)
        before = zero if r0 == 0 else ref[r0 - 1:r0, :]
        after = zero if r0 + PREP_ROWS == seq_len else ref[r0 + PREP_ROWS:r0 + PREP_ROWS + 1, :]
        local = lax.broadcasted_iota(jnp.int32, (PREP_ROWS, 1), 0)
        prev = jnp.where(local == 0, before, pltpu.roll(f, 1, 0))
        nxt = jnp.where(local == PREP_ROWS - 1, after, pltpu.roll(f, PREP_ROWS - 1, 0))
        return f * (1.0 - mu_ref[0:1] - mu_ref[1:2]) + mu_ref[0:1] * prev + mu_ref[1:2] * nxt

    bd = bd_ref[...]
    kv = kvec_ref[...]
    for r0 in range(0, seq_len, PREP_ROWS):
        rows = slice(r0, r0 + PREP_ROWS)
        r = shifted(fr_ref, mur_ref, r0)
        k = shifted(fk_ref, muk_ref, r0)
        v = shifted(fv_ref, muv_ref, r0)
        lora = shifted(fl_ref, mul_ref, r0)
        wd = jnp.tanh(lora[:, 0:2 * LORA_W])
        ad = lora[:, 2 * LORA_W:2 * LORA_W + 2 * LORA_A]
        gd = lora[:, 2 * LORA_W + 2 * LORA_A:]
        g_scr[rows, :] = _dot(_sigmoid(gd), g2_ref[...])
        kk = k * kv[0:1]
        kk = kk * lax.rsqrt(_segsum(kk * kk, bd, pieces=1) + 1e-12)
        pre_ref[0, rows, :] = r
        pre_ref[1, rows, :] = v
        pre_ref[2, rows, :] = kk
        ksum = None
        for d in range(2):
            w_raw = w0_ref[d:d + 1] + _dot(wd, w2_ref[d])
            pre_ref[7 + d, rows, :] = -math.exp(-0.5) * _sigmoid(w_raw)
            a = _sigmoid(a0_ref[d:d + 1] + _dot(ad, a2_ref[d]))
            kd = k * (1.0 + (a - 1.0) * kv[1:2])
            pre_ref[3 + d, rows, :] = kd
            pre_ref[5 + d, rows, :] = kk * a
            ksum = kd if ksum is None else ksum + kd
        bonus_scr[rows, :] = _segsum(r * ksum * kv[2:3], bd) * v

    nc = seq_len // CHUNK
    c2 = 2 * CHUNK
    lane = lax.broadcasted_iota(jnp.int32, (CHUNK, PAIR), 1)
    head0 = lane < B_HEAD
    row = lax.broadcasted_iota(jnp.int32, (c2, c2), 0)
    col = lax.broadcasted_iota(jnp.int32, (c2, c2), 1)
    same = (row // CHUNK) == (col // CHUNK)
    eye = jnp.where(row == col, 1.0, 0.0)
    xor_rc = jnp.bitwise_xor(row, col)
    mm = _mm

    def stack(x):
        return jnp.concatenate([jnp.where(head0, x, 0.0), jnp.where(head0, 0.0, x)], axis=0)

    def fold(x):
        return x[0:CHUNK] + x[CHUNK:c2]

    def chunk_rows(cc):
        return pl.ds(pl.multiple_of(cc * CHUNK, CHUNK), CHUNK)

    def chunk_local(cc, d, q):
        rows = chunk_rows(cc)
        lanes = pl.ds(q * PAIR, PAIR)
        z = 2 * q + d
        r = pre_ref[0, rows, lanes]
        v = pre_ref[1, rows, lanes]
        kk = pre_ref[2, rows, lanes]
        k = pre_ref[3 + d, rows, lanes]
        b = pre_ref[5 + d, rows, lanes]
        lw = pre_ref[7 + d, rows, lanes]
        incl = same & ((col <= row) if d == 0 else (col >= row))
        strict = same & ((col < row) if d == 0 else (col > row))
        g = _cumsum_rows(lw, reverse=(d == 1))
        gp = g - lw
        gtot = g[CHUNK - 1:CHUNK, :] if d == 0 else g[0:1, :]
        gm = 0.5 * gtot
        kx = k * jnp.exp(gm - g)
        bx = b * jnp.exp(gm - g)
        lhs = jnp.concatenate([stack(r * jnp.exp(g - gm)), stack(kk * jnp.exp(gp - gm))], axis=0)
        rhs = jnp.concatenate([kx, kx, bx, bx], axis=0)
        a_all = mm(lhs, rhs, _NT)
        yield
        a_rk = jnp.where(incl, a_all[0:c2, 0:c2], 0.0)
        a_rb = jnp.where(incl, a_all[0:c2, c2:2 * c2], 0.0)
        a_kk = jnp.where(strict, a_all[c2:2 * c2, 0:c2], 0.0)
        n_mat = jnp.where(strict, a_all[c2:2 * c2, c2:2 * c2], 0.0)
        vs = stack(v)
        akv = mm(a_kk, vs, _NN)
        y0 = mm(a_rk, vs, _NN)
        lower = (col < row) if d == 0 else (col > row)
        p_inv = eye - jnp.where(lower & (xor_rc == 1), n_mat, 0.0)
        yield
        s = 2
        while s < CHUNK:
            c_s = jnp.where(lower & (xor_rc >= s) & (xor_rc < 2 * s), n_mat, 0.0)
            t = mm(c_s, p_inv, _NN)
            yield
            p_inv = p_inv - mm(p_inv, t, _NN)
            yield
            s *= 2
        x = jnp.concatenate([stack(kk * jnp.exp(gp)), akv], axis=1)
        w12 = mm(p_inv, x, _NN)
        yield
        arb_w = mm(a_rb, w12, _NN)
        et = jnp.exp(gtot - g)
        bes = stack(b * et)
        d0_scr[z, cc] = mm(jnp.concatenate([vs, -w12[:, PAIR:]], axis=0),
                           jnp.concatenate([stack(k * et), bes], axis=0), _TN)
        m1_scr[z, cc] = mm(w12[:, 0:PAIR], bes, _TN).astype(BF16)
        yield
        r1_scr[z, cc] = (stack(r * jnp.exp(g)) - arb_w[:, 0:PAIR]).astype(BF16)
        y_scr[z, rows, :] = fold(y0 - arb_w[:, PAIR:])
        gam_scr[z, cc] = jnp.broadcast_to(jnp.exp(gtot), (SUBLANES, PAIR))

    def local_body(grp, carry):
        _round_robin(chunk_local(grp * unroll + u, d, q)
                     for u in range(unroll) for q in range(pairs) for d in range(2))
        return carry

    lax.fori_loop(0, nc // unroll, local_body, 0)

    for z in range(2 * pairs):
        s_scr[z] = jnp.zeros((PAIR, PAIR), F32)
        if has_state:
            s_scr[z, 0:B_HEAD, 0:B_HEAD] = s0_ref[0, z % 2, 2 * (z // 2)]
            s_scr[z, B_HEAD:PAIR, B_HEAD:PAIR] = s0_ref[0, z % 2, 2 * (z // 2) + 1]

    def carry_body(grp, carry):
        states = [s_scr[z] for z in range(2 * pairs)]
        for u in range(unroll):
            c = grp * unroll + u
            for z in range(2 * pairs):
                cc = c if z % 2 == 0 else nc - 1 - c
                rows = chunk_rows(cc)
                s0 = states[z]
                y_scr[z, rows, :] += fold(mm(r1_scr[z, cc], s0, _NT))
                states[z] = s0 * gam_scr[z, cc, 0:1, :] - mm(s0, m1_scr[z, cc], _NN) + d0_scr[z, cc]
        for z in range(2 * pairs):
            s_scr[z] = states[z]
        return carry

    lax.fori_loop(0, nc // unroll, carry_body, 0)
    bdm = bdm_ref[...]
    for r0 in range(0, seq_len, PREP_ROWS):
        rows = slice(r0, r0 + PREP_ROWS)
        ys = jnp.concatenate([y_scr[2 * q, rows, :] + y_scr[2 * q + 1, rows, :] for q in range(pairs)],
                             axis=1)
        dlt = ys - _segsum(ys, bdm)
        yn = dlt * lax.rsqrt(_segsum(dlt * dlt, bdm, pieces=1) + LNX_EPS)
        yb = (yn * lnx_ref[0:1] + lnx_ref[1:2] + bonus_scr[rows, :]) * g_scr[rows, :]
        yb_ref[rows, :] = yb.astype(BF16)
    for z in range(2 * pairs):
        sf_ref[0, z % 2, 2 * (z // 2)] = s_scr[z, 0:B_HEAD, 0:B_HEAD]
        sf_ref[0, z % 2, 2 * (z // 2) + 1] = s_scr[z, B_HEAD:PAIR, B_HEAD:PAIR]


def _rwkv_mix(fb, s0, p, n_seq, seq_len, pairs=2, unroll=4):
    t = fb.shape[0]
    nc = seq_len // CHUNK
    w = pairs * PAIR
    nw = B_WIDTH // w
    lora_w = 2 * LORA_W + 2 * LORA_A + LORA_G
    assert seq_len % PREP_ROWS == 0 and (3 * B_WIDTH) % lora_w == 0
    col = lambda rows, base: pl.BlockSpec((rows, w), lambda b, q: (0, base + q))
    seq = lambda base: pl.BlockSpec((seq_len, w), lambda b, q: (b, base + q))
    lora_blk = 3 * B_WIDTH // lora_w
    sblk = pl.BlockSpec((1, 2, 2 * pairs, B_HEAD, B_HEAD), lambda b, q: (b, 0, q, 0, 0))
    mat = lambda dtype: pltpu.VMEM((2 * pairs, nc, PAIR, PAIR), dtype)
    in_specs = [
        seq(0), seq(nw), seq(2 * nw),
        pl.BlockSpec((seq_len, lora_w), lambda b, q: (b, lora_blk)),
        col(2, 0), col(2, nw), col(2, 2 * nw),
        pl.BlockSpec((2, lora_w), lambda b, q: (0, lora_blk)),
        col(2, 0),
        pl.BlockSpec((2, 2 * LORA_W, w), lambda b, q: (0, 0, q)),
        col(2, 0),
        pl.BlockSpec((2, 2 * LORA_A, w), lambda b, q: (0, 0, q)),
        col(LORA_G, 0), col(3, 0), col(2, 0),
        pl.BlockSpec((w, w), lambda b, q: (0, 0)),
        pl.BlockSpec((w, w), lambda b, q: (0, 0)),
    ]
    operands = [fb, fb, fb, fb, p["mu"], p["mu"], p["mu"], p["mu"], p["w0"], p["w2"], p["a0"], p["a2"],
                p["g2"], p["kvec"], p["lnx"], _block_diag(1.0, w), _block_diag(1.0 / B_HEAD, w)]
    if s0 is not None:
        in_specs.append(sblk)
        operands.append(s0)
    return pl.pallas_call(
        functools.partial(_rwkv_kernel, seq_len=seq_len, has_state=s0 is not None, pairs=pairs,
                          unroll=unroll),
        grid=(n_seq, nw),
        in_specs=in_specs,
        out_specs=[pl.BlockSpec((seq_len, w), lambda b, q: (b, q)), sblk],
        out_shape=[
            jax.ShapeDtypeStruct((t, B_WIDTH), BF16),
            jax.ShapeDtypeStruct((n_seq, 2, B_HEADS, B_HEAD, B_HEAD), F32),
        ],
        scratch_shapes=[pltpu.VMEM((9, seq_len, w), F32), pltpu.VMEM((seq_len, w), F32),
                        pltpu.VMEM((seq_len, w), F32), mat(BF16), mat(BF16), mat(F32),
                        pltpu.VMEM((2 * pairs, nc, SUBLANES, PAIR), F32),
                        pltpu.VMEM((2 * pairs, seq_len, PAIR), F32),
                        pltpu.VMEM((2 * pairs, PAIR, PAIR), F32)],
        compiler_params=_cparams("parallel", "parallel"),
        name="rwkv_mix",
    )(*operands)


def _even_out_kernel(x_ref, mod_ref, gains_ref, ya_ref, yb_ref, w_ref, o_ref):
    m = (jnp.dot(ya_ref[...], w_ref[0:A_WIDTH, :], preferred_element_type=F32)
         + jnp.dot(yb_ref[...], w_ref[A_WIDTH:, :], preferred_element_type=F32))
    mod = mod_ref[0]
    o_ref[...] = x_ref[...] + mod[2:3] * (_rms(m) * gains_ref[1:2])


def _even_out(x, mod, gains, ya, yb, w, seq_len, mod_row0, tm):
    t = x.shape[0]
    row = lambda i: (i, 0)
    return pl.pallas_call(
        _even_out_kernel,
        grid=(t // tm,),
        in_specs=[
            pl.BlockSpec((tm, D_MODEL), row),
            _mod_spec(tm, seq_len, mod_row0),
            pl.BlockSpec((4, D_MODEL), lambda i: (0, 0)),
            pl.BlockSpec((tm, A_WIDTH), row),
            pl.BlockSpec((tm, B_WIDTH), row),
            pl.BlockSpec((D_MODEL, D_MODEL), lambda i: (0, 0)),
        ],
        out_specs=pl.BlockSpec((tm, D_MODEL), row),
        out_shape=jax.ShapeDtypeStruct((t, D_MODEL), F32),
        compiler_params=_cparams("parallel"),
        name="even_out",
    )(x, mod, gains, ya, yb, w)


def _odd_kernel(x_ref, mod_ref, gains_ref, csc_ref, cl_ref, sl_ref, w_ref, o_ref,
                tc_scr, ts_scr, m_scr, *, seq_len):
    mod = mod_ref[0]
    gd = C_GROUP_DIM
    n_seq = x_ref.shape[0] // seq_len
    csc = csc_ref[...]

    for s in range(n_seq):
        rows = pl.ds(s * seq_len, seq_len)
        h = _modnorm(x_ref[rows, :], gains_ref[0:1], mod[1:2], mod[0:1]).astype(BF16)
        for g in range(C_GROUPS):
            t = jnp.dot(h[:, g * gd:(g + 1) * gd], csc, preferred_element_type=F32)
            tc_scr[rows, g * gd:(g + 1) * gd] = t[:, 0:gd].astype(BF16)
            ts_scr[rows, g * gd:(g + 1) * gd] = t[:, gd:].astype(BF16)

    def position_dft(s, cb):
        rows = pl.ds(s * seq_len, seq_len)
        cols = pl.ds(cb * ODD_COLS, ODD_COLS)
        f = (jnp.dot(cl_ref[...], tc_scr[rows, cols], preferred_element_type=F32)
             - jnp.dot(sl_ref[...], ts_scr[rows, cols], preferred_element_type=F32))
        yield
        part = jnp.dot(f.astype(BF16), w_ref[cols, :], preferred_element_type=F32)
        yield
        if cb == 0:
            m_scr[rows, :] = part
        else:
            m_scr[rows, :] += part

    _round_robin(position_dft(s, cb) for s in range(n_seq) for cb in range(D_MODEL // ODD_COLS))
    o_ref[...] = x_ref[...] + mod[2:3] * (_rms(m_scr[...]) * gains_ref[1:2])


def _dft_tables(n):
    idx = np.arange(n, dtype=np.int64)
    ang = 2.0 * np.pi * ((idx[:, None] * idx[None, :]) % n).astype(np.float64) / n
    s = 1.0 / math.sqrt(n)
    return np.cos(ang) * s, np.sin(ang) * s


def _odd_mixer(x, mod, gains, w, n_seq, seq_len, mod_row0):
    cc, sc = _dft_tables(C_GROUP_DIM)
    cl, sl = _dft_tables(seq_len)
    csc, cl, sl = (jnp.asarray(a, F32).astype(BF16) for a in (np.concatenate([cc, sc], axis=1), cl, sl))
    full = lambda shape: pl.BlockSpec(shape, lambda i: (0,) * len(shape))
    row = lambda i: (i, 0)
    tm = max(seq_len, ODD_ROWS)
    assert tm % seq_len == 0 and (n_seq * seq_len) % tm == 0
    return pl.pallas_call(
        functools.partial(_odd_kernel, seq_len=seq_len),
        grid=(n_seq * seq_len // tm,),
        in_specs=[
            pl.BlockSpec((tm, D_MODEL), row),
            _mod_spec(tm, seq_len, mod_row0),
            full((4, D_MODEL)), full((C_GROUP_DIM, 2 * C_GROUP_DIM)),
            full((seq_len, seq_len)), full((seq_len, seq_len)), full((D_MODEL, D_MODEL)),
        ],
        out_specs=pl.BlockSpec((tm, D_MODEL), row),
        out_shape=jax.ShapeDtypeStruct(x.shape, F32),
        scratch_shapes=[pltpu.VMEM((tm, D_MODEL), BF16), pltpu.VMEM((tm, D_MODEL), BF16),
                        pltpu.VMEM((tm, D_MODEL), F32)],
        compiler_params=_cparams("parallel"),
        name="odd_mixer",
    )(x, mod, gains, csc, cl, sl, w)


def _ffn_kernel(x_ref, mod_ref, gains_ref, wu_ref, wg_ref, cw_ref, cb_ref, wo_ref, o_ref,
                h_scr, acc_scr, *, seq_len):
    j = pl.program_id(1)
    mod = mod_ref[0]

    @pl.when(j == 0)
    def _():
        h_scr[...] = _modnorm(x_ref[...], gains_ref[2:3], mod[4:5], mod[3:4]).astype(BF16)
        acc_scr[...] = jnp.zeros_like(acc_scr)

    n_sub = h_scr.shape[0] // FF_ROWS
    gate = [None] * n_sub
    zero_row = jnp.zeros((1, FF_CHUNK), F32)
    wu = wu_ref[...].astype(BF16)
    wg = wg_ref[...].astype(BF16)
    wo = wo_ref[...].astype(BF16)

    def sub_tile(i):
        rows = pl.ds(i * FF_ROWS, FF_ROWS)
        h = h_scr[rows, :]
        u = jnp.dot(h, wu, preferred_element_type=F32)
        g = gate[i] = jnp.dot(h, wg, preferred_element_type=F32)
        yield
        before = zero_row if (i * FF_ROWS) % seq_len == 0 else gate[i - 1][FF_ROWS - 1:FF_ROWS, :]
        after = zero_row if ((i + 1) * FF_ROWS) % seq_len == 0 else gate[i + 1][0:1, :]
        pos = lax.broadcasted_iota(jnp.int32, (FF_ROWS, 1), 0)
        prev = jnp.where(pos == 0, before, pltpu.roll(g, 1, 0))
        nxt = jnp.where(pos == FF_ROWS - 1, after, pltpu.roll(g, FF_ROWS - 1, 0))
        gc = prev * cw_ref[0:1] + g * cw_ref[1:2] + nxt * cw_ref[2:3] + cb_ref[...]
        act = (_silu(gc) * u).astype(BF16)
        yield
        acc_scr[rows, :] += jnp.dot(act, wo, preferred_element_type=F32)

    _round_robin(sub_tile(i) for i in range(n_sub))

    @pl.when(j == pl.num_programs(1) - 1)
    def _():
        o_ref[...] = x_ref[...] + mod[5:6] * (_rms(acc_scr[...]) * gains_ref[3:4])


def _conv_ffn(x, mod, gains, layer, w_in, conv_w, conv_b, w_out, seq_len, mod_row0, tm):
    t = x.shape[0]
    nj = D_FF // FF_CHUNK
    assert seq_len % FF_ROWS == 0 and tm % seq_len == 0
    return pl.pallas_call(
        functools.partial(_ffn_kernel, seq_len=seq_len),
        grid=(t // tm, nj),
        in_specs=[
            pl.BlockSpec((tm, D_MODEL), lambda i, j: (i, 0)),
            _mod_spec(tm, seq_len, mod_row0),
            pl.BlockSpec((4, D_MODEL), lambda i, j: (0, 0)),
            pl.BlockSpec((None, D_MODEL, FF_CHUNK), lambda i, j: (layer, 0, j)),
            pl.BlockSpec((None, D_MODEL, FF_CHUNK), lambda i, j: (layer, 0, nj + j)),
            pl.BlockSpec((3, FF_CHUNK), lambda i, j: (0, j)),
            pl.BlockSpec((1, FF_CHUNK), lambda i, j: (0, j)),
            pl.BlockSpec((None, FF_CHUNK, D_MODEL), lambda i, j: (layer, j, 0)),
        ],
        out_specs=pl.BlockSpec((tm, D_MODEL), lambda i, j: (i, 0)),
        out_shape=jax.ShapeDtypeStruct((t, D_MODEL), F32),
        scratch_shapes=[pltpu.VMEM((tm, D_MODEL), BF16), pltpu.VMEM((tm, D_MODEL), F32)],
        compiler_params=_cparams("parallel", "arbitrary"),
        name="conv_ffn",
    )(x, mod, gains, w_in, w_in, conv_w, conv_b, w_out)


def _rope_tables(seq_len):
    rows = seq_len // GRID_W
    row = jnp.repeat(jnp.arange(rows, dtype=F32), GRID_W)
    col = jnp.tile(jnp.arange(GRID_W, dtype=F32), rows)
    n_freq = A_HALF // 4
    inv = ROPE_BASE ** (-jnp.arange(n_freq, dtype=F32) / n_freq)
    ang = jnp.concatenate([row[:, None] * inv, col[:, None] * inv], axis=-1)
    cos_t = jnp.tile(jnp.repeat(jnp.cos(ang), 2, axis=1), (1, 2))
    sin_t = jnp.tile(jnp.repeat(jnp.sin(ang), 2, axis=1), (1, 2))
    sign = jnp.where(jnp.arange(A_HEAD_DIM) % 2 == 0, -1.0, 1.0).astype(F32)
    return cos_t, sin_t * sign


def _pad_lora(w):
    z = jnp.zeros_like(w[0])
    return jnp.stack([jnp.concatenate([w[0], z], axis=0), jnp.concatenate([z, w[1]], axis=0)])


def _block_diag(value, width):
    head = np.arange(width) // B_HEAD
    return jnp.asarray(np.where(head[:, None] == head[None, :], value, 0.0), BF16)


def kernel(x_prompt, x_sample, cache_k, cache_v, state_wkv, c, c_ctx, w_ada, b_ada, norm_gains,
           w_in_even, w_out_even, diff_lambda, diff_subln, rwkv_shift_mu, rwkv_w0, rwkv_w2, rwkv_a0,
           rwkv_a2, rwkv_g2, rwkv_kvec, rwkv_lnx, w_out_odd, w_ffn_in, ffn_conv, ffn_conv_b,
           w_ffn_out):
    n_ctx, l_ctx, _ = x_prompt.shape
    n_lat, l_lat, _ = x_sample.shape
    assert 1 + n_lat <= MOD_ROWS
    cvec = jnp.concatenate(
        [c_ctx[None, :], c, jnp.zeros((MOD_ROWS - 1 - n_lat, D_MODEL), F32)], axis=0)
    mods = _modulation(cvec, w_ada, b_ada).reshape(DEPTH, MOD_ROWS, 6, D_MODEL)

    per_layer_bf16 = lambda w: [w[i].astype(BF16) for i in range(w.shape[0])]
    w_in_even_b = per_layer_bf16(w_in_even)
    w_out_even_b = per_layer_bf16(w_out_even)
    w_out_odd_b = per_layer_bf16(w_out_odd)
    cos_t, sin_t = _rope_tables(l_lat)

    def run_group(x, n_seq, seq_len, mod_row0, latent):
        ctx_out = None
        for l in range(DEPTH):
            mod = mods[l]
            gains = norm_gains[l]
            if l % 2 == 0:
                e = l // 2
                lam_init = 0.8 - 0.6 * math.exp(-0.3 * l)
                q, k, v, fb, *kv_cache = _even_in(
                    x, mod, gains, w_in_even_b[e], seq_len, mod_row0, min(seq_len, IN_TILE_ROWS),
                    rope=(cos_t, sin_t) if latent else None, cache=not latent)
                if latent:
                    past = cache_k.shape[2]
                    ctx = (cache_k[:, e].reshape(n_seq * past, A_WIDTH),
                           cache_v[:, e].reshape(n_seq * past, A_WIDTH))
                    s0 = state_wkv[:, e]
                else:
                    ctx = None
                    s0 = None
                ya = _attention(q, k, v, diff_lambda[e], diff_subln[e][None, :], lam_init,
                                n_seq, seq_len, ctx)
                rwkv = {
                    "mu": rwkv_shift_mu[e], "w0": rwkv_w0[e], "w2": _pad_lora(rwkv_w2[e]).astype(BF16),
                    "a0": rwkv_a0[e], "a2": _pad_lora(rwkv_a2[e]).astype(BF16),
                    "g2": rwkv_g2[e].astype(BF16), "kvec": rwkv_kvec[e], "lnx": rwkv_lnx[e],
                }
                yb, s_fin = _rwkv_mix(fb, s0, rwkv, n_seq, seq_len, pairs=4 if seq_len <= 256 else 2)
                x = _even_out(x, mod, gains, ya, yb, w_out_even_b[e], seq_len, mod_row0, OUT_TILE_ROWS)
                if not latent:
                    ctx_out = (*kv_cache, s_fin)
            else:
                x = _odd_mixer(x, mod, gains, w_out_odd_b[l // 2], n_seq, seq_len, mod_row0)
            x = _conv_ffn(x, mod, gains, l, w_ffn_in, ffn_conv[l], ffn_conv_b[l][None, :],
                          w_ffn_out, seq_len, mod_row0, max(seq_len, FFN_TILE_ROWS))
        return x, ctx_out

    y_ctx, (k_new, v_new, s_new) = run_group(
        x_prompt.reshape(n_ctx * l_ctx, D_MODEL), n_ctx, l_ctx, 0, False)
    y_lat, _ = run_group(x_sample.reshape(n_lat * l_lat, D_MODEL), n_lat, l_lat, 1, True)

    n_even = (DEPTH + 1) // 2
    assert n_even == 1
    return (
        y_ctx.reshape(n_ctx, l_ctx, D_MODEL),
        y_lat.reshape(n_lat, l_lat, D_MODEL),
        k_new,
        v_new,
        s_new.reshape(n_ctx, n_even, 2, B_HEADS, B_HEAD, B_HEAD),
    )
```

```python
import functools
import math

import numpy as np
import jax
import jax.numpy as jnp
from jax import lax
from jax.experimental import pallas as pl
from jax.experimental.pallas import tpu as pltpu

F32 = jnp.float32
BF16 = jnp.bfloat16

D_MODEL = 1024
DEPTH = 2
GRID_W = 64
A_WIDTH = D_MODEL // 2
A_HEADS = 4
A_HEAD_DIM = A_WIDTH // A_HEADS
A_HALF = A_HEAD_DIM // 2
B_WIDTH = D_MODEL - A_WIDTH
B_HEAD = 64
B_HEADS = B_WIDTH // B_HEAD
LORA_W = 64
LORA_A = 64
LORA_G = 128
IN_B = 3 * B_WIDTH + 2 * LORA_W + 2 * LORA_A + LORA_G
IN_EVEN = 3 * A_WIDTH + IN_B
C_GROUPS = 8
C_GROUP_DIM = D_MODEL // C_GROUPS
D_FF = 2816
ROPE_BASE = 10000.0
RMS_EPS = 1e-6
LNX_EPS = 64e-5

LANES = 128
SUBLANES = 8
PAIR = 2 * B_HEAD
CHUNK = 64
MOD_COLS = 1536
IN_TILE_ROWS = 512
OUT_TILE_ROWS = 1024
FFN_TILE_ROWS = 1024
FF_CHUNK = 256
FF_ROWS = 256
PREP_ROWS = 256
ATTN_Q_ROWS = 256
ODD_ROWS = 1024
ODD_COLS = 256
VMEM_LIMIT = 48 * 1024 * 1024
MOD_ROWS = 8


def _cparams(*sem):
    return pltpu.CompilerParams(dimension_semantics=sem, vmem_limit_bytes=VMEM_LIMIT)


def _sigmoid(x):
    return 1.0 / (1.0 + jnp.exp(-x))


def _silu(x):
    return x * (0.5 + 0.5 * jnp.tanh(0.5 * x))


def _dot(a, b):
    return jnp.dot(a.astype(BF16), b.astype(BF16), preferred_element_type=F32)


_NN = (((1,), (0,)), ((), ()))
_NT = (((1,), (1,)), ((), ()))
_TN = (((0,), (0,)), ((), ()))


def _split_bf16(x, n):
    parts = []
    rem = x
    for i in range(n):
        p = rem.astype(BF16)
        parts.append(p)
        if i + 1 < n:
            rem = rem - p.astype(F32)
    return parts


def _mm(a, b, dims):
    return lax.dot_general(a.astype(BF16), b.astype(BF16), dims, preferred_element_type=F32)


def _segsum(x, bd, pieces=2):
    acc = None
    for p in _split_bf16(x, pieces):
        t = jnp.dot(p, bd, preferred_element_type=F32)
        acc = t if acc is None else acc + t
    return acc


def _rms(x):
    return x * lax.rsqrt(jnp.mean(x * x, axis=-1, keepdims=True) + RMS_EPS)


def _modnorm(x, gain, scale, shift):
    return _rms(x) * gain * (1.0 + scale) + shift


def _mod_kernel(c_ref, w_ref, b_ref, o_ref):
    c = c_ref[...]
    s = c * _sigmoid(c)
    o_ref[0] = _dot(s, w_ref[0]) + b_ref[0]


def _modulation(cvec, w_ada, b_ada):
    tn = MOD_COLS
    n = 6 * D_MODEL
    assert n % tn == 0
    return pl.pallas_call(
        _mod_kernel,
        grid=(DEPTH, n // tn),
        in_specs=[
            pl.BlockSpec((MOD_ROWS, D_MODEL), lambda l, j: (0, 0)),
            pl.BlockSpec((1, D_MODEL, tn), lambda l, j: (l, 0, j)),
            pl.BlockSpec((1, 1, tn), lambda l, j: (l, 0, j)),
        ],
        out_specs=pl.BlockSpec((1, MOD_ROWS, tn), lambda l, j: (l, 0, j)),
        out_shape=jax.ShapeDtypeStruct((DEPTH, MOD_ROWS, n), F32),
        compiler_params=_cparams("parallel", "parallel"),
        name="modulation",
    )(cvec, w_ada, b_ada.reshape(DEPTH, 1, n))


def _mod_spec(tm, seq_len, mod_row0):
    if mod_row0 == 0:
        return pl.BlockSpec((1, 6, D_MODEL), lambda i, *_: (0, 0, 0))
    return pl.BlockSpec((1, 6, D_MODEL), lambda i, *_: (mod_row0 + (i * tm) // seq_len, 0, 0))


def _rope(x, cos_t, sin_t):
    lane = lax.broadcasted_iota(jnp.int32, x.shape, 1)
    nxt = pltpu.roll(x, LANES - 1, 1)
    prv = pltpu.roll(x, 1, 1)
    partner = jnp.where(lane % 2 == 0, nxt, prv)
    return x * cos_t + partner * sin_t


def _even_in_kernel(*refs, rope, cache):
    x_ref, mod_ref, gains_ref, w_ref = refs[:4]
    refs = refs[4:]
    if rope:
        cos_ref, sin_ref = refs[:2]
        refs = refs[2:]
    q_ref, k_ref, v_ref, fb_ref = refs[:4]
    cache_refs = refs[4:]
    mod = mod_ref[0]
    h = _modnorm(x_ref[...], gains_ref[0:1], mod[1:2], mod[0:1]).astype(BF16)
    a = A_WIDTH
    hd = A_HEAD_DIM
    for n, out_ref in enumerate((q_ref, k_ref, v_ref)):
        y = jnp.dot(h, w_ref[:, n * a:(n + 1) * a], preferred_element_type=F32)
        if cache and n > 0:
            for hh in range(A_HEADS):
                cache_refs[n - 1][0, 0, :, hh, :] = y[:, hh * hd:(hh + 1) * hd]
        if rope and n < 2:
            for hh in range(A_HEADS):
                cols = slice(hh * hd, (hh + 1) * hd)
                out_ref[:, cols] = _rope(y[:, cols], cos_ref[...], sin_ref[...]).astype(BF16)
        else:
            out_ref[...] = y.astype(BF16)
    fb_ref[...] = jnp.dot(h, w_ref[:, 3 * a:], preferred_element_type=F32)


def _even_in(x, mod, gains, w, seq_len, mod_row0, tm, rope=None, cache=False):
    t = x.shape[0]
    assert seq_len % tm == 0
    per_seq = seq_len // tm
    row = lambda i: (i, 0)
    operands = [x, mod, gains, w]
    in_specs = [
        pl.BlockSpec((tm, D_MODEL), row),
        _mod_spec(tm, seq_len, mod_row0),
        pl.BlockSpec((4, D_MODEL), lambda i: (0, 0)),
        pl.BlockSpec((D_MODEL, IN_EVEN), lambda i: (0, 0)),
    ]
    if rope is not None:
        tab = pl.BlockSpec((tm, A_HEAD_DIM), lambda i: (i % per_seq, 0))
        operands += list(rope)
        in_specs += [tab, tab]
    out_specs = [pl.BlockSpec((tm, A_WIDTH), row)] * 3 + [pl.BlockSpec((tm, IN_B), row)]
    out_shape = [jax.ShapeDtypeStruct((t, A_WIDTH), BF16)] * 3 + [jax.ShapeDtypeStruct((t, IN_B), F32)]
    if cache:
        out_specs += [pl.BlockSpec((1, 1, tm, A_HEADS, A_HEAD_DIM),
                                   lambda i: (i // per_seq, 0, i % per_seq, 0, 0))] * 2
        out_shape += [jax.ShapeDtypeStruct((t // seq_len, 1, seq_len, A_HEADS, A_HEAD_DIM), F32)] * 2
    return pl.pallas_call(
        functools.partial(_even_in_kernel, rope=rope is not None, cache=cache),
        grid=(t // tm,),
        in_specs=in_specs,
        out_specs=out_specs,
        out_shape=out_shape,
        compiler_params=_cparams("parallel"),
        name="even_in",
    )(*operands)


def _attn_kernel(*refs, has_ctx, lam_init):
    if has_ctx:
        lam_ref, sub_ref, q_ref, k_ref, v_ref, ck_ref, cv_ref, o_ref = refs
    else:
        lam_ref, sub_ref, q_ref, k_ref, v_ref, o_ref = refs
    lp = lam_ref[...]
    l1 = jnp.sum(lp[0:1] * lp[1:2], axis=-1, keepdims=True)
    l2 = jnp.sum(lp[2:3] * lp[3:4], axis=-1, keepdims=True)
    lam = jnp.exp(l1) - jnp.exp(l2) + lam_init
    hd = A_HEAD_DIM
    lane = lax.broadcasted_iota(jnp.int32, (q_ref.shape[0], hd), 1)
    scale = A_HALF ** -0.5

    def head(h):
        cols = pl.ds(h * hd, hd)
        q = q_ref[:, cols]
        k = k_ref[:, cols].astype(BF16)
        if has_ctx:
            ck = ck_ref[:, cols].astype(BF16)
        scores = []
        for m in range(2):
            qm = jnp.where((lane < A_HALF) == (m == 0), q, 0.0).astype(BF16)
            s = lax.dot_general(qm, k, _NT, preferred_element_type=F32) * scale
            sc = lax.dot_general(qm, ck, _NT, preferred_element_type=F32) * scale if has_ctx else None
            scores.append((s, sc))
        yield
        probs = []
        for s, sc in scores:
            mx = jnp.max(s, axis=-1, keepdims=True)
            if has_ctx:
                mx = jnp.maximum(mx, jnp.max(sc, axis=-1, keepdims=True))
                ec = jnp.exp(sc - mx)
            e = jnp.exp(s - mx)
            z = jnp.sum(e, axis=-1, keepdims=True)
            if has_ctx:
                z = z + jnp.sum(ec, axis=-1, keepdims=True)
            inv = 1.0 / z
            probs.append((e * inv, ec * inv if has_ctx else None))
        w = (probs[0][0] - lam * probs[1][0]).astype(BF16)
        if has_ctx:
            wc = (probs[0][1] - lam * probs[1][1]).astype(BF16)
        yield
        o = jnp.dot(w, v_ref[:, cols].astype(BF16), preferred_element_type=F32)
        if has_ctx:
            o = o + jnp.dot(wc, cv_ref[:, cols].astype(BF16), preferred_element_type=F32)
        yield
        o_ref[:, cols] = (_rms(o) * sub_ref[...] * (1.0 - lam_init)).astype(BF16)

    _round_robin(head(h) for h in range(A_HEADS))


def _attention(q, k, v, lam_p, subln, lam_init, n_seq, seq_len, ctx=None):
    t = q.shape[0]
    tq = min(seq_len, ATTN_Q_ROWS)
    nq = seq_len // tq
    small = [pl.BlockSpec((4, A_HALF), lambda b, i: (0, 0)),
             pl.BlockSpec((1, A_HEAD_DIM), lambda b, i: (0, 0))]
    qblk = pl.BlockSpec((tq, A_WIDTH), lambda b, i: (b * nq + i, 0))
    kblk = pl.BlockSpec((seq_len, A_WIDTH), lambda b, i: (b, 0))
    operands = [lam_p, subln, q, k, v]
    in_specs = small + [qblk, kblk, kblk]
    if ctx is not None:
        ck, cv = ctx
        cblk = pl.BlockSpec((ck.shape[0] // n_seq, A_WIDTH), lambda b, i: (b, 0))
        operands += [ck, cv]
        in_specs += [cblk, cblk]
    return pl.pallas_call(
        functools.partial(_attn_kernel, has_ctx=ctx is not None, lam_init=lam_init),
        grid=(n_seq, nq),
        in_specs=in_specs,
        out_specs=qblk,
        out_shape=jax.ShapeDtypeStruct((t, A_WIDTH), BF16),
        compiler_params=_cparams("parallel", "parallel"),
        name="diff_attn",
    )(*operands)


def _cumsum_rows(x, reverse):
    n = x.shape[0]
    ridx = lax.broadcasted_iota(jnp.int32, x.shape, 0)
    s = 1
    while s < n:
        if reverse:
            x = x + jnp.where(ridx < n - s, pltpu.roll(x, n - s, 0), 0.0)
        else:
            x = x + jnp.where(ridx >= s, pltpu.roll(x, s, 0), 0.0)
        s *= 2
    return x


def _half_rows(m, s, first):
    off = 0 if first else s
    return jnp.concatenate([m[b + off:b + off + s] for b in range(0, m.shape[0], 2 * s)], axis=0)


def _spread_rows(h, s, first):
    zero = jnp.zeros((s, h.shape[1]), h.dtype)
    pieces = []
    for i in range(h.shape[0] // s):
        blk = h[i * s:(i + 1) * s]
        pieces += [blk, zero] if first else [zero, blk]
    return jnp.concatenate(pieces, axis=0)


def _round_robin(gens):
    gens = list(gens)
    while gens:
        alive = []
        for gen in gens:
            try:
                next(gen)
                alive.append(gen)
            except StopIteration:
                pass
        gens = alive


def _rwkv_kernel(*refs, seq_len, has_state, pairs, unroll):
    (fr_ref, fk_ref, fv_ref, fl_ref, mur_ref, muk_ref, muv_ref, mul_ref, w0_ref, w2_ref, a0_ref,
     a2_ref, g2_ref, kvec_ref, lnx_ref, bd_ref, bdm_ref) = refs[:17]
    refs = refs[17:]
    if has_state:
        s0_ref = refs[0]
        refs = refs[1:]
    yb_ref, sf_ref, pre_ref, g_scr, bonus_scr, r1_scr, m1_scr, d0_scr, gam_scr, y_scr, s_scr = refs

    def shifted(ref, mu_ref, r0):
        f = ref[r0:r0 + PREP_ROWS, :]
        zero = jnp.zeros((1, f.shape[1]), F32)
        before = zero if r0 == 0 else ref[r0 - 1:r0, :]
        after = zero if r0 + PREP_ROWS == seq_len else ref[r0 + PREP_ROWS:r0 + PREP_ROWS + 1, :]
        local = lax.broadcasted_iota(jnp.int32, (PREP_ROWS, 1), 0)
        prev = jnp.where(local == 0, before, pltpu.roll(f, 1, 0))
        nxt = jnp.where(local == PREP_ROWS - 1, after, pltpu.roll(f, PREP_ROWS - 1, 0))
        return f * (1.0 - mu_ref[0:1] - mu_ref[1:2]) + mu_ref[0:1] * prev + mu_ref[1:2] * nxt

    bd = bd_ref[...]
    kv = kvec_ref[...]
    for r0 in range(0, seq_len, PREP_ROWS):
        rows = slice(r0, r0 + PREP_ROWS)
        r = shifted(fr_ref, mur_ref, r0)
        k = shifted(fk_ref, muk_ref, r0)
        v = shifted(fv_ref, muv_ref, r0)
        lora = shifted(fl_ref, mul_ref, r0)
        wd = jnp.tanh(lora[:, 0:2 * LORA_W])
        ad = lora[:, 2 * LORA_W:2 * LORA_W + 2 * LORA_A]
        gd = lora[:, 2 * LORA_W + 2 * LORA_A:]
        g_scr[rows, :] = _dot(_sigmoid(gd), g2_ref[...])
        kk = k * kv[0:1]
        kk = kk * lax.rsqrt(_segsum(kk * kk, bd, pieces=1) + 1e-12)
        pre_ref[0, rows, :] = r
        pre_ref[1, rows, :] = v
        pre_ref[2, rows, :] = kk
        ksum = None
        for d in range(2):
            w_raw = w0_ref[d:d + 1] + _dot(wd, w2_ref[d])
            pre_ref[7 + d, rows, :] = -math.exp(-0.5) * _sigmoid(w_raw)
            a = _sigmoid(a0_ref[d:d + 1] + _dot(ad, a2_ref[d]))
            kd = k * (1.0 + (a - 1.0) * kv[1:2])
            pre_ref[3 + d, rows, :] = kd
            pre_ref[5 + d, rows, :] = kk * a
            ksum = kd if ksum is None else ksum + kd
        bonus_scr[rows, :] = _segsum(r * ksum * kv[2:3], bd) * v

    nc = seq_len // CHUNK
    c2 = 2 * CHUNK
    lane = lax.broadcasted_iota(jnp.int32, (CHUNK, PAIR), 1)
    head0 = lane < B_HEAD
    row = lax.broadcasted_iota(jnp.int32, (c2, c2), 0)
    col = lax.broadcasted_iota(jnp.int32, (c2, c2), 1)
    same = (row // CHUNK) == (col // CHUNK)
    eye = jnp.where(row == col, 1.0, 0.0)
    xor_rc = jnp.bitwise_xor(row, col)
    mm = _mm

    def stack(x):
        return jnp.concatenate([jnp.where(head0, x, 0.0), jnp.where(head0, 0.0, x)], axis=0)

    def fold(x):
        return x[0:CHUNK] + x[CHUNK:c2]

    def chunk_rows(cc):
        return pl.ds(pl.multiple_of(cc * CHUNK, CHUNK), CHUNK)

    def chunk_local(cc, d, q):
        rows = chunk_rows(cc)
        lanes = pl.ds(q * PAIR, PAIR)
        z = 2 * q + d
        r = pre_ref[0, rows, lanes]
        v = pre_ref[1, rows, lanes]
        kk = pre_ref[2, rows, lanes]
        k = pre_ref[3 + d, rows, lanes]
        b = pre_ref[5 + d, rows, lanes]
        lw = pre_ref[7 + d, rows, lanes]
        incl = same & ((col <= row) if d == 0 else (col >= row))
        strict = same & ((col < row) if d == 0 else (col > row))
        g = _cumsum_rows(lw, reverse=(d == 1))
        gp = g - lw
        gtot = g[CHUNK - 1:CHUNK, :] if d == 0 else g[0:1, :]
        gm = 0.5 * gtot
        kx = k * jnp.exp(gm - g)
        bx = b * jnp.exp(gm - g)
        lhs = jnp.concatenate([stack(r * jnp.exp(g - gm)), stack(kk * jnp.exp(gp - gm))], axis=0)
        rhs = jnp.concatenate([kx, kx, bx, bx], axis=0)
        a_all = mm(lhs, rhs, _NT)
        yield
        a_rk = jnp.where(incl, a_all[0:c2, 0:c2], 0.0)
        a_rb = jnp.where(incl, a_all[0:c2, c2:2 * c2], 0.0)
        a_kk = jnp.where(strict, a_all[c2:2 * c2, 0:c2], 0.0)
        n_mat = jnp.where(strict, a_all[c2:2 * c2, c2:2 * c2], 0.0)
        vs = stack(v)
        akv = mm(a_kk, vs, _NN)
        y0 = mm(a_rk, vs, _NN)
        lower = (col < row) if d == 0 else (col > row)
        p_inv = eye - jnp.where(lower & (xor_rc == 1), n_mat, 0.0)
        yield
        s = 2
        while s < SUBLANES:
            c_s = jnp.where(lower & (xor_rc >= s) & (xor_rc < 2 * s), n_mat, 0.0)
            t = mm(c_s, p_inv, _NN)
            yield
            p_inv = p_inv - mm(p_inv, t, _NN)
            yield
            s *= 2
        while s < CHUNK:
            c_s = jnp.where(lower & (xor_rc >= s) & (xor_rc < 2 * s), n_mat, 0.0)
            t = _spread_rows(mm(_half_rows(c_s, s, d == 1), p_inv, _NN), s, d == 1)
            yield
            upd = mm(_half_rows(p_inv, s, d == 1), t, _NN)
            p_inv = p_inv - _spread_rows(upd, s, d == 1)
            yield
            s *= 2
        x = jnp.concatenate([stack(kk * jnp.exp(gp)), akv], axis=1)
        w12 = mm(p_inv, x, _NN)
        yield
        arb_w = mm(a_rb, w12, _NN)
        et = jnp.exp(gtot - g)
        bes = stack(b * et)
        d0_scr[z, cc] = mm(jnp.concatenate([vs, -w12[:, PAIR:]], axis=0),
                           jnp.concatenate([stack(k * et), bes], axis=0), _TN)
        m1_scr[z, cc] = mm(w12[:, 0:PAIR], bes, _TN).astype(BF16)
        yield
        r1_scr[z, cc] = (stack(r * jnp.exp(g)) - arb_w[:, 0:PAIR]).astype(BF16)
        y_scr[z, rows, :] = fold(y0 - arb_w[:, PAIR:])
        gam_scr[z, cc] = jnp.broadcast_to(jnp.exp(gtot), (SUBLANES, PAIR))

    def local_body(grp, carry):
        _round_robin(chunk_local(grp * unroll + u, d, q)
                     for u in range(unroll) for q in range(pairs) for d in range(2))
        return carry

    lax.fori_loop(0, nc // unroll, local_body, 0)

    for z in range(2 * pairs):
        s_scr[z] = jnp.zeros((PAIR, PAIR), F32)
        if has_state:
            s_scr[z, 0:B_HEAD, 0:B_HEAD] = s0_ref[0, z % 2, 2 * (z // 2)]
            s_scr[z, B_HEAD:PAIR, B_HEAD:PAIR] = s0_ref[0, z % 2, 2 * (z // 2) + 1]

    def carry_body(grp, carry):
        states = [s_scr[z] for z in range(2 * pairs)]
        for u in range(unroll):
            c = grp * unroll + u
            for z in range(2 * pairs):
                cc = c if z % 2 == 0 else nc - 1 - c
                rows = chunk_rows(cc)
                s0 = states[z]
                y_scr[z, rows, :] += fold(mm(r1_scr[z, cc], s0, _NT))
                states[z] = s0 * gam_scr[z, cc, 0:1, :] - mm(s0, m1_scr[z, cc], _NN) + d0_scr[z, cc]
        for z in range(2 * pairs):
            s_scr[z] = states[z]
        return carry

    lax.fori_loop(0, nc // unroll, carry_body, 0)
    bdm = bdm_ref[...]
    for r0 in range(0, seq_len, PREP_ROWS):
        rows = slice(r0, r0 + PREP_ROWS)
        ys = jnp.concatenate([y_scr[2 * q, rows, :] + y_scr[2 * q + 1, rows, :] for q in range(pairs)],
                             axis=1)
        dlt = ys - _segsum(ys, bdm)
        yn = dlt * lax.rsqrt(_segsum(dlt * dlt, bdm, pieces=1) + LNX_EPS)
        yb = (yn * lnx_ref[0:1] + lnx_ref[1:2] + bonus_scr[rows, :]) * g_scr[rows, :]
        yb_ref[rows, :] = yb.astype(BF16)
    for z in range(2 * pairs):
        sf_ref[0, z % 2, 2 * (z // 2)] = s_scr[z, 0:B_HEAD, 0:B_HEAD]
        sf_ref[0, z % 2, 2 * (z // 2) + 1] = s_scr[z, B_HEAD:PAIR, B_HEAD:PAIR]


def _rwkv_mix(fb, s0, p, n_seq, seq_len, pairs=2, unroll=4):
    t = fb.shape[0]
    nc = seq_len // CHUNK
    w = pairs * PAIR
    nw = B_WIDTH // w
    lora_w = 2 * LORA_W + 2 * LORA_A + LORA_G
    assert seq_len % PREP_ROWS == 0 and (3 * B_WIDTH) % lora_w == 0
    col = lambda rows, base: pl.BlockSpec((rows, w), lambda b, q: (0, base + q))
    seq = lambda base: pl.BlockSpec((seq_len, w), lambda b, q: (b, base + q))
    lora_blk = 3 * B_WIDTH // lora_w
    sblk = pl.BlockSpec((1, 2, 2 * pairs, B_HEAD, B_HEAD), lambda b, q: (b, 0, q, 0, 0))
    mat = lambda dtype: pltpu.VMEM((2 * pairs, nc, PAIR, PAIR), dtype)
    in_specs = [
        seq(0), seq(nw), seq(2 * nw),
        pl.BlockSpec((seq_len, lora_w), lambda b, q: (b, lora_blk)),
        col(2, 0), col(2, nw), col(2, 2 * nw),
        pl.BlockSpec((2, lora_w), lambda b, q: (0, lora_blk)),
        col(2, 0),
        pl.BlockSpec((2, 2 * LORA_W, w), lambda b, q: (0, 0, q)),
        col(2, 0),
        pl.BlockSpec((2, 2 * LORA_A, w), lambda b, q: (0, 0, q)),
        col(LORA_G, 0), col(3, 0), col(2, 0),
        pl.BlockSpec((w, w), lambda b, q: (0, 0)),
        pl.BlockSpec((w, w), lambda b, q: (0, 0)),
    ]
    operands = [fb, fb, fb, fb, p["mu"], p["mu"], p["mu"], p["mu"], p["w0"], p["w2"], p["a0"], p["a2"],
                p["g2"], p["kvec"], p["lnx"], _block_diag(1.0, w), _block_diag(1.0 / B_HEAD, w)]
    if s0 is not None:
        in_specs.append(sblk)
        operands.append(s0)
    return pl.pallas_call(
        functools.partial(_rwkv_kernel, seq_len=seq_len, has_state=s0 is not None, pairs=pairs,
                          unroll=unroll),
        grid=(n_seq, nw),
        in_specs=in_specs,
        out_specs=[pl.BlockSpec((seq_len, w), lambda b, q: (b, q)), sblk],
        out_shape=[
            jax.ShapeDtypeStruct((t, B_WIDTH), BF16),
            jax.ShapeDtypeStruct((n_seq, 2, B_HEADS, B_HEAD, B_HEAD), F32),
        ],
        scratch_shapes=[pltpu.VMEM((9, seq_len, w), F32), pltpu.VMEM((seq_len, w), F32),
                        pltpu.VMEM((seq_len, w), F32), mat(BF16), mat(BF16), mat(F32),
                        pltpu.VMEM((2 * pairs, nc, SUBLANES, PAIR), F32),
                        pltpu.VMEM((2 * pairs, seq_len, PAIR), F32),
                        pltpu.VMEM((2 * pairs, PAIR, PAIR), F32)],
        compiler_params=_cparams("parallel", "parallel"),
        name="rwkv_mix",
    )(*operands)


def _even_out_kernel(x_ref, mod_ref, gains_ref, ya_ref, yb_ref, w_ref, o_ref):
    m = (jnp.dot(ya_ref[...], w_ref[0:A_WIDTH, :], preferred_element_type=F32)
         + jnp.dot(yb_ref[...], w_ref[A_WIDTH:, :], preferred_element_type=F32))
    mod = mod_ref[0]
    o_ref[...] = x_ref[...] + mod[2:3] * (_rms(m) * gains_ref[1:2])


def _even_out(x, mod, gains, ya, yb, w, seq_len, mod_row0, tm):
    t = x.shape[0]
    row = lambda i: (i, 0)
    return pl.pallas_call(
        _even_out_kernel,
        grid=(t // tm,),
        in_specs=[
            pl.BlockSpec((tm, D_MODEL), row),
            _mod_spec(tm, seq_len, mod_row0),
            pl.BlockSpec((4, D_MODEL), lambda i: (0, 0)),
            pl.BlockSpec((tm, A_WIDTH), row),
            pl.BlockSpec((tm, B_WIDTH), row),
            pl.BlockSpec((D_MODEL, D_MODEL), lambda i: (0, 0)),
        ],
        out_specs=pl.BlockSpec((tm, D_MODEL), row),
        out_shape=jax.ShapeDtypeStruct((t, D_MODEL), F32),
        compiler_params=_cparams("parallel"),
        name="even_out",
    )(x, mod, gains, ya, yb, w)


def _odd_kernel(x_ref, mod_ref, gains_ref, csc_ref, cl_ref, sl_ref, w_ref, o_ref,
                tc_scr, ts_scr, m_scr, *, seq_len):
    mod = mod_ref[0]
    gd = C_GROUP_DIM
    n_seq = x_ref.shape[0] // seq_len
    csc = csc_ref[...]

    for s in range(n_seq):
        rows = pl.ds(s * seq_len, seq_len)
        h = _modnorm(x_ref[rows, :], gains_ref[0:1], mod[1:2], mod[0:1]).astype(BF16)
        for g in range(C_GROUPS):
            t = jnp.dot(h[:, g * gd:(g + 1) * gd], csc, preferred_element_type=F32)
            tc_scr[rows, g * gd:(g + 1) * gd] = t[:, 0:gd].astype(BF16)
            ts_scr[rows, g * gd:(g + 1) * gd] = t[:, gd:].astype(BF16)

    def position_dft(s, cb):
        rows = pl.ds(s * seq_len, seq_len)
        cols = pl.ds(cb * ODD_COLS, ODD_COLS)
        f = (jnp.dot(cl_ref[...], tc_scr[rows, cols], preferred_element_type=F32)
             - jnp.dot(sl_ref[...], ts_scr[rows, cols], preferred_element_type=F32))
        yield
        part = jnp.dot(f.astype(BF16), w_ref[cols, :], preferred_element_type=F32)
        yield
        if cb == 0:
            m_scr[rows, :] = part
        else:
            m_scr[rows, :] += part

    _round_robin(position_dft(s, cb) for s in range(n_seq) for cb in range(D_MODEL // ODD_COLS))
    o_ref[...] = x_ref[...] + mod[2:3] * (_rms(m_scr[...]) * gains_ref[1:2])


def _dft_tables(n):
    idx = np.arange(n, dtype=np.int64)
    ang = 2.0 * np.pi * ((idx[:, None] * idx[None, :]) % n).astype(np.float64) / n
    s = 1.0 / math.sqrt(n)
    return np.cos(ang) * s, np.sin(ang) * s


def _odd_mixer(x, mod, gains, w, n_seq, seq_len, mod_row0):
    cc, sc = _dft_tables(C_GROUP_DIM)
    cl, sl = _dft_tables(seq_len)
    csc, cl, sl = (jnp.asarray(a, F32).astype(BF16) for a in (np.concatenate([cc, sc], axis=1), cl, sl))
    full = lambda shape: pl.BlockSpec(shape, lambda i: (0,) * len(shape))
    row = lambda i: (i, 0)
    tm = max(seq_len, ODD_ROWS)
    assert tm % seq_len == 0 and (n_seq * seq_len) % tm == 0
    return pl.pallas_call(
        functools.partial(_odd_kernel, seq_len=seq_len),
        grid=(n_seq * seq_len // tm,),
        in_specs=[
            pl.BlockSpec((tm, D_MODEL), row),
            _mod_spec(tm, seq_len, mod_row0),
            full((4, D_MODEL)), full((C_GROUP_DIM, 2 * C_GROUP_DIM)),
            full((seq_len, seq_len)), full((seq_len, seq_len)), full((D_MODEL, D_MODEL)),
        ],
        out_specs=pl.BlockSpec((tm, D_MODEL), row),
        out_shape=jax.ShapeDtypeStruct(x.shape, F32),
        scratch_shapes=[pltpu.VMEM((tm, D_MODEL), BF16), pltpu.VMEM((tm, D_MODEL), BF16),
                        pltpu.VMEM((tm, D_MODEL), F32)],
        compiler_params=_cparams("parallel"),
        name="odd_mixer",
    )(x, mod, gains, csc, cl, sl, w)


def _ffn_kernel(x_ref, mod_ref, gains_ref, wu_ref, wg_ref, cw_ref, cb_ref, wo_ref, o_ref,
                h_scr, acc_scr, *, seq_len):
    j = pl.program_id(1)
    mod = mod_ref[0]

    @pl.when(j == 0)
    def _():
        h_scr[...] = _modnorm(x_ref[...], gains_ref[2:3], mod[4:5], mod[3:4]).astype(BF16)
        acc_scr[...] = jnp.zeros_like(acc_scr)

    n_sub = h_scr.shape[0] // FF_ROWS
    gate = [None] * n_sub
    zero_row = jnp.zeros((1, FF_CHUNK), F32)
    wu = wu_ref[...].astype(BF16)
    wg = wg_ref[...].astype(BF16)
    wo = wo_ref[...].astype(BF16)

    def sub_tile(i):
        rows = pl.ds(i * FF_ROWS, FF_ROWS)
        h = h_scr[rows, :]
        u = jnp.dot(h, wu, preferred_element_type=F32)
        g = gate[i] = jnp.dot(h, wg, preferred_element_type=F32)
        yield
        before = zero_row if (i * FF_ROWS) % seq_len == 0 else gate[i - 1][FF_ROWS - 1:FF_ROWS, :]
        after = zero_row if ((i + 1) * FF_ROWS) % seq_len == 0 else gate[i + 1][0:1, :]
        pos = lax.broadcasted_iota(jnp.int32, (FF_ROWS, 1), 0)
        prev = jnp.where(pos == 0, before, pltpu.roll(g, 1, 0))
        nxt = jnp.where(pos == FF_ROWS - 1, after, pltpu.roll(g, FF_ROWS - 1, 0))
        gc = prev * cw_ref[0:1] + g * cw_ref[1:2] + nxt * cw_ref[2:3] + cb_ref[...]
        act = (_silu(gc) * u).astype(BF16)
        yield
        acc_scr[rows, :] += jnp.dot(act, wo, preferred_element_type=F32)

    _round_robin(sub_tile(i) for i in range(n_sub))

    @pl.when(j == pl.num_programs(1) - 1)
    def _():
        o_ref[...] = x_ref[...] + mod[5:6] * (_rms(acc_scr[...]) * gains_ref[3:4])


def _conv_ffn(x, mod, gains, layer, w_in, conv_w, conv_b, w_out, seq_len, mod_row0, tm):
    t = x.shape[0]
    nj = D_FF // FF_CHUNK
    assert seq_len % FF_ROWS == 0 and tm % seq_len == 0
    return pl.pallas_call(
        functools.partial(_ffn_kernel, seq_len=seq_len),
        grid=(t // tm, nj),
        in_specs=[
            pl.BlockSpec((tm, D_MODEL), lambda i, j: (i, 0)),
            _mod_spec(tm, seq_len, mod_row0),
            pl.BlockSpec((4, D_MODEL), lambda i, j: (0, 0)),
            pl.BlockSpec((None, D_MODEL, FF_CHUNK), lambda i, j: (layer, 0, j)),
            pl.BlockSpec((None, D_MODEL, FF_CHUNK), lambda i, j: (layer, 0, nj + j)),
            pl.BlockSpec((3, FF_CHUNK), lambda i, j: (0, j)),
            pl.BlockSpec((1, FF_CHUNK), lambda i, j: (0, j)),
            pl.BlockSpec((None, FF_CHUNK, D_MODEL), lambda i, j: (layer, j, 0)),
        ],
        out_specs=pl.BlockSpec((tm, D_MODEL), lambda i, j: (i, 0)),
        out_shape=jax.ShapeDtypeStruct((t, D_MODEL), F32),
        scratch_shapes=[pltpu.VMEM((tm, D_MODEL), BF16), pltpu.VMEM((tm, D_MODEL), F32)],
        compiler_params=_cparams("parallel", "arbitrary"),
        name="conv_ffn",
    )(x, mod, gains, w_in, w_in, conv_w, conv_b, w_out)


def _rope_tables(seq_len):
    rows = seq_len // GRID_W
    row = jnp.repeat(jnp.arange(rows, dtype=F32), GRID_W)
    col = jnp.tile(jnp.arange(GRID_W, dtype=F32), rows)
    n_freq = A_HALF // 4
    inv = ROPE_BASE ** (-jnp.arange(n_freq, dtype=F32) / n_freq)
    ang = jnp.concatenate([row[:, None] * inv, col[:, None] * inv], axis=-1)
    cos_t = jnp.tile(jnp.repeat(jnp.cos(ang), 2, axis=1), (1, 2))
    sin_t = jnp.tile(jnp.repeat(jnp.sin(ang), 2, axis=1), (1, 2))
    sign = jnp.where(jnp.arange(A_HEAD_DIM) % 2 == 0, -1.0, 1.0).astype(F32)
    return cos_t, sin_t * sign


def _pad_lora(w):
    z = jnp.zeros_like(w[0])
    return jnp.stack([jnp.concatenate([w[0], z], axis=0), jnp.concatenate([z, w[1]], axis=0)])


def _block_diag(value, width):
    head = np.arange(width) // B_HEAD
    return jnp.asarray(np.where(head[:, None] == head[None, :], value, 0.0), BF16)


def kernel(x_prompt, x_sample, cache_k, cache_v, state_wkv, c, c_ctx, w_ada, b_ada, norm_gains,
           w_in_even, w_out_even, diff_lambda, diff_subln, rwkv_shift_mu, rwkv_w0, rwkv_w2, rwkv_a0,
           rwkv_a2, rwkv_g2, rwkv_kvec, rwkv_lnx, w_out_odd, w_ffn_in, ffn_conv, ffn_conv_b,
           w_ffn_out):
    n_ctx, l_ctx, _ = x_prompt.shape
    n_lat, l_lat, _ = x_sample.shape
    assert 1 + n_lat <= MOD_ROWS
    cvec = jnp.concatenate(
        [c_ctx[None, :], c, jnp.zeros((MOD_ROWS - 1 - n_lat, D_MODEL), F32)], axis=0)
    mods = _modulation(cvec, w_ada, b_ada).reshape(DEPTH, MOD_ROWS, 6, D_MODEL)

    per_layer_bf16 = lambda w: [w[i].astype(BF16) for i in range(w.shape[0])]
    w_in_even_b = per_layer_bf16(w_in_even)
    w_out_even_b = per_layer_bf16(w_out_even)
    w_out_odd_b = per_layer_bf16(w_out_odd)
    cos_t, sin_t = _rope_tables(l_lat)

    def run_group(x, n_seq, seq_len, mod_row0, latent):
        ctx_out = None
        for l in range(DEPTH):
            mod = mods[l]
            gains = norm_gains[l]
            if l % 2 == 0:
                e = l // 2
                lam_init = 0.8 - 0.6 * math.exp(-0.3 * l)
                q, k, v, fb, *kv_cache = _even_in(
                    x, mod, gains, w_in_even_b[e], seq_len, mod_row0, min(seq_len, IN_TILE_ROWS),
                    rope=(cos_t, sin_t) if latent else None, cache=not latent)
                if latent:
                    past = cache_k.shape[2]
                    ctx = (cache_k[:, e].reshape(n_seq * past, A_WIDTH),
                           cache_v[:, e].reshape(n_seq * past, A_WIDTH))
                    s0 = state_wkv[:, e]
                else:
                    ctx = None
                    s0 = None
                ya = _attention(q, k, v, diff_lambda[e], diff_subln[e][None, :], lam_init,
                                n_seq, seq_len, ctx)
                rwkv = {
                    "mu": rwkv_shift_mu[e], "w0": rwkv_w0[e], "w2": _pad_lora(rwkv_w2[e]).astype(BF16),
                    "a0": rwkv_a0[e], "a2": _pad_lora(rwkv_a2[e]).astype(BF16),
                    "g2": rwkv_g2[e].astype(BF16), "kvec": rwkv_kvec[e], "lnx": rwkv_lnx[e],
                }
                yb, s_fin = _rwkv_mix(fb, s0, rwkv, n_seq, seq_len, pairs=4 if seq_len <= 256 else 2)
                x = _even_out(x, mod, gains, ya, yb, w_out_even_b[e], seq_len, mod_row0, OUT_TILE_ROWS)
                if not latent:
                    ctx_out = (*kv_cache, s_fin)
            else:
                x = _odd_mixer(x, mod, gains, w_out_odd_b[l // 2], n_seq, seq_len, mod_row0)
            x = _conv_ffn(x, mod, gains, l, w_ffn_in, ffn_conv[l], ffn_conv_b[l][None, :],
                          w_ffn_out, seq_len, mod_row0, max(seq_len, FFN_TILE_ROWS))
        return x, ctx_out

    y_ctx, (k_new, v_new, s_new) = run_group(
        x_prompt.reshape(n_ctx * l_ctx, D_MODEL), n_ctx, l_ctx, 0, False)
    y_lat, _ = run_group(x_sample.reshape(n_lat * l_lat, D_MODEL), n_lat, l_lat, 1, True)

    n_even = (DEPTH + 1) // 2
    assert n_even == 1
    return (
        y_ctx.reshape(n_ctx, l_ctx, D_MODEL),
        y_lat.reshape(n_lat, l_lat, D_MODEL),
        k_new,
        v_new,
        s_new.reshape(n_ctx, n_even, 2, B_HEADS, B_HEAD, B_HEAD),
    )
```

```python
import functools
import math

import numpy as np
import jax
import jax.numpy as jnp
from jax import lax
from jax.experimental import pallas as pl
from jax.experimental.pallas import tpu as pltpu

F32 = jnp.float32
BF16 = jnp.bfloat16

D_MODEL = 1024
DEPTH = 2
GRID_W = 64
A_WIDTH = D_MODEL // 2
A_HEADS = 4
A_HEAD_DIM = A_WIDTH // A_HEADS
A_HALF = A_HEAD_DIM // 2
B_WIDTH = D_MODEL - A_WIDTH
B_HEAD = 64
B_HEADS = B_WIDTH // B_HEAD
LORA_W = 64
LORA_A = 64
LORA_G = 128
IN_B = 3 * B_WIDTH + 2 * LORA_W + 2 * LORA_A + LORA_G
IN_EVEN = 3 * A_WIDTH + IN_B
C_GROUPS = 8
C_GROUP_DIM = D_MODEL // C_GROUPS
D_FF = 2816
ROPE_BASE = 10000.0
RMS_EPS = 1e-6
LNX_EPS = 64e-5

LANES = 128
SUBLANES = 8
PAIR = 2 * B_HEAD
CHUNK = 64
MOD_COLS = 1536
IN_TILE_ROWS = 512
OUT_TILE_ROWS = 1024
FFN_TILE_ROWS = 1024
FF_CHUNK = 256
FF_ROWS = 256
PREP_ROWS = 256
ATTN_Q_ROWS = 256
ODD_ROWS = 1024
ODD_COLS = 256
VMEM_LIMIT = 48 * 1024 * 1024
MOD_ROWS = 8


def _cparams(*sem):
    return pltpu.CompilerParams(dimension_semantics=sem, vmem_limit_bytes=VMEM_LIMIT)


def _sigmoid(x):
    return 1.0 / (1.0 + jnp.exp(-x))


def _silu(x):
    return x * (0.5 + 0.5 * jnp.tanh(0.5 * x))


def _dot(a, b):
    return jnp.dot(a.astype(BF16), b.astype(BF16), preferred_element_type=F32)


_NN = (((1,), (0,)), ((), ()))
_NT = (((1,), (1,)), ((), ()))
_TN = (((0,), (0,)), ((), ()))


def _split_bf16(x, n):
    parts = []
    rem = x
    for i in range(n):
        p = rem.astype(BF16)
        parts.append(p)
        if i + 1 < n:
            rem = rem - p.astype(F32)
    return parts


def _mm(a, b, dims):
    return lax.dot_general(a.astype(BF16), b.astype(BF16), dims, preferred_element_type=F32)


def _segsum(x, bd, pieces=2):
    acc = None
    for p in _split_bf16(x, pieces):
        t = jnp.dot(p, bd, preferred_element_type=F32)
        acc = t if acc is None else acc + t
    return acc


def _rms(x):
    return x * lax.rsqrt(jnp.mean(x * x, axis=-1, keepdims=True) + RMS_EPS)


def _modnorm(x, gain, scale, shift):
    return _rms(x) * gain * (1.0 + scale) + shift


def _mod_kernel(c_ref, w_ref, b_ref, o_ref):
    c = c_ref[...]
    s = c * _sigmoid(c)
    o_ref[0] = _dot(s, w_ref[0]) + b_ref[0]


def _modulation(cvec, w_ada, b_ada):
    tn = MOD_COLS
    n = 6 * D_MODEL
    assert n % tn == 0
    return pl.pallas_call(
        _mod_kernel,
        grid=(DEPTH, n // tn),
        in_specs=[
            pl.BlockSpec((MOD_ROWS, D_MODEL), lambda l, j: (0, 0)),
            pl.BlockSpec((1, D_MODEL, tn), lambda l, j: (l, 0, j)),
            pl.BlockSpec((1, 1, tn), lambda l, j: (l, 0, j)),
        ],
        out_specs=pl.BlockSpec((1, MOD_ROWS, tn), lambda l, j: (l, 0, j)),
        out_shape=jax.ShapeDtypeStruct((DEPTH, MOD_ROWS, n), F32),
        compiler_params=_cparams("parallel", "parallel"),
        name="modulation",
    )(cvec, w_ada, b_ada.reshape(DEPTH, 1, n))


def _mod_spec(tm, seq_len, mod_row0):
    if mod_row0 == 0:
        return pl.BlockSpec((1, 6, D_MODEL), lambda i, *_: (0, 0, 0))
    return pl.BlockSpec((1, 6, D_MODEL), lambda i, *_: (mod_row0 + (i * tm) // seq_len, 0, 0))


def _rope(x, cos_t, sin_t):
    lane = lax.broadcasted_iota(jnp.int32, x.shape, 1)
    nxt = pltpu.roll(x, LANES - 1, 1)
    prv = pltpu.roll(x, 1, 1)
    partner = jnp.where(lane % 2 == 0, nxt, prv)
    return x * cos_t + partner * sin_t


def _even_in_kernel(*refs, rope, cache):
    x_ref, mod_ref, gains_ref, w_ref = refs[:4]
    refs = refs[4:]
    if rope:
        cos_ref, sin_ref = refs[:2]
        refs = refs[2:]
    q_ref, k_ref, v_ref, fb_ref = refs[:4]
    cache_refs = refs[4:]
    mod = mod_ref[0]
    h = _modnorm(x_ref[...], gains_ref[0:1], mod[1:2], mod[0:1]).astype(BF16)
    a = A_WIDTH
    hd = A_HEAD_DIM
    for n, out_ref in enumerate((q_ref, k_ref, v_ref)):
        y = jnp.dot(h, w_ref[:, n * a:(n + 1) * a], preferred_element_type=F32)
        if cache and n > 0:
            rows_c = cache_refs[n - 1].shape[2]
            for sq in range(cache_refs[n - 1].shape[0]):
                for hh in range(A_HEADS):
                    cache_refs[n - 1][sq, 0, :, hh, :] = y[sq * rows_c:(sq + 1) * rows_c, hh * hd:(hh + 1) * hd]
        if rope and n < 2:
            for hh in range(A_HEADS):
                cols = slice(hh * hd, (hh + 1) * hd)
                out_ref[:, cols] = _rope(y[:, cols], cos_ref[...], sin_ref[...]).astype(BF16)
        else:
            out_ref[...] = y.astype(BF16)
    fb_ref[...] = jnp.dot(h, w_ref[:, 3 * a:], preferred_element_type=F32)


def _even_in(x, mod, gains, w, seq_len, mod_row0, tm, rope=None, cache=False):
    t = x.shape[0]
    assert seq_len % tm == 0 or (tm % seq_len == 0 and rope is None and mod_row0 == 0)
    per_seq = max(seq_len // tm, 1)
    seqs = max(tm // seq_len, 1)
    row = lambda i: (i, 0)
    operands = [x, mod, gains, w]
    in_specs = [
        pl.BlockSpec((tm, D_MODEL), row),
        _mod_spec(tm, seq_len, mod_row0),
        pl.BlockSpec((4, D_MODEL), lambda i: (0, 0)),
        pl.BlockSpec((D_MODEL, IN_EVEN), lambda i: (0, 0)),
    ]
    if rope is not None:
        tab = pl.BlockSpec((tm, A_HEAD_DIM), lambda i: (i % per_seq, 0))
        operands += list(rope)
        in_specs += [tab, tab]
    out_specs = [pl.BlockSpec((tm, A_WIDTH), row)] * 3 + [pl.BlockSpec((tm, IN_B), row)]
    out_shape = [jax.ShapeDtypeStruct((t, A_WIDTH), BF16)] * 3 + [jax.ShapeDtypeStruct((t, IN_B), F32)]
    if cache:
        out_specs += [pl.BlockSpec((seqs, 1, tm // seqs, A_HEADS, A_HEAD_DIM),
                                   lambda i: (i // per_seq, 0, i % per_seq, 0, 0))] * 2
        out_shape += [jax.ShapeDtypeStruct((t // seq_len, 1, seq_len, A_HEADS, A_HEAD_DIM), F32)] * 2
    return pl.pallas_call(
        functools.partial(_even_in_kernel, rope=rope is not None, cache=cache),
        grid=(t // tm,),
        in_specs=in_specs,
        out_specs=out_specs,
        out_shape=out_shape,
        compiler_params=_cparams("parallel"),
        name="even_in",
    )(*operands)


def _attn_kernel(*refs, has_ctx, lam_init):
    if has_ctx:
        lam_ref, sub_ref, q_ref, k_ref, v_ref, ck_ref, cv_ref, o_ref = refs
    else:
        lam_ref, sub_ref, q_ref, k_ref, v_ref, o_ref = refs
    lp = lam_ref[...]
    l1 = jnp.sum(lp[0:1] * lp[1:2], axis=-1, keepdims=True)
    l2 = jnp.sum(lp[2:3] * lp[3:4], axis=-1, keepdims=True)
    lam = jnp.exp(l1) - jnp.exp(l2) + lam_init
    hd = A_HEAD_DIM
    lane = lax.broadcasted_iota(jnp.int32, (q_ref.shape[0], hd), 1)
    scale = A_HALF ** -0.5

    def head(h):
        cols = pl.ds(h * hd, hd)
        q = q_ref[:, cols]
        k = k_ref[:, cols].astype(BF16)
        if has_ctx:
            ck = ck_ref[:, cols].astype(BF16)
        scores = []
        for m in range(2):
            qm = jnp.where((lane < A_HALF) == (m == 0), q, 0.0).astype(BF16)
            s = lax.dot_general(qm, k, _NT, preferred_element_type=F32) * scale
            sc = lax.dot_general(qm, ck, _NT, preferred_element_type=F32) * scale if has_ctx else None
            scores.append((s, sc))
        yield
        probs = []
        for s, sc in scores:
            mx = jnp.max(s, axis=-1, keepdims=True)
            if has_ctx:
                mx = jnp.maximum(mx, jnp.max(sc, axis=-1, keepdims=True))
                ec = jnp.exp(sc - mx)
            e = jnp.exp(s - mx)
            z = jnp.sum(e, axis=-1, keepdims=True)
            if has_ctx:
                z = z + jnp.sum(ec, axis=-1, keepdims=True)
            inv = 1.0 / z
            probs.append((e * inv, ec * inv if has_ctx else None))
        w = (probs[0][0] - lam * probs[1][0]).astype(BF16)
        if has_ctx:
            wc = (probs[0][1] - lam * probs[1][1]).astype(BF16)
        yield
        o = jnp.dot(w, v_ref[:, cols].astype(BF16), preferred_element_type=F32)
        if has_ctx:
            o = o + jnp.dot(wc, cv_ref[:, cols].astype(BF16), preferred_element_type=F32)
        yield
        o_ref[:, cols] = (_rms(o) * sub_ref[...] * (1.0 - lam_init)).astype(BF16)

    _round_robin(head(h) for h in range(A_HEADS))


def _attention(q, k, v, lam_p, subln, lam_init, n_seq, seq_len, ctx=None):
    t = q.shape[0]
    tq = min(seq_len, ATTN_Q_ROWS)
    nq = seq_len // tq
    small = [pl.BlockSpec((4, A_HALF), lambda b, i: (0, 0)),
             pl.BlockSpec((1, A_HEAD_DIM), lambda b, i: (0, 0))]
    qblk = pl.BlockSpec((tq, A_WIDTH), lambda b, i: (b * nq + i, 0))
    kblk = pl.BlockSpec((seq_len, A_WIDTH), lambda b, i: (b, 0))
    operands = [lam_p, subln, q, k, v]
    in_specs = small + [qblk, kblk, kblk]
    if ctx is not None:
        ck, cv = ctx
        cblk = pl.BlockSpec((ck.shape[0] // n_seq, A_WIDTH), lambda b, i: (b, 0))
        operands += [ck, cv]
        in_specs += [cblk, cblk]
    return pl.pallas_call(
        functools.partial(_attn_kernel, has_ctx=ctx is not None, lam_init=lam_init),
        grid=(n_seq, nq),
        in_specs=in_specs,
        out_specs=qblk,
        out_shape=jax.ShapeDtypeStruct((t, A_WIDTH), BF16),
        compiler_params=_cparams("parallel", "parallel"),
        name="diff_attn",
    )(*operands)


def _cumsum_rows(x, reverse):
    n = x.shape[0]
    ridx = lax.broadcasted_iota(jnp.int32, x.shape, 0)
    s = 1
    while s < n:
        if reverse:
            x = x + jnp.where(ridx < n - s, pltpu.roll(x, n - s, 0), 0.0)
        else:
            x = x + jnp.where(ridx >= s, pltpu.roll(x, s, 0), 0.0)
        s *= 2
    return x


def _half_rows(m, s, first):
    off = 0 if first else s
    return jnp.concatenate([m[b + off:b + off + s] for b in range(0, m.shape[0], 2 * s)], axis=0)


def _spread_rows(h, s, first):
    zero = jnp.zeros((s, h.shape[1]), h.dtype)
    pieces = []
    for i in range(h.shape[0] // s):
        blk = h[i * s:(i + 1) * s]
        pieces += [blk, zero] if first else [zero, blk]
    return jnp.concatenate(pieces, axis=0)


def _round_robin(gens):
    gens = list(gens)
    while gens:
        alive = []
        for gen in gens:
            try:
                next(gen)
                alive.append(gen)
            except StopIteration:
                pass
        gens = alive


def _rwkv_kernel(*refs, seq_len, has_state, pairs, unroll):
    (fr_ref, fk_ref, fv_ref, fl_ref, mur_ref, muk_ref, muv_ref, mul_ref, w0_ref, w2_ref, a0_ref,
     a2_ref, g2_ref, kvec_ref, lnx_ref, bd_ref, bdm_ref) = refs[:17]
    refs = refs[17:]
    if has_state:
        s0_ref = refs[0]
        refs = refs[1:]
    yb_ref, sf_ref, pre_ref, g_scr, bonus_scr, r1_scr, m1_scr, d0_scr, gam_scr, y_scr, s_scr = refs

    def shifted(ref, mu_ref, r0):
        f = ref[r0:r0 + PREP_ROWS, :]
        zero = jnp.zeros((1, f.shape[1]), F32)
        before = zero if r0 == 0 else ref[r0 - 1:r0, :]
        after = zero if r0 + PREP_ROWS == seq_len else ref[r0 + PREP_ROWS:r0 + PREP_ROWS + 1, :]
        local = lax.broadcasted_iota(jnp.int32, (PREP_ROWS, 1), 0)
        prev = jnp.where(local == 0, before, pltpu.roll(f, 1, 0))
        nxt = jnp.where(local == PREP_ROWS - 1, after, pltpu.roll(f, PREP_ROWS - 1, 0))
        return f * (1.0 - mu_ref[0:1] - mu_ref[1:2]) + mu_ref[0:1] * prev + mu_ref[1:2] * nxt

    bd = bd_ref[...]
    kv = kvec_ref[...]
    for r0 in range(0, seq_len, PREP_ROWS):
        rows = slice(r0, r0 + PREP_ROWS)
        r = shifted(fr_ref, mur_ref, r0)
        k = shifted(fk_ref, muk_ref, r0)
        v = shifted(fv_ref, muv_ref, r0)
        lora = shifted(fl_ref, mul_ref, r0)
        wd = jnp.tanh(lora[:, 0:2 * LORA_W])
        ad = lora[:, 2 * LORA_W:2 * LORA_W + 2 * LORA_A]
        gd = lora[:, 2 * LORA_W + 2 * LORA_A:]
        g_scr[rows, :] = _dot(_sigmoid(gd), g2_ref[...])
        kk = k * kv[0:1]
        kk = kk * lax.rsqrt(_segsum(kk * kk, bd, pieces=1) + 1e-12)
        pre_ref[0, rows, :] = r
        pre_ref[1, rows, :] = v
        pre_ref[2, rows, :] = kk
        ksum = None
        for d in range(2):
            w_raw = w0_ref[d:d + 1] + _dot(wd, w2_ref[d])
            pre_ref[7 + d, rows, :] = -math.exp(-0.5) * _sigmoid(w_raw)
            a = _sigmoid(a0_ref[d:d + 1] + _dot(ad, a2_ref[d]))
            kd = k * (1.0 + (a - 1.0) * kv[1:2])
            pre_ref[3 + d, rows, :] = kd
            pre_ref[5 + d, rows, :] = kk * a
            ksum = kd if ksum is None else ksum + kd
        bonus_scr[rows, :] = _segsum(r * ksum * kv[2:3], bd) * v

    nc = seq_len // CHUNK
    c2 = 2 * CHUNK
    lane = lax.broadcasted_iota(jnp.int32, (CHUNK, PAIR), 1)
    head0 = lane < B_HEAD
    row = lax.broadcasted_iota(jnp.int32, (c2, c2), 0)
    col = lax.broadcasted_iota(jnp.int32, (c2, c2), 1)
    same = (row // CHUNK) == (col // CHUNK)
    eye = jnp.where(row == col, 1.0, 0.0)
    xor_rc = jnp.bitwise_xor(row, col)
    mm = _mm

    def stack(x):
        return jnp.concatenate([jnp.where(head0, x, 0.0), jnp.where(head0, 0.0, x)], axis=0)

    def fold(x):
        return x[0:CHUNK] + x[CHUNK:c2]

    def chunk_rows(cc):
        return pl.ds(pl.multiple_of(cc * CHUNK, CHUNK), CHUNK)

    def chunk_local(cc, d, q):
        rows = chunk_rows(cc)
        lanes = pl.ds(q * PAIR, PAIR)
        z = 2 * q + d
        r = pre_ref[0, rows, lanes]
        v = pre_ref[1, rows, lanes]
        kk = pre_ref[2, rows, lanes]
        k = pre_ref[3 + d, rows, lanes]
        b = pre_ref[5 + d, rows, lanes]
        lw = pre_ref[7 + d, rows, lanes]
        incl = same & ((col <= row) if d == 0 else (col >= row))
        strict = same & ((col < row) if d == 0 else (col > row))
        g = _cumsum_rows(lw, reverse=(d == 1))
        gp = g - lw
        gtot = g[CHUNK - 1:CHUNK, :] if d == 0 else g[0:1, :]
        gm = 0.5 * gtot
        kx = k * jnp.exp(gm - g)
        bx = b * jnp.exp(gm - g)
        lhs = jnp.concatenate([stack(r * jnp.exp(g - gm)), stack(kk * jnp.exp(gp - gm))], axis=0)
        rhs = jnp.concatenate([kx, kx, bx, bx], axis=0)
        a_all = mm(lhs, rhs, _NT)
        yield
        a_rk = jnp.where(incl, a_all[0:c2, 0:c2], 0.0)
        a_rb = jnp.where(incl, a_all[0:c2, c2:2 * c2], 0.0)
        a_kk = jnp.where(strict, a_all[c2:2 * c2, 0:c2], 0.0)
        n_mat = jnp.where(strict, a_all[c2:2 * c2, c2:2 * c2], 0.0)
        vs = stack(v)
        akv = mm(a_kk, vs, _NN)
        y0 = mm(a_rk, vs, _NN)
        lower = (col < row) if d == 0 else (col > row)
        p_inv = eye - jnp.where(lower & (xor_rc == 1), n_mat, 0.0)
        yield
        s = 2
        while s < SUBLANES:
            c_s = jnp.where(lower & (xor_rc >= s) & (xor_rc < 2 * s), n_mat, 0.0)
            t = mm(c_s, p_inv, _NN)
            yield
            p_inv = p_inv - mm(p_inv, t, _NN)
            yield
            s *= 2
        while s < CHUNK:
            c_s = jnp.where(lower & (xor_rc >= s) & (xor_rc < 2 * s), n_mat, 0.0)
            t = _spread_rows(mm(_half_rows(c_s, s, d == 1), p_inv, _NN), s, d == 1)
            yield
            upd = mm(_half_rows(p_inv, s, d == 1), t, _NN)
            p_inv = p_inv - _spread_rows(upd, s, d == 1)
            yield
            s *= 2
        x = jnp.concatenate([stack(kk * jnp.exp(gp)), akv], axis=1)
        w12 = mm(p_inv, x, _NN)
        yield
        arb_w = mm(a_rb, w12, _NN)
        et = jnp.exp(gtot - g)
        bes = stack(b * et)
        d0_scr[z, cc] = mm(jnp.concatenate([vs, -w12[:, PAIR:]], axis=0),
                           jnp.concatenate([stack(k * et), bes], axis=0), _TN)
        m1_scr[z, cc] = mm(w12[:, 0:PAIR], bes, _TN).astype(BF16)
        yield
        r1_scr[z, cc] = (stack(r * jnp.exp(g)) - arb_w[:, 0:PAIR]).astype(BF16)
        y_scr[z, rows, :] = fold(y0 - arb_w[:, PAIR:])
        gam_scr[z, cc] = jnp.broadcast_to(jnp.exp(gtot), (SUBLANES, PAIR))

    def local_body(grp, carry):
        _round_robin(chunk_local(grp * unroll + u, d, q)
                     for u in range(unroll) for q in range(pairs) for d in range(2))
        return carry

    lax.fori_loop(0, nc // unroll, local_body, 0)

    for z in range(2 * pairs):
        s_scr[z] = jnp.zeros((PAIR, PAIR), F32)
        if has_state:
            s_scr[z, 0:B_HEAD, 0:B_HEAD] = s0_ref[0, z % 2, 2 * (z // 2)]
            s_scr[z, B_HEAD:PAIR, B_HEAD:PAIR] = s0_ref[0, z % 2, 2 * (z // 2) + 1]

    def carry_body(grp, carry):
        states = [s_scr[z] for z in range(2 * pairs)]
        for u in range(unroll):
            c = grp * unroll + u
            for z in range(2 * pairs):
                cc = c if z % 2 == 0 else nc - 1 - c
                rows = chunk_rows(cc)
                s0 = states[z]
                y_scr[z, rows, :] += fold(mm(r1_scr[z, cc], s0, _NT))
                states[z] = s0 * gam_scr[z, cc, 0:1, :] - mm(s0, m1_scr[z, cc], _NN) + d0_scr[z, cc]
        for z in range(2 * pairs):
            s_scr[z] = states[z]
        return carry

    lax.fori_loop(0, nc // unroll, carry_body, 0)
    bdm = bdm_ref[...]
    for r0 in range(0, seq_len, PREP_ROWS):
        rows = slice(r0, r0 + PREP_ROWS)
        ys = jnp.concatenate([y_scr[2 * q, rows, :] + y_scr[2 * q + 1, rows, :] for q in range(pairs)],
                             axis=1)
        dlt = ys - _segsum(ys, bdm)
        yn = dlt * lax.rsqrt(_segsum(dlt * dlt, bdm, pieces=1) + LNX_EPS)
        yb = (yn * lnx_ref[0:1] + lnx_ref[1:2] + bonus_scr[rows, :]) * g_scr[rows, :]
        yb_ref[rows, :] = yb.astype(BF16)
    for z in range(2 * pairs):
        sf_ref[0, z % 2, 2 * (z // 2)] = s_scr[z, 0:B_HEAD, 0:B_HEAD]
        sf_ref[0, z % 2, 2 * (z // 2) + 1] = s_scr[z, B_HEAD:PAIR, B_HEAD:PAIR]


def _rwkv_mix(fb, s0, p, n_seq, seq_len, pairs=2, unroll=4):
    t = fb.shape[0]
    nc = seq_len // CHUNK
    w = pairs * PAIR
    nw = B_WIDTH // w
    lora_w = 2 * LORA_W + 2 * LORA_A + LORA_G
    assert seq_len % PREP_ROWS == 0 and (3 * B_WIDTH) % lora_w == 0
    col = lambda rows, base: pl.BlockSpec((rows, w), lambda b, q: (0, base + q))
    seq = lambda base: pl.BlockSpec((seq_len, w), lambda b, q: (b, base + q))
    lora_blk = 3 * B_WIDTH // lora_w
    sblk = pl.BlockSpec((1, 2, 2 * pairs, B_HEAD, B_HEAD), lambda b, q: (b, 0, q, 0, 0))
    mat = lambda dtype: pltpu.VMEM((2 * pairs, nc, PAIR, PAIR), dtype)
    in_specs = [
        seq(0), seq(nw), seq(2 * nw),
        pl.BlockSpec((seq_len, lora_w), lambda b, q: (b, lora_blk)),
        col(2, 0), col(2, nw), col(2, 2 * nw),
        pl.BlockSpec((2, lora_w), lambda b, q: (0, lora_blk)),
        col(2, 0),
        pl.BlockSpec((2, 2 * LORA_W, w), lambda b, q: (0, 0, q)),
        col(2, 0),
        pl.BlockSpec((2, 2 * LORA_A, w), lambda b, q: (0, 0, q)),
        col(LORA_G, 0), col(3, 0), col(2, 0),
        pl.BlockSpec((w, w), lambda b, q: (0, 0)),
        pl.BlockSpec((w, w), lambda b, q: (0, 0)),
    ]
    operands = [fb, fb, fb, fb, p["mu"], p["mu"], p["mu"], p["mu"], p["w0"], p["w2"], p["a0"], p["a2"],
                p["g2"], p["kvec"], p["lnx"], _block_diag(1.0, w), _block_diag(1.0 / B_HEAD, w)]
    if s0 is not None:
        in_specs.append(sblk)
        operands.append(s0)
    return pl.pallas_call(
        functools.partial(_rwkv_kernel, seq_len=seq_len, has_state=s0 is not None, pairs=pairs,
                          unroll=unroll),
        grid=(n_seq, nw),
        in_specs=in_specs,
        out_specs=[pl.BlockSpec((seq_len, w), lambda b, q: (b, q)), sblk],
        out_shape=[
            jax.ShapeDtypeStruct((t, B_WIDTH), BF16),
            jax.ShapeDtypeStruct((n_seq, 2, B_HEADS, B_HEAD, B_HEAD), F32),
        ],
        scratch_shapes=[pltpu.VMEM((9, seq_len, w), F32), pltpu.VMEM((seq_len, w), F32),
                        pltpu.VMEM((seq_len, w), F32), mat(BF16), mat(BF16), mat(F32),
                        pltpu.VMEM((2 * pairs, nc, SUBLANES, PAIR), F32),
                        pltpu.VMEM((2 * pairs, seq_len, PAIR), F32),
                        pltpu.VMEM((2 * pairs, PAIR, PAIR), F32)],
        compiler_params=_cparams("parallel", "parallel"),
        name="rwkv_mix",
    )(*operands)


def _even_out_kernel(x_ref, mod_ref, gains_ref, ya_ref, yb_ref, w_ref, o_ref):
    m = (jnp.dot(ya_ref[...], w_ref[0:A_WIDTH, :], preferred_element_type=F32)
         + jnp.dot(yb_ref[...], w_ref[A_WIDTH:, :], preferred_element_type=F32))
    mod = mod_ref[0]
    o_ref[...] = x_ref[...] + mod[2:3] * (_rms(m) * gains_ref[1:2])


def _even_out(x, mod, gains, ya, yb, w, seq_len, mod_row0, tm):
    t = x.shape[0]
    row = lambda i: (i, 0)
    return pl.pallas_call(
        _even_out_kernel,
        grid=(t // tm,),
        in_specs=[
            pl.BlockSpec((tm, D_MODEL), row),
            _mod_spec(tm, seq_len, mod_row0),
            pl.BlockSpec((4, D_MODEL), lambda i: (0, 0)),
            pl.BlockSpec((tm, A_WIDTH), row),
            pl.BlockSpec((tm, B_WIDTH), row),
            pl.BlockSpec((D_MODEL, D_MODEL), lambda i: (0, 0)),
        ],
        out_specs=pl.BlockSpec((tm, D_MODEL), row),
        out_shape=jax.ShapeDtypeStruct((t, D_MODEL), F32),
        compiler_params=_cparams("parallel"),
        name="even_out",
    )(x, mod, gains, ya, yb, w)


def _odd_kernel(x_ref, mod_ref, gains_ref, csc_ref, cl_ref, sl_ref, w_ref, o_ref,
                tc_scr, ts_scr, m_scr, *, seq_len):
    mod = mod_ref[0]
    gd = C_GROUP_DIM
    n_seq = x_ref.shape[0] // seq_len
    csc = csc_ref[...]

    for s in range(n_seq):
        rows = pl.ds(s * seq_len, seq_len)
        h = _modnorm(x_ref[rows, :], gains_ref[0:1], mod[1:2], mod[0:1]).astype(BF16)
        for g in range(C_GROUPS):
            t = jnp.dot(h[:, g * gd:(g + 1) * gd], csc, preferred_element_type=F32)
            tc_scr[rows, g * gd:(g + 1) * gd] = t[:, 0:gd].astype(BF16)
            ts_scr[rows, g * gd:(g + 1) * gd] = t[:, gd:].astype(BF16)

    def position_dft(s, cb):
        rows = pl.ds(s * seq_len, seq_len)
        cols = pl.ds(cb * ODD_COLS, ODD_COLS)
        f = (jnp.dot(cl_ref[...], tc_scr[rows, cols], preferred_element_type=F32)
             - jnp.dot(sl_ref[...], ts_scr[rows, cols], preferred_element_type=F32))
        yield
        part = jnp.dot(f.astype(BF16), w_ref[cols, :], preferred_element_type=F32)
        yield
        if cb == 0:
            m_scr[rows, :] = part
        else:
            m_scr[rows, :] += part

    _round_robin(position_dft(s, cb) for s in range(n_seq) for cb in range(D_MODEL // ODD_COLS))
    o_ref[...] = x_ref[...] + mod[2:3] * (_rms(m_scr[...]) * gains_ref[1:2])


def _dft_tables(n):
    idx = np.arange(n, dtype=np.int64)
    ang = 2.0 * np.pi * ((idx[:, None] * idx[None, :]) % n).astype(np.float64) / n
    s = 1.0 / math.sqrt(n)
    return np.cos(ang) * s, np.sin(ang) * s


def _odd_mixer(x, mod, gains, w, n_seq, seq_len, mod_row0):
    cc, sc = _dft_tables(C_GROUP_DIM)
    cl, sl = _dft_tables(seq_len)
    csc, cl, sl = (jnp.asarray(a, F32).astype(BF16) for a in (np.concatenate([cc, sc], axis=1), cl, sl))
    full = lambda shape: pl.BlockSpec(shape, lambda i: (0,) * len(shape))
    row = lambda i: (i, 0)
    tm = max(seq_len, ODD_ROWS)
    assert tm % seq_len == 0 and (n_seq * seq_len) % tm == 0
    return pl.pallas_call(
        functools.partial(_odd_kernel, seq_len=seq_len),
        grid=(n_seq * seq_len // tm,),
        in_specs=[
            pl.BlockSpec((tm, D_MODEL), row),
            _mod_spec(tm, seq_len, mod_row0),
            full((4, D_MODEL)), full((C_GROUP_DIM, 2 * C_GROUP_DIM)),
            full((seq_len, seq_len)), full((seq_len, seq_len)), full((D_MODEL, D_MODEL)),
        ],
        out_specs=pl.BlockSpec((tm, D_MODEL), row),
        out_shape=jax.ShapeDtypeStruct(x.shape, F32),
        scratch_shapes=[pltpu.VMEM((tm, D_MODEL), BF16), pltpu.VMEM((tm, D_MODEL), BF16),
                        pltpu.VMEM((tm, D_MODEL), F32)],
        compiler_params=_cparams("parallel"),
        name="odd_mixer",
    )(x, mod, gains, csc, cl, sl, w)


def _ffn_kernel(x_ref, mod_ref, gains_ref, wu_ref, wg_ref, cw_ref, cb_ref, wo_ref, o_ref,
                h_scr, acc_scr, *, seq_len):
    j = pl.program_id(1)
    mod = mod_ref[0]

    @pl.when(j == 0)
    def _():
        h_scr[...] = _modnorm(x_ref[...], gains_ref[2:3], mod[4:5], mod[3:4]).astype(BF16)
        acc_scr[...] = jnp.zeros_like(acc_scr)

    n_sub = h_scr.shape[0] // FF_ROWS
    gate = [None] * n_sub
    zero_row = jnp.zeros((1, FF_CHUNK), F32)
    wu = wu_ref[...].astype(BF16)
    wg = wg_ref[...].astype(BF16)
    wo = wo_ref[...].astype(BF16)

    def sub_tile(i):
        rows = pl.ds(i * FF_ROWS, FF_ROWS)
        h = h_scr[rows, :]
        u = jnp.dot(h, wu, preferred_element_type=F32)
        g = gate[i] = jnp.dot(h, wg, preferred_element_type=F32)
        yield
        before = zero_row if (i * FF_ROWS) % seq_len == 0 else gate[i - 1][FF_ROWS - 1:FF_ROWS, :]
        after = zero_row if ((i + 1) * FF_ROWS) % seq_len == 0 else gate[i + 1][0:1, :]
        pos = lax.broadcasted_iota(jnp.int32, (FF_ROWS, 1), 0)
        prev = jnp.where(pos == 0, before, pltpu.roll(g, 1, 0))
        nxt = jnp.where(pos == FF_ROWS - 1, after, pltpu.roll(g, FF_ROWS - 1, 0))
        gc = prev * cw_ref[0:1] + g * cw_ref[1:2] + nxt * cw_ref[2:3] + cb_ref[...]
        act = (_silu(gc) * u).astype(BF16)
        yield
        acc_scr[rows, :] += jnp.dot(act, wo, preferred_element_type=F32)

    _round_robin(sub_tile(i) for i in range(n_sub))

    @pl.when(j == pl.num_programs(1) - 1)
    def _():
        o_ref[...] = x_ref[...] + mod[5:6] * (_rms(acc_scr[...]) * gains_ref[3:4])


def _conv_ffn(x, mod, gains, layer, w_in, conv_w, conv_b, w_out, seq_len, mod_row0, tm):
    t = x.shape[0]
    nj = D_FF // FF_CHUNK
    assert seq_len % FF_ROWS == 0 and tm % seq_len == 0
    return pl.pallas_call(
        functools.partial(_ffn_kernel, seq_len=seq_len),
        grid=(t // tm, nj),
        in_specs=[
            pl.BlockSpec((tm, D_MODEL), lambda i, j: (i, 0)),
            _mod_spec(tm, seq_len, mod_row0),
            pl.BlockSpec((4, D_MODEL), lambda i, j: (0, 0)),
            pl.BlockSpec((None, D_MODEL, FF_CHUNK), lambda i, j: (layer, 0, j)),
            pl.BlockSpec((None, D_MODEL, FF_CHUNK), lambda i, j: (layer, 0, nj + j)),
            pl.BlockSpec((3, FF_CHUNK), lambda i, j: (0, j)),
            pl.BlockSpec((1, FF_CHUNK), lambda i, j: (0, j)),
            pl.BlockSpec((None, FF_CHUNK, D_MODEL), lambda i, j: (layer, j, 0)),
        ],
        out_specs=pl.BlockSpec((tm, D_MODEL), lambda i, j: (i, 0)),
        out_shape=jax.ShapeDtypeStruct((t, D_MODEL), F32),
        scratch_shapes=[pltpu.VMEM((tm, D_MODEL), BF16), pltpu.VMEM((tm, D_MODEL), F32)],
        compiler_params=_cparams("parallel", "arbitrary"),
        name="conv_ffn",
    )(x, mod, gains, w_in, w_in, conv_w, conv_b, w_out)


def _rope_tables(seq_len):
    rows = seq_len // GRID_W
    row = jnp.repeat(jnp.arange(rows, dtype=F32), GRID_W)
    col = jnp.tile(jnp.arange(GRID_W, dtype=F32), rows)
    n_freq = A_HALF // 4
    inv = ROPE_BASE ** (-jnp.arange(n_freq, dtype=F32) / n_freq)
    ang = jnp.concatenate([row[:, None] * inv, col[:, None] * inv], axis=-1)
    cos_t = jnp.tile(jnp.repeat(jnp.cos(ang), 2, axis=1), (1, 2))
    sin_t = jnp.tile(jnp.repeat(jnp.sin(ang), 2, axis=1), (1, 2))
    sign = jnp.where(jnp.arange(A_HEAD_DIM) % 2 == 0, -1.0, 1.0).astype(F32)
    return cos_t, sin_t * sign


def _pad_lora(w):
    z = jnp.zeros_like(w[0])
    return jnp.stack([jnp.concatenate([w[0], z], axis=0), jnp.concatenate([z, w[1]], axis=0)])


def _block_diag(value, width):
    head = np.arange(width) // B_HEAD
    return jnp.asarray(np.where(head[:, None] == head[None, :], value, 0.0), BF16)


def kernel(x_prompt, x_sample, cache_k, cache_v, state_wkv, c, c_ctx, w_ada, b_ada, norm_gains,
           w_in_even, w_out_even, diff_lambda, diff_subln, rwkv_shift_mu, rwkv_w0, rwkv_w2, rwkv_a0,
           rwkv_a2, rwkv_g2, rwkv_kvec, rwkv_lnx, w_out_odd, w_ffn_in, ffn_conv, ffn_conv_b,
           w_ffn_out):
    n_ctx, l_ctx, _ = x_prompt.shape
    n_lat, l_lat, _ = x_sample.shape
    assert 1 + n_lat <= MOD_ROWS
    cvec = jnp.concatenate(
        [c_ctx[None, :], c, jnp.zeros((MOD_ROWS - 1 - n_lat, D_MODEL), F32)], axis=0)
    mods = _modulation(cvec, w_ada, b_ada).reshape(DEPTH, MOD_ROWS, 6, D_MODEL)

    per_layer_bf16 = lambda w: [w[i].astype(BF16) for i in range(w.shape[0])]
    w_in_even_b = per_layer_bf16(w_in_even)
    w_out_even_b = per_layer_bf16(w_out_even)
    w_out_odd_b = per_layer_bf16(w_out_odd)
    cos_t, sin_t = _rope_tables(l_lat)

    def run_group(x, n_seq, seq_len, mod_row0, latent):
        ctx_out = None
        for l in range(DEPTH):
            mod = mods[l]
            gains = norm_gains[l]
            if l % 2 == 0:
                e = l // 2
                lam_init = 0.8 - 0.6 * math.exp(-0.3 * l)
                q, k, v, fb, *kv_cache = _even_in(
                    x, mod, gains, w_in_even_b[e], seq_len, mod_row0, IN_TILE_ROWS,
                    rope=(cos_t, sin_t) if latent else None, cache=not latent)
                if latent:
                    past = cache_k.shape[2]
                    ctx = (cache_k[:, e].reshape(n_seq * past, A_WIDTH),
                           cache_v[:, e].reshape(n_seq * past, A_WIDTH))
                    s0 = state_wkv[:, e]
                else:
                    ctx = None
                    s0 = None
                ya = _attention(q, k, v, diff_lambda[e], diff_subln[e][None, :], lam_init,
                                n_seq, seq_len, ctx)
                rwkv = {
                    "mu": rwkv_shift_mu[e], "w0": rwkv_w0[e], "w2": _pad_lora(rwkv_w2[e]).astype(BF16),
                    "a0": rwkv_a0[e], "a2": _pad_lora(rwkv_a2[e]).astype(BF16),
                    "g2": rwkv_g2[e].astype(BF16), "kvec": rwkv_kvec[e], "lnx": rwkv_lnx[e],
                }
                yb, s_fin = _rwkv_mix(fb, s0, rwkv, n_seq, seq_len, pairs=4 if seq_len <= 256 else 2)
                x = _even_out(x, mod, gains, ya, yb, w_out_even_b[e], seq_len, mod_row0, OUT_TILE_ROWS)
                if not latent:
                    ctx_out = (*kv_cache, s_fin)
            else:
                x = _odd_mixer(x, mod, gains, w_out_odd_b[l // 2], n_seq, seq_len, mod_row0)
            x = _conv_ffn(x, mod, gains, l, w_ffn_in, ffn_conv[l], ffn_conv_b[l][None, :],
                          w_ffn_out, seq_len, mod_row0, max(seq_len, FFN_TILE_ROWS))
        return x, ctx_out

    y_ctx, (k_new, v_new, s_new) = run_group(
        x_prompt.reshape(n_ctx * l_ctx, D_MODEL), n_ctx, l_ctx, 0, False)
    y_lat, _ = run_group(x_sample.reshape(n_lat * l_lat, D_MODEL), n_lat, l_lat, 1, True)

    n_even = (DEPTH + 1) // 2
    assert n_even == 1
    return (
        y_ctx.reshape(n_ctx, l_ctx, D_MODEL),
        y_lat.reshape(n_lat, l_lat, D_MODEL),
        k_new,
        v_new,
        s_new.reshape(n_ctx, n_even, 2, B_HEADS, B_HEAD, B_HEAD),
    )
```

```python
import functools
import math

import numpy as np
import jax
import jax.numpy as jnp
from jax import lax
from jax.experimental import pallas as pl
from jax.experimental.pallas import tpu as pltpu

F32 = jnp.float32
BF16 = jnp.bfloat16

D_MODEL = 1024
DEPTH = 2
GRID_W = 64
A_WIDTH = D_MODEL // 2
A_HEADS = 4
A_HEAD_DIM = A_WIDTH // A_HEADS
A_HALF = A_HEAD_DIM // 2
B_WIDTH = D_MODEL - A_WIDTH
B_HEAD = 64
B_HEADS = B_WIDTH // B_HEAD
LORA_W = 64
LORA_A = 64
LORA_G = 128
IN_B = 3 * B_WIDTH + 2 * LORA_W + 2 * LORA_A + LORA_G
IN_EVEN = 3 * A_WIDTH + IN_B
C_GROUPS = 8
C_GROUP_DIM = D_MODEL // C_GROUPS
D_FF = 2816
ROPE_BASE = 10000.0
RMS_EPS = 1e-6
LNX_EPS = 64e-5

LANES = 128
SUBLANES = 8
PAIR = 2 * B_HEAD
CHUNK = 64
MOD_COLS = 1536
IN_TILE_ROWS = 512
OUT_TILE_ROWS = 1024
FFN_TILE_ROWS = 1024
FF_CHUNK = 256
FF_ROWS = 256
PREP_ROWS = 256
ATTN_Q_ROWS = 256
ODD_ROWS = 1024
ODD_COLS = 256
VMEM_LIMIT = 48 * 1024 * 1024
MOD_ROWS = 8


def _cparams(*sem):
    return pltpu.CompilerParams(dimension_semantics=sem, vmem_limit_bytes=VMEM_LIMIT)


def _sigmoid(x):
    return 1.0 / (1.0 + jnp.exp(-x))


def _silu(x):
    return x * (0.5 + 0.5 * jnp.tanh(0.5 * x))


def _dot(a, b):
    return jnp.dot(a.astype(BF16), b.astype(BF16), preferred_element_type=F32)


_NN = (((1,), (0,)), ((), ()))
_NT = (((1,), (1,)), ((), ()))
_TN = (((0,), (0,)), ((), ()))


def _split_bf16(x, n):
    parts = []
    rem = x
    for i in range(n):
        p = rem.astype(BF16)
        parts.append(p)
        if i + 1 < n:
            rem = rem - p.astype(F32)
    return parts


def _mm(a, b, dims):
    return lax.dot_general(a.astype(BF16), b.astype(BF16), dims, preferred_element_type=F32)


def _segsum(x, bd, pieces=2):
    acc = None
    for p in _split_bf16(x, pieces):
        t = jnp.dot(p, bd, preferred_element_type=F32)
        acc = t if acc is None else acc + t
    return acc


def _rms(x):
    return x * lax.rsqrt(jnp.mean(x * x, axis=-1, keepdims=True) + RMS_EPS)


def _modnorm(x, gain, scale, shift):
    return _rms(x) * gain * (1.0 + scale) + shift


def _mod_kernel(c_ref, w_ref, b_ref, o_ref):
    c = c_ref[...]
    s = c * _sigmoid(c)
    o_ref[0] = _dot(s, w_ref[0]) + b_ref[0]


def _modulation(cvec, w_ada, b_ada):
    tn = MOD_COLS
    n = 6 * D_MODEL
    assert n % tn == 0
    return pl.pallas_call(
        _mod_kernel,
        grid=(DEPTH, n // tn),
        in_specs=[
            pl.BlockSpec((MOD_ROWS, D_MODEL), lambda l, j: (0, 0)),
            pl.BlockSpec((1, D_MODEL, tn), lambda l, j: (l, 0, j)),
            pl.BlockSpec((1, 1, tn), lambda l, j: (l, 0, j)),
        ],
        out_specs=pl.BlockSpec((1, MOD_ROWS, tn), lambda l, j: (l, 0, j)),
        out_shape=jax.ShapeDtypeStruct((DEPTH, MOD_ROWS, n), F32),
        compiler_params=_cparams("parallel", "parallel"),
        name="modulation",
    )(cvec, w_ada, b_ada.reshape(DEPTH, 1, n))


def _mod_spec(tm, seq_len, mod_row0):
    if mod_row0 == 0:
        return pl.BlockSpec((1, 6, D_MODEL), lambda i, *_: (0, 0, 0))
    return pl.BlockSpec((1, 6, D_MODEL), lambda i, *_: (mod_row0 + (i * tm) // seq_len, 0, 0))


def _rope(x, cos_t, sin_t):
    lane = lax.broadcasted_iota(jnp.int32, x.shape, 1)
    nxt = pltpu.roll(x, LANES - 1, 1)
    prv = pltpu.roll(x, 1, 1)
    partner = jnp.where(lane % 2 == 0, nxt, prv)
    return x * cos_t + partner * sin_t


def _even_in_kernel(*refs, rope, cache):
    x_ref, mod_ref, gains_ref, w_ref = refs[:4]
    refs = refs[4:]
    if rope:
        cos_ref, sin_ref = refs[:2]
        refs = refs[2:]
    q_ref, k_ref, v_ref, fb_ref = refs[:4]
    cache_refs = refs[4:]
    mod = mod_ref[0]
    h = _modnorm(x_ref[...], gains_ref[0:1], mod[1:2], mod[0:1]).astype(BF16)
    a = A_WIDTH
    hd = A_HEAD_DIM
    for n, out_ref in enumerate((q_ref, k_ref, v_ref)):
        y = jnp.dot(h, w_ref[:, n * a:(n + 1) * a], preferred_element_type=F32)
        if cache and n > 0:
            rows_c = cache_refs[n - 1].shape[2]
            for sq in range(cache_refs[n - 1].shape[0]):
                for hh in range(A_HEADS):
                    cache_refs[n - 1][sq, 0, :, hh, :] = y[sq * rows_c:(sq + 1) * rows_c, hh * hd:(hh + 1) * hd]
        if rope and n < 2:
            for hh in range(A_HEADS):
                cols = slice(hh * hd, (hh + 1) * hd)
                out_ref[:, cols] = _rope(y[:, cols], cos_ref[...], sin_ref[...]).astype(BF16)
        else:
            out_ref[...] = y.astype(BF16)
    fb_ref[...] = jnp.dot(h, w_ref[:, 3 * a:], preferred_element_type=F32)


def _even_in(x, mod, gains, w, seq_len, mod_row0, tm, rope=None, cache=False):
    t = x.shape[0]
    assert seq_len % tm == 0 or (tm % seq_len == 0 and rope is None and mod_row0 == 0)
    per_seq = max(seq_len // tm, 1)
    seqs = max(tm // seq_len, 1)
    row = lambda i: (i, 0)
    operands = [x, mod, gains, w]
    in_specs = [
        pl.BlockSpec((tm, D_MODEL), row),
        _mod_spec(tm, seq_len, mod_row0),
        pl.BlockSpec((4, D_MODEL), lambda i: (0, 0)),
        pl.BlockSpec((D_MODEL, IN_EVEN), lambda i: (0, 0)),
    ]
    if rope is not None:
        tab = pl.BlockSpec((tm, A_HEAD_DIM), lambda i: (i % per_seq, 0))
        operands += list(rope)
        in_specs += [tab, tab]
    out_specs = [pl.BlockSpec((tm, A_WIDTH), row)] * 3 + [pl.BlockSpec((tm, IN_B), row)]
    out_shape = [jax.ShapeDtypeStruct((t, A_WIDTH), BF16)] * 3 + [jax.ShapeDtypeStruct((t, IN_B), F32)]
    if cache:
        out_specs += [pl.BlockSpec((seqs, 1, tm // seqs, A_HEADS, A_HEAD_DIM),
                                   lambda i: (i // per_seq, 0, i % per_seq, 0, 0))] * 2
        out_shape += [jax.ShapeDtypeStruct((t // seq_len, 1, seq_len, A_HEADS, A_HEAD_DIM), F32)] * 2
    return pl.pallas_call(
        functools.partial(_even_in_kernel, rope=rope is not None, cache=cache),
        grid=(t // tm,),
        in_specs=in_specs,
        out_specs=out_specs,
        out_shape=out_shape,
        compiler_params=_cparams("parallel"),
        name="even_in",
    )(*operands)


def _attn_kernel(*refs, has_ctx, lam_init):
    if has_ctx:
        lam_ref, sub_ref, q_ref, k_ref, v_ref, ck_ref, cv_ref, o_ref = refs
    else:
        lam_ref, sub_ref, q_ref, k_ref, v_ref, o_ref = refs
    lp = lam_ref[...]
    l1 = jnp.sum(lp[0:1] * lp[1:2], axis=-1, keepdims=True)
    l2 = jnp.sum(lp[2:3] * lp[3:4], axis=-1, keepdims=True)
    lam = jnp.exp(l1) - jnp.exp(l2) + lam_init
    hd = A_HEAD_DIM
    lane = lax.broadcasted_iota(jnp.int32, (q_ref.shape[0], hd), 1)
    scale = A_HALF ** -0.5

    def head(h):
        cols = pl.ds(h * hd, hd)
        q = q_ref[:, cols]
        k = k_ref[:, cols].astype(BF16)
        if has_ctx:
            ck = ck_ref[:, cols].astype(BF16)
        scores = []
        for m in range(2):
            qm = jnp.where((lane < A_HALF) == (m == 0), q, 0.0).astype(BF16)
            s = lax.dot_general(qm, k, _NT, preferred_element_type=F32) * scale
            sc = lax.dot_general(qm, ck, _NT, preferred_element_type=F32) * scale if has_ctx else None
            scores.append((s, sc))
        yield
        probs = []
        for s, sc in scores:
            mx = jnp.max(s, axis=-1, keepdims=True)
            if has_ctx:
                mx = jnp.maximum(mx, jnp.max(sc, axis=-1, keepdims=True))
                ec = jnp.exp(sc - mx)
            e = jnp.exp(s - mx)
            z = jnp.sum(e, axis=-1, keepdims=True)
            if has_ctx:
                z = z + jnp.sum(ec, axis=-1, keepdims=True)
            inv = 1.0 / z
            probs.append((e * inv, ec * inv if has_ctx else None))
        w = (probs[0][0] - lam * probs[1][0]).astype(BF16)
        if has_ctx:
            wc = (probs[0][1] - lam * probs[1][1]).astype(BF16)
        yield
        o = jnp.dot(w, v_ref[:, cols].astype(BF16), preferred_element_type=F32)
        if has_ctx:
            o = o + jnp.dot(wc, cv_ref[:, cols].astype(BF16), preferred_element_type=F32)
        yield
        o_ref[:, cols] = (_rms(o) * sub_ref[...] * (1.0 - lam_init)).astype(BF16)

    _round_robin(head(h) for h in range(A_HEADS))


def _attention(q, k, v, lam_p, subln, lam_init, n_seq, seq_len, ctx=None):
    t = q.shape[0]
    tq = min(seq_len, ATTN_Q_ROWS)
    nq = seq_len // tq
    small = [pl.BlockSpec((4, A_HALF), lambda b, i: (0, 0)),
             pl.BlockSpec((1, A_HEAD_DIM), lambda b, i: (0, 0))]
    qblk = pl.BlockSpec((tq, A_WIDTH), lambda b, i: (b * nq + i, 0))
    kblk = pl.BlockSpec((seq_len, A_WIDTH), lambda b, i: (b, 0))
    operands = [lam_p, subln, q, k, v]
    in_specs = small + [qblk, kblk, kblk]
    if ctx is not None:
        ck, cv = ctx
        cblk = pl.BlockSpec((ck.shape[0] // n_seq, A_WIDTH), lambda b, i: (b, 0))
        operands += [ck, cv]
        in_specs += [cblk, cblk]
    return pl.pallas_call(
        functools.partial(_attn_kernel, has_ctx=ctx is not None, lam_init=lam_init),
        grid=(n_seq, nq),
        in_specs=in_specs,
        out_specs=qblk,
        out_shape=jax.ShapeDtypeStruct((t, A_WIDTH), BF16),
        compiler_params=_cparams("parallel", "parallel"),
        name="diff_attn",
    )(*operands)


def _cumsum_rows(x, reverse):
    n = x.shape[0]
    ridx = lax.broadcasted_iota(jnp.int32, x.shape, 0)
    s = 1
    while s < n:
        if reverse:
            x = x + jnp.where(ridx < n - s, pltpu.roll(x, n - s, 0), 0.0)
        else:
            x = x + jnp.where(ridx >= s, pltpu.roll(x, s, 0), 0.0)
        s *= 2
    return x


def _half_rows(m, s, first):
    off = 0 if first else s
    return jnp.concatenate([m[b + off:b + off + s] for b in range(0, m.shape[0], 2 * s)], axis=0)


def _spread_rows(h, s, first):
    zero = jnp.zeros((s, h.shape[1]), h.dtype)
    pieces = []
    for i in range(h.shape[0] // s):
        blk = h[i * s:(i + 1) * s]
        pieces += [blk, zero] if first else [zero, blk]
    return jnp.concatenate(pieces, axis=0)


def _round_robin(gens):
    gens = list(gens)
    while gens:
        alive = []
        for gen in gens:
            try:
                next(gen)
                alive.append(gen)
            except StopIteration:
                pass
        gens = alive


def _rwkv_kernel(*refs, seq_len, has_state, pairs, unroll):
    (fr_ref, fk_ref, fv_ref, fl_ref, mur_ref, muk_ref, muv_ref, mul_ref, w0_ref, w2_ref, a0_ref,
     a2_ref, g2_ref, kvec_ref, lnx_ref, bd_ref, bdm_ref) = refs[:17]
    refs = refs[17:]
    if has_state:
        s0_ref = refs[0]
        refs = refs[1:]
    yb_ref, sf_ref, pre_ref, g_scr, bonus_scr, r1_scr, m1_scr, d0_scr, gam_scr, y_scr, s_scr = refs

    def shifted(ref, mu_ref, r0):
        f = ref[r0:r0 + PREP_ROWS, :]
        zero = jnp.zeros((1, f.shape[1]), F32)
        before = zero if r0 == 0 else ref[r0 - 1:r0, :]
        after = zero if r0 + PREP_ROWS == seq_len else ref[r0 + PREP_ROWS:r0 + PREP_ROWS + 1, :]
        local = lax.broadcasted_iota(jnp.int32, (PREP_ROWS, 1), 0)
        prev = jnp.where(local == 0, before, pltpu.roll(f, 1, 0))
        nxt = jnp.where(local == PREP_ROWS - 1, after, pltpu.roll(f, PREP_ROWS - 1, 0))
        return f * (1.0 - mu_ref[0:1] - mu_ref[1:2]) + mu_ref[0:1] * prev + mu_ref[1:2] * nxt

    bd = bd_ref[...]
    kv = kvec_ref[...]

    def prologue(r0):
        rows = slice(r0, r0 + PREP_ROWS)
        r = shifted(fr_ref, mur_ref, r0)
        k = shifted(fk_ref, muk_ref, r0)
        v = shifted(fv_ref, muv_ref, r0)
        lora = shifted(fl_ref, mul_ref, r0)
        wd = jnp.tanh(lora[:, 0:2 * LORA_W])
        ad = lora[:, 2 * LORA_W:2 * LORA_W + 2 * LORA_A]
        gd = lora[:, 2 * LORA_W + 2 * LORA_A:]
        g_scr[rows, :] = _dot(_sigmoid(gd), g2_ref[...])
        kk = k * kv[0:1]
        kk = kk * lax.rsqrt(_segsum(kk * kk, bd, pieces=1) + 1e-12)
        pre_ref[0, rows, :] = r
        pre_ref[1, rows, :] = v
        pre_ref[2, rows, :] = kk
        ksum = None
        for d in range(2):
            w_raw = w0_ref[d:d + 1] + _dot(wd, w2_ref[d])
            pre_ref[7 + d, rows, :] = -math.exp(-0.5) * _sigmoid(w_raw)
            a = _sigmoid(a0_ref[d:d + 1] + _dot(ad, a2_ref[d]))
            kd = k * (1.0 + (a - 1.0) * kv[1:2])
            pre_ref[3 + d, rows, :] = kd
            pre_ref[5 + d, rows, :] = kk * a
            ksum = kd if ksum is None else ksum + kd
        bonus_scr[rows, :] = _segsum(r * ksum * kv[2:3], bd) * v

    nc = seq_len // CHUNK
    c2 = 2 * CHUNK
    lane = lax.broadcasted_iota(jnp.int32, (CHUNK, PAIR), 1)
    head0 = lane < B_HEAD
    row = lax.broadcasted_iota(jnp.int32, (c2, c2), 0)
    col = lax.broadcasted_iota(jnp.int32, (c2, c2), 1)
    same = (row // CHUNK) == (col // CHUNK)
    eye = jnp.where(row == col, 1.0, 0.0)
    xor_rc = jnp.bitwise_xor(row, col)
    mm = _mm

    def stack(x):
        return jnp.concatenate([jnp.where(head0, x, 0.0), jnp.where(head0, 0.0, x)], axis=0)

    def fold(x):
        return x[0:CHUNK] + x[CHUNK:c2]

    def chunk_rows(cc):
        start = cc * CHUNK
        return pl.ds(start if isinstance(start, int) else pl.multiple_of(start, CHUNK), CHUNK)

    def chunk_local(cc, d, q):
        rows = chunk_rows(cc)
        lanes = pl.ds(q * PAIR, PAIR)
        z = 2 * q + d
        r = pre_ref[0, rows, lanes]
        v = pre_ref[1, rows, lanes]
        kk = pre_ref[2, rows, lanes]
        k = pre_ref[3 + d, rows, lanes]
        b = pre_ref[5 + d, rows, lanes]
        lw = pre_ref[7 + d, rows, lanes]
        incl = same & ((col <= row) if d == 0 else (col >= row))
        strict = same & ((col < row) if d == 0 else (col > row))
        g = _cumsum_rows(lw, reverse=(d == 1))
        gp = g - lw
        gtot = g[CHUNK - 1:CHUNK, :] if d == 0 else g[0:1, :]
        gm = 0.5 * gtot
        kx = k * jnp.exp(gm - g)
        bx = b * jnp.exp(gm - g)
        lhs = jnp.concatenate([stack(r * jnp.exp(g - gm)), stack(kk * jnp.exp(gp - gm))], axis=0)
        rhs = jnp.concatenate([kx, kx, bx, bx], axis=0)
        a_all = mm(lhs, rhs, _NT)
        yield
        a_rk = jnp.where(incl, a_all[0:c2, 0:c2], 0.0)
        a_rb = jnp.where(incl, a_all[0:c2, c2:2 * c2], 0.0)
        a_kk = jnp.where(strict, a_all[c2:2 * c2, 0:c2], 0.0)
        n_mat = jnp.where(strict, a_all[c2:2 * c2, c2:2 * c2], 0.0)
        vs = stack(v)
        akv = mm(a_kk, vs, _NN)
        y0 = mm(a_rk, vs, _NN)
        lower = (col < row) if d == 0 else (col > row)
        p_inv = eye - jnp.where(lower & (xor_rc == 1), n_mat, 0.0)
        yield
        s = 2
        while s < SUBLANES:
            c_s = jnp.where(lower & (xor_rc >= s) & (xor_rc < 2 * s), n_mat, 0.0)
            t = mm(c_s, p_inv, _NN)
            yield
            p_inv = p_inv - mm(p_inv, t, _NN)
            yield
            s *= 2
        while s < CHUNK:
            c_s = jnp.where(lower & (xor_rc >= s) & (xor_rc < 2 * s), n_mat, 0.0)
            t = _spread_rows(mm(_half_rows(c_s, s, d == 1), p_inv, _NN), s, d == 1)
            yield
            upd = mm(_half_rows(p_inv, s, d == 1), t, _NN)
            p_inv = p_inv - _spread_rows(upd, s, d == 1)
            yield
            s *= 2
        x = jnp.concatenate([stack(kk * jnp.exp(gp)), akv], axis=1)
        w12 = mm(p_inv, x, _NN)
        yield
        arb_w = mm(a_rb, w12, _NN)
        et = jnp.exp(gtot - g)
        bes = stack(b * et)
        d0_scr[z, cc] = mm(jnp.concatenate([vs, -w12[:, PAIR:]], axis=0),
                           jnp.concatenate([stack(k * et), bes], axis=0), _TN)
        m1_scr[z, cc] = mm(w12[:, 0:PAIR], bes, _TN).astype(BF16)
        yield
        r1_scr[z, cc] = (stack(r * jnp.exp(g)) - arb_w[:, 0:PAIR]).astype(BF16)
        y_scr[z, rows, :] = fold(y0 - arb_w[:, PAIR:])
        gam_scr[z, cc] = jnp.broadcast_to(jnp.exp(gtot), (SUBLANES, PAIR))

    assert PREP_ROWS == unroll * CHUNK
    prologue(0)
    for grp in range(nc // unroll):
        if (grp + 1) * PREP_ROWS < seq_len:
            prologue((grp + 1) * PREP_ROWS)
        _round_robin(chunk_local(grp * unroll + u, d, q)
                     for u in range(unroll) for q in range(pairs) for d in range(2))

    for z in range(2 * pairs):
        s_scr[z] = jnp.zeros((PAIR, PAIR), F32)
        if has_state:
            s_scr[z, 0:B_HEAD, 0:B_HEAD] = s0_ref[0, z % 2, 2 * (z // 2)]
            s_scr[z, B_HEAD:PAIR, B_HEAD:PAIR] = s0_ref[0, z % 2, 2 * (z // 2) + 1]

    def carry_body(grp, carry):
        states = [s_scr[z] for z in range(2 * pairs)]
        for u in range(unroll):
            c = grp * unroll + u
            for z in range(2 * pairs):
                cc = c if z % 2 == 0 else nc - 1 - c
                rows = chunk_rows(cc)
                s0 = states[z]
                y_scr[z, rows, :] += fold(mm(r1_scr[z, cc], s0, _NT))
                states[z] = s0 * gam_scr[z, cc, 0:1, :] - mm(s0, m1_scr[z, cc], _NN) + d0_scr[z, cc]
        for z in range(2 * pairs):
            s_scr[z] = states[z]
        return carry

    lax.fori_loop(0, nc // unroll, carry_body, 0)
    bdm = bdm_ref[...]
    for r0 in range(0, seq_len, PREP_ROWS):
        rows = slice(r0, r0 + PREP_ROWS)
        ys = jnp.concatenate([y_scr[2 * q, rows, :] + y_scr[2 * q + 1, rows, :] for q in range(pairs)],
                             axis=1)
        dlt = ys - _segsum(ys, bdm)
        yn = dlt * lax.rsqrt(_segsum(dlt * dlt, bdm, pieces=1) + LNX_EPS)
        yb = (yn * lnx_ref[0:1] + lnx_ref[1:2] + bonus_scr[rows, :]) * g_scr[rows, :]
        yb_ref[rows, :] = yb.astype(BF16)
    for z in range(2 * pairs):
        sf_ref[0, z % 2, 2 * (z // 2)] = s_scr[z, 0:B_HEAD, 0:B_HEAD]
        sf_ref[0, z % 2, 2 * (z // 2) + 1] = s_scr[z, B_HEAD:PAIR, B_HEAD:PAIR]


def _rwkv_mix(fb, s0, p, n_seq, seq_len, pairs=2, unroll=4):
    t = fb.shape[0]
    nc = seq_len // CHUNK
    w = pairs * PAIR
    nw = B_WIDTH // w
    lora_w = 2 * LORA_W + 2 * LORA_A + LORA_G
    assert seq_len % PREP_ROWS == 0 and (3 * B_WIDTH) % lora_w == 0
    col = lambda rows, base: pl.BlockSpec((rows, w), lambda b, q: (0, base + q))
    seq = lambda base: pl.BlockSpec((seq_len, w), lambda b, q: (b, base + q))
    lora_blk = 3 * B_WIDTH // lora_w
    sblk = pl.BlockSpec((1, 2, 2 * pairs, B_HEAD, B_HEAD), lambda b, q: (b, 0, q, 0, 0))
    mat = lambda dtype: pltpu.VMEM((2 * pairs, nc, PAIR, PAIR), dtype)
    in_specs = [
        seq(0), seq(nw), seq(2 * nw),
        pl.BlockSpec((seq_len, lora_w), lambda b, q: (b, lora_blk)),
        col(2, 0), col(2, nw), col(2, 2 * nw),
        pl.BlockSpec((2, lora_w), lambda b, q: (0, lora_blk)),
        col(2, 0),
        pl.BlockSpec((2, 2 * LORA_W, w), lambda b, q: (0, 0, q)),
        col(2, 0),
        pl.BlockSpec((2, 2 * LORA_A, w), lambda b, q: (0, 0, q)),
        col(LORA_G, 0), col(3, 0), col(2, 0),
        pl.BlockSpec((w, w), lambda b, q: (0, 0)),
        pl.BlockSpec((w, w), lambda b, q: (0, 0)),
    ]
    operands = [fb, fb, fb, fb, p["mu"], p["mu"], p["mu"], p["mu"], p["w0"], p["w2"], p["a0"], p["a2"],
                p["g2"], p["kvec"], p["lnx"], _block_diag(1.0, w), _block_diag(1.0 / B_HEAD, w)]
    if s0 is not None:
        in_specs.append(sblk)
        operands.append(s0)
    return pl.pallas_call(
        functools.partial(_rwkv_kernel, seq_len=seq_len, has_state=s0 is not None, pairs=pairs,
                          unroll=unroll),
        grid=(n_seq, nw),
        in_specs=in_specs,
        out_specs=[pl.BlockSpec((seq_len, w), lambda b, q: (b, q)), sblk],
        out_shape=[
            jax.ShapeDtypeStruct((t, B_WIDTH), BF16),
            jax.ShapeDtypeStruct((n_seq, 2, B_HEADS, B_HEAD, B_HEAD), F32),
        ],
        scratch_shapes=[pltpu.VMEM((9, seq_len, w), F32), pltpu.VMEM((seq_len, w), F32),
                        pltpu.VMEM((seq_len, w), F32), mat(BF16), mat(BF16), mat(F32),
                        pltpu.VMEM((2 * pairs, nc, SUBLANES, PAIR), F32),
                        pltpu.VMEM((2 * pairs, seq_len, PAIR), F32),
                        pltpu.VMEM((2 * pairs, PAIR, PAIR), F32)],
        compiler_params=_cparams("parallel", "parallel"),
        name="rwkv_mix",
    )(*operands)


def _even_out_kernel(x_ref, mod_ref, gains_ref, ya_ref, yb_ref, w_ref, o_ref):
    m = (jnp.dot(ya_ref[...], w_ref[0:A_WIDTH, :], preferred_element_type=F32)
         + jnp.dot(yb_ref[...], w_ref[A_WIDTH:, :], preferred_element_type=F32))
    mod = mod_ref[0]
    o_ref[...] = x_ref[...] + mod[2:3] * (_rms(m) * gains_ref[1:2])


def _even_out(x, mod, gains, ya, yb, w, seq_len, mod_row0, tm):
    t = x.shape[0]
    row = lambda i: (i, 0)
    return pl.pallas_call(
        _even_out_kernel,
        grid=(t // tm,),
        in_specs=[
            pl.BlockSpec((tm, D_MODEL), row),
            _mod_spec(tm, seq_len, mod_row0),
            pl.BlockSpec((4, D_MODEL), lambda i: (0, 0)),
            pl.BlockSpec((tm, A_WIDTH), row),
            pl.BlockSpec((tm, B_WIDTH), row),
            pl.BlockSpec((D_MODEL, D_MODEL), lambda i: (0, 0)),
        ],
        out_specs=pl.BlockSpec((tm, D_MODEL), row),
        out_shape=jax.ShapeDtypeStruct((t, D_MODEL), F32),
        compiler_params=_cparams("parallel"),
        name="even_out",
    )(x, mod, gains, ya, yb, w)


def _odd_kernel(x_ref, mod_ref, gains_ref, csc_ref, cl_ref, sl_ref, w_ref, o_ref,
                tc_scr, ts_scr, m_scr, *, seq_len):
    mod = mod_ref[0]
    gd = C_GROUP_DIM
    n_seq = x_ref.shape[0] // seq_len
    csc = csc_ref[...]

    for s in range(n_seq):
        rows = pl.ds(s * seq_len, seq_len)
        h = _modnorm(x_ref[rows, :], gains_ref[0:1], mod[1:2], mod[0:1]).astype(BF16)
        for g in range(C_GROUPS):
            t = jnp.dot(h[:, g * gd:(g + 1) * gd], csc, preferred_element_type=F32)
            tc_scr[rows, g * gd:(g + 1) * gd] = t[:, 0:gd].astype(BF16)
            ts_scr[rows, g * gd:(g + 1) * gd] = t[:, gd:].astype(BF16)

    def position_dft(s, cb):
        rows = pl.ds(s * seq_len, seq_len)
        cols = pl.ds(cb * ODD_COLS, ODD_COLS)
        f = (jnp.dot(cl_ref[...], tc_scr[rows, cols], preferred_element_type=F32)
             - jnp.dot(sl_ref[...], ts_scr[rows, cols], preferred_element_type=F32))
        yield
        part = jnp.dot(f.astype(BF16), w_ref[cols, :], preferred_element_type=F32)
        yield
        if cb == 0:
            m_scr[rows, :] = part
        else:
            m_scr[rows, :] += part

    _round_robin(position_dft(s, cb) for s in range(n_seq) for cb in range(D_MODEL // ODD_COLS))
    o_ref[...] = x_ref[...] + mod[2:3] * (_rms(m_scr[...]) * gains_ref[1:2])


def _dft_tables(n):
    idx = np.arange(n, dtype=np.int64)
    ang = 2.0 * np.pi * ((idx[:, None] * idx[None, :]) % n).astype(np.float64) / n
    s = 1.0 / math.sqrt(n)
    return np.cos(ang) * s, np.sin(ang) * s


def _odd_mixer(x, mod, gains, w, n_seq, seq_len, mod_row0):
    cc, sc = _dft_tables(C_GROUP_DIM)
    cl, sl = _dft_tables(seq_len)
    csc, cl, sl = (jnp.asarray(a, F32).astype(BF16) for a in (np.concatenate([cc, sc], axis=1), cl, sl))
    full = lambda shape: pl.BlockSpec(shape, lambda i: (0,) * len(shape))
    row = lambda i: (i, 0)
    tm = max(seq_len, ODD_ROWS)
    assert tm % seq_len == 0 and (n_seq * seq_len) % tm == 0
    return pl.pallas_call(
        functools.partial(_odd_kernel, seq_len=seq_len),
        grid=(n_seq * seq_len // tm,),
        in_specs=[
            pl.BlockSpec((tm, D_MODEL), row),
            _mod_spec(tm, seq_len, mod_row0),
            full((4, D_MODEL)), full((C_GROUP_DIM, 2 * C_GROUP_DIM)),
            full((seq_len, seq_len)), full((seq_len, seq_len)), full((D_MODEL, D_MODEL)),
        ],
        out_specs=pl.BlockSpec((tm, D_MODEL), row),
        out_shape=jax.ShapeDtypeStruct(x.shape, F32),
        scratch_shapes=[pltpu.VMEM((tm, D_MODEL), BF16), pltpu.VMEM((tm, D_MODEL), BF16),
                        pltpu.VMEM((tm, D_MODEL), F32)],
        compiler_params=_cparams("parallel"),
        name="odd_mixer",
    )(x, mod, gains, csc, cl, sl, w)


def _ffn_kernel(x_ref, mod_ref, gains_ref, wu_ref, wg_ref, cw_ref, cb_ref, wo_ref, o_ref,
                h_scr, acc_scr, *, seq_len):
    j = pl.program_id(1)
    mod = mod_ref[0]

    @pl.when(j == 0)
    def _():
        h_scr[...] = _modnorm(x_ref[...], gains_ref[2:3], mod[4:5], mod[3:4]).astype(BF16)
        acc_scr[...] = jnp.zeros_like(acc_scr)

    n_sub = h_scr.shape[0] // FF_ROWS
    gate = [None] * n_sub
    zero_row = jnp.zeros((1, FF_CHUNK), F32)
    wu = wu_ref[...].astype(BF16)
    wg = wg_ref[...].astype(BF16)
    wo = wo_ref[...].astype(BF16)

    def sub_tile(i):
        rows = pl.ds(i * FF_ROWS, FF_ROWS)
        h = h_scr[rows, :]
        u = jnp.dot(h, wu, preferred_element_type=F32)
        g = gate[i] = jnp.dot(h, wg, preferred_element_type=F32)
        yield
        before = zero_row if (i * FF_ROWS) % seq_len == 0 else gate[i - 1][FF_ROWS - 1:FF_ROWS, :]
        after = zero_row if ((i + 1) * FF_ROWS) % seq_len == 0 else gate[i + 1][0:1, :]
        pos = lax.broadcasted_iota(jnp.int32, (FF_ROWS, 1), 0)
        prev = jnp.where(pos == 0, before, pltpu.roll(g, 1, 0))
        nxt = jnp.where(pos == FF_ROWS - 1, after, pltpu.roll(g, FF_ROWS - 1, 0))
        gc = prev * cw_ref[0:1] + g * cw_ref[1:2] + nxt * cw_ref[2:3] + cb_ref[...]
        act = (_silu(gc) * u).astype(BF16)
        yield
        acc_scr[rows, :] += jnp.dot(act, wo, preferred_element_type=F32)

    _round_robin(sub_tile(i) for i in range(n_sub))

    @pl.when(j == pl.num_programs(1) - 1)
    def _():
        o_ref[...] = x_ref[...] + mod[5:6] * (_rms(acc_scr[...]) * gains_ref[3:4])


def _conv_ffn(x, mod, gains, layer, w_in, conv_w, conv_b, w_out, seq_len, mod_row0, tm):
    t = x.shape[0]
    nj = D_FF // FF_CHUNK
    assert seq_len % FF_ROWS == 0 and tm % seq_len == 0
    return pl.pallas_call(
        functools.partial(_ffn_kernel, seq_len=seq_len),
        grid=(t // tm, nj),
        in_specs=[
            pl.BlockSpec((tm, D_MODEL), lambda i, j: (i, 0)),
            _mod_spec(tm, seq_len, mod_row0),
            pl.BlockSpec((4, D_MODEL), lambda i, j: (0, 0)),
            pl.BlockSpec((None, D_MODEL, FF_CHUNK), lambda i, j: (layer, 0, j)),
            pl.BlockSpec((None, D_MODEL, FF_CHUNK), lambda i, j: (layer, 0, nj + j)),
            pl.BlockSpec((3, FF_CHUNK), lambda i, j: (0, j)),
            pl.BlockSpec((1, FF_CHUNK), lambda i, j: (0, j)),
            pl.BlockSpec((None, FF_CHUNK, D_MODEL), lambda i, j: (layer, j, 0)),
        ],
        out_specs=pl.BlockSpec((tm, D_MODEL), lambda i, j: (i, 0)),
        out_shape=jax.ShapeDtypeStruct((t, D_MODEL), F32),
        scratch_shapes=[pltpu.VMEM((tm, D_MODEL), BF16), pltpu.VMEM((tm, D_MODEL), F32)],
        compiler_params=_cparams("parallel", "arbitrary"),
        name="conv_ffn",
    )(x, mod, gains, w_in, w_in, conv_w, conv_b, w_out)


def _rope_tables(seq_len):
    rows = seq_len // GRID_W
    row = jnp.repeat(jnp.arange(rows, dtype=F32), GRID_W)
    col = jnp.tile(jnp.arange(GRID_W, dtype=F32), rows)
    n_freq = A_HALF // 4
    inv = ROPE_BASE ** (-jnp.arange(n_freq, dtype=F32) / n_freq)
    ang = jnp.concatenate([row[:, None] * inv, col[:, None] * inv], axis=-1)
    cos_t = jnp.tile(jnp.repeat(jnp.cos(ang), 2, axis=1), (1, 2))
    sin_t = jnp.tile(jnp.repeat(jnp.sin(ang), 2, axis=1), (1, 2))
    sign = jnp.where(jnp.arange(A_HEAD_DIM) % 2 == 0, -1.0, 1.0).astype(F32)
    return cos_t, sin_t * sign


def _pad_lora(w):
    z = jnp.zeros_like(w[0])
    return jnp.stack([jnp.concatenate([w[0], z], axis=0), jnp.concatenate([z, w[1]], axis=0)])


def _block_diag(value, width):
    head = np.arange(width) // B_HEAD
    return jnp.asarray(np.where(head[:, None] == head[None, :], value, 0.0), BF16)


def kernel(x_prompt, x_sample, cache_k, cache_v, state_wkv, c, c_ctx, w_ada, b_ada, norm_gains,
           w_in_even, w_out_even, diff_lambda, diff_subln, rwkv_shift_mu, rwkv_w0, rwkv_w2, rwkv_a0,
           rwkv_a2, rwkv_g2, rwkv_kvec, rwkv_lnx, w_out_odd, w_ffn_in, ffn_conv, ffn_conv_b,
           w_ffn_out):
    n_ctx, l_ctx, _ = x_prompt.shape
    n_lat, l_lat, _ = x_sample.shape
    assert 1 + n_lat <= MOD_ROWS
    cvec = jnp.concatenate(
        [c_ctx[None, :], c, jnp.zeros((MOD_ROWS - 1 - n_lat, D_MODEL), F32)], axis=0)
    mods = _modulation(cvec, w_ada, b_ada).reshape(DEPTH, MOD_ROWS, 6, D_MODEL)

    per_layer_bf16 = lambda w: [w[i].astype(BF16) for i in range(w.shape[0])]
    w_in_even_b = per_layer_bf16(w_in_even)
    w_out_even_b = per_layer_bf16(w_out_even)
    w_out_odd_b = per_layer_bf16(w_out_odd)
    cos_t, sin_t = _rope_tables(l_lat)

    def run_group(x, n_seq, seq_len, mod_row0, latent):
        ctx_out = None
        for l in range(DEPTH):
            mod = mods[l]
            gains = norm_gains[l]
            if l % 2 == 0:
                e = l // 2
                lam_init = 0.8 - 0.6 * math.exp(-0.3 * l)
                q, k, v, fb, *kv_cache = _even_in(
                    x, mod, gains, w_in_even_b[e], seq_len, mod_row0, IN_TILE_ROWS,
                    rope=(cos_t, sin_t) if latent else None, cache=not latent)
                if latent:
                    past = cache_k.shape[2]
                    ctx = (cache_k[:, e].reshape(n_seq * past, A_WIDTH),
                           cache_v[:, e].reshape(n_seq * past, A_WIDTH))
                    s0 = state_wkv[:, e]
                else:
                    ctx = None
                    s0 = None
                ya = _attention(q, k, v, diff_lambda[e], diff_subln[e][None, :], lam_init,
                                n_seq, seq_len, ctx)
                rwkv = {
                    "mu": rwkv_shift_mu[e], "w0": rwkv_w0[e], "w2": _pad_lora(rwkv_w2[e]).astype(BF16),
                    "a0": rwkv_a0[e], "a2": _pad_lora(rwkv_a2[e]).astype(BF16),
                    "g2": rwkv_g2[e].astype(BF16), "kvec": rwkv_kvec[e], "lnx": rwkv_lnx[e],
                }
                yb, s_fin = _rwkv_mix(fb, s0, rwkv, n_seq, seq_len, pairs=4 if seq_len <= 256 else 2)
                x = _even_out(x, mod, gains, ya, yb, w_out_even_b[e], seq_len, mod_row0, OUT_TILE_ROWS)
                if not latent:
                    ctx_out = (*kv_cache, s_fin)
            else:
                x = _odd_mixer(x, mod, gains, w_out_odd_b[l // 2], n_seq, seq_len, mod_row0)
            x = _conv_ffn(x, mod, gains, l, w_ffn_in, ffn_conv[l], ffn_conv_b[l][None, :],
                          w_ffn_out, seq_len, mod_row0, max(seq_len, FFN_TILE_ROWS))
        return x, ctx_out

    y_ctx, (k_new, v_new, s_new) = run_group(
        x_prompt.reshape(n_ctx * l_ctx, D_MODEL), n_ctx, l_ctx, 0, False)
    y_lat, _ = run_group(x_sample.reshape(n_lat * l_lat, D_MODEL), n_lat, l_lat, 1, True)

    n_even = (DEPTH + 1) // 2
    assert n_even == 1
    return (
        y_ctx.reshape(n_ctx, l_ctx, D_MODEL),
        y_lat.reshape(n_lat, l_lat, D_MODEL),
        k_new,
        v_new,
        s_new.reshape(n_ctx, n_even, 2, B_HEADS, B_HEAD, B_HEAD),
    )
```

```python
import functools
import math

import numpy as np
import jax
import jax.numpy as jnp
from jax import lax
from jax.experimental import pallas as pl
from jax.experimental.pallas import tpu as pltpu

F32 = jnp.float32
BF16 = jnp.bfloat16

D_MODEL = 1024
DEPTH = 2
GRID_W = 64
A_WIDTH = D_MODEL // 2
A_HEADS = 4
A_HEAD_DIM = A_WIDTH // A_HEADS
A_HALF = A_HEAD_DIM // 2
B_WIDTH = D_MODEL - A_WIDTH
B_HEAD = 64
B_HEADS = B_WIDTH // B_HEAD
LORA_W = 64
LORA_A = 64
LORA_G = 128
IN_B = 3 * B_WIDTH + 2 * LORA_W + 2 * LORA_A + LORA_G
IN_EVEN = 3 * A_WIDTH + IN_B
C_GROUPS = 8
C_GROUP_DIM = D_MODEL // C_GROUPS
D_FF = 2816
ROPE_BASE = 10000.0
RMS_EPS = 1e-6
LNX_EPS = 64e-5

LANES = 128
SUBLANES = 8
PAIR = 2 * B_HEAD
CHUNK = 64
MOD_COLS = 1536
IN_TILE_ROWS = 512
OUT_TILE_ROWS = 1024
FFN_TILE_ROWS = 1024
FF_CHUNK = 256
FF_ROWS = 256
PREP_ROWS = 256
ATTN_Q_ROWS = 256
ODD_ROWS = 1024
ODD_COLS = 256
VMEM_LIMIT = 48 * 1024 * 1024
MOD_ROWS = 8


def _cparams(*sem):
    return pltpu.CompilerParams(dimension_semantics=sem, vmem_limit_bytes=VMEM_LIMIT)


def _sigmoid(x):
    return 1.0 / (1.0 + jnp.exp(-x))


def _silu(x):
    return x * (0.5 + 0.5 * jnp.tanh(0.5 * x))


def _dot(a, b):
    return jnp.dot(a.astype(BF16), b.astype(BF16), preferred_element_type=F32)


_NN = (((1,), (0,)), ((), ()))
_NT = (((1,), (1,)), ((), ()))
_TN = (((0,), (0,)), ((), ()))


def _split_bf16(x, n):
    parts = []
    rem = x
    for i in range(n):
        p = rem.astype(BF16)
        parts.append(p)
        if i + 1 < n:
            rem = rem - p.astype(F32)
    return parts


def _mm(a, b, dims):
    return lax.dot_general(a.astype(BF16), b.astype(BF16), dims, preferred_element_type=F32)


def _segsum(x, bd, pieces=2):
    acc = None
    for p in _split_bf16(x, pieces):
        t = jnp.dot(p, bd, preferred_element_type=F32)
        acc = t if acc is None else acc + t
    return acc


def _rms(x):
    return x * lax.rsqrt(jnp.mean(x * x, axis=-1, keepdims=True) + RMS_EPS)


def _modnorm(x, gain, scale, shift):
    return _rms(x) * gain * (1.0 + scale) + shift


def _mod_kernel(c_ref, w_ref, b_ref, o_ref):
    c = c_ref[...]
    s = c * _sigmoid(c)
    o_ref[0] = _dot(s, w_ref[0]) + b_ref[0]


def _modulation(cvec, w_ada, b_ada):
    tn = MOD_COLS
    n = 6 * D_MODEL
    assert n % tn == 0
    return pl.pallas_call(
        _mod_kernel,
        grid=(DEPTH, n // tn),
        in_specs=[
            pl.BlockSpec((MOD_ROWS, D_MODEL), lambda l, j: (0, 0)),
            pl.BlockSpec((1, D_MODEL, tn), lambda l, j: (l, 0, j)),
            pl.BlockSpec((1, 1, tn), lambda l, j: (l, 0, j)),
        ],
        out_specs=pl.BlockSpec((1, MOD_ROWS, tn), lambda l, j: (l, 0, j)),
        out_shape=jax.ShapeDtypeStruct((DEPTH, MOD_ROWS, n), F32),
        compiler_params=_cparams("parallel", "parallel"),
        name="modulation",
    )(cvec, w_ada, b_ada.reshape(DEPTH, 1, n))


def _mod_spec(tm, seq_len, mod_row0):
    if mod_row0 == 0:
        return pl.BlockSpec((1, 6, D_MODEL), lambda i, *_: (0, 0, 0))
    return pl.BlockSpec((1, 6, D_MODEL), lambda i, *_: (mod_row0 + (i * tm) // seq_len, 0, 0))


def _rope(x, cos_t, sin_t):
    lane = lax.broadcasted_iota(jnp.int32, x.shape, 1)
    nxt = pltpu.roll(x, LANES - 1, 1)
    prv = pltpu.roll(x, 1, 1)
    partner = jnp.where(lane % 2 == 0, nxt, prv)
    return x * cos_t + partner * sin_t


def _even_in_kernel(*refs, rope, cache):
    x_ref, mod_ref, gains_ref, w_ref = refs[:4]
    refs = refs[4:]
    if rope:
        cos_ref, sin_ref = refs[:2]
        refs = refs[2:]
    q_ref, k_ref, v_ref, fb_ref = refs[:4]
    cache_refs = refs[4:]
    mod = mod_ref[0]
    h = _modnorm(x_ref[...], gains_ref[0:1], mod[1:2], mod[0:1]).astype(BF16)
    a = A_WIDTH
    hd = A_HEAD_DIM
    for n, out_ref in enumerate((q_ref, k_ref, v_ref)):
        y = jnp.dot(h, w_ref[:, n * a:(n + 1) * a], preferred_element_type=F32)
        if cache and n > 0:
            rows_c = cache_refs[n - 1].shape[2]
            for sq in range(cache_refs[n - 1].shape[0]):
                for hh in range(A_HEADS):
                    cache_refs[n - 1][sq, 0, :, hh, :] = y[sq * rows_c:(sq + 1) * rows_c, hh * hd:(hh + 1) * hd]
        if rope and n < 2:
            for hh in range(A_HEADS):
                cols = slice(hh * hd, (hh + 1) * hd)
                out_ref[:, cols] = _rope(y[:, cols], cos_ref[...], sin_ref[...]).astype(BF16)
        else:
            out_ref[...] = y.astype(BF16)
    fb_ref[...] = jnp.dot(h, w_ref[:, 3 * a:], preferred_element_type=F32)


def _even_in(x, mod, gains, w, seq_len, mod_row0, tm, rope=None, cache=False):
    t = x.shape[0]
    assert seq_len % tm == 0 or (tm % seq_len == 0 and rope is None and mod_row0 == 0)
    per_seq = max(seq_len // tm, 1)
    seqs = max(tm // seq_len, 1)
    row = lambda i: (i, 0)
    operands = [x, mod, gains, w]
    in_specs = [
        pl.BlockSpec((tm, D_MODEL), row),
        _mod_spec(tm, seq_len, mod_row0),
        pl.BlockSpec((4, D_MODEL), lambda i: (0, 0)),
        pl.BlockSpec((D_MODEL, IN_EVEN), lambda i: (0, 0)),
    ]
    if rope is not None:
        tab = pl.BlockSpec((tm, A_HEAD_DIM), lambda i: (i % per_seq, 0))
        operands += list(rope)
        in_specs += [tab, tab]
    out_specs = [pl.BlockSpec((tm, A_WIDTH), row)] * 3 + [pl.BlockSpec((tm, IN_B), row)]
    out_shape = [jax.ShapeDtypeStruct((t, A_WIDTH), BF16)] * 3 + [jax.ShapeDtypeStruct((t, IN_B), F32)]
    if cache:
        out_specs += [pl.BlockSpec((seqs, 1, tm // seqs, A_HEADS, A_HEAD_DIM),
                                   lambda i: (i // per_seq, 0, i % per_seq, 0, 0))] * 2
        out_shape += [jax.ShapeDtypeStruct((t // seq_len, 1, seq_len, A_HEADS, A_HEAD_DIM), F32)] * 2
    return pl.pallas_call(
        functools.partial(_even_in_kernel, rope=rope is not None, cache=cache),
        grid=(t // tm,),
        in_specs=in_specs,
        out_specs=out_specs,
        out_shape=out_shape,
        compiler_params=_cparams("parallel"),
        name="even_in",
    )(*operands)


def _attn_kernel(*refs, has_ctx, lam_init):
    if has_ctx:
        lam_ref, sub_ref, q_ref, k_ref, v_ref, ck_ref, cv_ref, o_ref = refs
    else:
        lam_ref, sub_ref, q_ref, k_ref, v_ref, o_ref = refs
    lp = lam_ref[...]
    l1 = jnp.sum(lp[0:1] * lp[1:2], axis=-1, keepdims=True)
    l2 = jnp.sum(lp[2:3] * lp[3:4], axis=-1, keepdims=True)
    lam = jnp.exp(l1) - jnp.exp(l2) + lam_init
    hd = A_HEAD_DIM
    lane = lax.broadcasted_iota(jnp.int32, (q_ref.shape[0], hd), 1)
    scale = A_HALF ** -0.5

    def head(h):
        cols = pl.ds(h * hd, hd)
        q = q_ref[:, cols]
        k = k_ref[:, cols].astype(BF16)
        if has_ctx:
            ck = ck_ref[:, cols].astype(BF16)
        scores = []
        for m in range(2):
            qm = jnp.where((lane < A_HALF) == (m == 0), q, 0.0).astype(BF16)
            s = lax.dot_general(qm, k, _NT, preferred_element_type=F32) * scale
            sc = lax.dot_general(qm, ck, _NT, preferred_element_type=F32) * scale if has_ctx else None
            scores.append((s, sc))
        yield
        probs = []
        for s, sc in scores:
            mx = jnp.max(s, axis=-1, keepdims=True)
            if has_ctx:
                mx = jnp.maximum(mx, jnp.max(sc, axis=-1, keepdims=True))
                ec = jnp.exp(sc - mx)
            e = jnp.exp(s - mx)
            z = jnp.sum(e, axis=-1, keepdims=True)
            if has_ctx:
                z = z + jnp.sum(ec, axis=-1, keepdims=True)
            inv = 1.0 / z
            probs.append((e * inv, ec * inv if has_ctx else None))
        w = (probs[0][0] - lam * probs[1][0]).astype(BF16)
        if has_ctx:
            wc = (probs[0][1] - lam * probs[1][1]).astype(BF16)
        yield
        o = jnp.dot(w, v_ref[:, cols].astype(BF16), preferred_element_type=F32)
        if has_ctx:
            o = o + jnp.dot(wc, cv_ref[:, cols].astype(BF16), preferred_element_type=F32)
        yield
        o_ref[:, cols] = (_rms(o) * sub_ref[...] * (1.0 - lam_init)).astype(BF16)

    _round_robin(head(h) for h in range(A_HEADS))


def _attention(q, k, v, lam_p, subln, lam_init, n_seq, seq_len, ctx=None):
    t = q.shape[0]
    tq = min(seq_len, ATTN_Q_ROWS)
    nq = seq_len // tq
    small = [pl.BlockSpec((4, A_HALF), lambda b, i: (0, 0)),
             pl.BlockSpec((1, A_HEAD_DIM), lambda b, i: (0, 0))]
    qblk = pl.BlockSpec((tq, A_WIDTH), lambda b, i: (b * nq + i, 0))
    kblk = pl.BlockSpec((seq_len, A_WIDTH), lambda b, i: (b, 0))
    operands = [lam_p, subln, q, k, v]
    in_specs = small + [qblk, kblk, kblk]
    if ctx is not None:
        ck, cv = ctx
        cblk = pl.BlockSpec((ck.shape[0] // n_seq, A_WIDTH), lambda b, i: (b, 0))
        operands += [ck, cv]
        in_specs += [cblk, cblk]
    return pl.pallas_call(
        functools.partial(_attn_kernel, has_ctx=ctx is not None, lam_init=lam_init),
        grid=(n_seq, nq),
        in_specs=in_specs,
        out_specs=qblk,
        out_shape=jax.ShapeDtypeStruct((t, A_WIDTH), BF16),
        compiler_params=_cparams("parallel", "parallel"),
        name="diff_attn",
    )(*operands)


def _cumsum_rows(x, reverse):
    n = x.shape[0]
    ridx = lax.broadcasted_iota(jnp.int32, x.shape, 0)
    s = 1
    while s < n:
        if reverse:
            x = x + jnp.where(ridx < n - s, pltpu.roll(x, n - s, 0), 0.0)
        else:
            x = x + jnp.where(ridx >= s, pltpu.roll(x, s, 0), 0.0)
        s *= 2
    return x


def _half_rows(m, s, first):
    off = 0 if first else s
    return jnp.concatenate([m[b + off:b + off + s] for b in range(0, m.shape[0], 2 * s)], axis=0)


def _spread_rows(h, s, first):
    zero = jnp.zeros((s, h.shape[1]), h.dtype)
    pieces = []
    for i in range(h.shape[0] // s):
        blk = h[i * s:(i + 1) * s]
        pieces += [blk, zero] if first else [zero, blk]
    return jnp.concatenate(pieces, axis=0)


def _round_robin(gens):
    gens = list(gens)
    while gens:
        alive = []
        for gen in gens:
            try:
                next(gen)
                alive.append(gen)
            except StopIteration:
                pass
        gens = alive


def _rwkv_kernel(*refs, seq_len, has_state, pairs, unroll):
    (fr_ref, fk_ref, fv_ref, fl_ref, mur_ref, muk_ref, muv_ref, mul_ref, w0_ref, w2_ref, a0_ref,
     a2_ref, g2_ref, kvec_ref, lnx_ref, bd_ref, bdm_ref) = refs[:17]
    refs = refs[17:]
    if has_state:
        s0_ref = refs[0]
        refs = refs[1:]
    yb_ref, sf_ref, pre_ref, g_scr, bonus_scr, r1_scr, m1_scr, d0_scr, gam_scr, y_scr, s_scr = refs

    def shifted(ref, mu_ref, r0):
        f = ref[r0:r0 + PREP_ROWS, :]
        zero = jnp.zeros((1, f.shape[1]), F32)
        before = zero if r0 == 0 else ref[r0 - 1:r0, :]
        after = zero if r0 + PREP_ROWS == seq_len else ref[r0 + PREP_ROWS:r0 + PREP_ROWS + 1, :]
        local = lax.broadcasted_iota(jnp.int32, (PREP_ROWS, 1), 0)
        prev = jnp.where(local == 0, before, pltpu.roll(f, 1, 0))
        nxt = jnp.where(local == PREP_ROWS - 1, after, pltpu.roll(f, PREP_ROWS - 1, 0))
        return f * (1.0 - mu_ref[0:1] - mu_ref[1:2]) + mu_ref[0:1] * prev + mu_ref[1:2] * nxt

    bd = bd_ref[...]
    kv = kvec_ref[...]

    def prologue(r0):
        rows = slice(r0, r0 + PREP_ROWS)
        r = shifted(fr_ref, mur_ref, r0)
        k = shifted(fk_ref, muk_ref, r0)
        v = shifted(fv_ref, muv_ref, r0)
        lora = shifted(fl_ref, mul_ref, r0)
        wd = jnp.tanh(lora[:, 0:2 * LORA_W])
        ad = lora[:, 2 * LORA_W:2 * LORA_W + 2 * LORA_A]
        gd = lora[:, 2 * LORA_W + 2 * LORA_A:]
        g_scr[rows, :] = _dot(_sigmoid(gd), g2_ref[...])
        kk = k * kv[0:1]
        kk = kk * lax.rsqrt(_segsum(kk * kk, bd, pieces=1) + 1e-12)
        pre_ref[0, rows, :] = r
        pre_ref[1, rows, :] = v
        pre_ref[2, rows, :] = kk
        ksum = None
        for d in range(2):
            w_raw = w0_ref[d:d + 1] + _dot(wd, w2_ref[d])
            pre_ref[7 + d, rows, :] = -math.exp(-0.5) * _sigmoid(w_raw)
            a = _sigmoid(a0_ref[d:d + 1] + _dot(ad, a2_ref[d]))
            kd = k * (1.0 + (a - 1.0) * kv[1:2])
            pre_ref[3 + d, rows, :] = kd
            pre_ref[5 + d, rows, :] = kk * a
            ksum = kd if ksum is None else ksum + kd
        bonus_scr[rows, :] = _segsum(r * ksum * kv[2:3], bd) * v

    nc = seq_len // CHUNK
    c2 = 2 * CHUNK
    lane = lax.broadcasted_iota(jnp.int32, (CHUNK, PAIR), 1)
    head0 = lane < B_HEAD
    row = lax.broadcasted_iota(jnp.int32, (c2, c2), 0)
    col = lax.broadcasted_iota(jnp.int32, (c2, c2), 1)
    same = (row // CHUNK) == (col // CHUNK)
    eye = jnp.where(row == col, 1.0, 0.0)
    xor_rc = jnp.bitwise_xor(row, col)
    mm = _mm

    def stack(x):
        return jnp.concatenate([jnp.where(head0, x, 0.0), jnp.where(head0, 0.0, x)], axis=0)

    def fold(x):
        return x[0:CHUNK] + x[CHUNK:c2]

    def chunk_rows(cc):
        start = cc * CHUNK
        return pl.ds(start if isinstance(start, int) else pl.multiple_of(start, CHUNK), CHUNK)

    def chunk_local(cc, d, q):
        rows = chunk_rows(cc)
        lanes = pl.ds(q * PAIR, PAIR)
        z = 2 * q + d
        r = pre_ref[0, rows, lanes]
        v = pre_ref[1, rows, lanes]
        kk = pre_ref[2, rows, lanes]
        k = pre_ref[3 + d, rows, lanes]
        b = pre_ref[5 + d, rows, lanes]
        lw = pre_ref[7 + d, rows, lanes]
        incl = same & ((col <= row) if d == 0 else (col >= row))
        strict = same & ((col < row) if d == 0 else (col > row))
        g = _cumsum_rows(lw, reverse=(d == 1))
        gp = g - lw
        gtot = g[CHUNK - 1:CHUNK, :] if d == 0 else g[0:1, :]
        gm = 0.5 * gtot
        kx = k * jnp.exp(gm - g)
        bx = b * jnp.exp(gm - g)
        lhs = jnp.concatenate([stack(r * jnp.exp(g - gm)), stack(kk * jnp.exp(gp - gm))], axis=0)
        rhs = jnp.concatenate([kx, kx, bx, bx], axis=0)
        a_all = mm(lhs, rhs, _NT)
        yield
        a_rk = jnp.where(incl, a_all[0:c2, 0:c2], 0.0)
        a_rb = jnp.where(incl, a_all[0:c2, c2:2 * c2], 0.0)
        a_kk = jnp.where(strict, a_all[c2:2 * c2, 0:c2], 0.0)
        n_mat = jnp.where(strict, a_all[c2:2 * c2, c2:2 * c2], 0.0)
        vs = stack(v)
        akv = mm(a_kk, vs, _NN)
        y0 = mm(a_rk, vs, _NN)
        lower = (col < row) if d == 0 else (col > row)
        p_inv = eye - jnp.where(lower & (xor_rc == 1), n_mat, 0.0)
        yield
        s = 2
        while s < SUBLANES:
            c_s = jnp.where(lower & (xor_rc >= s) & (xor_rc < 2 * s), n_mat, 0.0)
            t = mm(c_s, p_inv, _NN)
            yield
            p_inv = p_inv - mm(p_inv, t, _NN)
            yield
            s *= 2
        while s < CHUNK:
            c_s = jnp.where(lower & (xor_rc >= s) & (xor_rc < 2 * s), n_mat, 0.0)
            t = _spread_rows(mm(_half_rows(c_s, s, d == 1), p_inv, _NN), s, d == 1)
            yield
            upd = mm(_half_rows(p_inv, s, d == 1), t, _NN)
            p_inv = p_inv - _spread_rows(upd, s, d == 1)
            yield
            s *= 2
        x = jnp.concatenate([stack(kk * jnp.exp(gp)), akv], axis=1)
        w12 = mm(p_inv, x, _NN)
        yield
        arb_w = mm(a_rb, w12, _NN)
        et = jnp.exp(gtot - g)
        bes = stack(b * et)
        d0_scr[z, cc] = mm(jnp.concatenate([vs, -w12[:, PAIR:]], axis=0),
                           jnp.concatenate([stack(k * et), bes], axis=0), _TN)
        m1_scr[z, cc] = mm(w12[:, 0:PAIR], bes, _TN).astype(BF16)
        yield
        r1_scr[z, cc] = (stack(r * jnp.exp(g)) - arb_w[:, 0:PAIR]).astype(BF16)
        y_scr[z, rows, :] = fold(y0 - arb_w[:, PAIR:])
        gam_scr[z, cc] = jnp.broadcast_to(jnp.exp(gtot), (SUBLANES, PAIR))

    assert PREP_ROWS == unroll * CHUNK
    prologue(0)
    for grp in range(nc // unroll):
        if (grp + 1) * PREP_ROWS < seq_len:
            prologue((grp + 1) * PREP_ROWS)
        _round_robin(chunk_local(grp * unroll + u, d, q)
                     for u in range(unroll) for q in range(pairs) for d in range(2))

    for z in range(2 * pairs):
        s_scr[z] = jnp.zeros((PAIR, PAIR), F32)
        if has_state:
            s_scr[z, 0:B_HEAD, 0:B_HEAD] = s0_ref[0, z % 2, 2 * (z // 2)]
            s_scr[z, B_HEAD:PAIR, B_HEAD:PAIR] = s0_ref[0, z % 2, 2 * (z // 2) + 1]

    def carry_body(grp, carry):
        states = [s_scr[z] for z in range(2 * pairs)]
        for u in range(unroll):
            c = grp * unroll + u
            for z in range(2 * pairs):
                cc = c if z % 2 == 0 else nc - 1 - c
                rows = chunk_rows(cc)
                s0 = states[z]
                y_scr[z, rows, :] += fold(mm(r1_scr[z, cc], s0, _NT))
                states[z] = s0 * gam_scr[z, cc, 0:1, :] - mm(s0, m1_scr[z, cc], _NN) + d0_scr[z, cc]
        for z in range(2 * pairs):
            s_scr[z] = states[z]
        return carry

    for grp in range(nc // unroll):
        carry_body(grp, 0)
    bdm = bdm_ref[...]
    for r0 in range(0, seq_len, PREP_ROWS):
        rows = slice(r0, r0 + PREP_ROWS)
        ys = jnp.concatenate([y_scr[2 * q, rows, :] + y_scr[2 * q + 1, rows, :] for q in range(pairs)],
                             axis=1)
        dlt = ys - _segsum(ys, bdm)
        yn = dlt * lax.rsqrt(_segsum(dlt * dlt, bdm, pieces=1) + LNX_EPS)
        yb = (yn * lnx_ref[0:1] + lnx_ref[1:2] + bonus_scr[rows, :]) * g_scr[rows, :]
        yb_ref[rows, :] = yb.astype(BF16)
    for z in range(2 * pairs):
        sf_ref[0, z % 2, 2 * (z // 2)] = s_scr[z, 0:B_HEAD, 0:B_HEAD]
        sf_ref[0, z % 2, 2 * (z // 2) + 1] = s_scr[z, B_HEAD:PAIR, B_HEAD:PAIR]


def _rwkv_mix(fb, s0, p, n_seq, seq_len, pairs=2, unroll=4):
    t = fb.shape[0]
    nc = seq_len // CHUNK
    w = pairs * PAIR
    nw = B_WIDTH // w
    lora_w = 2 * LORA_W + 2 * LORA_A + LORA_G
    assert seq_len % PREP_ROWS == 0 and (3 * B_WIDTH) % lora_w == 0
    col = lambda rows, base: pl.BlockSpec((rows, w), lambda b, q: (0, base + q))
    seq = lambda base: pl.BlockSpec((seq_len, w), lambda b, q: (b, base + q))
    lora_blk = 3 * B_WIDTH // lora_w
    sblk = pl.BlockSpec((1, 2, 2 * pairs, B_HEAD, B_HEAD), lambda b, q: (b, 0, q, 0, 0))
    mat = lambda dtype: pltpu.VMEM((2 * pairs, nc, PAIR, PAIR), dtype)
    in_specs = [
        seq(0), seq(nw), seq(2 * nw),
        pl.BlockSpec((seq_len, lora_w), lambda b, q: (b, lora_blk)),
        col(2, 0), col(2, nw), col(2, 2 * nw),
        pl.BlockSpec((2, lora_w), lambda b, q: (0, lora_blk)),
        col(2, 0),
        pl.BlockSpec((2, 2 * LORA_W, w), lambda b, q: (0, 0, q)),
        col(2, 0),
        pl.BlockSpec((2, 2 * LORA_A, w), lambda b, q: (0, 0, q)),
        col(LORA_G, 0), col(3, 0), col(2, 0),
        pl.BlockSpec((w, w), lambda b, q: (0, 0)),
        pl.BlockSpec((w, w), lambda b, q: (0, 0)),
    ]
    operands = [fb, fb, fb, fb, p["mu"], p["mu"], p["mu"], p["mu"], p["w0"], p["w2"], p["a0"], p["a2"],
                p["g2"], p["kvec"], p["lnx"], _block_diag(1.0, w), _block_diag(1.0 / B_HEAD, w)]
    if s0 is not None:
        in_specs.append(sblk)
        operands.append(s0)
    return pl.pallas_call(
        functools.partial(_rwkv_kernel, seq_len=seq_len, has_state=s0 is not None, pairs=pairs,
                          unroll=unroll),
        grid=(n_seq, nw),
        in_specs=in_specs,
        out_specs=[pl.BlockSpec((seq_len, w), lambda b, q: (b, q)), sblk],
        out_shape=[
            jax.ShapeDtypeStruct((t, B_WIDTH), BF16),
            jax.ShapeDtypeStruct((n_seq, 2, B_HEADS, B_HEAD, B_HEAD), F32),
        ],
        scratch_shapes=[pltpu.VMEM((9, seq_len, w), F32), pltpu.VMEM((seq_len, w), F32),
                        pltpu.VMEM((seq_len, w), F32), mat(BF16), mat(BF16), mat(F32),
                        pltpu.VMEM((2 * pairs, nc, SUBLANES, PAIR), F32),
                        pltpu.VMEM((2 * pairs, seq_len, PAIR), F32),
                        pltpu.VMEM((2 * pairs, PAIR, PAIR), F32)],
        compiler_params=_cparams("parallel", "parallel"),
        name="rwkv_mix",
    )(*operands)


def _even_out_kernel(x_ref, mod_ref, gains_ref, ya_ref, yb_ref, w_ref, o_ref):
    m = (jnp.dot(ya_ref[...], w_ref[0:A_WIDTH, :], preferred_element_type=F32)
         + jnp.dot(yb_ref[...], w_ref[A_WIDTH:, :], preferred_element_type=F32))
    mod = mod_ref[0]
    o_ref[...] = x_ref[...] + mod[2:3] * (_rms(m) * gains_ref[1:2])


def _even_out(x, mod, gains, ya, yb, w, seq_len, mod_row0, tm):
    t = x.shape[0]
    row = lambda i: (i, 0)
    return pl.pallas_call(
        _even_out_kernel,
        grid=(t // tm,),
        in_specs=[
            pl.BlockSpec((tm, D_MODEL), row),
            _mod_spec(tm, seq_len, mod_row0),
            pl.BlockSpec((4, D_MODEL), lambda i: (0, 0)),
            pl.BlockSpec((tm, A_WIDTH), row),
            pl.BlockSpec((tm, B_WIDTH), row),
            pl.BlockSpec((D_MODEL, D_MODEL), lambda i: (0, 0)),
        ],
        out_specs=pl.BlockSpec((tm, D_MODEL), row),
        out_shape=jax.ShapeDtypeStruct((t, D_MODEL), F32),
        compiler_params=_cparams("parallel"),
        name="even_out",
    )(x, mod, gains, ya, yb, w)


def _odd_kernel(x_ref, mod_ref, gains_ref, csc_ref, cl_ref, sl_ref, w_ref, o_ref,
                tc_scr, ts_scr, m_scr, *, seq_len):
    mod = mod_ref[0]
    gd = C_GROUP_DIM
    n_seq = x_ref.shape[0] // seq_len
    csc = csc_ref[...]

    for s in range(n_seq):
        rows = pl.ds(s * seq_len, seq_len)
        h = _modnorm(x_ref[rows, :], gains_ref[0:1], mod[1:2], mod[0:1]).astype(BF16)
        for g in range(C_GROUPS):
            t = jnp.dot(h[:, g * gd:(g + 1) * gd], csc, preferred_element_type=F32)
            tc_scr[rows, g * gd:(g + 1) * gd] = t[:, 0:gd].astype(BF16)
            ts_scr[rows, g * gd:(g + 1) * gd] = t[:, gd:].astype(BF16)

    def position_dft(s, cb):
        rows = pl.ds(s * seq_len, seq_len)
        cols = pl.ds(cb * ODD_COLS, ODD_COLS)
        f = (jnp.dot(cl_ref[...], tc_scr[rows, cols], preferred_element_type=F32)
             - jnp.dot(sl_ref[...], ts_scr[rows, cols], preferred_element_type=F32))
        yield
        part = jnp.dot(f.astype(BF16), w_ref[cols, :], preferred_element_type=F32)
        yield
        if cb == 0:
            m_scr[rows, :] = part
        else:
            m_scr[rows, :] += part

    _round_robin(position_dft(s, cb) for s in range(n_seq) for cb in range(D_MODEL // ODD_COLS))
    o_ref[...] = x_ref[...] + mod[2:3] * (_rms(m_scr[...]) * gains_ref[1:2])


def _dft_tables(n):
    idx = np.arange(n, dtype=np.int64)
    ang = 2.0 * np.pi * ((idx[:, None] * idx[None, :]) % n).astype(np.float64) / n
    s = 1.0 / math.sqrt(n)
    return np.cos(ang) * s, np.sin(ang) * s


def _odd_mixer(x, mod, gains, w, n_seq, seq_len, mod_row0):
    cc, sc = _dft_tables(C_GROUP_DIM)
    cl, sl = _dft_tables(seq_len)
    csc, cl, sl = (jnp.asarray(a, F32).astype(BF16) for a in (np.concatenate([cc, sc], axis=1), cl, sl))
    full = lambda shape: pl.BlockSpec(shape, lambda i: (0,) * len(shape))
    row = lambda i: (i, 0)
    tm = max(seq_len, ODD_ROWS)
    assert tm % seq_len == 0 and (n_seq * seq_len) % tm == 0
    return pl.pallas_call(
        functools.partial(_odd_kernel, seq_len=seq_len),
        grid=(n_seq * seq_len // tm,),
        in_specs=[
            pl.BlockSpec((tm, D_MODEL), row),
            _mod_spec(tm, seq_len, mod_row0),
            full((4, D_MODEL)), full((C_GROUP_DIM, 2 * C_GROUP_DIM)),
            full((seq_len, seq_len)), full((seq_len, seq_len)), full((D_MODEL, D_MODEL)),
        ],
        out_specs=pl.BlockSpec((tm, D_MODEL), row),
        out_shape=jax.ShapeDtypeStruct(x.shape, F32),
        scratch_shapes=[pltpu.VMEM((tm, D_MODEL), BF16), pltpu.VMEM((tm, D_MODEL), BF16),
                        pltpu.VMEM((tm, D_MODEL), F32)],
        compiler_params=_cparams("parallel"),
        name="odd_mixer",
    )(x, mod, gains, csc, cl, sl, w)


def _ffn_kernel(x_ref, mod_ref, gains_ref, wu_ref, wg_ref, cw_ref, cb_ref, wo_ref, o_ref,
                h_scr, acc_scr, *, seq_len):
    j = pl.program_id(1)
    mod = mod_ref[0]

    @pl.when(j == 0)
    def _():
        h_scr[...] = _modnorm(x_ref[...], gains_ref[2:3], mod[4:5], mod[3:4]).astype(BF16)
        acc_scr[...] = jnp.zeros_like(acc_scr)

    n_sub = h_scr.shape[0] // FF_ROWS
    gate = [None] * n_sub
    zero_row = jnp.zeros((1, FF_CHUNK), F32)
    wu = wu_ref[...].astype(BF16)
    wg = wg_ref[...].astype(BF16)
    wo = wo_ref[...].astype(BF16)

    def sub_tile(i):
        rows = pl.ds(i * FF_ROWS, FF_ROWS)
        h = h_scr[rows, :]
        u = jnp.dot(h, wu, preferred_element_type=F32)
        g = gate[i] = jnp.dot(h, wg, preferred_element_type=F32)
        yield
        before = zero_row if (i * FF_ROWS) % seq_len == 0 else gate[i - 1][FF_ROWS - 1:FF_ROWS, :]
        after = zero_row if ((i + 1) * FF_ROWS) % seq_len == 0 else gate[i + 1][0:1, :]
        pos = lax.broadcasted_iota(jnp.int32, (FF_ROWS, 1), 0)
        prev = jnp.where(pos == 0, before, pltpu.roll(g, 1, 0))
        nxt = jnp.where(pos == FF_ROWS - 1, after, pltpu.roll(g, FF_ROWS - 1, 0))
        gc = prev * cw_ref[0:1] + g * cw_ref[1:2] + nxt * cw_ref[2:3] + cb_ref[...]
        act = (_silu(gc) * u).astype(BF16)
        yield
        acc_scr[rows, :] += jnp.dot(act, wo, preferred_element_type=F32)

    _round_robin(sub_tile(i) for i in range(n_sub))

    @pl.when(j == pl.num_programs(1) - 1)
    def _():
        o_ref[...] = x_ref[...] + mod[5:6] * (_rms(acc_scr[...]) * gains_ref[3:4])


def _conv_ffn(x, mod, gains, layer, w_in, conv_w, conv_b, w_out, seq_len, mod_row0, tm):
    t = x.shape[0]
    nj = D_FF // FF_CHUNK
    assert seq_len % FF_ROWS == 0 and tm % seq_len == 0
    return pl.pallas_call(
        functools.partial(_ffn_kernel, seq_len=seq_len),
        grid=(t // tm, nj),
        in_specs=[
            pl.BlockSpec((tm, D_MODEL), lambda i, j: (i, 0)),
            _mod_spec(tm, seq_len, mod_row0),
            pl.BlockSpec((4, D_MODEL), lambda i, j: (0, 0)),
            pl.BlockSpec((None, D_MODEL, FF_CHUNK), lambda i, j: (layer, 0, j)),
            pl.BlockSpec((None, D_MODEL, FF_CHUNK), lambda i, j: (layer, 0, nj + j)),
            pl.BlockSpec((3, FF_CHUNK), lambda i, j: (0, j)),
            pl.BlockSpec((1, FF_CHUNK), lambda i, j: (0, j)),
            pl.BlockSpec((None, FF_CHUNK, D_MODEL), lambda i, j: (layer, j, 0)),
        ],
        out_specs=pl.BlockSpec((tm, D_MODEL), lambda i, j: (i, 0)),
        out_shape=jax.ShapeDtypeStruct((t, D_MODEL), F32),
        scratch_shapes=[pltpu.VMEM((tm, D_MODEL), BF16), pltpu.VMEM((tm, D_MODEL), F32)],
        compiler_params=_cparams("parallel", "arbitrary"),
        name="conv_ffn",
    )(x, mod, gains, w_in, w_in, conv_w, conv_b, w_out)


def _rope_tables(seq_len):
    rows = seq_len // GRID_W
    row = jnp.repeat(jnp.arange(rows, dtype=F32), GRID_W)
    col = jnp.tile(jnp.arange(GRID_W, dtype=F32), rows)
    n_freq = A_HALF // 4
    inv = ROPE_BASE ** (-jnp.arange(n_freq, dtype=F32) / n_freq)
    ang = jnp.concatenate([row[:, None] * inv, col[:, None] * inv], axis=-1)
    cos_t = jnp.tile(jnp.repeat(jnp.cos(ang), 2, axis=1), (1, 2))
    sin_t = jnp.tile(jnp.repeat(jnp.sin(ang), 2, axis=1), (1, 2))
    sign = jnp.where(jnp.arange(A_HEAD_DIM) % 2 == 0, -1.0, 1.0).astype(F32)
    return cos_t, sin_t * sign


def _pad_lora(w):
    z = jnp.zeros_like(w[0])
    return jnp.stack([jnp.concatenate([w[0], z], axis=0), jnp.concatenate([z, w[1]], axis=0)])


def _block_diag(value, width):
    head = np.arange(width) // B_HEAD
    return jnp.asarray(np.where(head[:, None] == head[None, :], value, 0.0), BF16)


def kernel(x_prompt, x_sample, cache_k, cache_v, state_wkv, c, c_ctx, w_ada, b_ada, norm_gains,
           w_in_even, w_out_even, diff_lambda, diff_subln, rwkv_shift_mu, rwkv_w0, rwkv_w2, rwkv_a0,
           rwkv_a2, rwkv_g2, rwkv_kvec, rwkv_lnx, w_out_odd, w_ffn_in, ffn_conv, ffn_conv_b,
           w_ffn_out):
    n_ctx, l_ctx, _ = x_prompt.shape
    n_lat, l_lat, _ = x_sample.shape
    assert 1 + n_lat <= MOD_ROWS
    cvec = jnp.concatenate(
        [c_ctx[None, :], c, jnp.zeros((MOD_ROWS - 1 - n_lat, D_MODEL), F32)], axis=0)
    mods = _modulation(cvec, w_ada, b_ada).reshape(DEPTH, MOD_ROWS, 6, D_MODEL)

    per_layer_bf16 = lambda w: [w[i].astype(BF16) for i in range(w.shape[0])]
    w_in_even_b = per_layer_bf16(w_in_even)
    w_out_even_b = per_layer_bf16(w_out_even)
    w_out_odd_b = per_layer_bf16(w_out_odd)
    cos_t, sin_t = _rope_tables(l_lat)

    def run_group(x, n_seq, seq_len, mod_row0, latent):
        ctx_out = None
        for l in range(DEPTH):
            mod = mods[l]
            gains = norm_gains[l]
            if l % 2 == 0:
                e = l // 2
                lam_init = 0.8 - 0.6 * math.exp(-0.3 * l)
                q, k, v, fb, *kv_cache = _even_in(
                    x, mod, gains, w_in_even_b[e], seq_len, mod_row0, IN_TILE_ROWS,
                    rope=(cos_t, sin_t) if latent else None, cache=not latent)
                if latent:
                    past = cache_k.shape[2]
                    ctx = (cache_k[:, e].reshape(n_seq * past, A_WIDTH),
                           cache_v[:, e].reshape(n_seq * past, A_WIDTH))
                    s0 = state_wkv[:, e]
                else:
                    ctx = None
                    s0 = None
                ya = _attention(q, k, v, diff_lambda[e], diff_subln[e][None, :], lam_init,
                                n_seq, seq_len, ctx)
                rwkv = {
                    "mu": rwkv_shift_mu[e], "w0": rwkv_w0[e], "w2": _pad_lora(rwkv_w2[e]).astype(BF16),
                    "a0": rwkv_a0[e], "a2": _pad_lora(rwkv_a2[e]).astype(BF16),
                    "g2": rwkv_g2[e].astype(BF16), "kvec": rwkv_kvec[e], "lnx": rwkv_lnx[e],
                }
                yb, s_fin = _rwkv_mix(fb, s0, rwkv, n_seq, seq_len, pairs=4 if seq_len <= 256 else 2)
                x = _even_out(x, mod, gains, ya, yb, w_out_even_b[e], seq_len, mod_row0, OUT_TILE_ROWS)
                if not latent:
                    ctx_out = (*kv_cache, s_fin)
            else:
                x = _odd_mixer(x, mod, gains, w_out_odd_b[l // 2], n_seq, seq_len, mod_row0)
            x = _conv_ffn(x, mod, gains, l, w_ffn_in, ffn_conv[l], ffn_conv_b[l][None, :],
                          w_ffn_out, seq_len, mod_row0, max(seq_len, FFN_TILE_ROWS))
        return x, ctx_out

    y_ctx, (k_new, v_new, s_new) = run_group(
        x_prompt.reshape(n_ctx * l_ctx, D_MODEL), n_ctx, l_ctx, 0, False)
    y_lat, _ = run_group(x_sample.reshape(n_lat * l_lat, D_MODEL), n_lat, l_lat, 1, True)

    n_even = (DEPTH + 1) // 2
    assert n_even == 1
    return (
        y_ctx.reshape(n_ctx, l_ctx, D_MODEL),
        y_lat.reshape(n_lat, l_lat, D_MODEL),
        k_new,
        v_new,
        s_new.reshape(n_ctx, n_even, 2, B_HEADS, B_HEAD, B_HEAD),
    )
```
